```python
import math
import jax, jax.numpy as jnp
from jax import lax
import numpy as np

D_MODEL = 2048
BATCH = 16
SEQ = 2048
DEPTH = 2
DEC_BATCH = 16
DEC_SEQ = 64
PAST_LEN = 4096

CHUNK = 64
QBLOCK = 128
N_EVEN = (DEPTH + 1) // 2
N_ODD = DEPTH // 2
DN_ALPHA = float((2.0 * DEPTH) ** 0.25)
DN_BETA = float((8.0 * DEPTH) ** -0.25)
LN_EPS = 1e-5
RMS_EPS = 1e-6

MLA_HEADS = 8
MLA_NOPE = 128
MLA_ROPE = 64
MLA_QK = MLA_NOPE + MLA_ROPE
MLA_VDIM = 128
MLA_KV_LORA = 512
ROPE_THETA = 10000.0
MLA_COLS = MLA_HEADS * MLA_QK + MLA_KV_LORA + MLA_ROPE

RWKV_HEADS = 16
RWKV_HD = 64
RWKV_W = RWKV_HEADS * RWKV_HD
RWKV_DECAY_LORA = 96
RWKV_A_LORA = 96
RWKV_G_LORA = 256
RWKV_COLS = 3 * RWKV_W + RWKV_DECAY_LORA + RWKV_A_LORA + RWKV_G_LORA
RWKV_GN_EPS = 64e-5
EVEN_IN = MLA_COLS + RWKV_COLS
EVEN_MIX = MLA_HEADS * MLA_VDIM + RWKV_W

FOX_HEADS = 8
FOX_HD = 128
FOX_W = FOX_HEADS * FOX_HD
FOX_COLS = 4 * FOX_W + FOX_HEADS

GDN_HEADS = 8
GDN_HD = 128
GDN_W = GDN_HEADS * GDN_HD
GDN_CONV = 4
GDN_CHUNK = 64
GDN_QKV = 3 * GDN_W
GDN_COLS = GDN_QKV + 2 * GDN_HEADS + GDN_W
ODD_IN = FOX_COLS + GDN_COLS
ODD_MIX = FOX_W + GDN_W

N_EXPERTS = 64
TOP_K = 6
N_GROUPS = 8
TOPK_GROUPS = 4
EXPERT_DIM = 512
SHARED_DIM = 512
ROUTED_SCALE = 2.5
MOE_BLOCK = 256

kernel_name = 'hybrid_streaming_encoder_step'


def layer_norm(x, g, b):
    xf = x.astype(jnp.float32)
    mu = jnp.mean(xf, -1, keepdims=True)
    var = jnp.mean(jnp.square(xf - mu), -1, keepdims=True)
    return ((xf - mu) * lax.rsqrt(var + LN_EPS) * g.astype(jnp.float32) + b.astype(jnp.float32)).astype(x.dtype)


def rms_norm(x, g):
    xf = x.astype(jnp.float32)
    y = xf * lax.rsqrt(jnp.mean(jnp.square(xf), -1, keepdims=True) + RMS_EPS) * g.astype(jnp.float32)
    return y.astype(x.dtype)


def l2_normalize(x):
    xf = x.astype(jnp.float32)
    return xf * lax.rsqrt(jnp.sum(jnp.square(xf), -1, keepdims=True) + 1e-6)


def apply_rope(x, pos):
    half = MLA_ROPE // 2
    inv_freq = ROPE_THETA ** (-jnp.arange(half, dtype=jnp.float32) / half)
    ang = pos.astype(jnp.float32)[:, None] * inv_freq[None, :]
    ang = ang.reshape((ang.shape[0],) + (1,) * (x.ndim - 3) + (half,))
    cos, sin = jnp.cos(ang), jnp.sin(ang)
    xf = x.astype(jnp.float32)
    x1, x2 = xf[..., :half], xf[..., half:]
    return jnp.concatenate([x1 * cos - x2 * sin, x2 * cos + x1 * sin], -1).astype(x.dtype)


def causal_conv(u, prev, w):
    t = u.shape[1]
    full = jnp.concatenate([prev.astype(u.dtype), u], axis=1)
    out = full[:, 0:t] * w[0]
    for i in range(1, GDN_CONV):
        out = out + full[:, i:i + t] * w[i]
    return out, full[:, t:]


def block_attention(q, k, v, q_pos, k_pos, scale, per_frame, q_bias=None, k_bias=None):
    B, tq, H, _ = q.shape
    qb = min(QBLOCK, tq)
    nb = tq // qb

    def one_block(qi, pi, bi=None):
        s = jnp.einsum('bqhd,bkhd->bhqk', qi, k).astype(jnp.float32) * scale
        if bi is not None:
            s = s + (jnp.swapaxes(bi, 1, 2)[..., None] - jnp.swapaxes(k_bias, 1, 2)[:, :, None, :])
        if per_frame:
            allowed = k_pos[None, :] <= pi[:, None]
        else:
            allowed = (k_pos[None, :] // CHUNK) <= (pi[:, None] // CHUNK)
        s = jnp.where(allowed, s, -jnp.inf)
        p = jax.nn.softmax(s, axis=-1)
        return jnp.einsum('bhqk,bkhd->bqhd', p.astype(v.dtype), v)

    blocks = lambda t: jnp.moveaxis(t.reshape((t.shape[0], nb, qb) + t.shape[2:]), 1, 0)
    xs = (blocks(q), q_pos.reshape(nb, qb))
    if q_bias is not None:
        xs = xs + (blocks(q_bias),)
    o = lax.map(lambda a: one_block(*a), xs)
    return jnp.moveaxis(o, 0, 1).reshape(B, tq, H, v.shape[-1])


def mla_mixer(cols, pos, k_pos, ckv_past, kr_past, kv_norm, w_uk, w_uv):
    B, t, _ = cols.shape
    q = cols[..., :MLA_HEADS * MLA_QK].reshape(B, t, MLA_HEADS, MLA_QK)
    q = jnp.concatenate([q[..., :MLA_NOPE], apply_rope(q[..., MLA_NOPE:], pos)], -1)
    off = MLA_HEADS * MLA_QK
    ckv = rms_norm(cols[..., off:off + MLA_KV_LORA], kv_norm)
    kr = apply_rope(cols[..., off + MLA_KV_LORA:off + MLA_KV_LORA + MLA_ROPE], pos)
    ckv_all, kr_all = ckv, kr
    if ckv_past is not None:
        ckv_all = jnp.concatenate([ckv_past.astype(ckv.dtype), ckv], axis=1)
        kr_all = jnp.concatenate([kr_past.astype(kr.dtype), kr], axis=1)
    s_len = ckv_all.shape[1]
    k_nope = jnp.einsum('bsc,chd->bshd', ckv_all, w_uk)
    vals = jnp.einsum('bsc,chd->bshd', ckv_all, w_uv)
    keys = jnp.concatenate([k_nope, jnp.broadcast_to(kr_all[:, :, None, :], (B, s_len, MLA_HEADS, MLA_ROPE))], -1)
    o = block_attention(q, keys, vals, pos, k_pos, MLA_QK ** -0.5, per_frame=False)
    return o.reshape(B, t, MLA_HEADS * MLA_VDIM), ckv, kr


def rwkv7_mixer(cols, shift_prev, s0, p):
    B, t, _ = cols.shape
    prev = jnp.concatenate([shift_prev[:, None].astype(cols.dtype), cols[:, :-1]], axis=1)
    xs = (cols + (prev - cols) * p['mu']).astype(jnp.float32)
    r, k, v = xs[..., :RWKV_W], xs[..., RWKV_W:2 * RWKV_W], xs[..., 2 * RWKV_W:3 * RWKV_W]
    o1 = 3 * RWKV_W
    o2 = o1 + RWKV_DECAY_LORA
    o3 = o2 + RWKV_A_LORA
    w_raw = p['w0'] + jnp.tanh(xs[..., o1:o2]) @ p['w2']
    log_w = -jnp.exp(-jax.nn.softplus(-w_raw) - 0.5)
    a = jax.nn.sigmoid(p['a0'] + xs[..., o2:o3] @ p['a2'])
    g = jax.nn.sigmoid(xs[..., o3:]) @ p['g2']
    heads = lambda z: z.reshape(B, t, RWKV_HEADS, RWKV_HD)
    kk = l2_normalize(heads(k * p['k_k']))
    k = k * (1.0 + (a - 1.0) * p['k_a'])
    rh, kh, vh, ah, dh = heads(r), heads(k), heads(v), heads(a), jnp.exp(heads(log_w))

    def step(S, inp):
        r_t, k_t, v_t, kk_t, a_t, w_t = inp
        sa = jnp.einsum('bhvk,bhk->bhv', S, -kk_t)
        S = S * w_t[:, :, None, :] + sa[..., None] * (kk_t * a_t)[:, :, None, :] + v_t[..., None] * k_t[:, :, None, :]
        return S, jnp.einsum('bhvk,bhk->bhv', S, r_t)

    tm = lambda z: jnp.moveaxis(z, 1, 0)
    s_fin, y = lax.scan(step, s0.astype(jnp.float32), (tm(rh), tm(kh), tm(vh), tm(kk), tm(ah), tm(dh)))
    y = jnp.moveaxis(y, 0, 1)
    mu = jnp.mean(y, -1, keepdims=True)
    var = jnp.mean(jnp.square(y - mu), -1, keepdims=True)
    y = ((y - mu) * lax.rsqrt(var + RWKV_GN_EPS)).reshape(B, t, RWKV_W) * p['ln_g'] + p['ln_b']
    bonus = jnp.sum(rh * kh * p['r_k'], -1, keepdims=True) * vh
    y = (y + bonus.reshape(B, t, RWKV_W)) * g
    return y.astype(cols.dtype), cols[:, -1], s_fin.astype(cols.dtype)


def fox_mixer(cols, pos, k_pos, k_past, v_past, logf_past, p):
    B, t, _ = cols.shape
    heads = lambda z: z.reshape(B, t, FOX_HEADS, FOX_HD)
    q = rms_norm(heads(cols[..., :FOX_W]), p['q_norm'])
    k = rms_norm(heads(cols[..., FOX_W:2 * FOX_W]), p['k_norm'])
    v = heads(cols[..., 2 * FOX_W:3 * FOX_W])
    gate = jax.nn.sigmoid(cols[..., 3 * FOX_W:4 * FOX_W].astype(jnp.float32))
    logf = jax.nn.log_sigmoid(cols[..., 4 * FOX_W:].astype(jnp.float32) + p['f_bias'].astype(jnp.float32))
    k_all, v_all, logf_all = k, v, logf
    if k_past is not None:
        k_all = jnp.concatenate([k_past.astype(k.dtype), k], axis=1)
        v_all = jnp.concatenate([v_past.astype(v.dtype), v], axis=1)
        logf_all = jnp.concatenate([logf_past.astype(jnp.float32), logf], axis=1)
    cum = jnp.cumsum(logf_all, axis=1)
    o = block_attention(q, k_all, v_all, pos, k_pos, FOX_HD ** -0.5, per_frame=True, q_bias=cum[:, -t:], k_bias=cum)
    o = (o.reshape(B, t, FOX_W).astype(jnp.float32) * gate).astype(cols.dtype)
    return o, k, v, logf.astype(cols.dtype)


def chunk_gated_delta(q, k, v, g, beta, s0):
    B, t, H, _ = q.shape
    c = min(GDN_CHUNK, t)
    n = t // c

    def chunks(z):
        z = z.reshape((B, n, c, H) + z.shape[3:])
        return jnp.moveaxis(jnp.moveaxis(z, 1, 0), 3, 2)

    qc, kc, vc, gc, bc = chunks(q), chunks(k), chunks(v), chunks(g), chunks(beta)
    gc = jnp.cumsum(gc, axis=-1)
    idx = jnp.arange(c)
    tril = idx[:, None] >= idx[None, :]
    strict = idx[:, None] > idx[None, :]
    diff = gc[..., :, None] - gc[..., None, :]
    decay = jnp.where(tril, jnp.exp(jnp.where(tril, diff, 0.0)), 0.0)
    kb = kc * bc[..., None]
    low = jnp.where(strict, jnp.einsum('...id,...jd->...ij', kb, kc) * decay, 0.0)
    eye = jnp.eye(c, dtype=jnp.float32)
    t_inv = lax.linalg.triangular_solve(eye + low, jnp.broadcast_to(eye, low.shape), left_side=True, lower=True, unit_diagonal=True)
    u = t_inv @ (vc * bc[..., None])
    w = t_inv @ (kb * jnp.exp(gc)[..., None])
    a_qk = jnp.where(tril, jnp.einsum('...id,...jd->...ij', qc, kc) * decay, 0.0)

    def step(S, inp):
        q_i, k_i, u_i, w_i, g_i, a_i = inp
        v_new = u_i - w_i @ S
        o_i = (q_i * jnp.exp(g_i)[..., None]) @ S + a_i @ v_new
        g_last = g_i[..., -1:]
        S = S * jnp.exp(g_last)[..., None] + jnp.einsum('bhck,bhcv->bhkv', k_i * jnp.exp(g_last - g_i)[..., None], v_new)
        return S, o_i

    s_fin, o = lax.scan(step, s0, (qc, kc, u, w, gc, a_qk))
    o = jnp.swapaxes(jnp.moveaxis(o, 0, 1), 2, 3).reshape(B, t, H, -1)
    return o, s_fin


def gdn_mixer(cols, conv_prev, s0, p):
    B, t, _ = cols.shape
    qkv, conv_new = causal_conv(cols[..., :GDN_QKV], conv_prev, p['conv_w'])
    qkv = jax.nn.silu(qkv.astype(jnp.float32))
    heads = lambda z: z.reshape(B, t, GDN_HEADS, GDN_HD)
    q = l2_normalize(heads(qkv[..., :GDN_W])) * (GDN_HD ** -0.5)
    k = l2_normalize(heads(qkv[..., GDN_W:2 * GDN_W]))
    v = heads(qkv[..., 2 * GDN_W:])
    a_in = cols[..., GDN_QKV:GDN_QKV + GDN_HEADS].astype(jnp.float32)
    b_in = cols[..., GDN_QKV + GDN_HEADS:GDN_QKV + 2 * GDN_HEADS].astype(jnp.float32)
    z = heads(cols[..., GDN_QKV + 2 * GDN_HEADS:]).astype(jnp.float32)
    g = -jnp.exp(p['a_log'].astype(jnp.float32)) * jax.nn.softplus(a_in + p['dt_bias'].astype(jnp.float32))
    beta = jax.nn.sigmoid(b_in)
    o, s_fin = chunk_gated_delta(q, k, v, g, beta, s0.astype(jnp.float32))
    o = rms_norm(o, p['norm']) * jax.nn.silu(z)
    return o.reshape(B, t, GDN_W).astype(cols.dtype), conv_new, s_fin.astype(cols.dtype)


def routed_experts(xt, eidx, gates, w_gate, w_up, w_down):
    n_assign = eidx.shape[0] * TOP_K
    flat_e = eidx.reshape(-1)
    order = jnp.argsort(flat_e)
    sorted_e = flat_e[order]
    counts = jnp.bincount(flat_e, length=N_EXPERTS)
    padded = (counts + MOE_BLOCK - 1) // MOE_BLOCK * MOE_BLOCK
    pad_end = jnp.cumsum(padded)
    pad_start = pad_end - padded
    start = jnp.cumsum(counts) - counts
    dest = pad_start[sorted_e] + jnp.arange(n_assign) - start[sorted_e]
    n_blocks = -(-n_assign // MOE_BLOCK) + N_EXPERTS
    cap = n_blocks * MOE_BLOCK
    row_tok = jnp.zeros((cap,), jnp.int32).at[dest].set((order // TOP_K).astype(jnp.int32))
    row_gate = jnp.zeros((cap,), jnp.float32).at[dest].set(gates.reshape(-1)[order])
    blk_exp = jnp.minimum(jnp.searchsorted(pad_end, jnp.arange(n_blocks) * MOE_BLOCK, side='right'), N_EXPERTS - 1)

    def body(acc, blk):
        tok, gt, e = blk
        xb = xt[tok]
        h = jax.nn.silu(xb @ w_gate[e]) * (xb @ w_up[e])
        yb = (h @ w_down[e]).astype(jnp.float32) * gt[:, None]
        return acc.at[tok].add(yb), None

    acc, _ = lax.scan(body, jnp.zeros(xt.shape, jnp.float32),
                      (row_tok.reshape(n_blocks, MOE_BLOCK), row_gate.reshape(n_blocks, MOE_BLOCK), blk_exp))
    return acc


def moe_ffn(x, router, router_bias, w_gate, w_up, w_down, ws_gate, ws_up, ws_down):
    B, t, D = x.shape
    xt = x.reshape(B * t, D)
    n_tok = B * t
    per_group = N_EXPERTS // N_GROUPS
    scores = jax.nn.sigmoid((xt @ router).astype(jnp.float32))
    biased = scores + router_bias.astype(jnp.float32)
    grp = jnp.sum(lax.top_k(biased.reshape(n_tok, N_GROUPS, per_group), 2)[0], -1)
    _, gsel = lax.top_k(grp, TOPK_GROUPS)
    gmask = jnp.any(gsel[..., None] == jnp.arange(N_GROUPS), axis=1)
    emask = jnp.repeat(gmask, per_group, axis=1)
    _, eidx = lax.top_k(jnp.where(emask, biased, -jnp.inf), TOP_K)
    sel = jnp.take_along_axis(scores, eidx, axis=1)
    gates = sel / jnp.sum(sel, -1, keepdims=True) * ROUTED_SCALE
    routed = routed_experts(xt, eidx, gates, w_gate, w_up, w_down)
    shared = (jax.nn.silu(xt @ ws_gate) * (xt @ ws_up)) @ ws_down
    return (routed.astype(x.dtype) + shared).reshape(B, t, D)


def setup_inputs(seed: int = 0) -> dict:
    key = jax.random.key(seed)
    ks = iter(jax.random.split(key, 64))

    def nrm(shape, scale):
        return scale * jax.random.normal(next(ks), shape, jnp.float32)

    def uni(shape, lo, hi):
        return jax.random.uniform(next(ks), shape, jnp.float32, lo, hi)

    dt = jnp.exp(uni((N_ODD, GDN_HEADS), math.log(1e-3), math.log(1e-1)))
    return {
        'x_prompt': nrm((BATCH, SEQ, D_MODEL), 1.0),
        'x_sample': nrm((DEC_BATCH, DEC_SEQ, D_MODEL), 1.0),
        'cache_mla_ckv': nrm((N_EVEN, DEC_BATCH, PAST_LEN, MLA_KV_LORA), 1.0),
        'cache_mla_krope': nrm((N_EVEN, DEC_BATCH, PAST_LEN, MLA_ROPE), 1.0),
        'state_rwkv_shift': nrm((N_EVEN, DEC_BATCH, RWKV_COLS), 1.0),
        'state_rwkv_wkv': nrm((N_EVEN, DEC_BATCH, RWKV_HEADS, RWKV_HD, RWKV_HD), 0.3),
        'cache_fox_k': nrm((N_ODD, DEC_BATCH, PAST_LEN, FOX_HEADS, FOX_HD), 1.0),
        'cache_fox_v': nrm((N_ODD, DEC_BATCH, PAST_LEN, FOX_HEADS, FOX_HD), 1.0),
        'cache_fox_logf': jax.nn.log_sigmoid(2.0 + nrm((N_ODD, DEC_BATCH, PAST_LEN, FOX_HEADS), 1.0)),
        'state_gdn_conv': nrm((N_ODD, DEC_BATCH, GDN_CONV - 1, GDN_QKV), 1.0),
        'state_gdn_wkv': nrm((N_ODD, DEC_BATCH, GDN_HEADS, GDN_HD, GDN_HD), 0.1),
        'ln1_g': 1.0 + nrm((DEPTH, D_MODEL), 0.02),
        'ln1_b': nrm((DEPTH, D_MODEL), 0.02),
        'ln2_g': 1.0 + nrm((DEPTH, D_MODEL), 0.02),
        'ln2_b': nrm((DEPTH, D_MODEL), 0.02),
        'ev_w_in': nrm((N_EVEN, D_MODEL, EVEN_IN), D_MODEL ** -0.5),
        'ev_w_out': nrm((N_EVEN, EVEN_MIX, D_MODEL), EVEN_MIX ** -0.5 * DN_BETA),
        'mla_kv_norm': 1.0 + nrm((N_EVEN, MLA_KV_LORA), 0.02),
        'mla_w_uk': nrm((N_EVEN, MLA_KV_LORA, MLA_HEADS, MLA_NOPE), MLA_KV_LORA ** -0.5),
        'mla_w_uv': nrm((N_EVEN, MLA_KV_LORA, MLA_HEADS, MLA_VDIM), MLA_KV_LORA ** -0.5),
        'rwkv_mu': uni((N_EVEN, RWKV_COLS), 0.0, 1.0),
        'rwkv_w0': -2.0 + nrm((N_EVEN, RWKV_W), 0.5),
        'rwkv_w2': nrm((N_EVEN, RWKV_DECAY_LORA, RWKV_W), 0.1 * RWKV_DECAY_LORA ** -0.5),
        'rwkv_a0': nrm((N_EVEN, RWKV_W), 0.5),
        'rwkv_a2': nrm((N_EVEN, RWKV_A_LORA, RWKV_W), RWKV_A_LORA ** -0.5),
        'rwkv_g2': nrm((N_EVEN, RWKV_G_LORA, RWKV_W), RWKV_G_LORA ** -0.5),
        'rwkv_k_k': 0.85 + nrm((N_EVEN, RWKV_W), 0.05),
        'rwkv_k_a': 1.0 + nrm((N_EVEN, RWKV_W), 0.05),
        'rwkv_r_k': nrm((N_EVEN, RWKV_HEADS, RWKV_HD), 0.1),
        'rwkv_ln_g': 1.0 + nrm((N_EVEN, RWKV_W), 0.02),
        'rwkv_ln_b': nrm((N_EVEN, RWKV_W), 0.02),
        'od_w_in': nrm((N_ODD, D_MODEL, ODD_IN), D_MODEL ** -0.5),
        'od_w_out': nrm((N_ODD, ODD_MIX, D_MODEL), ODD_MIX ** -0.5 * DN_BETA),
        'fox_q_norm': 1.0 + nrm((N_ODD, FOX_HD), 0.02),
        'fox_k_norm': 1.0 + nrm((N_ODD, FOX_HD), 0.02),
        'fox_f_bias': 2.0 + nrm((N_ODD, FOX_HEADS), 0.5),
        'gdn_conv_w': nrm((N_ODD, GDN_CONV, GDN_QKV), 0.5),
        'gdn_a_log': jnp.log(uni((N_ODD, GDN_HEADS), 1.0, 16.0)),
        'gdn_dt_bias': dt + jnp.log(-jnp.expm1(-dt)),
        'gdn_norm': 1.0 + nrm((N_ODD, GDN_HD), 0.02),
        'moe_router': nrm((DEPTH, D_MODEL, N_EXPERTS), D_MODEL ** -0.5),
        'moe_router_bias': nrm((DEPTH, N_EXPERTS), 0.01),
        'moe_w_gate': nrm((DEPTH, N_EXPERTS, D_MODEL, EXPERT_DIM), D_MODEL ** -0.5),
        'moe_w_up': nrm((DEPTH, N_EXPERTS, D_MODEL, EXPERT_DIM), D_MODEL ** -0.5),
        'moe_w_down': nrm((DEPTH, N_EXPERTS, EXPERT_DIM, D_MODEL), EXPERT_DIM ** -0.5 * DN_BETA),
        'moe_ws_gate': nrm((DEPTH, D_MODEL, SHARED_DIM), D_MODEL ** -0.5),
        'moe_ws_up': nrm((DEPTH, D_MODEL, SHARED_DIM), D_MODEL ** -0.5),
        'moe_ws_down': nrm((DEPTH, SHARED_DIM, D_MODEL), SHARED_DIM ** -0.5 * DN_BETA),
    }


def reference(x_prompt, x_sample, cache_mla_ckv, cache_mla_krope, state_rwkv_shift, state_rwkv_wkv,
              cache_fox_k, cache_fox_v, cache_fox_logf, state_gdn_conv, state_gdn_wkv,
              ln1_g, ln1_b, ln2_g, ln2_b,
              ev_w_in, ev_w_out, mla_kv_norm, mla_w_uk, mla_w_uv,
              rwkv_mu, rwkv_w0, rwkv_w2, rwkv_a0, rwkv_a2, rwkv_g2, rwkv_k_k, rwkv_k_a, rwkv_r_k, rwkv_ln_g, rwkv_ln_b,
              od_w_in, od_w_out, fox_q_norm, fox_k_norm, fox_f_bias, gdn_conv_w, gdn_a_log, gdn_dt_bias, gdn_norm,
              moe_router, moe_router_bias, moe_w_gate, moe_w_up, moe_w_down, moe_ws_gate, moe_ws_up, moe_ws_down):
    b_p, t_p = x_prompt.shape[0], x_prompt.shape[1]
    t_s = x_sample.shape[1]
    pos_p = jnp.arange(t_p, dtype=jnp.int32)
    pos_s = PAST_LEN + jnp.arange(t_s, dtype=jnp.int32)
    kpos_s = jnp.arange(PAST_LEN + t_s, dtype=jnp.int32)
    names = ('ckv_p', 'kr_p', 'sh_p', 'rwkv_p', 'fk_p', 'fv_p', 'fl_p', 'cv_p', 'gdn_p',
             'ckv_s', 'kr_s', 'sh_s', 'rwkv_s', 'fk_s', 'fv_s', 'fl_s', 'cv_s', 'gdn_s')
    st = {n: [] for n in names}
    xp, xs = x_prompt, x_sample
    for layer in range(DEPTH):
        if layer % 2 == 0:
            li = layer // 2
            rw = {'mu': rwkv_mu[li], 'w0': rwkv_w0[li], 'w2': rwkv_w2[li], 'a0': rwkv_a0[li], 'a2': rwkv_a2[li],
                  'g2': rwkv_g2[li], 'k_k': rwkv_k_k[li], 'k_a': rwkv_k_a[li], 'r_k': rwkv_r_k[li],
                  'ln_g': rwkv_ln_g[li], 'ln_b': rwkv_ln_b[li]}

            def mixer(x, pos, kpos, ckv_past, kr_past, shift_prev, s0):
                cols = x @ ev_w_in[li]
                o_a, ckv, kr = mla_mixer(cols[..., :MLA_COLS], pos, kpos, ckv_past, kr_past,
                                         mla_kv_norm[li], mla_w_uk[li], mla_w_uv[li])
                o_b, shift, s_fin = rwkv7_mixer(cols[..., MLA_COLS:], shift_prev, s0, rw)
                return jnp.concatenate([o_a, o_b], axis=-1) @ ev_w_out[li], ckv, kr, shift, s_fin

            mp, a1, a2, a3, a4 = mixer(xp, pos_p, pos_p, None, None,
                                       jnp.zeros((b_p, RWKV_COLS), xp.dtype),
                                       jnp.zeros((b_p, RWKV_HEADS, RWKV_HD, RWKV_HD), xp.dtype))
            ms, b1, b2, b3, b4 = mixer(xs, pos_s, kpos_s, cache_mla_ckv[li], cache_mla_krope[li],
                                       state_rwkv_shift[li], state_rwkv_wkv[li])
            for n, val in (('ckv_p', a1), ('kr_p', a2), ('sh_p', a3), ('rwkv_p', a4),
                           ('ckv_s', b1), ('kr_s', b2), ('sh_s', b3), ('rwkv_s', b4)):
                st[n].append(val)
        else:
            li = layer // 2
            fx = {'q_norm': fox_q_norm[li], 'k_norm': fox_k_norm[li], 'f_bias': fox_f_bias[li]}
            gd = {'conv_w': gdn_conv_w[li], 'a_log': gdn_a_log[li], 'dt_bias': gdn_dt_bias[li], 'norm': gdn_norm[li]}

            def mixer(x, pos, kpos, k_past, v_past, lf_past, conv_prev, s0):
                cols = x @ od_w_in[li]
                o_c, k_new, v_new, lf_new = fox_mixer(cols[..., :FOX_COLS], pos, kpos, k_past, v_past, lf_past, fx)
                o_d, conv_new, s_fin = gdn_mixer(cols[..., FOX_COLS:], conv_prev, s0, gd)
                return jnp.concatenate([o_c, o_d], axis=-1) @ od_w_out[li], k_new, v_new, lf_new, conv_new, s_fin

            mp, a1, a2, a3, a4, a5 = mixer(xp, pos_p, pos_p, None, None, None,
                                           jnp.zeros((b_p, GDN_CONV - 1, GDN_QKV), xp.dtype),
                                           jnp.zeros((b_p, GDN_HEADS, GDN_HD, GDN_HD), xp.dtype))
            ms, b1, b2, b3, b4, b5 = mixer(xs, pos_s, kpos_s, cache_fox_k[li], cache_fox_v[li], cache_fox_logf[li],
                                           state_gdn_conv[li], state_gdn_wkv[li])
            for n, val in (('fk_p', a1), ('fv_p', a2), ('fl_p', a3), ('cv_p', a4), ('gdn_p', a5),
                           ('fk_s', b1), ('fv_s', b2), ('fl_s', b3), ('cv_s', b4), ('gdn_s', b5)):
                st[n].append(val)
        xp = layer_norm(DN_ALPHA * xp + mp, ln1_g[layer], ln1_b[layer])
        xs = layer_norm(DN_ALPHA * xs + ms, ln1_g[layer], ln1_b[layer])
        moe_args = (moe_router[layer], moe_router_bias[layer], moe_w_gate[layer], moe_w_up[layer], moe_w_down[layer],
                    moe_ws_gate[layer], moe_ws_up[layer], moe_ws_down[layer])
        xp = layer_norm(DN_ALPHA * xp + moe_ffn(xp, *moe_args), ln2_g[layer], ln2_b[layer])
        xs = layer_norm(DN_ALPHA * xs + moe_ffn(xs, *moe_args), ln2_g[layer], ln2_b[layer])
    return (xp, xs,
            jnp.stack(st['ckv_p']), jnp.stack(st['kr_p']), jnp.stack(st['sh_p']), jnp.stack(st['rwkv_p']),
            jnp.stack(st['fk_p']), jnp.stack(st['fv_p']), jnp.stack(st['fl_p']), jnp.stack(st['cv_p']), jnp.stack(st['gdn_p']),
            jnp.stack(st['ckv_s']), jnp.stack(st['kr_s']), jnp.stack(st['sh_s']), jnp.stack(st['rwkv_s']),
            jnp.stack(st['fk_s']), jnp.stack(st['fv_s']), jnp.stack(st['fl_s']), jnp.stack(st['cv_s']), jnp.stack(st['gdn_s']))
```

```python
import functools
import math

import jax
import jax.numpy as jnp
from jax import lax
from jax.experimental import pallas as pl
from jax.experimental.pallas import tpu as pltpu

F32 = jnp.float32
BF16 = jnp.bfloat16
HP = lax.Precision.HIGHEST

D_MODEL = 2048
DEPTH = 2
DN_ALPHA = float((2.0 * DEPTH) ** 0.25)
LN_EPS = 1e-5
RMS_EPS = 1e-6
CHUNK = 64
LANES = 128

MLA_HEADS, MLA_NOPE, MLA_ROPE, MLA_VDIM, MLA_KV_LORA = 8, 128, 64, 128, 512
MLA_QK = MLA_NOPE + MLA_ROPE
ROPE_THETA = 10000.0
RWKV_HEADS, RWKV_HD = 16, 64
RWKV_W = RWKV_HEADS * RWKV_HD
RWKV_DECAY_LORA, RWKV_A_LORA, RWKV_G_LORA = 96, 96, 256
RWKV_GN_EPS = 64e-5
RWKV_PCOLS = 3 * RWKV_W + 2 * LANES + RWKV_G_LORA
FOX_HEADS, FOX_HD = 8, 128
FOX_W = FOX_HEADS * FOX_HD
GDN_HEADS, GDN_HD, GDN_CONV = 8, 128, 4
GDN_W = GDN_HEADS * GDN_HD
GDN_QKV = 3 * GDN_W
N_EXPERTS, TOP_K, N_GROUPS, TOPK_GROUPS = 64, 6, 8, 4
EXPERT_DIM, SHARED_DIM = 512, 512
ROUTED_SCALE = 2.5
SCAN_CHUNK = 64
NEG_BIG = -1e30

VMEM_LIMIT_BYTES = 56 * 1024 * 1024


def _cp(*sem):
    return pltpu.CompilerParams(dimension_semantics=sem, vmem_limit_bytes=VMEM_LIMIT_BYTES)


def _dot(a, b, prec=None):
    return jnp.dot(a, b, precision=prec, preferred_element_type=F32)


def _dot_nt(a, b, prec=None):
    return lax.dot_general(a, b, (((1,), (1,)), ((), ())), precision=prec, preferred_element_type=F32)


def _dot_tn(a, b, prec=None):
    return lax.dot_general(a, b, (((0,), (0,)), ((), ())), precision=prec, preferred_element_type=F32)


def _sigmoid(x):
    return 1.0 / (1.0 + jnp.exp(-x))


def _softplus(x):
    return jnp.maximum(x, 0.0) + jnp.log(1.0 + jnp.exp(-jnp.abs(x)))


def _silu(x):
    return x * _sigmoid(x)


def _mm(pairs, *, tm, tn, epilogue, outs, aux=()):
    m, n = pairs[0][0].shape[0], pairs[0][1].shape[1]
    assert m % tm == 0 and n % tn == 0, (m, n, tm, tn)
    nj, ni = n // tn, m // tm
    in_specs, args = [], []
    for a, w in pairs:
        k = a.shape[1]
        in_specs += [pl.BlockSpec((tm, k), lambda j, i: (i, 0)), pl.BlockSpec((k, tn), lambda j, i: (0, j))]
        args += [a, w]
    for arr, kind in aux:
        if kind == 'row':
            in_specs.append(pl.BlockSpec((tm, arr.shape[1]), lambda j, i: (i, 0)))
        elif kind == 'rowcol':
            in_specs.append(pl.BlockSpec((tm, tn), lambda j, i: (i, j)))
        elif kind == 'row0':
            in_specs.append(pl.BlockSpec(arr.shape, lambda j, i: (0, 0)))
        else:
            in_specs.append(pl.BlockSpec((1, tn), lambda j, i: (0, j)))
        args.append(arr)
    out_shape = [jax.ShapeDtypeStruct((m, ow * nj), dt) for ow, dt in outs]
    out_specs = [pl.BlockSpec((tm, ow), lambda j, i: (i, j)) for ow, dt in outs]
    n_pairs, n_aux = len(pairs), len(aux)

    def body(*refs):
        acc = None
        for p in range(n_pairs):
            d = _dot(refs[2 * p][...], refs[2 * p + 1][...])
            acc = d if acc is None else acc + d
        res = epilogue(acc, *[r[...] for r in refs[2 * n_pairs:2 * n_pairs + n_aux]])
        for o_ref, val in zip(refs[2 * n_pairs + n_aux:], res):
            o_ref[...] = val.astype(o_ref.dtype)

    res = pl.pallas_call(body, grid=(nj, ni), in_specs=in_specs, out_specs=out_specs, out_shape=out_shape,
                         compiler_params=_cp("parallel", "parallel"))(*args)
    return res


def _rope_lanes(x, tab):
    c, s1, s2 = tab[:, :LANES], tab[:, LANES:2 * LANES], tab[:, 2 * LANES:]
    n = x.shape[1]
    lo = pltpu.roll(x, n - MLA_ROPE // 2, 1)
    hi = pltpu.roll(x, MLA_ROPE // 2, 1)
    if n > LANES:
        reps = n // LANES
        c, s1, s2 = (jnp.concatenate([t] * reps, axis=1) for t in (c, s1, s2))
    return x * c + lo * s1 + hi * s2


def _ep_rms(acc, g):
    y = acc * lax.rsqrt(jnp.mean(acc * acc, axis=-1, keepdims=True) + RMS_EPS) * g
    return y, y


def _ep_ln(acc, x, g, b):
    z = DN_ALPHA * x + acc
    mu = jnp.mean(z, axis=-1, keepdims=True)
    zc = z - mu
    var = jnp.mean(zc * zc, axis=-1, keepdims=True)
    y = zc * lax.rsqrt(var + LN_EPS) * g + b
    return y, y


def _flash(qs, ks, v, *, nb, tq_total, tk_total, q_row0, q_off, tq, tk, chunked, nh, bias=None, gate=None,
           out_dtype=BF16):
    assert tq_total % tq == 0 and tk_total % tk == 0 and q_row0 % tq == 0
    nq, nk = tq_total // tq, tk_total // tk
    qrb0 = q_row0 // tq
    n_q, n_k = len(qs), len(ks)
    has_bias, has_gate = bias is not None, gate is not None

    def last_k(qi):
        return jnp.minimum(nk - 1, (q_off + qi * tq + tq - 1) // tk)

    q_map = lambda b, h, qi, ki: (qrb0 + b * nq + qi, h)
    in_specs, args = [], []
    for q in qs:
        in_specs.append(pl.BlockSpec((tq, LANES), q_map))
        args.append(q)
    for arr, c0, per_head in ks:
        if per_head:
            in_specs.append(pl.BlockSpec((tk, LANES), lambda b, h, qi, ki, c0=c0: (b * nk + jnp.minimum(ki, last_k(qi)), c0 + h)))
        else:
            in_specs.append(pl.BlockSpec((tk, LANES), lambda b, h, qi, ki, c0=c0: (b * nk + jnp.minimum(ki, last_k(qi)), c0)))
        args.append(arr)
    v_arr, v_c0 = v
    in_specs.append(pl.BlockSpec((tk, LANES), lambda b, h, qi, ki: (b * nk + jnp.minimum(ki, last_k(qi)), v_c0 + h)))
    args.append(v_arr)
    if has_bias:
        in_specs.append(pl.BlockSpec((1, 1, tq, 1), lambda b, h, qi, ki: (b, h, qi, 0)))
        in_specs.append(pl.BlockSpec((1, 1, 1, tk), lambda b, h, qi, ki: (b, h, 0, jnp.minimum(ki, last_k(qi)))))
        args += list(bias)
    if has_gate:
        in_specs.append(pl.BlockSpec((tq, LANES), q_map))
        args.append(gate)

    def body(*refs):
        q_refs = refs[:n_q]
        k_refs = refs[n_q:n_q + n_k]
        v_ref = refs[n_q + n_k]
        pos = n_q + n_k + 1
        if has_bias:
            qb_ref, kb_ref = refs[pos], refs[pos + 1]
            pos += 2
        if has_gate:
            gate_ref = refs[pos]
            pos += 1
        o_ref, m_ref, l_ref, acc_ref = refs[pos:pos + 4]
        qi, ki = pl.program_id(2), pl.program_id(3)

        @pl.when(ki == 0)
        def _():
            m_ref[...] = jnp.full(m_ref.shape, NEG_BIG, F32)
            l_ref[...] = jnp.zeros(l_ref.shape, F32)
            acc_ref[...] = jnp.zeros(acc_ref.shape, F32)

        @pl.when(ki <= last_k(qi))
        def _():
            s = None
            for q_ref, k_ref in zip(q_refs, k_refs):
                d = _dot_nt(q_ref[...], k_ref[...])
                s = d if s is None else s + d
            if has_bias:
                s = s + (qb_ref[0, 0] - kb_ref[0, 0])
            qpos = q_off + qi * tq + lax.broadcasted_iota(jnp.int32, (tq, 1), 0)
            qlim = (qpos | (CHUNK - 1)) if chunked else qpos
            kpos = ki * tk + lax.broadcasted_iota(jnp.int32, (1, tk), 1)
            s = jnp.where(kpos <= qlim, s, NEG_BIG)
            m_old = m_ref[...]
            m_new = jnp.maximum(m_old, jnp.max(s, axis=-1, keepdims=True))
            p = jnp.exp(s - m_new)
            alpha = jnp.exp(m_old - m_new)
            l_ref[...] = alpha * l_ref[...] + jnp.sum(p, axis=-1, keepdims=True)
            acc_ref[...] = alpha * acc_ref[...] + _dot(p.astype(BF16), v_ref[...])
            m_ref[...] = m_new

        @pl.when(ki == nk - 1)
        def _():
            o = acc_ref[...] / l_ref[...]
            if has_gate:
                o = o * gate_ref[...]
            o_ref[...] = o.astype(o_ref.dtype)

    return pl.pallas_call(
        body, grid=(nb, nh, nq, nk), in_specs=in_specs,
        out_specs=pl.BlockSpec((tq, LANES), lambda b, h, qi, ki: (b * nq + qi, h)),
        out_shape=jax.ShapeDtypeStruct((nb * tq_total, nh * LANES), out_dtype),
        scratch_shapes=[pltpu.VMEM((tq, 1), F32), pltpu.VMEM((tq, 1), F32), pltpu.VMEM((tq, LANES), F32)],
        compiler_params=_cp("parallel", "parallel", "parallel", "arbitrary"))(*args)


def _head_sum_matrices(n_heads, hd):
    lane = jnp.arange(n_heads * hd)[:, None] // hd
    e = (lane == jnp.arange(LANES)[None, :]).astype(F32)
    return e, e.T


def _rwkv_prep(cols, shift_prev, prm, *, nb, t, row0):
    tt = min(256, t)
    assert t % tt == 0 and row0 % tt == 0
    nt = t // tt
    rb0 = row0 // tt
    w = RWKV_W

    def body(c_ref, sp_ref, mu_ref, w0_ref, a0_ref, kk_ref, ka_ref, w2_ref, a2_ref, g2_ref, e_ref, et_ref,
             r_o, k_o, v_o, kk_o, b_o, lw_o, g_o, sh_o, carry):
        ti = pl.program_id(1)

        @pl.when(ti == 0)
        def _():
            carry[...] = sp_ref[0]

        c = c_ref[...]
        row = lax.broadcasted_iota(jnp.int32, (tt, 1), 0)
        prev = jnp.where(row == 0, carry[...], pltpu.roll(c, 1, 0))
        carry[...] = c[tt - 1:tt, :]
        sh_o[0] = c[tt - 1:tt, :]
        xs = c + (prev - c) * mu_ref[...]
        r, k, v = xs[:, :w], xs[:, w:2 * w], xs[:, 2 * w:3 * w]
        o1 = 3 * w
        w_raw = w0_ref[...] + _dot(jnp.tanh(xs[:, o1:o1 + LANES]), w2_ref[...], HP)
        log_w = -jnp.exp(-_softplus(-w_raw) - 0.5)
        a = _sigmoid(a0_ref[...] + _dot(xs[:, o1 + LANES:o1 + 2 * LANES], a2_ref[...], HP))
        g = _dot(_sigmoid(xs[:, o1 + 2 * LANES:]), g2_ref[...], HP)
        kk = k * kk_ref[...]
        ss = _dot(kk * kk, e_ref[...], HP)
        kk = kk * _dot(lax.rsqrt(ss + 1e-6), et_ref[...], HP)
        r_o[...] = r
        k_o[...] = k * (1.0 + (a - 1.0) * ka_ref[...])
        v_o[...] = v
        kk_o[...] = kk
        b_o[...] = kk * a
        lw_o[...] = log_w
        g_o[...] = g

    e, et = _head_sum_matrices(RWKV_HEADS, RWKV_HD)
    tok = pl.BlockSpec((tt, w), lambda b, ti: (b * nt + ti, 0))
    full = lambda arr: pl.BlockSpec(arr.shape, lambda b, ti: (0,) * arr.ndim)
    params = [prm['mu'], prm['w0'], prm['a0'], prm['k_k'], prm['k_a'], prm['w2'], prm['a2'], prm['g2'], e, et]
    outs = pl.pallas_call(
        body, grid=(nb, nt),
        in_specs=[pl.BlockSpec((tt, RWKV_PCOLS), lambda b, ti: (rb0 + b * nt + ti, 0)),
                  pl.BlockSpec((1, 1, RWKV_PCOLS), lambda b, ti: (b, 0, 0))] + [full(p) for p in params],
        out_specs=[tok] * 7 + [pl.BlockSpec((1, 1, RWKV_PCOLS), lambda b, ti: (b, 0, 0))],
        out_shape=[jax.ShapeDtypeStruct((nb * t, w), F32)] * 7 + [jax.ShapeDtypeStruct((nb, 1, RWKV_PCOLS), F32)],
        scratch_shapes=[pltpu.VMEM((1, RWKV_PCOLS), F32)],
        compiler_params=_cp("parallel", "arbitrary"))(cols, shift_prev, *params)
    return outs


def _cumsum_rows(x, n):
    row = lax.broadcasted_iota(jnp.int32, (n, 1), 0)
    sh = 1
    while sh < n:
        x = x + jnp.where(row >= sh, pltpu.roll(x, sh, 0), 0.0)
        sh *= 2
    return x


def _unit_lower_inverse(nmat, n, prec):
    eye = (lax.broadcasted_iota(jnp.int32, (n, n), 0) == lax.broadcasted_iota(jnp.int32, (n, n), 1)).astype(F32)
    t = eye + nmat
    p = nmat
    steps = int(math.ceil(math.log2(n))) - 1
    for _ in range(steps):
        p = _dot(p, p, prec)
        t = t + _dot(p, t, prec)
    return t


def _rwkv_scan(r, k, v, kk, bvec, logw, g, s0, prm, *, nb, t):
    c = min(SCAN_CHUNK, t)
    assert t % c == 0
    nc = t // c
    hd, nh = RWKV_HD, RWKV_HEADS

    def body(r_ref, k_ref, v_ref, kk_ref, b_ref, lw_ref, g_ref, s0_ref, rk_ref, lg_ref, lb_ref, o_ref, sf_ref, st):
        ci = pl.program_id(1)

        @pl.when(ci == 0)
        def _():
            st[...] = s0_ref[0]

        lw = lw_ref[...]
        lc = _cumsum_rows(lw, c)
        rr, kx, vv, bb = r_ref[...], k_ref[...], v_ref[...], b_ref[...]
        g_inv = jnp.exp(-lc)
        g_end = jnp.exp(lc[c - 1:c, :] - lc)
        at = -kk_ref[...] * jnp.exp(lc - lw)
        rt = rr * jnp.exp(lc)
        bt, kt = bb * g_inv, kx * g_inv
        bg, kg = bb * g_end, kx * g_end
        w_end = jnp.exp(lc[c - 1:c, :])
        ri = lax.broadcasted_iota(jnp.int32, (c, c), 0)
        cj = lax.broadcasted_iota(jnp.int32, (c, c), 1)
        strict, incl = ri > cj, ri >= cj
        gate, rkp, lng, lnb = g_ref[...], rk_ref[...], lg_ref[...], lb_ref[...]
        outs = []
        for h in range(nh):
            sl = slice(h * hd, (h + 1) * hd)
            s_h = st[h]
            lhs = jnp.concatenate([at[:, sl], rt[:, sl]], axis=0)
            rhs = jnp.concatenate([bt[:, sl], kt[:, sl]], axis=0)
            a_all = _dot_nt(lhs, rhs, HP)
            x_all = _dot_nt(lhs, s_h, HP)
            tinv = _unit_lower_inverse(jnp.where(strict, a_all[:c, :c], 0.0), c, HP)
            a_ak = jnp.where(strict, a_all[:c, c:], 0.0)
            v_h = vv[:, sl]
            u = _dot(tinv, x_all[:c] + _dot(a_ak, v_h, HP), HP)
            uv = jnp.concatenate([u, v_h], axis=0)
            a_r = jnp.concatenate([jnp.where(incl, a_all[c:, :c], 0.0), jnp.where(incl, a_all[c:, c:], 0.0)], axis=1)
            y = x_all[c:] + _dot(a_r, uv, HP)
            st[h] = s_h * w_end[:, sl] + _dot_tn(uv, jnp.concatenate([bg[:, sl], kg[:, sl]], axis=0), HP)
            mu = jnp.mean(y, axis=-1, keepdims=True)
            yc = y - mu
            var = jnp.mean(yc * yc, axis=-1, keepdims=True)
            yn = yc * lax.rsqrt(var + RWKV_GN_EPS) * lng[:, sl] + lnb[:, sl]
            bonus = jnp.sum(rr[:, sl] * kx[:, sl] * rkp[:, sl], axis=-1, keepdims=True) * v_h
            outs.append((yn + bonus) * gate[:, sl])
        o_ref[...] = jnp.concatenate(outs, axis=1).astype(o_ref.dtype)

        @pl.when(ci == nc - 1)
        def _():
            sf_ref[0] = st[...]

    tok = pl.BlockSpec((c, RWKV_W), lambda b, ci: (b * nc + ci, 0))
    stt = pl.BlockSpec((1, nh, hd, hd), lambda b, ci: (b, 0, 0, 0))
    par = pl.BlockSpec((1, RWKV_W), lambda b, ci: (0, 0))
    return pl.pallas_call(
        body, grid=(nb, nc),
        in_specs=[tok] * 7 + [stt, par, par, par],
        out_specs=[tok, stt],
        out_shape=[jax.ShapeDtypeStruct((nb * t, RWKV_W), BF16), jax.ShapeDtypeStruct((nb, nh, hd, hd), F32)],
        scratch_shapes=[pltpu.VMEM((nh, hd, hd), F32)],
        compiler_params=_cp("parallel", "arbitrary"))(r, k, v, kk, bvec, logw, g, s0, prm['r_k'], prm['ln_g'], prm['ln_b'])


class _Geom:
    def __init__(self, bp, tp, bs, ts, past):
        self.bp, self.tp, self.bs, self.ts, self.past = bp, tp, bs, ts, past
        self.mp, self.ms = bp * tp, bs * ts
        self.m = self.mp + self.ms
        self.tm = math.gcd(self.m, 512)
        assert self.tm % 16 == 0 and tp % CHUNK == 0 and past % CHUNK == 0 and ts == CHUNK
        self.tq_p = min(512, tp)
        self.tk_s = 384
        self.kpad = -(-(past + ts) // self.tk_s) * self.tk_s


def _pad_cols(w, n):
    return jnp.pad(w, ((0, 0), (0, n - w.shape[1])))


def _rope_table(geo):
    half = MLA_ROPE // 2
    pos = jnp.concatenate([jnp.tile(jnp.arange(geo.tp, dtype=jnp.int32), geo.bp),
                           geo.past + jnp.tile(jnp.arange(geo.ts, dtype=jnp.int32), geo.bs)])
    inv_freq = ROPE_THETA ** (-jnp.arange(half, dtype=F32) / half)
    ang = pos.astype(F32)[:, None] * inv_freq[None, :]
    cos, sin = jnp.cos(ang), jnp.sin(ang)
    z = jnp.zeros_like(cos)
    zz = jnp.zeros((pos.shape[0], LANES - MLA_ROPE), F32)
    return jnp.concatenate([cos, cos, zz, -sin, z, zz, z, sin, zz], axis=1)


def _rwkv_cols_layout(a):
    o1 = 3 * RWKV_W
    o2 = o1 + RWKV_DECAY_LORA
    o3 = o2 + RWKV_A_LORA
    pad = [(0, 0)] * (a.ndim - 1)
    return jnp.concatenate([a[..., :o1],
                            jnp.pad(a[..., o1:o2], pad + [(0, LANES - RWKV_DECAY_LORA)]),
                            jnp.pad(a[..., o2:o3], pad + [(0, LANES - RWKV_A_LORA)]),
                            a[..., o3:]], axis=-1)


def _rwkv_cols_unlayout(a):
    o1 = 3 * RWKV_W
    return jnp.concatenate([a[..., :o1], a[..., o1:o1 + RWKV_DECAY_LORA],
                            a[..., o1 + LANES:o1 + LANES + RWKV_A_LORA], a[..., o1 + 2 * LANES:]], axis=-1)


def _even_mixer(geo, xb, rope_tab, w_in, kv_norm, w_uk, w_uv, rw, cache_ckv, cache_kr, shift_prev, wkv0):
    tm, m, mp = geo.tm, geo.m, geo.mp
    d = w_in.shape[0]
    nq = MLA_HEADS * MLA_QK
    wq = w_in[:, :nq].reshape(d, MLA_HEADS, MLA_QK)
    w_qn = wq[:, :, :MLA_NOPE].reshape(d, MLA_HEADS * MLA_NOPE).astype(BF16)
    w_qr = jnp.pad(wq[:, :, MLA_NOPE:], ((0, 0), (0, 0), (0, LANES - MLA_ROPE))).reshape(d, MLA_HEADS * LANES).astype(BF16)
    w_ckv = w_in[:, nq:nq + MLA_KV_LORA].astype(BF16)
    w_kr = _pad_cols(w_in[:, nq + MLA_KV_LORA:nq + MLA_KV_LORA + MLA_ROPE], LANES).astype(BF16)
    w_rw = _rwkv_cols_layout(w_in[:, nq + MLA_KV_LORA + MLA_ROPE:]).astype(BF16)
    scale = MLA_QK ** -0.5

    (qn,) = _mm([(xb, w_qn)], tm=tm, tn=512, epilogue=lambda acc: (acc * scale,), outs=[(512, BF16)])
    (qr,) = _mm([(xb, w_qr)], tm=tm, tn=MLA_HEADS * LANES, aux=[(rope_tab, 'row')],
                epilogue=lambda acc, tab: (_rope_lanes(acc, tab) * scale,), outs=[(MLA_HEADS * LANES, BF16)])
    ckv, ckv_b = _mm([(xb, w_ckv)], tm=tm, tn=MLA_KV_LORA, aux=[(kv_norm.reshape(1, -1), 'col')],
                     epilogue=_ep_rms, outs=[(MLA_KV_LORA, F32), (MLA_KV_LORA, BF16)])
    kr, kr_b = _mm([(xb, w_kr)], tm=tm, tn=LANES, aux=[(rope_tab, 'row')],
                   epilogue=lambda acc, tab: (_rope_lanes(acc, tab),) * 2, outs=[(LANES, F32), (LANES, BF16)])
    (rwc,) = _mm([(xb, w_rw)], tm=tm, tn=512, epilogue=lambda acc: (acc,), outs=[(512, F32)])

    w_kv = jnp.concatenate([w_uk.reshape(MLA_KV_LORA, -1), w_uv.reshape(MLA_KV_LORA, -1)], axis=1).astype(BF16)
    nkv = w_kv.shape[1]
    (kv_new,) = _mm([(ckv_b, w_kv)], tm=tm, tn=512, epilogue=lambda acc: (acc,), outs=[(512, BF16)])
    o_p = _flash([qn, qr], [(kv_new, 0, True), (kr_b, 0, False)], (kv_new, MLA_HEADS),
                 nb=geo.bp, tq_total=geo.tp, tk_total=geo.tp, q_row0=0, q_off=0, tq=geo.tq_p, tk=geo.tq_p,
                 chunked=True, nh=MLA_HEADS)
    padk = geo.kpad - geo.past - geo.ts
    ckv_all = jnp.concatenate([cache_ckv.astype(BF16), ckv_b[mp:].reshape(geo.bs, geo.ts, -1),
                               jnp.zeros((geo.bs, padk, MLA_KV_LORA), BF16)], axis=1).reshape(geo.bs * geo.kpad, -1)
    kr_all = jnp.concatenate([jnp.pad(cache_kr, ((0, 0), (0, 0), (0, LANES - MLA_ROPE))).astype(BF16),
                              kr_b[mp:].reshape(geo.bs, geo.ts, -1),
                              jnp.zeros((geo.bs, padk, LANES), BF16)], axis=1).reshape(geo.bs * geo.kpad, -1)
    (kv_all,) = _mm([(ckv_all, w_kv)], tm=geo.tk_s, tn=512, epilogue=lambda acc: (acc,), outs=[(512, BF16)])
    o_s = _flash([qn, qr], [(kv_all, 0, True), (kr_all, 0, False)], (kv_all, MLA_HEADS),
                 nb=geo.bs, tq_total=geo.ts, tk_total=geo.kpad, q_row0=mp, q_off=geo.past, tq=geo.ts, tk=geo.tk_s,
                 chunked=True, nh=MLA_HEADS)
    o_mla = jnp.concatenate([o_p, o_s], axis=0)

    outs_p = _rwkv_prep(rwc, jnp.zeros((geo.bp, 1, RWKV_PCOLS), F32), rw, nb=geo.bp, t=geo.tp, row0=0)
    outs_s = _rwkv_prep(rwc, _rwkv_cols_layout(shift_prev)[:, None, :], rw, nb=geo.bs, t=geo.ts, row0=mp)
    y_p, wkv_p = _rwkv_scan(*outs_p[:7], jnp.zeros((geo.bp, RWKV_HEADS, RWKV_HD, RWKV_HD), F32), rw, nb=geo.bp, t=geo.tp)
    y_s, wkv_s = _rwkv_scan(*outs_s[:7], wkv0, rw, nb=geo.bs, t=geo.ts)
    o_rwkv = jnp.concatenate([y_p, y_s], axis=0)
    states = dict(
        ckv_p=ckv[:mp].reshape(geo.bp, geo.tp, -1), ckv_s=ckv[mp:].reshape(geo.bs, geo.ts, -1),
        kr_p=kr[:mp, :MLA_ROPE].reshape(geo.bp, geo.tp, -1), kr_s=kr[mp:, :MLA_ROPE].reshape(geo.bs, geo.ts, -1),
        sh_p=_rwkv_cols_unlayout(outs_p[7][:, 0]), sh_s=_rwkv_cols_unlayout(outs_s[7][:, 0]),
        rwkv_p=wkv_p, rwkv_s=wkv_s)
    return o_mla, o_rwkv, states


def _head_rms(x, g, nh, scale=1.0):
    outs = []
    for h in range(nh):
        seg = x[:, h * LANES:(h + 1) * LANES]
        outs.append(seg * lax.rsqrt(jnp.mean(seg * seg, axis=-1, keepdims=True) + RMS_EPS) * (g * scale))
    return jnp.concatenate(outs, axis=1)


def _head_l2(x, nh, scale=1.0):
    outs = []
    for h in range(nh):
        seg = x[:, h * LANES:(h + 1) * LANES]
        outs.append(seg * (lax.rsqrt(jnp.sum(seg * seg, axis=-1, keepdims=True) + 1e-6) * scale))
    return jnp.concatenate(outs, axis=1)


def _cumsum_time(x):
    nb, t, w = x.shape

    def body(x_ref, o_ref):
        o_ref[0] = _cumsum_rows(x_ref[0], t)

    spec = pl.BlockSpec((1, t, w), lambda b: (b, 0, 0))
    return pl.pallas_call(body, grid=(nb,), in_specs=[spec], out_specs=spec,
                          out_shape=jax.ShapeDtypeStruct(x.shape, F32), compiler_params=_cp("parallel"))(x)


def _gdn_prep(cols, conv_prev, conv_w, *, nb, t, row0):
    tt = min(256, t)
    assert t % tt == 0 and row0 % tt == 0
    nt = t // tt
    rb0 = row0 // tt
    w = GDN_W

    def body(c_ref, cp_ref, cw_ref, q_o, k_o, v_o, carry):
        ti = pl.program_id(1)

        @pl.when(ti == 0)
        def _():
            carry[...] = cp_ref[0]

        c = c_ref[...]
        ext = jnp.concatenate([carry[...], c], axis=0)
        carry[...] = c[tt - 8:tt, :]
        cw = cw_ref[...]
        acc = c * cw[GDN_CONV - 1:GDN_CONV, :]
        for j in range(GDN_CONV - 1):
            back = GDN_CONV - 1 - j
            acc = acc + ext[8 - back:8 - back + tt, :] * cw[j:j + 1, :]
        qkv = _silu(acc)
        q_o[...] = _head_l2(qkv[:, :w], GDN_HEADS, GDN_HD ** -0.5)
        k_o[...] = _head_l2(qkv[:, w:2 * w], GDN_HEADS)
        v_o[...] = qkv[:, 2 * w:]

    tok = pl.BlockSpec((tt, w), lambda b, ti: (b * nt + ti, 0))
    return pl.pallas_call(
        body, grid=(nb, nt),
        in_specs=[pl.BlockSpec((tt, GDN_QKV), lambda b, ti: (rb0 + b * nt + ti, 0)),
                  pl.BlockSpec((1, 8, GDN_QKV), lambda b, ti: (b, 0, 0)),
                  pl.BlockSpec((8, GDN_QKV), lambda b, ti: (0, 0))],
        out_specs=[tok] * 3,
        out_shape=[jax.ShapeDtypeStruct((nb * t, w), F32)] * 3,
        scratch_shapes=[pltpu.VMEM((8, GDN_QKV), F32)],
        compiler_params=_cp("parallel", "arbitrary"))(cols, conv_prev, conv_w)


def _gdn_scan(q, k, v, gb, z, s0, norm_g, *, nb, t, gb_row0, z_row0):
    c = min(SCAN_CHUNK, t)
    assert t % c == 0 and gb_row0 % c == 0 and z_row0 % c == 0
    nc = t // c
    nh, hd = GDN_HEADS, GDN_HD

    def body(q_ref, k_ref, v_ref, gb_ref, z_ref, s0_ref, ng_ref, o_ref, sf_ref, st):
        ci = pl.program_id(1)

        @pl.when(ci == 0)
        def _():
            st[...] = s0_ref[0]

        gbv = gb_ref[...]
        gc = _cumsum_rows(gbv, c)
        gct = gc.T
        ri = lax.broadcasted_iota(jnp.int32, (c, c), 0)
        cj = lax.broadcasted_iota(jnp.int32, (c, c), 1)
        tril, strict = ri >= cj, ri > cj
        qq, kk, vv, zz, ng = q_ref[...], k_ref[...], v_ref[...], z_ref[...], ng_ref[...]
        outs = []
        for h in range(nh):
            sl = slice(h * hd, (h + 1) * hd)
            gcol = gc[:, h:h + 1]
            grow = gct[h:h + 1, :]
            bcol = gbv[:, nh + h:nh + h + 1]
            decay = jnp.where(tril, jnp.exp(jnp.where(tril, gcol - grow, 0.0)), 0.0)
            k_h, q_h, v_h = kk[:, sl], qq[:, sl], vv[:, sl]
            kb = k_h * bcol
            prods = _dot_nt(jnp.concatenate([kb, q_h], axis=0), k_h, HP)
            low = jnp.where(strict, prods[:c] * decay, 0.0)
            a_qk = jnp.where(tril, prods[c:] * decay, 0.0)
            tinv = _unit_lower_inverse(-low, c, HP)
            uw = _dot(tinv, jnp.concatenate([v_h * bcol, kb * jnp.exp(gcol)], axis=1), HP)
            s_h = st[h]
            v_new = uw[:, :hd] - _dot(uw[:, hd:], s_h, HP)
            o = _dot(q_h * jnp.exp(gcol), s_h, HP) + _dot(a_qk, v_new, HP)
            g_last = gc[c - 1:c, h:h + 1]
            st[h] = s_h * jnp.exp(g_last) + _dot_tn(k_h * jnp.exp(g_last - gcol), v_new, HP)
            o = o * lax.rsqrt(jnp.mean(o * o, axis=-1, keepdims=True) + RMS_EPS) * ng
            outs.append(o * _silu(zz[:, sl]))
        o_ref[...] = jnp.concatenate(outs, axis=1).astype(o_ref.dtype)

        @pl.when(ci == nc - 1)
        def _():
            sf_ref[0] = st[...]

    tok = pl.BlockSpec((c, GDN_W), lambda b, ci: (b * nc + ci, 0))
    stt = pl.BlockSpec((1, nh, hd, hd), lambda b, ci: (b, 0, 0, 0))
    return pl.pallas_call(
        body, grid=(nb, nc),
        in_specs=[tok] * 3 + [pl.BlockSpec((c, LANES), lambda b, ci: (gb_row0 // c + b * nc + ci, 0)),
                              pl.BlockSpec((c, GDN_W), lambda b, ci: (z_row0 // c + b * nc + ci, 0)),
                              stt, pl.BlockSpec((1, hd), lambda b, ci: (0, 0))],
        out_specs=[tok, stt],
        out_shape=[jax.ShapeDtypeStruct((nb * t, GDN_W), BF16), jax.ShapeDtypeStruct((nb, nh, hd, hd), F32)],
        scratch_shapes=[pltpu.VMEM((nh, hd, hd), F32)],
        compiler_params=_cp("parallel", "arbitrary"))(q, k, v, gb, z, s0, norm_g)


def _odd_mixer(geo, xb, w_in, fx, gd, cache_k, cache_v, cache_logf, conv_prev, gdn0):
    tm, m, mp = geo.tm, geo.m, geo.mp
    fw = FOX_W
    o_f = 4 * fw
    o_g = o_f + FOX_HEADS
    w_q, w_k, w_v, w_gate = (w_in[:, i * fw:(i + 1) * fw].astype(BF16) for i in range(4))
    w_f = _pad_cols(w_in[:, o_f:o_g], LANES).astype(BF16)
    w_qkv = w_in[:, o_g:o_g + GDN_QKV].astype(BF16)
    w_ab = _pad_cols(w_in[:, o_g + GDN_QKV:o_g + GDN_QKV + 2 * GDN_HEADS], LANES).astype(BF16)
    w_z = w_in[:, o_g + GDN_QKV + 2 * GDN_HEADS:].astype(BF16)
    qn, kn = fx['q_norm'].reshape(1, -1), fx['k_norm'].reshape(1, -1)
    f_bias = jnp.pad(fx['f_bias'], (0, LANES - FOX_HEADS)).reshape(1, -1)
    scale = FOX_HD ** -0.5

    (q,) = _mm([(xb, w_q)], tm=tm, tn=fw, aux=[(qn, 'row0')],
               epilogue=lambda acc, g: (_head_rms(acc, g, FOX_HEADS, scale),), outs=[(fw, BF16)])
    k, k_b = _mm([(xb, w_k)], tm=tm, tn=fw, aux=[(kn, 'row0')],
                 epilogue=lambda acc, g: (_head_rms(acc, g, FOX_HEADS),) * 2, outs=[(fw, F32), (fw, BF16)])
    v, v_b = _mm([(xb, w_v)], tm=tm, tn=512, epilogue=lambda acc: (acc, acc), outs=[(512, F32), (512, BF16)])
    (gate,) = _mm([(xb, w_gate)], tm=tm, tn=512, epilogue=lambda acc: (_sigmoid(acc),), outs=[(512, F32)])
    (logf,) = _mm([(xb, w_f)], tm=tm, tn=LANES, aux=[(f_bias, 'col')],
                  epilogue=lambda acc, fb: (-_softplus(-(acc + fb)),), outs=[(LANES, F32)])

    cum_p = _cumsum_time(logf[:mp].reshape(geo.bp, geo.tp, LANES))[:, :, :FOX_HEADS]
    bias_p = (jnp.transpose(cum_p, (0, 2, 1))[..., None], jnp.transpose(cum_p, (0, 2, 1))[:, :, None, :])
    o_p = _flash([q], [(k_b, 0, True)], (v_b, 0), nb=geo.bp, tq_total=geo.tp, tk_total=geo.tp, q_row0=0, q_off=0,
                 tq=geo.tq_p, tk=geo.tq_p, chunked=False, nh=FOX_HEADS, bias=bias_p, gate=gate)
    padk = geo.kpad - geo.past - geo.ts
    lf_all = jnp.concatenate([jnp.pad(cache_logf, ((0, 0), (0, 0), (0, LANES - FOX_HEADS))),
                              logf[mp:].reshape(geo.bs, geo.ts, LANES), jnp.zeros((geo.bs, padk, LANES), F32)], axis=1)
    cum_s = jnp.transpose(_cumsum_time(lf_all)[:, :, :FOX_HEADS], (0, 2, 1))
    bias_s = (cum_s[:, :, geo.past:geo.past + geo.ts, None], cum_s[:, :, None, :])
    k_all = jnp.concatenate([cache_k.reshape(geo.bs, geo.past, fw).astype(BF16), k_b[mp:].reshape(geo.bs, geo.ts, fw),
                             jnp.zeros((geo.bs, padk, fw), BF16)], axis=1).reshape(geo.bs * geo.kpad, fw)
    v_all = jnp.concatenate([cache_v.reshape(geo.bs, geo.past, fw).astype(BF16), v_b[mp:].reshape(geo.bs, geo.ts, fw),
                             jnp.zeros((geo.bs, padk, fw), BF16)], axis=1).reshape(geo.bs * geo.kpad, fw)
    o_s = _flash([q], [(k_all, 0, True)], (v_all, 0), nb=geo.bs, tq_total=geo.ts, tk_total=geo.kpad, q_row0=mp,
                 q_off=geo.past, tq=geo.ts, tk=geo.tk_s, chunked=False, nh=FOX_HEADS, bias=bias_s, gate=gate)
    o_fox = jnp.concatenate([o_p, o_s], axis=0)

    (qkv_raw,) = _mm([(xb, w_qkv)], tm=tm, tn=512, epilogue=lambda acc: (acc,), outs=[(512, F32)])
    lane = jnp.arange(LANES)
    neg_a = jnp.where(lane < GDN_HEADS, -jnp.exp(jnp.pad(gd['a_log'], (0, LANES - GDN_HEADS))), 0.0).reshape(1, -1)
    dtb = jnp.pad(gd['dt_bias'], (0, LANES - GDN_HEADS)).reshape(1, -1)
    is_g = (lane < GDN_HEADS).astype(F32).reshape(1, -1)
    (gb,) = _mm([(xb, w_ab)], tm=tm, tn=LANES, aux=[(neg_a, 'col'), (dtb, 'col'), (is_g, 'col')],
                epilogue=lambda acc, na, db, ig: (jnp.where(ig > 0.5, na * _softplus(acc + db), _sigmoid(acc)),),
                outs=[(LANES, F32)])
    (z,) = _mm([(xb, w_z)], tm=tm, tn=512, epilogue=lambda acc: (acc,), outs=[(512, F32)])
    conv_w = jnp.pad(gd['conv_w'], ((0, 8 - GDN_CONV), (0, 0)))
    norm_g = gd['norm'].reshape(1, -1)
    prev8 = lambda a: jnp.pad(a, ((0, 0), (8 - (GDN_CONV - 1), 0), (0, 0)))
    q_p, k_p, v_p = _gdn_prep(qkv_raw, jnp.zeros((geo.bp, 8, GDN_QKV), F32), conv_w, nb=geo.bp, t=geo.tp, row0=0)
    q_s, k_s, v_s = _gdn_prep(qkv_raw, prev8(conv_prev), conv_w, nb=geo.bs, t=geo.ts, row0=mp)
    y_p, s_p = _gdn_scan(q_p, k_p, v_p, gb, z, jnp.zeros((geo.bp, GDN_HEADS, GDN_HD, GDN_HD), F32), norm_g,
                         nb=geo.bp, t=geo.tp, gb_row0=0, z_row0=0)
    y_s, s_s = _gdn_scan(q_s, k_s, v_s, gb, z, gdn0, norm_g, nb=geo.bs, t=geo.ts, gb_row0=mp, z_row0=mp)
    o_gdn = jnp.concatenate([y_p, y_s], axis=0)
    nconv = GDN_CONV - 1
    states = dict(
        fk_p=k[:mp].reshape(geo.bp, geo.tp, FOX_HEADS, FOX_HD), fk_s=k[mp:].reshape(geo.bs, geo.ts, FOX_HEADS, FOX_HD),
        fv_p=v[:mp].reshape(geo.bp, geo.tp, FOX_HEADS, FOX_HD), fv_s=v[mp:].reshape(geo.bs, geo.ts, FOX_HEADS, FOX_HD),
        fl_p=logf[:mp, :FOX_HEADS].reshape(geo.bp, geo.tp, FOX_HEADS),
        fl_s=logf[mp:, :FOX_HEADS].reshape(geo.bs, geo.ts, FOX_HEADS),
        cv_p=qkv_raw[:mp].reshape(geo.bp, geo.tp, GDN_QKV)[:, geo.tp - nconv:],
        cv_s=qkv_raw[mp:].reshape(geo.bs, geo.ts, GDN_QKV)[:, geo.ts - nconv:],
        gdn_p=s_p, gdn_s=s_s)
    return o_fox, o_gdn, states


MOE_BLOCK = 256
MOE_COMBINE_ROWS = 128


def _moe_route(x, router, router_bias, *, tm):
    m = x.shape[0]
    assert m % tm == 0
    per_group = N_EXPERTS // N_GROUPS

    def lane_max(v):
        return jnp.max(v, axis=-1, keepdims=True)

    def first_lane(mask, lane):
        return jnp.min(jnp.where(mask, lane.astype(F32), float(LANES)), axis=-1, keepdims=True).astype(jnp.int32)

    def group_all(v, lane, op):
        sh = 1
        while sh < per_group:
            partner = jnp.where((lane & sh) == 0, pltpu.roll(v, LANES - sh, 1), pltpu.roll(v, sh, 1))
            v = op(v, partner)
            sh *= 2
        return v

    def body(x_ref, r_ref, b_ref, e_o, g_o, k_o, cnt_o, carry):
        i = pl.program_id(0)

        @pl.when(i == 0)
        def _():
            carry[...] = jnp.zeros(carry.shape, F32)

        lane = lax.broadcasted_iota(jnp.int32, (tm, LANES), 1)
        valid = lane < N_EXPERTS
        neg = -jnp.inf
        scores = _sigmoid(_dot(x_ref[...], r_ref[...], HP))
        biased = jnp.where(valid, scores + b_ref[...], neg)
        m1 = group_all(biased, lane, jnp.maximum)
        first = group_all(jnp.where(biased == m1, lane, LANES), lane, jnp.minimum)
        m2 = group_all(jnp.where(lane == first, neg, biased), lane, jnp.maximum)
        grp = jnp.where(valid & ((lane & (per_group - 1)) == 0), m1 + m2, neg)
        emask = jnp.zeros((tm, LANES), jnp.bool_)
        for _ in range(TOPK_GROUPS):
            idx = first_lane(grp == lane_max(grp), lane)
            emask = emask | ((lane - idx >= 0) & (lane - idx < per_group))
            grp = jnp.where(lane == idx, neg, grp)
        cur = jnp.where(emask & valid, biased, neg)
        picks = []
        sel = jnp.zeros((tm, LANES), jnp.bool_)
        for _ in range(TOP_K):
            idx = first_lane(cur == lane_max(cur), lane)
            pick = lane == idx
            picks.append((idx, pick))
            sel = sel | pick
            cur = jnp.where(pick, neg, cur)
        selw = jnp.where(sel, scores, 0.0)
        gates_dense = selw / jnp.sum(selw, axis=-1, keepdims=True) * ROUTED_SCALE
        ri = lax.broadcasted_iota(jnp.int32, (tm, tm), 0)
        cj = lax.broadcasted_iota(jnp.int32, (tm, tm), 1)
        p01 = jnp.where(sel, 1.0, 0.0)
        before = _dot(jnp.where(ri > cj, 1.0, 0.0).astype(BF16), p01.astype(BF16)) + carry[...]
        carry[...] = carry[...] + jnp.sum(p01, axis=0, keepdims=True)
        cnt_o[...] = carry[...]
        e_out = jnp.zeros((tm, LANES), jnp.int32)
        g_out = jnp.zeros((tm, LANES), F32)
        k_out = jnp.zeros((tm, LANES), jnp.int32)
        for j, (idx, pick) in enumerate(picks):
            e_out = jnp.where(lane == j, idx, e_out)
            g_out = jnp.where(lane == j, jnp.sum(jnp.where(pick, gates_dense, 0.0), axis=-1, keepdims=True), g_out)
            rank = jnp.sum(jnp.where(pick, before, 0.0), axis=-1, keepdims=True)
            k_out = jnp.where(lane == j, rank.astype(jnp.int32), k_out)
        e_o[...] = e_out
        g_o[...] = g_out
        k_o[...] = k_out

    tok = pl.BlockSpec((tm, LANES), lambda i: (i, 0))
    one = pl.BlockSpec((1, LANES), lambda i: (0, 0))
    return pl.pallas_call(
        body, grid=(m // tm,),
        in_specs=[pl.BlockSpec((tm, x.shape[1]), lambda i: (i, 0)), pl.BlockSpec(router.shape, lambda i: (0, 0)), one],
        out_specs=[tok, tok, tok, one],
        out_shape=[jax.ShapeDtypeStruct((m, LANES), jnp.int32), jax.ShapeDtypeStruct((m, LANES), F32),
                   jax.ShapeDtypeStruct((m, LANES), jnp.int32), jax.ShapeDtypeStruct((1, LANES), F32)],
        scratch_shapes=[pltpu.VMEM((1, LANES), F32)],
        compiler_params=_cp("arbitrary"))(x, router, router_bias)


def _moe_dispatch(x_words, dest, cap, *, tb):
    m, wd = x_words.shape
    nt = m // tb

    def body(dest_ref, x_ref, init_ref, out_ref, sem):
        del init_ref

        def row_copy(i, j):
            return pltpu.make_async_copy(x_ref.at[pl.ds(i, 1)], out_ref.at[pl.ds(dest_ref[0, 0, i * TOP_K + j], 1)], sem)

        def issue(i, carry):
            for j in range(TOP_K):
                row_copy(i, j).start()
            return carry

        def drain(i, carry):
            for j in range(TOP_K):
                row_copy(i, j).wait()
            return carry

        lax.fori_loop(0, tb, issue, 0)
        lax.fori_loop(0, tb, drain, 0)

    return pl.pallas_call(
        body, grid=(nt,),
        in_specs=[pl.BlockSpec((1, 1, tb * TOP_K), lambda i: (i, 0, 0), memory_space=pltpu.SMEM),
                  pl.BlockSpec((tb, wd), lambda i: (i, 0)),
                  pl.BlockSpec(memory_space=pl.ANY)],
        out_specs=pl.BlockSpec(memory_space=pl.ANY),
        out_shape=jax.ShapeDtypeStruct((cap, wd), x_words.dtype),
        scratch_shapes=[pltpu.SemaphoreType.DMA(())],
        input_output_aliases={2: 0},
        compiler_params=_cp("arbitrary"))(dest, x_words, jnp.zeros((cap, wd), x_words.dtype))


def _moe_experts(xs, blk_exp, n_used, w_gate, w_up, w_down):
    cap, d = xs.shape
    nblk = cap // MOE_BLOCK
    ed = w_gate.shape[2]

    def body(be_ref, nu_ref, x_ref, wg_ref, wu_ref, wd_ref, o_ref):
        @pl.when(pl.program_id(0) < nu_ref[0])
        def _():
            x = x_ref[...]
            h = _silu(_dot(x, wg_ref[0])) * _dot(x, wu_ref[0])
            o_ref[...] = _dot(h.astype(BF16), wd_ref[0])

    blk = lambda i, be, nu: (jnp.minimum(i, nu[0] - 1), 0)
    wmap = lambda i, be, nu: (be[jnp.minimum(i, nu[0] - 1)], 0, 0)
    grid_spec = pltpu.PrefetchScalarGridSpec(
        num_scalar_prefetch=2, grid=(nblk,),
        in_specs=[pl.BlockSpec((MOE_BLOCK, d), blk), pl.BlockSpec((1, d, ed), wmap), pl.BlockSpec((1, d, ed), wmap),
                  pl.BlockSpec((1, ed, d), wmap)],
        out_specs=pl.BlockSpec((MOE_BLOCK, d), blk))
    return pl.pallas_call(body, grid_spec=grid_spec, out_shape=jax.ShapeDtypeStruct((cap, d), F32),
                          compiler_params=_cp("arbitrary"))(blk_exp, n_used, xs, w_gate, w_up, w_down)


def _moe_combine(ys, dest, gates, *, tc):
    cap, d = ys.shape
    m = gates.shape[0]
    nt = m // tc

    def body(dest_ref, g_ref, y_ref, o_ref, buf, sem):
        def row_copy(i, j):
            return pltpu.make_async_copy(y_ref.at[pl.ds(dest_ref[0, 0, i * TOP_K + j], 1)], buf.at[j, pl.ds(i, 1)], sem)

        def issue(i, carry):
            for j in range(TOP_K):
                row_copy(i, j).start()
            return carry

        def drain(i, carry):
            for j in range(TOP_K):
                row_copy(i, j).wait()
            return carry

        lax.fori_loop(0, tc, issue, 0)
        lax.fori_loop(0, tc, drain, 0)
        g = g_ref[...]
        acc = buf[0] * g[:, 0:1]
        for j in range(1, TOP_K):
            acc = acc + buf[j] * g[:, j:j + 1]
        o_ref[...] = acc

    return pl.pallas_call(
        body, grid=(nt,),
        in_specs=[pl.BlockSpec((1, 1, tc * TOP_K), lambda i: (i, 0, 0), memory_space=pltpu.SMEM),
                  pl.BlockSpec((tc, LANES), lambda i: (i, 0)),
                  pl.BlockSpec(memory_space=pl.ANY)],
        out_specs=pl.BlockSpec((tc, d), lambda i: (i, 0)),
        out_shape=jax.ShapeDtypeStruct((m, d), F32),
        scratch_shapes=[pltpu.VMEM((TOP_K, tc, d), F32), pltpu.SemaphoreType.DMA(())],
        compiler_params=_cp("arbitrary"))(dest, gates, ys)


def _moe_layer(geo, x, xb, router, router_bias, w_gate, w_up, w_down, ws_gate, ws_up, ws_down, ln_g, ln_b):
    m, d = x.shape
    tm = geo.tm
    eidx, gates, rank, counts = _moe_route(x, _pad_cols(router, LANES), jnp.pad(router_bias, (0, LANES - N_EXPERTS)).reshape(1, -1), tm=tm)
    cnt = counts[0, :N_EXPERTS].astype(jnp.int32)
    padded = (cnt + MOE_BLOCK - 1) // MOE_BLOCK * MOE_BLOCK
    pad_end = jnp.cumsum(padded)
    pad_start = pad_end - padded
    n_blocks = -(-(m * TOP_K) // MOE_BLOCK) + N_EXPERTS
    cap = n_blocks * MOE_BLOCK
    blk_exp = jnp.minimum(jnp.searchsorted(pad_end, jnp.arange(n_blocks, dtype=jnp.int32) * MOE_BLOCK, side='right'),
                          N_EXPERTS - 1).astype(jnp.int32)
    n_used = (pad_end[-1:] // MOE_BLOCK).astype(jnp.int32)
    dest = pad_start[eidx[:, :TOP_K]] + rank[:, :TOP_K]

    tb = tm
    x_words = lax.bitcast_convert_type(xb.reshape(m, d // 2, 2), jnp.uint32)
    xs_words = _moe_dispatch(x_words, dest.reshape(m // tb, 1, tb * TOP_K), cap, tb=tb)
    xs = lax.bitcast_convert_type(xs_words, BF16).reshape(cap, d)
    ys = _moe_experts(xs, blk_exp, n_used, w_gate.astype(BF16), w_up.astype(BF16), w_down.astype(BF16))
    tc = min(MOE_COMBINE_ROWS, tm)
    routed = _moe_combine(ys, dest.reshape(m // tc, 1, tc * TOP_K), gates, tc=tc)

    sd = ws_gate.shape[1]
    w_sh = jnp.concatenate([ws_gate, ws_up], axis=1).astype(BF16)
    (hs,) = _mm([(xb, w_sh)], tm=tm, tn=2 * sd, epilogue=lambda acc: (_silu(acc[:, :sd]) * acc[:, sd:],), outs=[(sd, BF16)])
    return _mm([(hs, ws_down.astype(BF16))], tm=tm // 2, tn=d,
               aux=[(x, 'rowcol'), (routed, 'rowcol'), (ln_g.reshape(1, -1), 'col'), (ln_b.reshape(1, -1), 'col')],
               epilogue=lambda acc, xr, rt, g, b: _ep_ln(acc + rt, xr, g, b), outs=[(d, F32), (d, BF16)])


def _rwkv_params(li, mu, w0, w2, a0, a2, g2, k_k, k_a, r_k, ln_g, ln_b):
    row = lambda a: a[li].reshape(1, -1)
    padr = lambda a, n: jnp.pad(a, ((0, n - a.shape[0]), (0, 0)))
    return dict(mu=_rwkv_cols_layout(mu[li])[None, :], w0=row(w0), a0=row(a0), k_k=row(k_k), k_a=row(k_a),
                w2=padr(w2[li], LANES), a2=padr(a2[li], LANES), g2=g2[li], r_k=row(r_k), ln_g=row(ln_g), ln_b=row(ln_b))


def _mix_out(geo, x, o_a, o_b, w_out, ln_g, ln_b):
    d = x.shape[1]
    ka = o_a.shape[1]
    return _mm([(o_a, w_out[:ka].astype(BF16)), (o_b, w_out[ka:].astype(BF16))], tm=geo.tm // 2, tn=d,
               aux=[(x, 'rowcol'), (ln_g.reshape(1, -1), 'col'), (ln_b.reshape(1, -1), 'col')],
               epilogue=_ep_ln, outs=[(d, F32), (d, BF16)])


def kernel(x_prompt, x_sample, cache_mla_ckv, cache_mla_krope, state_rwkv_shift, state_rwkv_wkv, cache_fox_k, cache_fox_v, cache_fox_logf, state_gdn_conv, state_gdn_wkv, ln1_g, ln1_b, ln2_g, ln2_b, ev_w_in, ev_w_out, mla_kv_norm, mla_w_uk, mla_w_uv, rwkv_mu, rwkv_w0, rwkv_w2, rwkv_a0, rwkv_a2, rwkv_g2, rwkv_k_k, rwkv_k_a, rwkv_r_k, rwkv_ln_g, rwkv_ln_b, od_w_in, od_w_out, fox_q_norm, fox_k_norm, fox_f_bias, gdn_conv_w, gdn_a_log, gdn_dt_bias, gdn_norm, moe_router, moe_router_bias, moe_w_gate, moe_w_up, moe_w_down, moe_ws_gate, moe_ws_up, moe_ws_down):
    bp, tp, d = x_prompt.shape
    bs, ts, _ = x_sample.shape
    geo = _Geom(bp, tp, bs, ts, cache_mla_ckv.shape[2])
    x = jnp.concatenate([x_prompt.reshape(bp * tp, d), x_sample.reshape(bs * ts, d)], axis=0)
    xb = x.astype(BF16)
    rope_tab = _rope_table(geo)
    st = {}
    for layer in range(ln1_g.shape[0]):
        li = layer // 2
        if layer % 2 == 0:
            rw = _rwkv_params(li, rwkv_mu, rwkv_w0, rwkv_w2, rwkv_a0, rwkv_a2, rwkv_g2, rwkv_k_k, rwkv_k_a, rwkv_r_k,
                              rwkv_ln_g, rwkv_ln_b)
            o_a, o_b, new = _even_mixer(geo, xb, rope_tab, ev_w_in[li], mla_kv_norm[li], mla_w_uk[li], mla_w_uv[li], rw,
                                        cache_mla_ckv[li], cache_mla_krope[li], state_rwkv_shift[li], state_rwkv_wkv[li])
            w_out = ev_w_out[li]
        else:
            fx = {'q_norm': fox_q_norm[li], 'k_norm': fox_k_norm[li], 'f_bias': fox_f_bias[li]}
            gd = {'conv_w': gdn_conv_w[li], 'a_log': gdn_a_log[li], 'dt_bias': gdn_dt_bias[li], 'norm': gdn_norm[li]}
            o_a, o_b, new = _odd_mixer(geo, xb, od_w_in[li], fx, gd, cache_fox_k[li], cache_fox_v[li], cache_fox_logf[li],
                                       state_gdn_conv[li], state_gdn_wkv[li])
            w_out = od_w_out[li]
        for name, val in new.items():
            st.setdefault(name, []).append(val)
        x, xb = _mix_out(geo, x, o_a, o_b, w_out, ln1_g[layer], ln1_b[layer])
        x, xb = _moe_layer(geo, x, xb, moe_router[layer], moe_router_bias[layer], moe_w_gate[layer], moe_w_up[layer],
                           moe_w_down[layer], moe_ws_gate[layer], moe_ws_up[layer], moe_ws_down[layer],
                           ln2_g[layer], ln2_b[layer])
    names = ('ckv', 'kr', 'sh', 'rwkv', 'fk', 'fv', 'fl', 'cv', 'gdn')
    return ((x[:geo.mp].reshape(bp, tp, d), x[geo.mp:].reshape(bs, ts, d))
            + tuple(jnp.stack(st[n + '_p']) for n in names) + tuple(jnp.stack(st[n + '_s']) for n in names))
```

```python
import functools
import math

import jax
import jax.numpy as jnp
from jax import lax
from jax.experimental import pallas as pl
from jax.experimental.pallas import tpu as pltpu

F32 = jnp.float32
BF16 = jnp.bfloat16
HP = lax.Precision.HIGHEST

D_MODEL = 2048
DEPTH = 2
DN_ALPHA = float((2.0 * DEPTH) ** 0.25)
LN_EPS = 1e-5
RMS_EPS = 1e-6
CHUNK = 64
LANES = 128

MLA_HEADS, MLA_NOPE, MLA_ROPE, MLA_VDIM, MLA_KV_LORA = 8, 128, 64, 128, 512
MLA_QK = MLA_NOPE + MLA_ROPE
ROPE_THETA = 10000.0
RWKV_HEADS, RWKV_HD = 16, 64
RWKV_W = RWKV_HEADS * RWKV_HD
RWKV_DECAY_LORA, RWKV_A_LORA, RWKV_G_LORA = 96, 96, 256
RWKV_GN_EPS = 64e-5
RWKV_PCOLS = 3 * RWKV_W + 2 * LANES + RWKV_G_LORA
FOX_HEADS, FOX_HD = 8, 128
FOX_W = FOX_HEADS * FOX_HD
GDN_HEADS, GDN_HD, GDN_CONV = 8, 128, 4
GDN_W = GDN_HEADS * GDN_HD
GDN_QKV = 3 * GDN_W
N_EXPERTS, TOP_K, N_GROUPS, TOPK_GROUPS = 64, 6, 8, 4
EXPERT_DIM, SHARED_DIM = 512, 512
ROUTED_SCALE = 2.5
SCAN_CHUNK = 64
NEG_BIG = -1e30
RWKV_PASSES = dict(a=1, x=3, inv=1, u=1, y=1, s=3)
GDN_PASSES = dict(a=1, x=1, inv=1, u=1, y=1, s=1)

VMEM_LIMIT_BYTES = 56 * 1024 * 1024


def _cp(*sem):
    return pltpu.CompilerParams(dimension_semantics=("arbitrary",) * len(sem), vmem_limit_bytes=VMEM_LIMIT_BYTES)


def _dot(a, b, prec=None):
    return jnp.dot(a, b, precision=prec, preferred_element_type=F32)


def _dot_nt(a, b, prec=None):
    return lax.dot_general(a, b, (((1,), (1,)), ((), ())), precision=prec, preferred_element_type=F32)


def _dot_tn(a, b, prec=None):
    return lax.dot_general(a, b, (((0,), (0,)), ((), ())), precision=prec, preferred_element_type=F32)


def _sigmoid(x):
    return 1.0 / (1.0 + jnp.exp(-x))


def _softplus(x):
    return jnp.maximum(x, 0.0) + jnp.log(1.0 + jnp.exp(-jnp.abs(x)))


def _silu(x):
    return x * _sigmoid(x)


def _mm(pairs, *, tm, tn, epilogue, outs, aux=(), name="mm"):
    m, n = pairs[0][0].shape[0], pairs[0][1].shape[1]
    assert m % tm == 0 and n % tn == 0, (m, n, tm, tn)
    nj, ni = n // tn, m // tm
    in_specs, args = [], []
    for a, w in pairs:
        k = a.shape[1]
        in_specs += [pl.BlockSpec((tm, k), lambda j, i: (i, 0)), pl.BlockSpec((k, tn), lambda j, i: (0, j))]
        args += [a, w]
    for arr, kind in aux:
        if kind == 'row':
            in_specs.append(pl.BlockSpec((tm, arr.shape[1]), lambda j, i: (i, 0)))
        elif kind == 'rowcol':
            in_specs.append(pl.BlockSpec((tm, tn), lambda j, i: (i, j)))
        elif kind == 'row0':
            in_specs.append(pl.BlockSpec(arr.shape, lambda j, i: (0, 0)))
        else:
            in_specs.append(pl.BlockSpec((1, tn), lambda j, i: (0, j)))
        args.append(arr)
    out_shape = [jax.ShapeDtypeStruct((m, ow * nj), dt) for ow, dt in outs]
    out_specs = [pl.BlockSpec((tm, ow), lambda j, i: (i, j)) for ow, dt in outs]
    n_pairs, n_aux = len(pairs), len(aux)

    def body(*refs):
        acc = None
        for p in range(n_pairs):
            d = _dot(refs[2 * p][...], refs[2 * p + 1][...])
            acc = d if acc is None else acc + d
        res = epilogue(acc, *[r[...] for r in refs[2 * n_pairs:2 * n_pairs + n_aux]])
        for o_ref, val in zip(refs[2 * n_pairs + n_aux:], res):
            o_ref[...] = val.astype(o_ref.dtype)

    res = pl.pallas_call(body, grid=(nj, ni), in_specs=in_specs, out_specs=out_specs, out_shape=out_shape, name=name,
                         compiler_params=_cp("parallel", "parallel"))(*args)
    return res


def _rope_lanes(x, tab):
    c, s1, s2 = tab[:, :LANES], tab[:, LANES:2 * LANES], tab[:, 2 * LANES:]
    n = x.shape[1]
    lo = pltpu.roll(x, n - MLA_ROPE // 2, 1)
    hi = pltpu.roll(x, MLA_ROPE // 2, 1)
    if n > LANES:
        reps = n // LANES
        c, s1, s2 = (jnp.concatenate([t] * reps, axis=1) for t in (c, s1, s2))
    return x * c + lo * s1 + hi * s2


def _ep_rms(acc, g):
    y = acc * lax.rsqrt(jnp.mean(acc * acc, axis=-1, keepdims=True) + RMS_EPS) * g
    return y, y


def _ep_ln(acc, x, g, b):
    z = DN_ALPHA * x + acc
    mu = jnp.mean(z, axis=-1, keepdims=True)
    zc = z - mu
    var = jnp.mean(zc * zc, axis=-1, keepdims=True)
    y = zc * lax.rsqrt(var + LN_EPS) * g + b
    return y, y


def _flash(qs, ks, v, *, nb, tq_total, tk_total, q_row0, q_off, tq, tk, chunked, nh, bias=None, gate=None,
           out_dtype=BF16, name="flash"):
    assert tq_total % tq == 0 and tk_total % tk == 0 and q_row0 % tq == 0
    nq, nk = tq_total // tq, tk_total // tk
    qrb0 = q_row0 // tq
    n_q, n_k = len(qs), len(ks)
    has_bias, has_gate = bias is not None, gate is not None

    def last_k(qi):
        return jnp.minimum(nk - 1, (q_off + qi * tq + tq - 1) // tk)

    q_map = lambda b, h, qi, ki: (qrb0 + b * nq + qi, h)
    in_specs, args = [], []
    for q in qs:
        in_specs.append(pl.BlockSpec((tq, LANES), q_map))
        args.append(q)
    for arr, c0, per_head in ks:
        if per_head:
            in_specs.append(pl.BlockSpec((tk, LANES), lambda b, h, qi, ki, c0=c0: (b * nk + jnp.minimum(ki, last_k(qi)), c0 + h)))
        else:
            in_specs.append(pl.BlockSpec((tk, LANES), lambda b, h, qi, ki, c0=c0: (b * nk + jnp.minimum(ki, last_k(qi)), c0)))
        args.append(arr)
    v_arr, v_c0 = v
    in_specs.append(pl.BlockSpec((tk, LANES), lambda b, h, qi, ki: (b * nk + jnp.minimum(ki, last_k(qi)), v_c0 + h)))
    args.append(v_arr)
    if has_bias:
        in_specs.append(pl.BlockSpec((1, 1, tq, 1), lambda b, h, qi, ki: (b, h, qi, 0)))
        in_specs.append(pl.BlockSpec((1, 1, 1, tk), lambda b, h, qi, ki: (b, h, 0, jnp.minimum(ki, last_k(qi)))))
        args += list(bias)
    if has_gate:
        in_specs.append(pl.BlockSpec((tq, LANES), q_map))
        args.append(gate)

    def body(*refs):
        q_refs = refs[:n_q]
        k_refs = refs[n_q:n_q + n_k]
        v_ref = refs[n_q + n_k]
        pos = n_q + n_k + 1
        if has_bias:
            qb_ref, kb_ref = refs[pos], refs[pos + 1]
            pos += 2
        if has_gate:
            gate_ref = refs[pos]
            pos += 1
        o_ref, m_ref, l_ref, acc_ref = refs[pos:pos + 4]
        qi, ki = pl.program_id(2), pl.program_id(3)

        @pl.when(ki == 0)
        def _():
            m_ref[...] = jnp.full(m_ref.shape, NEG_BIG, F32)
            l_ref[...] = jnp.zeros(l_ref.shape, F32)
            acc_ref[...] = jnp.zeros(acc_ref.shape, F32)

        @pl.when(ki <= last_k(qi))
        def _():
            s = None
            for q_ref, k_ref in zip(q_refs, k_refs):
                d = _dot_nt(q_ref[...], k_ref[...])
                s = d if s is None else s + d
            if has_bias:
                s = s + (qb_ref[0, 0] - kb_ref[0, 0])
            qpos = q_off + qi * tq + lax.broadcasted_iota(jnp.int32, (tq, 1), 0)
            qlim = (qpos | (CHUNK - 1)) if chunked else qpos
            kpos = ki * tk + lax.broadcasted_iota(jnp.int32, (1, tk), 1)
            s = jnp.where(kpos <= qlim, s, NEG_BIG)
            m_old = m_ref[...]
            m_new = jnp.maximum(m_old, jnp.max(s, axis=-1, keepdims=True))
            p = jnp.exp(s - m_new)
            alpha = jnp.exp(m_old - m_new)
            l_ref[...] = alpha * l_ref[...] + jnp.sum(p, axis=-1, keepdims=True)
            acc_ref[...] = alpha * acc_ref[...] + _dot(p.astype(BF16), v_ref[...])
            m_ref[...] = m_new

        @pl.when(ki == nk - 1)
        def _():
            o = acc_ref[...] / l_ref[...]
            if has_gate:
                o = o * gate_ref[...]
            o_ref[...] = o.astype(o_ref.dtype)

    return pl.pallas_call(
        body, grid=(nb, nh, nq, nk), in_specs=in_specs, name=name,
        out_specs=pl.BlockSpec((tq, LANES), lambda b, h, qi, ki: (b * nq + qi, h)),
        out_shape=jax.ShapeDtypeStruct((nb * tq_total, nh * LANES), out_dtype),
        scratch_shapes=[pltpu.VMEM((tq, 1), F32), pltpu.VMEM((tq, 1), F32), pltpu.VMEM((tq, LANES), F32)],
        compiler_params=_cp("parallel", "parallel", "parallel", "arbitrary"))(*args)


def _head_sum_matrices(n_heads, hd):
    lane = jnp.arange(n_heads * hd)[:, None] // hd
    e = (lane == jnp.arange(LANES)[None, :]).astype(F32)
    return e, e.T


def _rwkv_prep(cols, shift_prev, prm, *, nb, t, row0):
    tt = min(256, t)
    assert t % tt == 0 and row0 % tt == 0
    nt = t // tt
    rb0 = row0 // tt
    w = RWKV_W

    def body(c_ref, sp_ref, mu_ref, w0_ref, a0_ref, kk_ref, ka_ref, w2_ref, a2_ref, g2_ref, e_ref, et_ref,
             r_o, k_o, v_o, kk_o, b_o, lw_o, g_o, sh_o, carry):
        ti = pl.program_id(1)

        @pl.when(ti == 0)
        def _():
            carry[...] = sp_ref[0]

        c = c_ref[...]
        row = lax.broadcasted_iota(jnp.int32, (tt, 1), 0)
        prev = jnp.where(row == 0, carry[...], pltpu.roll(c, 1, 0))
        carry[...] = c[tt - 1:tt, :]
        sh_o[0] = c[tt - 1:tt, :]
        xs = c + (prev - c) * mu_ref[...]
        r, k, v = xs[:, :w], xs[:, w:2 * w], xs[:, 2 * w:3 * w]
        o1 = 3 * w
        w_raw = w0_ref[...] + _dot(jnp.tanh(xs[:, o1:o1 + LANES]), w2_ref[...], HP)
        log_w = -jnp.exp(-_softplus(-w_raw) - 0.5)
        a = _sigmoid(a0_ref[...] + _dot(xs[:, o1 + LANES:o1 + 2 * LANES], a2_ref[...], HP))
        g = _dot(_sigmoid(xs[:, o1 + 2 * LANES:]), g2_ref[...], HP)
        kk = k * kk_ref[...]
        ss = _dot(kk * kk, e_ref[...], HP)
        kk = kk * _dot(lax.rsqrt(ss + 1e-6), et_ref[...], HP)
        r_o[...] = r
        k_o[...] = k * (1.0 + (a - 1.0) * ka_ref[...])
        v_o[...] = v
        kk_o[...] = kk
        b_o[...] = kk * a
        lw_o[...] = log_w
        g_o[...] = g

    e, et = _head_sum_matrices(RWKV_HEADS, RWKV_HD)
    tok = pl.BlockSpec((tt, w), lambda b, ti: (b * nt + ti, 0))
    full = lambda arr: pl.BlockSpec(arr.shape, lambda b, ti: (0,) * arr.ndim)
    params = [prm['mu'], prm['w0'], prm['a0'], prm['k_k'], prm['k_a'], prm['w2'], prm['a2'], prm['g2'], e, et]
    outs = pl.pallas_call(
        body, grid=(nb, nt), name="rwkv_prep",
        in_specs=[pl.BlockSpec((tt, RWKV_PCOLS), lambda b, ti: (rb0 + b * nt + ti, 0)),
                  pl.BlockSpec((1, 1, RWKV_PCOLS), lambda b, ti: (b, 0, 0))] + [full(p) for p in params],
        out_specs=[tok] * 7 + [pl.BlockSpec((1, 1, RWKV_PCOLS), lambda b, ti: (b, 0, 0))],
        out_shape=[jax.ShapeDtypeStruct((nb * t, w), F32)] * 7 + [jax.ShapeDtypeStruct((nb, 1, RWKV_PCOLS), F32)],
        scratch_shapes=[pltpu.VMEM((1, RWKV_PCOLS), F32)],
        compiler_params=_cp("parallel", "arbitrary"))(cols, shift_prev, *params)
    return outs


def _cumsum_rows(x, n):
    row = lax.broadcasted_iota(jnp.int32, (n, 1), 0)
    sh = 1
    while sh < n:
        x = x + jnp.where(row >= sh, pltpu.roll(x, sh, 0), 0.0)
        sh *= 2
    return x


def _split_bf16(x):
    hi = x.astype(BF16).astype(F32)
    return hi, x - hi


def _bmm(a, b, ca, cb, passes):
    dn = (((ca,), (cb,)), ((0,), (0,)))
    if passes == 6:
        return lax.dot_general(a, b, dn, precision=HP, preferred_element_type=F32)
    if passes == 3:
        ah, al = _split_bf16(a)
        bh, bl = _split_bf16(b)
        a = jnp.concatenate([ah, ah, al], axis=ca)
        b = jnp.concatenate([bh, bl, bh], axis=cb)
    return lax.dot_general(a.astype(BF16), b.astype(BF16), dn, preferred_element_type=F32)


def _heads(x, nh, hd):
    return jnp.stack([x[:, h * hd:(h + 1) * hd] for h in range(nh)], axis=0)


def _unheads(x):
    return jnp.concatenate([x[h] for h in range(x.shape[0])], axis=1)


def _unit_lower_inverse(nmat, n, passes):
    eye = (lax.broadcasted_iota(jnp.int32, (n, n), 0) == lax.broadcasted_iota(jnp.int32, (n, n), 1)).astype(F32)
    t = eye + nmat
    p = nmat
    steps = int(math.ceil(math.log2(n))) - 1
    for _ in range(steps):
        p = _bmm(p, p, 2, 1, passes)
        t = t + _bmm(p, t, 2, 1, passes)
    return t


def _rwkv_scan(r, k, v, kk, bvec, logw, g, s0, prm, *, nb, t):
    c = min(SCAN_CHUNK, t)
    assert t % c == 0
    nc = t // c
    hd, nh = RWKV_HD, RWKV_HEADS

    def body(r_ref, k_ref, v_ref, kk_ref, b_ref, lw_ref, g_ref, s0_ref, rk_ref, lg_ref, lb_ref, o_ref, sf_ref, st):
        ci = pl.program_id(1)

        @pl.when(ci == 0)
        def _():
            st[...] = s0_ref[0]

        lw = lw_ref[...]
        lc = _cumsum_rows(lw, c)
        rr, kx, vv, bb = r_ref[...], k_ref[...], v_ref[...], b_ref[...]
        g_inv = jnp.exp(-lc)
        g_end = jnp.exp(lc[c - 1:c, :] - lc)
        hs = lambda a: _heads(a, nh, hd)
        at, rt = hs(-kk_ref[...] * jnp.exp(lc - lw)), hs(rr * jnp.exp(lc))
        bt, kt = hs(bb * g_inv), hs(kx * g_inv)
        bg, kg = hs(bb * g_end), hs(kx * g_end)
        w_end = hs(jnp.exp(lc[c - 1:c, :]))
        v_h = hs(vv)
        ri = lax.broadcasted_iota(jnp.int32, (c, c), 0)
        cj = lax.broadcasted_iota(jnp.int32, (c, c), 1)
        strict = ri > cj
        incl2 = lax.broadcasted_iota(jnp.int32, (c, 2 * c), 0) >= (lax.broadcasted_iota(jnp.int32, (c, 2 * c), 1) & (c - 1))
        s_all = st[...]
        pp = RWKV_PASSES
        lhs = jnp.concatenate([at, rt], axis=1)
        a_all = _bmm(lhs, jnp.concatenate([bt, kt], axis=1), 2, 2, pp['a'])
        x_all = _bmm(lhs, s_all, 2, 2, pp['x'])
        tinv = _unit_lower_inverse(jnp.where(strict, a_all[:, :c, :c], 0.0), c, pp['inv'])
        a_ak = jnp.where(strict, a_all[:, :c, c:], 0.0)
        u = _bmm(tinv, x_all[:, :c] + _bmm(a_ak, v_h, 2, 1, pp['u']), 2, 1, pp['u'])
        uv = jnp.concatenate([u, v_h], axis=1)
        y = x_all[:, c:] + _bmm(jnp.where(incl2, a_all[:, c:, :], 0.0), uv, 2, 1, pp['y'])
        st[...] = s_all * w_end + _bmm(uv, jnp.concatenate([bg, kg], axis=1), 1, 1, pp['s'])
        mu = jnp.mean(y, axis=-1, keepdims=True)
        yc = y - mu
        var = jnp.mean(yc * yc, axis=-1, keepdims=True)
        yn = yc * lax.rsqrt(var + RWKV_GN_EPS) * hs(lg_ref[...]) + hs(lb_ref[...])
        bonus = jnp.sum(hs(rr * kx * rk_ref[...]), axis=-1, keepdims=True) * v_h
        o_ref[...] = (_unheads(yn + bonus) * g_ref[...]).astype(o_ref.dtype)

        @pl.when(ci == nc - 1)
        def _():
            sf_ref[0] = st[...]

    tok = pl.BlockSpec((c, RWKV_W), lambda b, ci: (b * nc + ci, 0))
    stt = pl.BlockSpec((1, nh, hd, hd), lambda b, ci: (b, 0, 0, 0))
    par = pl.BlockSpec((1, RWKV_W), lambda b, ci: (0, 0))
    return pl.pallas_call(
        body, grid=(nb, nc), name="rwkv_scan",
        in_specs=[tok] * 7 + [stt, par, par, par],
        out_specs=[tok, stt],
        out_shape=[jax.ShapeDtypeStruct((nb * t, RWKV_W), BF16), jax.ShapeDtypeStruct((nb, nh, hd, hd), F32)],
        scratch_shapes=[pltpu.VMEM((nh, hd, hd), F32)],
        compiler_params=_cp("parallel", "arbitrary"))(r, k, v, kk, bvec, logw, g, s0, prm['r_k'], prm['ln_g'], prm['ln_b'])


class _Geom:
    def __init__(self, bp, tp, bs, ts, past):
        self.bp, self.tp, self.bs, self.ts, self.past = bp, tp, bs, ts, past
        self.mp, self.ms = bp * tp, bs * ts
        self.m = self.mp + self.ms
        self.tm = math.gcd(self.m, 512)
        assert self.tm % 16 == 0 and tp % CHUNK == 0 and past % CHUNK == 0 and ts == CHUNK
        self.tq_p = min(512, tp)
        self.tk_s = 384
        self.kpad = -(-(past + ts) // self.tk_s) * self.tk_s


def _pad_cols(w, n):
    return jnp.pad(w, ((0, 0), (0, n - w.shape[1])))


def _rope_table(geo):
    half = MLA_ROPE // 2
    pos = jnp.concatenate([jnp.tile(jnp.arange(geo.tp, dtype=jnp.int32), geo.bp),
                           geo.past + jnp.tile(jnp.arange(geo.ts, dtype=jnp.int32), geo.bs)])
    inv_freq = ROPE_THETA ** (-jnp.arange(half, dtype=F32) / half)
    ang = pos.astype(F32)[:, None] * inv_freq[None, :]
    cos, sin = jnp.cos(ang), jnp.sin(ang)
    z = jnp.zeros_like(cos)
    zz = jnp.zeros((pos.shape[0], LANES - MLA_ROPE), F32)
    return jnp.concatenate([cos, cos, zz, -sin, z, zz, z, sin, zz], axis=1)


def _rwkv_cols_layout(a):
    o1 = 3 * RWKV_W
    o2 = o1 + RWKV_DECAY_LORA
    o3 = o2 + RWKV_A_LORA
    pad = [(0, 0)] * (a.ndim - 1)
    return jnp.concatenate([a[..., :o1],
                            jnp.pad(a[..., o1:o2], pad + [(0, LANES - RWKV_DECAY_LORA)]),
                            jnp.pad(a[..., o2:o3], pad + [(0, LANES - RWKV_A_LORA)]),
                            a[..., o3:]], axis=-1)


def _rwkv_cols_unlayout(a):
    o1 = 3 * RWKV_W
    return jnp.concatenate([a[..., :o1], a[..., o1:o1 + RWKV_DECAY_LORA],
                            a[..., o1 + LANES:o1 + LANES + RWKV_A_LORA], a[..., o1 + 2 * LANES:]], axis=-1)


def _even_mixer(geo, xb, rope_tab, w_in, kv_norm, w_uk, w_uv, rw, cache_ckv, cache_kr, shift_prev, wkv0):
    tm, m, mp = geo.tm, geo.m, geo.mp
    d = w_in.shape[0]
    nq = MLA_HEADS * MLA_QK
    wq = w_in[:, :nq].reshape(d, MLA_HEADS, MLA_QK)
    w_qn = wq[:, :, :MLA_NOPE].reshape(d, MLA_HEADS * MLA_NOPE).astype(BF16)
    w_qr = jnp.pad(wq[:, :, MLA_NOPE:], ((0, 0), (0, 0), (0, LANES - MLA_ROPE))).reshape(d, MLA_HEADS * LANES).astype(BF16)
    w_ckv = w_in[:, nq:nq + MLA_KV_LORA].astype(BF16)
    w_kr = _pad_cols(w_in[:, nq + MLA_KV_LORA:nq + MLA_KV_LORA + MLA_ROPE], LANES).astype(BF16)
    w_rw = _rwkv_cols_layout(w_in[:, nq + MLA_KV_LORA + MLA_ROPE:]).astype(BF16)
    scale = MLA_QK ** -0.5

    (qn,) = _mm([(xb, w_qn)], tm=tm, tn=512, epilogue=lambda acc: (acc * scale,), outs=[(512, BF16)])
    (qr,) = _mm([(xb, w_qr)], tm=tm, tn=MLA_HEADS * LANES, aux=[(rope_tab, 'row')],
                epilogue=lambda acc, tab: (_rope_lanes(acc, tab) * scale,), outs=[(MLA_HEADS * LANES, BF16)])
    ckv, ckv_b = _mm([(xb, w_ckv)], tm=tm, tn=MLA_KV_LORA, aux=[(kv_norm.reshape(1, -1), 'col')],
                     epilogue=_ep_rms, outs=[(MLA_KV_LORA, F32), (MLA_KV_LORA, BF16)])
    kr, kr_b = _mm([(xb, w_kr)], tm=tm, tn=LANES, aux=[(rope_tab, 'row')],
                   epilogue=lambda acc, tab: (_rope_lanes(acc, tab),) * 2, outs=[(LANES, F32), (LANES, BF16)])
    (rwc,) = _mm([(xb, w_rw)], tm=tm, tn=512, epilogue=lambda acc: (acc,), outs=[(512, F32)])

    w_kv = jnp.concatenate([w_uk.reshape(MLA_KV_LORA, -1), w_uv.reshape(MLA_KV_LORA, -1)], axis=1).astype(BF16)
    nkv = w_kv.shape[1]
    (kv_new,) = _mm([(ckv_b, w_kv)], tm=tm, tn=512, epilogue=lambda acc: (acc,), outs=[(512, BF16)])
    o_p = _flash([qn, qr], [(kv_new, 0, True), (kr_b, 0, False)], (kv_new, MLA_HEADS),
                 nb=geo.bp, tq_total=geo.tp, tk_total=geo.tp, q_row0=0, q_off=0, tq=geo.tq_p, tk=geo.tq_p,
                 chunked=True, nh=MLA_HEADS)
    padk = geo.kpad - geo.past - geo.ts
    ckv_all = jnp.concatenate([cache_ckv.astype(BF16), ckv_b[mp:].reshape(geo.bs, geo.ts, -1),
                               jnp.zeros((geo.bs, padk, MLA_KV_LORA), BF16)], axis=1).reshape(geo.bs * geo.kpad, -1)
    kr_all = jnp.concatenate([jnp.pad(cache_kr, ((0, 0), (0, 0), (0, LANES - MLA_ROPE))).astype(BF16),
                              kr_b[mp:].reshape(geo.bs, geo.ts, -1),
                              jnp.zeros((geo.bs, padk, LANES), BF16)], axis=1).reshape(geo.bs * geo.kpad, -1)
    (kv_all,) = _mm([(ckv_all, w_kv)], tm=geo.tk_s, tn=512, epilogue=lambda acc: (acc,), outs=[(512, BF16)])
    o_s = _flash([qn, qr], [(kv_all, 0, True), (kr_all, 0, False)], (kv_all, MLA_HEADS),
                 nb=geo.bs, tq_total=geo.ts, tk_total=geo.kpad, q_row0=mp, q_off=geo.past, tq=geo.ts, tk=geo.tk_s,
                 chunked=True, nh=MLA_HEADS)
    o_mla = jnp.concatenate([o_p, o_s], axis=0)

    outs_p = _rwkv_prep(rwc, jnp.zeros((geo.bp, 1, RWKV_PCOLS), F32), rw, nb=geo.bp, t=geo.tp, row0=0)
    outs_s = _rwkv_prep(rwc, _rwkv_cols_layout(shift_prev)[:, None, :], rw, nb=geo.bs, t=geo.ts, row0=mp)
    y_p, wkv_p = _rwkv_scan(*outs_p[:7], jnp.zeros((geo.bp, RWKV_HEADS, RWKV_HD, RWKV_HD), F32), rw, nb=geo.bp, t=geo.tp)
    y_s, wkv_s = _rwkv_scan(*outs_s[:7], wkv0, rw, nb=geo.bs, t=geo.ts)
    o_rwkv = jnp.concatenate([y_p, y_s], axis=0)
    states = dict(
        ckv_p=ckv[:mp].reshape(geo.bp, geo.tp, -1), ckv_s=ckv[mp:].reshape(geo.bs, geo.ts, -1),
        kr_p=kr[:mp, :MLA_ROPE].reshape(geo.bp, geo.tp, -1), kr_s=kr[mp:, :MLA_ROPE].reshape(geo.bs, geo.ts, -1),
        sh_p=_rwkv_cols_unlayout(outs_p[7][:, 0]), sh_s=_rwkv_cols_unlayout(outs_s[7][:, 0]),
        rwkv_p=wkv_p, rwkv_s=wkv_s)
    return o_mla, o_rwkv, states


def _head_rms(x, g, nh, scale=1.0):
    outs = []
    for h in range(nh):
        seg = x[:, h * LANES:(h + 1) * LANES]
        outs.append(seg * lax.rsqrt(jnp.mean(seg * seg, axis=-1, keepdims=True) + RMS_EPS) * (g * scale))
    return jnp.concatenate(outs, axis=1)


def _head_l2(x, nh, scale=1.0):
    outs = []
    for h in range(nh):
        seg = x[:, h * LANES:(h + 1) * LANES]
        outs.append(seg * (lax.rsqrt(jnp.sum(seg * seg, axis=-1, keepdims=True) + 1e-6) * scale))
    return jnp.concatenate(outs, axis=1)


def _cumsum_time(x):
    nb, t, w = x.shape

    def body(x_ref, o_ref):
        o_ref[0] = _cumsum_rows(x_ref[0], t)

    spec = pl.BlockSpec((1, t, w), lambda b: (b, 0, 0))
    return pl.pallas_call(body, grid=(nb,), in_specs=[spec], out_specs=spec, name="cumsum_time",
                          out_shape=jax.ShapeDtypeStruct(x.shape, F32), compiler_params=_cp("parallel"))(x)


def _gdn_prep(cols, conv_prev, conv_w, *, nb, t, row0):
    tt = min(256, t)
    assert t % tt == 0 and row0 % tt == 0
    nt = t // tt
    rb0 = row0 // tt
    w = GDN_W

    def body(c_ref, cp_ref, cw_ref, q_o, k_o, v_o, carry):
        ti = pl.program_id(1)

        @pl.when(ti == 0)
        def _():
            carry[...] = cp_ref[0]

        c = c_ref[...]
        ext = jnp.concatenate([carry[...], c], axis=0)
        carry[...] = c[tt - 8:tt, :]
        cw = cw_ref[...]
        acc = c * cw[GDN_CONV - 1:GDN_CONV, :]
        for j in range(GDN_CONV - 1):
            back = GDN_CONV - 1 - j
            acc = acc + ext[8 - back:8 - back + tt, :] * cw[j:j + 1, :]
        qkv = _silu(acc)
        q_o[...] = _head_l2(qkv[:, :w], GDN_HEADS, GDN_HD ** -0.5)
        k_o[...] = _head_l2(qkv[:, w:2 * w], GDN_HEADS)
        v_o[...] = qkv[:, 2 * w:]

    tok = pl.BlockSpec((tt, w), lambda b, ti: (b * nt + ti, 0))
    return pl.pallas_call(
        body, grid=(nb, nt), name="gdn_prep",
        in_specs=[pl.BlockSpec((tt, GDN_QKV), lambda b, ti: (rb0 + b * nt + ti, 0)),
                  pl.BlockSpec((1, 8, GDN_QKV), lambda b, ti: (b, 0, 0)),
                  pl.BlockSpec((8, GDN_QKV), lambda b, ti: (0, 0))],
        out_specs=[tok] * 3,
        out_shape=[jax.ShapeDtypeStruct((nb * t, w), F32)] * 3,
        scratch_shapes=[pltpu.VMEM((8, GDN_QKV), F32)],
        compiler_params=_cp("parallel", "arbitrary"))(cols, conv_prev, conv_w)


def _gdn_scan(q, k, v, gb, z, s0, norm_g, *, nb, t, gb_row0, z_row0):
    c = min(SCAN_CHUNK, t)
    assert t % c == 0 and gb_row0 % c == 0 and z_row0 % c == 0
    nc = t // c
    nh, hd = GDN_HEADS, GDN_HD

    def body(q_ref, k_ref, v_ref, gb_ref, z_ref, s0_ref, ng_ref, o_ref, sf_ref, st):
        ci = pl.program_id(1)

        @pl.when(ci == 0)
        def _():
            st[...] = s0_ref[0]

        gbv = gb_ref[...]
        gc = _cumsum_rows(gbv, c)
        gct = gc.T
        ri = lax.broadcasted_iota(jnp.int32, (c, c), 0)
        cj = lax.broadcasted_iota(jnp.int32, (c, c), 1)
        tril, strict = ri >= cj, ri > cj
        hs = lambda a: _heads(a, nh, hd)
        k_h, q_h, v_h = hs(k_ref[...]), hs(q_ref[...]), hs(v_ref[...])
        gcol = jnp.stack([gc[:, h:h + 1] for h in range(nh)], axis=0)
        grow = gct[:nh][:, None, :]
        bcol = jnp.stack([gbv[:, nh + h:nh + h + 1] for h in range(nh)], axis=0)
        decay = jnp.where(tril, jnp.exp(jnp.where(tril, gcol - grow, 0.0)), 0.0)
        pp = GDN_PASSES
        kb = k_h * bcol
        prods = _bmm(jnp.concatenate([kb, q_h], axis=1), k_h, 2, 2, pp['a'])
        low = jnp.where(strict, prods[:, :c] * decay, 0.0)
        a_qk = jnp.where(tril, prods[:, c:] * decay, 0.0)
        tinv = _unit_lower_inverse(-low, c, pp['inv'])
        e_g = jnp.exp(gcol)
        uw = _bmm(tinv, jnp.concatenate([v_h * bcol, kb * e_g], axis=2), 2, 1, pp['u'])
        s_all = st[...]
        ws_qs = _bmm(jnp.concatenate([uw[:, :, hd:], q_h * e_g], axis=1), s_all, 2, 1, pp['x'])
        v_new = uw[:, :, :hd] - ws_qs[:, :c]
        o = ws_qs[:, c:] + _bmm(a_qk, v_new, 2, 1, pp['y'])
        g_last = gcol[:, c - 1:c, :]
        st[...] = s_all * jnp.exp(g_last) + _bmm(k_h * jnp.exp(g_last - gcol), v_new, 1, 1, pp['s'])
        o = o * lax.rsqrt(jnp.mean(o * o, axis=-1, keepdims=True) + RMS_EPS) * ng_ref[...]
        o_ref[...] = (_unheads(o) * _silu(z_ref[...])).astype(o_ref.dtype)

        @pl.when(ci == nc - 1)
        def _():
            sf_ref[0] = st[...]

    tok = pl.BlockSpec((c, GDN_W), lambda b, ci: (b * nc + ci, 0))
    stt = pl.BlockSpec((1, nh, hd, hd), lambda b, ci: (b, 0, 0, 0))
    return pl.pallas_call(
        body, grid=(nb, nc), name="gdn_scan",
        in_specs=[tok] * 3 + [pl.BlockSpec((c, LANES), lambda b, ci: (gb_row0 // c + b * nc + ci, 0)),
                              pl.BlockSpec((c, GDN_W), lambda b, ci: (z_row0 // c + b * nc + ci, 0)),
                              stt, pl.BlockSpec((1, hd), lambda b, ci: (0, 0))],
        out_specs=[tok, stt],
        out_shape=[jax.ShapeDtypeStruct((nb * t, GDN_W), BF16), jax.ShapeDtypeStruct((nb, nh, hd, hd), F32)],
        scratch_shapes=[pltpu.VMEM((nh, hd, hd), F32)],
        compiler_params=_cp("parallel", "arbitrary"))(q, k, v, gb, z, s0, norm_g)


def _odd_mixer(geo, xb, w_in, fx, gd, cache_k, cache_v, cache_logf, conv_prev, gdn0):
    tm, m, mp = geo.tm, geo.m, geo.mp
    fw = FOX_W
    o_f = 4 * fw
    o_g = o_f + FOX_HEADS
    w_q, w_k, w_v, w_gate = (w_in[:, i * fw:(i + 1) * fw].astype(BF16) for i in range(4))
    w_f = _pad_cols(w_in[:, o_f:o_g], LANES).astype(BF16)
    w_qkv = w_in[:, o_g:o_g + GDN_QKV].astype(BF16)
    w_ab = _pad_cols(w_in[:, o_g + GDN_QKV:o_g + GDN_QKV + 2 * GDN_HEADS], LANES).astype(BF16)
    w_z = w_in[:, o_g + GDN_QKV + 2 * GDN_HEADS:].astype(BF16)
    qn, kn = fx['q_norm'].reshape(1, -1), fx['k_norm'].reshape(1, -1)
    f_bias = jnp.pad(fx['f_bias'], (0, LANES - FOX_HEADS)).reshape(1, -1)
    scale = FOX_HD ** -0.5

    (q,) = _mm([(xb, w_q)], tm=tm, tn=fw, aux=[(qn, 'row0')],
               epilogue=lambda acc, g: (_head_rms(acc, g, FOX_HEADS, scale),), outs=[(fw, BF16)])
    k, k_b = _mm([(xb, w_k)], tm=tm, tn=fw, aux=[(kn, 'row0')],
                 epilogue=lambda acc, g: (_head_rms(acc, g, FOX_HEADS),) * 2, outs=[(fw, F32), (fw, BF16)])
    v, v_b = _mm([(xb, w_v)], tm=tm, tn=512, epilogue=lambda acc: (acc, acc), outs=[(512, F32), (512, BF16)])
    (gate,) = _mm([(xb, w_gate)], tm=tm, tn=512, epilogue=lambda acc: (_sigmoid(acc),), outs=[(512, F32)])
    (logf,) = _mm([(xb, w_f)], tm=tm, tn=LANES, aux=[(f_bias, 'col')],
                  epilogue=lambda acc, fb: (-_softplus(-(acc + fb)),), outs=[(LANES, F32)])

    cum_p = _cumsum_time(logf[:mp].reshape(geo.bp, geo.tp, LANES))[:, :, :FOX_HEADS]
    bias_p = (jnp.transpose(cum_p, (0, 2, 1))[..., None], jnp.transpose(cum_p, (0, 2, 1))[:, :, None, :])
    o_p = _flash([q], [(k_b, 0, True)], (v_b, 0), nb=geo.bp, tq_total=geo.tp, tk_total=geo.tp, q_row0=0, q_off=0,
                 tq=geo.tq_p, tk=geo.tq_p, chunked=False, nh=FOX_HEADS, bias=bias_p, gate=gate)
    padk = geo.kpad - geo.past - geo.ts
    lf_all = jnp.concatenate([jnp.pad(cache_logf, ((0, 0), (0, 0), (0, LANES - FOX_HEADS))),
                              logf[mp:].reshape(geo.bs, geo.ts, LANES), jnp.zeros((geo.bs, padk, LANES), F32)], axis=1)
    cum_s = jnp.transpose(_cumsum_time(lf_all)[:, :, :FOX_HEADS], (0, 2, 1))
    bias_s = (cum_s[:, :, geo.past:geo.past + geo.ts, None], cum_s[:, :, None, :])
    k_all = jnp.concatenate([cache_k.reshape(geo.bs, geo.past, fw).astype(BF16), k_b[mp:].reshape(geo.bs, geo.ts, fw),
                             jnp.zeros((geo.bs, padk, fw), BF16)], axis=1).reshape(geo.bs * geo.kpad, fw)
    v_all = jnp.concatenate([cache_v.reshape(geo.bs, geo.past, fw).astype(BF16), v_b[mp:].reshape(geo.bs, geo.ts, fw),
                             jnp.zeros((geo.bs, padk, fw), BF16)], axis=1).reshape(geo.bs * geo.kpad, fw)
    o_s = _flash([q], [(k_all, 0, True)], (v_all, 0), nb=geo.bs, tq_total=geo.ts, tk_total=geo.kpad, q_row0=mp,
                 q_off=geo.past, tq=geo.ts, tk=geo.tk_s, chunked=False, nh=FOX_HEADS, bias=bias_s, gate=gate)
    o_fox = jnp.concatenate([o_p, o_s], axis=0)

    (qkv_raw,) = _mm([(xb, w_qkv)], tm=tm, tn=512, epilogue=lambda acc: (acc,), outs=[(512, F32)])
    lane = jnp.arange(LANES)
    neg_a = jnp.where(lane < GDN_HEADS, -jnp.exp(jnp.pad(gd['a_log'], (0, LANES - GDN_HEADS))), 0.0).reshape(1, -1)
    dtb = jnp.pad(gd['dt_bias'], (0, LANES - GDN_HEADS)).reshape(1, -1)
    is_g = (lane < GDN_HEADS).astype(F32).reshape(1, -1)
    (gb,) = _mm([(xb, w_ab)], tm=tm, tn=LANES, aux=[(neg_a, 'col'), (dtb, 'col'), (is_g, 'col')],
                epilogue=lambda acc, na, db, ig: (jnp.where(ig > 0.5, na * _softplus(acc + db), _sigmoid(acc)),),
                outs=[(LANES, F32)])
    (z,) = _mm([(xb, w_z)], tm=tm, tn=512, epilogue=lambda acc: (acc,), outs=[(512, F32)])
    conv_w = jnp.pad(gd['conv_w'], ((0, 8 - GDN_CONV), (0, 0)))
    norm_g = gd['norm'].reshape(1, -1)
    prev8 = lambda a: jnp.pad(a, ((0, 0), (8 - (GDN_CONV - 1), 0), (0, 0)))
    q_p, k_p, v_p = _gdn_prep(qkv_raw, jnp.zeros((geo.bp, 8, GDN_QKV), F32), conv_w, nb=geo.bp, t=geo.tp, row0=0)
    q_s, k_s, v_s = _gdn_prep(qkv_raw, prev8(conv_prev), conv_w, nb=geo.bs, t=geo.ts, row0=mp)
    y_p, s_p = _gdn_scan(q_p, k_p, v_p, gb, z, jnp.zeros((geo.bp, GDN_HEADS, GDN_HD, GDN_HD), F32), norm_g,
                         nb=geo.bp, t=geo.tp, gb_row0=0, z_row0=0)
    y_s, s_s = _gdn_scan(q_s, k_s, v_s, gb, z, gdn0, norm_g, nb=geo.bs, t=geo.ts, gb_row0=mp, z_row0=mp)
    o_gdn = jnp.concatenate([y_p, y_s], axis=0)
    nconv = GDN_CONV - 1
    states = dict(
        fk_p=k[:mp].reshape(geo.bp, geo.tp, FOX_HEADS, FOX_HD), fk_s=k[mp:].reshape(geo.bs, geo.ts, FOX_HEADS, FOX_HD),
        fv_p=v[:mp].reshape(geo.bp, geo.tp, FOX_HEADS, FOX_HD), fv_s=v[mp:].reshape(geo.bs, geo.ts, FOX_HEADS, FOX_HD),
        fl_p=logf[:mp, :FOX_HEADS].reshape(geo.bp, geo.tp, FOX_HEADS),
        fl_s=logf[mp:, :FOX_HEADS].reshape(geo.bs, geo.ts, FOX_HEADS),
        cv_p=qkv_raw[:mp].reshape(geo.bp, geo.tp, GDN_QKV)[:, geo.tp - nconv:],
        cv_s=qkv_raw[mp:].reshape(geo.bs, geo.ts, GDN_QKV)[:, geo.ts - nconv:],
        gdn_p=s_p, gdn_s=s_s)
    return o_fox, o_gdn, states


MOE_BLOCK = 256
MOE_COMBINE_ROWS = 128


def _moe_route(x, router, router_bias, *, tm):
    m = x.shape[0]
    assert m % tm == 0
    per_group = N_EXPERTS // N_GROUPS

    def lane_max(v):
        return jnp.max(v, axis=-1, keepdims=True)

    def first_lane(mask, lane):
        return jnp.min(jnp.where(mask, lane.astype(F32), float(LANES)), axis=-1, keepdims=True).astype(jnp.int32)

    def group_all(v, lane, op):
        sh = 1
        while sh < per_group:
            partner = jnp.where((lane & sh) == 0, pltpu.roll(v, LANES - sh, 1), pltpu.roll(v, sh, 1))
            v = op(v, partner)
            sh *= 2
        return v

    def body(x_ref, r_ref, b_ref, e_o, g_o, k_o, cnt_o, carry):
        i = pl.program_id(0)

        @pl.when(i == 0)
        def _():
            carry[...] = jnp.zeros(carry.shape, F32)

        lane = lax.broadcasted_iota(jnp.int32, (tm, LANES), 1)
        valid = lane < N_EXPERTS
        neg = -jnp.inf
        scores = _sigmoid(_dot(x_ref[...], r_ref[...], HP))
        biased = jnp.where(valid, scores + b_ref[...], neg)
        m1 = group_all(biased, lane, jnp.maximum)
        first = group_all(jnp.where(biased == m1, lane, LANES), lane, jnp.minimum)
        m2 = group_all(jnp.where(lane == first, neg, biased), lane, jnp.maximum)
        grp = jnp.where(valid & ((lane & (per_group - 1)) == 0), m1 + m2, neg)
        emask = jnp.zeros((tm, LANES), jnp.bool_)
        for _ in range(TOPK_GROUPS):
            idx = first_lane(grp == lane_max(grp), lane)
            emask = emask | ((lane - idx >= 0) & (lane - idx < per_group))
            grp = jnp.where(lane == idx, neg, grp)
        cur = jnp.where(emask & valid, biased, neg)
        picks = []
        sel = jnp.zeros((tm, LANES), jnp.bool_)
        for _ in range(TOP_K):
            idx = first_lane(cur == lane_max(cur), lane)
            pick = lane == idx
            picks.append((idx, pick))
            sel = sel | pick
            cur = jnp.where(pick, neg, cur)
        selw = jnp.where(sel, scores, 0.0)
        gates_dense = selw / jnp.sum(selw, axis=-1, keepdims=True) * ROUTED_SCALE
        ri = lax.broadcasted_iota(jnp.int32, (tm, tm), 0)
        cj = lax.broadcasted_iota(jnp.int32, (tm, tm), 1)
        p01 = jnp.where(sel, 1.0, 0.0)
        before = _dot(jnp.where(ri > cj, 1.0, 0.0).astype(BF16), p01.astype(BF16)) + carry[...]
        carry[...] = carry[...] + jnp.sum(p01, axis=0, keepdims=True)
        cnt_o[...] = carry[...]
        e_out = jnp.zeros((tm, LANES), jnp.int32)
        g_out = jnp.zeros((tm, LANES), F32)
        k_out = jnp.zeros((tm, LANES), jnp.int32)
        for j, (idx, pick) in enumerate(picks):
            e_out = jnp.where(lane == j, idx, e_out)
            g_out = jnp.where(lane == j, jnp.sum(jnp.where(pick, gates_dense, 0.0), axis=-1, keepdims=True), g_out)
            rank = jnp.sum(jnp.where(pick, before, 0.0), axis=-1, keepdims=True)
            k_out = jnp.where(lane == j, rank.astype(jnp.int32), k_out)
        e_o[...] = e_out
        g_o[...] = g_out
        k_o[...] = k_out

    tok = pl.BlockSpec((tm, LANES), lambda i: (i, 0))
    one = pl.BlockSpec((1, LANES), lambda i: (0, 0))
    return pl.pallas_call(
        body, grid=(m // tm,), name="moe_route",
        in_specs=[pl.BlockSpec((tm, x.shape[1]), lambda i: (i, 0)), pl.BlockSpec(router.shape, lambda i: (0, 0)), one],
        out_specs=[tok, tok, tok, one],
        out_shape=[jax.ShapeDtypeStruct((m, LANES), jnp.int32), jax.ShapeDtypeStruct((m, LANES), F32),
                   jax.ShapeDtypeStruct((m, LANES), jnp.int32), jax.ShapeDtypeStruct((1, LANES), F32)],
        scratch_shapes=[pltpu.VMEM((1, LANES), F32)],
        compiler_params=_cp("arbitrary"))(x, router, router_bias)


def _moe_dispatch(x_words, dest, cap, *, tb):
    m, wd = x_words.shape
    nt = m // tb

    def body(dest_ref, x_ref, out_ref, sem):
        def row_copy(i, j):
            return pltpu.make_async_copy(x_ref.at[pl.ds(i, 1)], out_ref.at[pl.ds(dest_ref[0, 0, i * TOP_K + j], 1)], sem)

        def issue(i, carry):
            for j in range(TOP_K):
                row_copy(i, j).start()
            return carry

        def drain(i, carry):
            for j in range(TOP_K):
                row_copy(i, j).wait()
            return carry

        lax.fori_loop(0, tb, issue, 0)
        lax.fori_loop(0, tb, drain, 0)

    return pl.pallas_call(
        body, grid=(nt,), name="moe_dispatch",
        in_specs=[pl.BlockSpec((1, 1, tb * TOP_K), lambda i: (i, 0, 0), memory_space=pltpu.SMEM),
                  pl.BlockSpec((tb, wd), lambda i: (i, 0))],
        out_specs=pl.BlockSpec(memory_space=pl.ANY),
        out_shape=jax.ShapeDtypeStruct((cap, wd), x_words.dtype),
        scratch_shapes=[pltpu.SemaphoreType.DMA(())],
        compiler_params=_cp("arbitrary"))(dest, x_words)


def _pack_halves(y):
    n = y.shape[1] // 2
    bits = lambda a: lax.bitcast_convert_type(a.astype(BF16).astype(F32), jnp.uint32)
    return (bits(y[:, :n]) >> 16) | (bits(y[:, n:]) & jnp.uint32(0xFFFF0000))


def _unpack_halves(w):
    lo = lax.bitcast_convert_type(w << 16, F32).astype(BF16)
    hi = lax.bitcast_convert_type(w & jnp.uint32(0xFFFF0000), F32).astype(BF16)
    return lo, hi


def _ep_ln_packed(acc, x, g, b):
    y, _ = _ep_ln(acc, x, g, b)
    return y, y, _pack_halves(y)


def _moe_experts(xs_words, blk_exp, blk_rows, n_used, w_gate, w_up, w_down):
    cap, half = xs_words.shape
    d = 2 * half
    nblk = cap // MOE_BLOCK
    ed = w_gate.shape[2]

    def body(be_ref, br_ref, nu_ref, x_ref, wg_ref, wu_ref, wd_ref, o_ref, wg_b, wu_b, wd_b):
        i = pl.program_id(0)

        @pl.when(i < nu_ref[0])
        def _():
            @pl.when((i == 0) | (be_ref[i] != be_ref[jnp.maximum(i - 1, 0)]))
            def _():
                wg_b[...] = wg_ref[0].astype(BF16)
                wu_b[...] = wu_ref[0].astype(BF16)
                wd_b[...] = wd_ref[0].astype(BF16)

            row = lax.broadcasted_iota(jnp.int32, (MOE_BLOCK, 1), 0)
            lo, hi = _unpack_halves(jnp.where(row < br_ref[i], x_ref[...], jnp.uint32(0)))
            hg = _dot(lo, wg_b[:half]) + _dot(hi, wg_b[half:])
            hu = _dot(lo, wu_b[:half]) + _dot(hi, wu_b[half:])
            o_ref[...] = _dot((_silu(hg) * hu).astype(BF16), wd_b[...])

    blk = lambda i, be, br, nu: (jnp.minimum(i, nu[0] - 1), 0)
    wmap = lambda i, be, br, nu: (be[jnp.minimum(i, nu[0] - 1)], 0, 0)
    grid_spec = pltpu.PrefetchScalarGridSpec(
        num_scalar_prefetch=3, grid=(nblk,),
        in_specs=[pl.BlockSpec((MOE_BLOCK, half), blk), pl.BlockSpec((1, d, ed), wmap), pl.BlockSpec((1, d, ed), wmap),
                  pl.BlockSpec((1, ed, d), wmap)],
        out_specs=pl.BlockSpec((MOE_BLOCK, d), blk),
        scratch_shapes=[pltpu.VMEM((d, ed), BF16), pltpu.VMEM((d, ed), BF16), pltpu.VMEM((ed, d), BF16)])
    return pl.pallas_call(body, grid_spec=grid_spec, out_shape=jax.ShapeDtypeStruct((cap, d), F32), name="moe_experts",
                          compiler_params=_cp("arbitrary"))(blk_exp, blk_rows, n_used, xs_words, w_gate, w_up, w_down)


def _moe_combine(ys, dest, gates, *, tc):
    cap, d = ys.shape
    m = gates.shape[0]
    nt = m // tc

    def body(dest_ref, g_ref, y_ref, o_ref, buf, sem):
        def row_copy(i, j):
            return pltpu.make_async_copy(y_ref.at[pl.ds(dest_ref[0, 0, i * TOP_K + j], 1)], buf.at[j, pl.ds(i, 1)], sem)

        def issue(i, carry):
            for j in range(TOP_K):
                row_copy(i, j).start()
            return carry

        def drain(i, carry):
            for j in range(TOP_K):
                row_copy(i, j).wait()
            return carry

        lax.fori_loop(0, tc, issue, 0)
        lax.fori_loop(0, tc, drain, 0)
        g = g_ref[...]
        acc = buf[0] * g[:, 0:1]
        for j in range(1, TOP_K):
            acc = acc + buf[j] * g[:, j:j + 1]
        o_ref[...] = acc

    return pl.pallas_call(
        body, grid=(nt,), name="moe_combine",
        in_specs=[pl.BlockSpec((1, 1, tc * TOP_K), lambda i: (i, 0, 0), memory_space=pltpu.SMEM),
                  pl.BlockSpec((tc, LANES), lambda i: (i, 0)),
                  pl.BlockSpec(memory_space=pl.ANY)],
        out_specs=pl.BlockSpec((tc, d), lambda i: (i, 0)),
        out_shape=jax.ShapeDtypeStruct((m, d), F32),
        scratch_shapes=[pltpu.VMEM((TOP_K, tc, d), F32), pltpu.SemaphoreType.DMA(())],
        compiler_params=_cp("arbitrary"))(dest, gates, ys)


def _moe_layer(geo, x, xb, x_words, router, router_bias, w_gate, w_up, w_down, ws_gate, ws_up, ws_down, ln_g, ln_b):
    m, d = x.shape
    tm = geo.tm
    eidx, gates, rank, counts = _moe_route(x, _pad_cols(router, LANES), jnp.pad(router_bias, (0, LANES - N_EXPERTS)).reshape(1, -1), tm=tm)
    cnt = counts[0, :N_EXPERTS].astype(jnp.int32)
    padded = (cnt + MOE_BLOCK - 1) // MOE_BLOCK * MOE_BLOCK
    pad_end = jnp.cumsum(padded)
    pad_start = pad_end - padded
    n_blocks = -(-(m * TOP_K) // MOE_BLOCK) + N_EXPERTS
    cap = n_blocks * MOE_BLOCK
    blk_row0 = jnp.arange(n_blocks, dtype=jnp.int32) * MOE_BLOCK
    blk_exp = jnp.minimum(jnp.sum((pad_end[None, :] <= blk_row0[:, None]).astype(jnp.int32), axis=1), N_EXPERTS - 1)
    blk_rows = jnp.clip(cnt[blk_exp] - (blk_row0 - pad_start[blk_exp]), 0, MOE_BLOCK).astype(jnp.int32)
    n_used = (pad_end[-1:] // MOE_BLOCK).astype(jnp.int32)
    dest = pad_start[eidx[:, :TOP_K]] + rank[:, :TOP_K]

    tb = tm
    xs_words = _moe_dispatch(x_words, dest.reshape(m // tb, 1, tb * TOP_K), cap, tb=tb)
    ys = _moe_experts(xs_words, blk_exp, blk_rows, n_used, w_gate, w_up, w_down)
    tc = min(MOE_COMBINE_ROWS, tm)
    routed = _moe_combine(ys, dest.reshape(m // tc, 1, tc * TOP_K), gates, tc=tc)

    sd = ws_gate.shape[1]
    w_sh = jnp.concatenate([ws_gate, ws_up], axis=1).astype(BF16)
    (hs,) = _mm([(xb, w_sh)], tm=tm, tn=2 * sd, epilogue=lambda acc: (_silu(acc[:, :sd]) * acc[:, sd:],), outs=[(sd, BF16)])
    return _mm([(hs, ws_down.astype(BF16))], tm=tm // 2, tn=d,
               aux=[(x, 'rowcol'), (routed, 'rowcol'), (ln_g.reshape(1, -1), 'col'), (ln_b.reshape(1, -1), 'col')],
               epilogue=lambda acc, xr, rt, g, b: _ep_ln(acc + rt, xr, g, b), outs=[(d, F32), (d, BF16)])


def _rwkv_params(li, mu, w0, w2, a0, a2, g2, k_k, k_a, r_k, ln_g, ln_b):
    row = lambda a: a[li].reshape(1, -1)
    padr = lambda a, n: jnp.pad(a, ((0, n - a.shape[0]), (0, 0)))
    return dict(mu=_rwkv_cols_layout(mu[li])[None, :], w0=row(w0), a0=row(a0), k_k=row(k_k), k_a=row(k_a),
                w2=padr(w2[li], LANES), a2=padr(a2[li], LANES), g2=g2[li], r_k=row(r_k), ln_g=row(ln_g), ln_b=row(ln_b))


def _mix_out(geo, x, o_a, o_b, w_out, ln_g, ln_b):
    d = x.shape[1]
    ka = o_a.shape[1]
    return _mm([(o_a, w_out[:ka].astype(BF16)), (o_b, w_out[ka:].astype(BF16))], tm=geo.tm // 2, tn=d,
               aux=[(x, 'rowcol'), (ln_g.reshape(1, -1), 'col'), (ln_b.reshape(1, -1), 'col')],
               epilogue=_ep_ln_packed,
               outs=[(d, F32), (d, BF16), (d // 2, jnp.uint32)], name="mix_out")


def kernel(x_prompt, x_sample, cache_mla_ckv, cache_mla_krope, state_rwkv_shift, state_rwkv_wkv, cache_fox_k, cache_fox_v, cache_fox_logf, state_gdn_conv, state_gdn_wkv, ln1_g, ln1_b, ln2_g, ln2_b, ev_w_in, ev_w_out, mla_kv_norm, mla_w_uk, mla_w_uv, rwkv_mu, rwkv_w0, rwkv_w2, rwkv_a0, rwkv_a2, rwkv_g2, rwkv_k_k, rwkv_k_a, rwkv_r_k, rwkv_ln_g, rwkv_ln_b, od_w_in, od_w_out, fox_q_norm, fox_k_norm, fox_f_bias, gdn_conv_w, gdn_a_log, gdn_dt_bias, gdn_norm, moe_router, moe_router_bias, moe_w_gate, moe_w_up, moe_w_down, moe_ws_gate, moe_ws_up, moe_ws_down):
    bp, tp, d = x_prompt.shape
    bs, ts, _ = x_sample.shape
    geo = _Geom(bp, tp, bs, ts, cache_mla_ckv.shape[2])
    x = jnp.concatenate([x_prompt.reshape(bp * tp, d), x_sample.reshape(bs * ts, d)], axis=0)
    xb = x.astype(BF16)
    rope_tab = _rope_table(geo)
    st = {}
    for layer in range(ln1_g.shape[0]):
        li = layer // 2
        if layer % 2 == 0:
            rw = _rwkv_params(li, rwkv_mu, rwkv_w0, rwkv_w2, rwkv_a0, rwkv_a2, rwkv_g2, rwkv_k_k, rwkv_k_a, rwkv_r_k,
                              rwkv_ln_g, rwkv_ln_b)
            o_a, o_b, new = _even_mixer(geo, xb, rope_tab, ev_w_in[li], mla_kv_norm[li], mla_w_uk[li], mla_w_uv[li], rw,
                                        cache_mla_ckv[li], cache_mla_krope[li], state_rwkv_shift[li], state_rwkv_wkv[li])
            w_out = ev_w_out[li]
        else:
            fx = {'q_norm': fox_q_norm[li], 'k_norm': fox_k_norm[li], 'f_bias': fox_f_bias[li]}
            gd = {'conv_w': gdn_conv_w[li], 'a_log': gdn_a_log[li], 'dt_bias': gdn_dt_bias[li], 'norm': gdn_norm[li]}
            o_a, o_b, new = _odd_mixer(geo, xb, od_w_in[li], fx, gd, cache_fox_k[li], cache_fox_v[li], cache_fox_logf[li],
                                       state_gdn_conv[li], state_gdn_wkv[li])
            w_out = od_w_out[li]
        for name, val in new.items():
            st.setdefault(name, []).append(val)
        x, xb, x_words = _mix_out(geo, x, o_a, o_b, w_out, ln1_g[layer], ln1_b[layer])
        x, xb = _moe_layer(geo, x, xb, x_words, moe_router[layer], moe_router_bias[layer], moe_w_gate[layer], moe_w_up[layer],
                           moe_w_down[layer], moe_ws_gate[layer], moe_ws_up[layer], moe_ws_down[layer],
                           ln2_g[layer], ln2_b[layer])
    names = ('ckv', 'kr', 'sh', 'rwkv', 'fk', 'fv', 'fl', 'cv', 'gdn')
    return ((x[:geo.mp].reshape(bp, tp, d), x[geo.mp:].reshape(bs, ts, d))
            + tuple(jnp.stack(st[n + '_p']) for n in names) + tuple(jnp.stack(st[n + '_s']) for n in names))
```

```python
import functools
import math

import jax
import jax.numpy as jnp
from jax import lax
from jax.experimental import pallas as pl
from jax.experimental.pallas import tpu as pltpu

F32 = jnp.float32
BF16 = jnp.bfloat16
HP = lax.Precision.HIGHEST

D_MODEL = 2048
DEPTH = 2
DN_ALPHA = float((2.0 * DEPTH) ** 0.25)
LN_EPS = 1e-5
RMS_EPS = 1e-6
CHUNK = 64
LANES = 128

MLA_HEADS, MLA_NOPE, MLA_ROPE, MLA_VDIM, MLA_KV_LORA = 8, 128, 64, 128, 512
MLA_QK = MLA_NOPE + MLA_ROPE
ROPE_THETA = 10000.0
RWKV_HEADS, RWKV_HD = 16, 64
RWKV_W = RWKV_HEADS * RWKV_HD
RWKV_DECAY_LORA, RWKV_A_LORA, RWKV_G_LORA = 96, 96, 256
RWKV_GN_EPS = 64e-5
RWKV_PCOLS = 3 * RWKV_W + 2 * LANES + RWKV_G_LORA
FOX_HEADS, FOX_HD = 8, 128
FOX_W = FOX_HEADS * FOX_HD
GDN_HEADS, GDN_HD, GDN_CONV = 8, 128, 4
GDN_W = GDN_HEADS * GDN_HD
GDN_QKV = 3 * GDN_W
N_EXPERTS, TOP_K, N_GROUPS, TOPK_GROUPS = 64, 6, 8, 4
EXPERT_DIM, SHARED_DIM = 512, 512
ROUTED_SCALE = 2.5
SCAN_CHUNK = 64
FLASH_KEY_SUBTILE = 1024
NEG_BIG = -1e30
RWKV_PASSES = dict(a=1, x=3, inv=1, u=1, y=1, s=3)
GDN_PASSES = dict(a=1, x=1, inv=1, u=1, y=1, s=1)

VMEM_LIMIT_BYTES = 56 * 1024 * 1024
MM_MAX_COLS = 1792


def _cp(*sem):
    return pltpu.CompilerParams(dimension_semantics=("arbitrary",) * len(sem), vmem_limit_bytes=VMEM_LIMIT_BYTES)


def _dot(a, b, prec=None):
    return jnp.dot(a, b, precision=prec, preferred_element_type=F32)


def _dot_nt(a, b, prec=None):
    return lax.dot_general(a, b, (((1,), (1,)), ((), ())), precision=prec, preferred_element_type=F32)


def _dot_tn(a, b, prec=None):
    return lax.dot_general(a, b, (((0,), (0,)), ((), ())), precision=prec, preferred_element_type=F32)


def _sigmoid(x):
    return 1.0 / (1.0 + jnp.exp(-x))


def _softplus(x):
    return jnp.maximum(x, 0.0) + jnp.log(1.0 + jnp.exp(-jnp.abs(x)))


def _silu(x):
    return x * _sigmoid(x)


def _mm(pairs, *, tm, tn, epilogue, outs, aux=(), name="mm"):
    m, n = pairs[0][0].shape[0], pairs[0][1].shape[1]
    if tn is None:
        tn = n if n <= MM_MAX_COLS else max(c for c in range(LANES, MM_MAX_COLS + 1, LANES) if n % c == 0)
    outs = [(tn if ow is None else ow, dt) for ow, dt in outs]
    assert m % tm == 0 and n % tn == 0, (m, n, tm, tn)
    nj, ni = n // tn, m // tm
    in_specs, args = [], []
    for a, w in pairs:
        k = a.shape[1]
        in_specs += [pl.BlockSpec((tm, k), lambda j, i: (i, 0)), pl.BlockSpec((k, tn), lambda j, i: (0, j))]
        args += [a, w]
    for arr, kind in aux:
        if kind == 'row':
            in_specs.append(pl.BlockSpec((tm, arr.shape[1]), lambda j, i: (i, 0)))
        elif kind == 'rowcol':
            in_specs.append(pl.BlockSpec((tm, tn), lambda j, i: (i, j)))
        elif kind == 'row0':
            in_specs.append(pl.BlockSpec(arr.shape, lambda j, i: (0, 0)))
        else:
            in_specs.append(pl.BlockSpec((1, tn), lambda j, i: (0, j)))
        args.append(arr)
    out_shape = [jax.ShapeDtypeStruct((m, ow * nj), dt) for ow, dt in outs]
    out_specs = [pl.BlockSpec((tm, ow), lambda j, i: (i, j)) for ow, dt in outs]
    n_pairs, n_aux = len(pairs), len(aux)

    def body(*refs):
        acc = None
        for p in range(n_pairs):
            d = _dot(refs[2 * p][...], refs[2 * p + 1][...])
            acc = d if acc is None else acc + d
        res = epilogue(acc, *[r[...] for r in refs[2 * n_pairs:2 * n_pairs + n_aux]])
        for o_ref, val in zip(refs[2 * n_pairs + n_aux:], res):
            o_ref[...] = val.astype(o_ref.dtype)

    res = pl.pallas_call(body, grid=(nj, ni), in_specs=in_specs, out_specs=out_specs, out_shape=out_shape, name=name,
                         compiler_params=_cp("parallel", "parallel"))(*args)
    return res


def _mm_t(w_t, a, *, nb, t, row0, name="mm_t"):
    n, k = w_t.shape
    tm = next((c for c in (512, 384, 256, 128) if t % c == 0), t)
    assert t % tm == 0 and row0 % tm == 0
    nt = t // tm

    def body(w_ref, a_ref, o_ref):
        o_ref[...] = _dot_nt(w_ref[...], a_ref[...]).astype(o_ref.dtype)

    return pl.pallas_call(
        body, grid=(nb, nt), name=name,
        in_specs=[pl.BlockSpec((n, k), lambda b, i: (0, 0)), pl.BlockSpec((tm, k), lambda b, i: (row0 // tm + b * nt + i, 0))],
        out_specs=pl.BlockSpec((n, tm), lambda b, i: (b, i)),
        out_shape=jax.ShapeDtypeStruct((nb * n, t), BF16),
        compiler_params=_cp("parallel", "parallel"))(w_t, a)


def _rope_lanes(x, tab):
    c, s1, s2 = tab[:, :LANES], tab[:, LANES:2 * LANES], tab[:, 2 * LANES:]
    n = x.shape[1]
    lo = pltpu.roll(x, n - MLA_ROPE // 2, 1)
    hi = pltpu.roll(x, MLA_ROPE // 2, 1)
    if n > LANES:
        reps = n // LANES
        c, s1, s2 = (jnp.concatenate([t] * reps, axis=1) for t in (c, s1, s2))
    return x * c + lo * s1 + hi * s2


def _ep_rms(acc, g):
    y = acc * lax.rsqrt(jnp.mean(acc * acc, axis=-1, keepdims=True) + RMS_EPS) * g
    return y, y


def _ep_ln(acc, x, g, b):
    z = DN_ALPHA * x + acc
    mu = jnp.mean(z, axis=-1, keepdims=True)
    zc = z - mu
    var = jnp.mean(zc * zc, axis=-1, keepdims=True)
    y = zc * lax.rsqrt(var + LN_EPS) * g + b
    return y, y


def _flash(qs, ks, vt, *, nb, tq_total, tk_total, q_row0, q_off, tq, tk, chunked, nh, bias=None, gate=None,
           out_dtype=BF16, name="flash"):
    assert tq_total % tq == 0 and tk_total % tk == 0 and q_row0 % tq == 0
    nq, nk = tq_total // tq, tk_total // tk
    qrb0 = q_row0 // tq
    ts = tk if tk % FLASH_KEY_SUBTILE else min(tk, FLASH_KEY_SUBTILE)
    for qi_s in range(nq):
        last_s = min(nk - 1, (q_off + qi_s * tq + tq - 1) // tk)
        assert last_s * tk <= q_off + qi_s * tq + 1, "key tiles before the last needed one must be fully visible"
    n_q, n_k = len(qs), len(ks)
    has_bias, has_gate = bias is not None, gate is not None

    def last_k(qi):
        return jnp.minimum(nk - 1, (q_off + qi * tq + tq - 1) // tk)

    kblk = lambda qi, ki: jnp.minimum(ki, last_k(qi))
    q_map = lambda b, h, qi, ki: (qrb0 + b * nq + qi, h)
    in_specs, args = [], []
    for q in qs:
        in_specs.append(pl.BlockSpec((tq, LANES), q_map))
        args.append(q)
    for arr, c0, per_head in ks:
        if per_head:
            in_specs.append(pl.BlockSpec((tk, LANES), lambda b, h, qi, ki, c0=c0: (b * nk + kblk(qi, ki), c0 + h)))
        else:
            in_specs.append(pl.BlockSpec((tk, LANES), lambda b, h, qi, ki, c0=c0: (b * nk + kblk(qi, ki), c0)))
        args.append(arr)
    in_specs.append(pl.BlockSpec((LANES, tk), lambda b, h, qi, ki: (b * nh + h, kblk(qi, ki))))
    args.append(vt)
    if has_bias:
        in_specs.append(pl.BlockSpec((1, 1, 1, tq), lambda b, h, qi, ki: (b, h, 0, qi)))
        in_specs.append(pl.BlockSpec((1, 1, tk, 1), lambda b, h, qi, ki: (b, h, kblk(qi, ki), 0)))
        args += list(bias)
    if has_gate:
        in_specs.append(pl.BlockSpec((tq, LANES), q_map))
        args.append(gate)

    def body(*refs):
        q_refs = refs[:n_q]
        k_refs = refs[n_q:n_q + n_k]
        vt_ref = refs[n_q + n_k]
        pos = n_q + n_k + 1
        if has_bias:
            qb_ref, kb_ref = refs[pos], refs[pos + 1]
            pos += 2
        if has_gate:
            gate_ref = refs[pos]
            pos += 1
        o_ref, m_ref, l_ref, acc_ref = refs[pos:pos + 4]
        qi, ki = pl.program_id(2), pl.program_id(3)

        @pl.when(ki == 0)
        def _():
            m_ref[...] = jnp.full(m_ref.shape, NEG_BIG, F32)
            l_ref[...] = jnp.zeros(l_ref.shape, F32)
            acc_ref[...] = jnp.zeros(acc_ref.shape, F32)

        def step(masked):
            q_all = jnp.concatenate([q_ref[...] for q_ref in q_refs], axis=1) if n_q > 1 else q_refs[0][...]
            qpos = q_off + qi * tq + lax.broadcasted_iota(jnp.int32, (1, tq), 1)
            qlim = (qpos | (CHUNK - 1)) if chunked else qpos
            subs = [slice(j * ts, (j + 1) * ts) for j in range(tk // ts)]
            ss = []
            for j, rows in enumerate(subs):
                k_all = jnp.concatenate([k_ref[rows, :] for k_ref in k_refs], axis=1) if n_k > 1 else k_refs[0][rows, :]
                s = _dot_nt(k_all, q_all)
                if has_bias:
                    s = s + (qb_ref[0, 0] - kb_ref[0, 0, rows, :])
                if masked:
                    kpos = ki * tk + j * ts + lax.broadcasted_iota(jnp.int32, (ts, 1), 0)
                    s = jnp.where(kpos <= qlim, s, NEG_BIG)
                ss.append(s)
            m_old = m_ref[...]
            m_new = m_old
            for s in ss:
                m_new = jnp.maximum(m_new, jnp.max(s, axis=0, keepdims=True))
            alpha = jnp.exp(m_old - m_new)
            l_new = alpha * l_ref[...]
            acc_new = alpha * acc_ref[...]
            for rows, s in zip(subs, ss):
                p = jnp.exp(s - m_new)
                l_new = l_new + jnp.sum(p, axis=0, keepdims=True)
                acc_new = acc_new + _dot(vt_ref[:, rows], p.astype(BF16))
            m_ref[...] = m_new
            l_ref[...] = l_new
            acc_ref[...] = acc_new

        @pl.when(ki < last_k(qi))
        def _():
            step(False)

        @pl.when(ki == last_k(qi))
        def _():
            step(True)

        @pl.when(ki == nk - 1)
        def _():
            o = (acc_ref[...] / l_ref[...]).T
            if has_gate:
                o = o * gate_ref[...]
            o_ref[...] = o.astype(o_ref.dtype)

    return pl.pallas_call(
        body, grid=(nb, nh, nq, nk), in_specs=in_specs, name=name,
        out_specs=pl.BlockSpec((tq, LANES), lambda b, h, qi, ki: (b * nq + qi, h)),
        out_shape=jax.ShapeDtypeStruct((nb * tq_total, nh * LANES), out_dtype),
        scratch_shapes=[pltpu.VMEM((1, tq), F32), pltpu.VMEM((1, tq), F32), pltpu.VMEM((LANES, tq), F32)],
        compiler_params=_cp("parallel", "parallel", "parallel", "arbitrary"))(*args)


def _head_sum_matrices(n_heads, hd):
    lane = jnp.arange(n_heads * hd)[:, None] // hd
    e = (lane == jnp.arange(LANES)[None, :]).astype(F32)
    return e, e.T


def _rwkv_prep(cols, shift_prev, prm, *, nb, t, row0):
    tt = min(256, t)
    assert t % tt == 0 and row0 % tt == 0
    nt = t // tt
    rb0 = row0 // tt
    w = RWKV_W

    def body(c_ref, sp_ref, mu_ref, w0_ref, a0_ref, kk_ref, ka_ref, w2_ref, a2_ref, g2_ref, e_ref, et_ref,
             r_o, k_o, v_o, kk_o, b_o, lw_o, g_o, sh_o, carry):
        ti = pl.program_id(1)

        @pl.when(ti == 0)
        def _():
            carry[...] = sp_ref[0]

        c = c_ref[...]
        row = lax.broadcasted_iota(jnp.int32, (tt, 1), 0)
        prev = jnp.where(row == 0, carry[...], pltpu.roll(c, 1, 0))
        carry[...] = c[tt - 1:tt, :]
        sh_o[0] = c[tt - 1:tt, :]
        xs = c + (prev - c) * mu_ref[...]
        r, k, v = xs[:, :w], xs[:, w:2 * w], xs[:, 2 * w:3 * w]
        o1 = 3 * w
        lora = lambda u, w_ref: _dot(u.astype(BF16), w_ref[...].astype(BF16))
        w_raw = w0_ref[...] + lora(jnp.tanh(xs[:, o1:o1 + LANES]), w2_ref)
        log_w = -jnp.exp(-_softplus(-w_raw) - 0.5)
        a = _sigmoid(a0_ref[...] + lora(xs[:, o1 + LANES:o1 + 2 * LANES], a2_ref))
        g = lora(_sigmoid(xs[:, o1 + 2 * LANES:]), g2_ref)

        def head_dot(u, sel_ref):
            hi, lo = _split_bf16(u)
            sel = sel_ref[...].astype(BF16)
            return _dot(hi.astype(BF16), sel) + _dot(lo.astype(BF16), sel)

        kk = k * kk_ref[...]
        kk = kk * head_dot(lax.rsqrt(head_dot(kk * kk, e_ref) + 1e-6), et_ref)
        r_o[...] = r
        k_o[...] = k * (1.0 + (a - 1.0) * ka_ref[...])
        v_o[...] = v
        kk_o[...] = kk
        b_o[...] = kk * a
        lw_o[...] = log_w
        g_o[...] = g

    e, et = _head_sum_matrices(RWKV_HEADS, RWKV_HD)
    tok = pl.BlockSpec((tt, w), lambda b, ti: (b * nt + ti, 0))
    full = lambda arr: pl.BlockSpec(arr.shape, lambda b, ti: (0,) * arr.ndim)
    params = [prm['mu'], prm['w0'], prm['a0'], prm['k_k'], prm['k_a'], prm['w2'], prm['a2'], prm['g2'], e, et]
    outs = pl.pallas_call(
        body, grid=(nb, nt), name="rwkv_prep",
        in_specs=[pl.BlockSpec((tt, RWKV_PCOLS), lambda b, ti: (rb0 + b * nt + ti, 0)),
                  pl.BlockSpec((1, 1, RWKV_PCOLS), lambda b, ti: (b, 0, 0))] + [full(p) for p in params],
        out_specs=[tok] * 7 + [pl.BlockSpec((1, 1, RWKV_PCOLS), lambda b, ti: (b, 0, 0))],
        out_shape=[jax.ShapeDtypeStruct((nb * t, w), F32)] * 7 + [jax.ShapeDtypeStruct((nb, 1, RWKV_PCOLS), F32)],
        scratch_shapes=[pltpu.VMEM((1, RWKV_PCOLS), F32)],
        compiler_params=_cp("parallel", "arbitrary"))(cols, shift_prev, *params)
    return outs


def _cumsum_rows(x, n):
    row = lax.broadcasted_iota(jnp.int32, (n, 1), 0)
    sh = 1
    while sh < n:
        x = x + jnp.where(row >= sh, pltpu.roll(x, sh, 0), 0.0)
        sh *= 2
    return x


def _split_bf16(x):
    hi = x.astype(BF16).astype(F32)
    return hi, x - hi


def _bmm(a, b, ca, cb, passes):
    dn = (((ca,), (cb,)), ((0,), (0,)))
    if passes == 6:
        return lax.dot_general(a, b, dn, precision=HP, preferred_element_type=F32)
    if passes == 3:
        ah, al = _split_bf16(a)
        bh, bl = _split_bf16(b)
        a = jnp.concatenate([ah, ah, al], axis=ca)
        b = jnp.concatenate([bh, bl, bh], axis=cb)
    return lax.dot_general(a.astype(BF16), b.astype(BF16), dn, preferred_element_type=F32)


def _heads(x, nh, hd):
    return jnp.stack([x[:, h * hd:(h + 1) * hd] for h in range(nh)], axis=0)


def _unheads(x):
    return jnp.concatenate([x[h] for h in range(x.shape[0])], axis=1)


def _unit_lower_inverse(nmat, n, passes):
    eye = (lax.broadcasted_iota(jnp.int32, (n, n), 0) == lax.broadcasted_iota(jnp.int32, (n, n), 1)).astype(F32)
    t = eye + nmat
    p = nmat
    steps = int(math.ceil(math.log2(n))) - 1
    for _ in range(steps):
        p = _bmm(p, p, 2, 1, passes)
        t = t + _bmm(p, t, 2, 1, passes)
    return t


def _rwkv_scan(r, k, v, kk, bvec, logw, g, s0, prm, *, nb, t):
    c = min(SCAN_CHUNK, t)
    assert t % c == 0
    nc = t // c
    hd, nh = RWKV_HD, RWKV_HEADS

    def body(r_ref, k_ref, v_ref, kk_ref, b_ref, lw_ref, g_ref, s0_ref, rk_ref, lg_ref, lb_ref, o_ref, sf_ref, st):
        ci = pl.program_id(1)

        @pl.when(ci == 0)
        def _():
            st[...] = s0_ref[0]

        lw = lw_ref[...]
        lc = _cumsum_rows(lw, c)
        rr, kx, vv, bb = r_ref[...], k_ref[...], v_ref[...], b_ref[...]
        g_inv = jnp.exp(-lc)
        g_end = jnp.exp(lc[c - 1:c, :] - lc)
        hs = lambda a: _heads(a, nh, hd)
        at, rt = hs(-kk_ref[...] * jnp.exp(lc - lw)), hs(rr * jnp.exp(lc))
        bt, kt = hs(bb * g_inv), hs(kx * g_inv)
        bg, kg = hs(bb * g_end), hs(kx * g_end)
        w_end = hs(jnp.exp(lc[c - 1:c, :]))
        v_h = hs(vv)
        ri = lax.broadcasted_iota(jnp.int32, (c, c), 0)
        cj = lax.broadcasted_iota(jnp.int32, (c, c), 1)
        strict = ri > cj
        incl2 = lax.broadcasted_iota(jnp.int32, (c, 2 * c), 0) >= (lax.broadcasted_iota(jnp.int32, (c, 2 * c), 1) & (c - 1))
        s_all = st[...]
        pp = RWKV_PASSES
        lhs = jnp.concatenate([at, rt], axis=1)
        a_all = _bmm(lhs, jnp.concatenate([bt, kt], axis=1), 2, 2, pp['a'])
        x_all = _bmm(lhs, s_all, 2, 2, pp['x'])
        tinv = _unit_lower_inverse(jnp.where(strict, a_all[:, :c, :c], 0.0), c, pp['inv'])
        a_ak = jnp.where(strict, a_all[:, :c, c:], 0.0)
        u = _bmm(tinv, x_all[:, :c] + _bmm(a_ak, v_h, 2, 1, pp['u']), 2, 1, pp['u'])
        uv = jnp.concatenate([u, v_h], axis=1)
        y = x_all[:, c:] + _bmm(jnp.where(incl2, a_all[:, c:, :], 0.0), uv, 2, 1, pp['y'])
        st[...] = s_all * w_end + _bmm(uv, jnp.concatenate([bg, kg], axis=1), 1, 1, pp['s'])
        mu = jnp.mean(y, axis=-1, keepdims=True)
        yc = y - mu
        var = jnp.mean(yc * yc, axis=-1, keepdims=True)
        yn = yc * lax.rsqrt(var + RWKV_GN_EPS) * hs(lg_ref[...]) + hs(lb_ref[...])
        bonus = jnp.sum(hs(rr * kx * rk_ref[...]), axis=-1, keepdims=True) * v_h
        o_ref[...] = (_unheads(yn + bonus) * g_ref[...]).astype(o_ref.dtype)

        @pl.when(ci == nc - 1)
        def _():
            sf_ref[0] = st[...]

    tok = pl.BlockSpec((c, RWKV_W), lambda b, ci: (b * nc + ci, 0))
    stt = pl.BlockSpec((1, nh, hd, hd), lambda b, ci: (b, 0, 0, 0))
    par = pl.BlockSpec((1, RWKV_W), lambda b, ci: (0, 0))
    return pl.pallas_call(
        body, grid=(nb, nc), name="rwkv_scan",
        in_specs=[tok] * 7 + [stt, par, par, par],
        out_specs=[tok, stt],
        out_shape=[jax.ShapeDtypeStruct((nb * t, RWKV_W), BF16), jax.ShapeDtypeStruct((nb, nh, hd, hd), F32)],
        scratch_shapes=[pltpu.VMEM((nh, hd, hd), F32)],
        compiler_params=_cp("parallel", "arbitrary"))(r, k, v, kk, bvec, logw, g, s0, prm['r_k'], prm['ln_g'], prm['ln_b'])


class _Geom:
    def __init__(self, bp, tp, bs, ts, past):
        self.bp, self.tp, self.bs, self.ts, self.past = bp, tp, bs, ts, past
        self.mp, self.ms = bp * tp, bs * ts
        self.m = self.mp + self.ms
        self.tm = math.gcd(self.m, 512)
        assert self.tm % 16 == 0 and tp % CHUNK == 0 and past % CHUNK == 0 and ts == CHUNK
        self.tq_p = min(1024, tp)
        self.kpad = -(-(past + ts) // (3 * LANES)) * (3 * LANES)
        self.tk_s = self.kpad // 3


def _pad_cols(w, n):
    return jnp.pad(w, ((0, 0), (0, n - w.shape[1])))


def _rope_table(geo):
    half = MLA_ROPE // 2
    pos = jnp.concatenate([jnp.tile(jnp.arange(geo.tp, dtype=jnp.int32), geo.bp),
                           geo.past + jnp.tile(jnp.arange(geo.ts, dtype=jnp.int32), geo.bs)])
    inv_freq = ROPE_THETA ** (-jnp.arange(half, dtype=F32) / half)
    ang = pos.astype(F32)[:, None] * inv_freq[None, :]
    cos, sin = jnp.cos(ang), jnp.sin(ang)
    z = jnp.zeros_like(cos)
    zz = jnp.zeros((pos.shape[0], LANES - MLA_ROPE), F32)
    return jnp.concatenate([cos, cos, zz, -sin, z, zz, z, sin, zz], axis=1)


def _rwkv_cols_layout(a):
    o1 = 3 * RWKV_W
    o2 = o1 + RWKV_DECAY_LORA
    o3 = o2 + RWKV_A_LORA
    pad = [(0, 0)] * (a.ndim - 1)
    return jnp.concatenate([a[..., :o1],
                            jnp.pad(a[..., o1:o2], pad + [(0, LANES - RWKV_DECAY_LORA)]),
                            jnp.pad(a[..., o2:o3], pad + [(0, LANES - RWKV_A_LORA)]),
                            a[..., o3:]], axis=-1)


def _rwkv_cols_unlayout(a):
    o1 = 3 * RWKV_W
    return jnp.concatenate([a[..., :o1], a[..., o1:o1 + RWKV_DECAY_LORA],
                            a[..., o1 + LANES:o1 + LANES + RWKV_A_LORA], a[..., o1 + 2 * LANES:]], axis=-1)


def _even_mixer(geo, xb, rope_tab, w_in, kv_norm, w_uk, w_uv, rw, cache_ckv, cache_kr, shift_prev, wkv0):
    tm, m, mp = geo.tm, geo.m, geo.mp
    d = w_in.shape[0]
    nq = MLA_HEADS * MLA_QK
    wq = w_in[:, :nq].reshape(d, MLA_HEADS, MLA_QK)
    w_qn = wq[:, :, :MLA_NOPE].reshape(d, MLA_HEADS * MLA_NOPE).astype(BF16)
    w_qr = jnp.pad(wq[:, :, MLA_NOPE:], ((0, 0), (0, 0), (0, LANES - MLA_ROPE))).reshape(d, MLA_HEADS * LANES).astype(BF16)
    w_ckv = w_in[:, nq:nq + MLA_KV_LORA].astype(BF16)
    w_kr = _pad_cols(w_in[:, nq + MLA_KV_LORA:nq + MLA_KV_LORA + MLA_ROPE], LANES).astype(BF16)
    w_rw = _rwkv_cols_layout(w_in[:, nq + MLA_KV_LORA + MLA_ROPE:]).astype(BF16)
    scale = MLA_QK ** -0.5

    (qn,) = _mm([(xb, w_qn)], tm=tm, tn=None, epilogue=lambda acc: (acc * scale,), outs=[(None, BF16)])
    (qr,) = _mm([(xb, w_qr)], tm=tm, tn=MLA_HEADS * LANES, aux=[(rope_tab, 'row')],
                epilogue=lambda acc, tab: (_rope_lanes(acc, tab) * scale,), outs=[(MLA_HEADS * LANES, BF16)])
    ckv, ckv_b = _mm([(xb, w_ckv)], tm=tm, tn=MLA_KV_LORA, aux=[(kv_norm.reshape(1, -1), 'col')],
                     epilogue=_ep_rms, outs=[(MLA_KV_LORA, F32), (MLA_KV_LORA, BF16)])
    kr, kr_b = _mm([(xb, w_kr)], tm=tm, tn=LANES, aux=[(rope_tab, 'row')],
                   epilogue=lambda acc, tab: (_rope_lanes(acc, tab),) * 2, outs=[(LANES, F32), (LANES, BF16)])
    (rwc,) = _mm([(xb, w_rw)], tm=tm, tn=None, epilogue=lambda acc: (acc,), outs=[(None, F32)])

    w_k = w_uk.reshape(MLA_KV_LORA, -1).astype(BF16)
    w_vt = w_uv.reshape(MLA_KV_LORA, -1).T.astype(BF16)
    (kn_new,) = _mm([(ckv_b, w_k)], tm=tm, tn=None, epilogue=lambda acc: (acc,), outs=[(None, BF16)])
    vt_p = _mm_t(w_vt, ckv_b, nb=geo.bp, t=geo.tp, row0=0)
    o_p = _flash([qn, qr], [(kn_new, 0, True), (kr_b, 0, False)], vt_p,
                 nb=geo.bp, tq_total=geo.tp, tk_total=geo.tp, q_row0=0, q_off=0, tq=geo.tq_p, tk=geo.tq_p,
                 chunked=True, nh=MLA_HEADS)
    padk = geo.kpad - geo.past - geo.ts
    ckv_all = jnp.concatenate([cache_ckv.astype(BF16), ckv_b[mp:].reshape(geo.bs, geo.ts, -1),
                               jnp.zeros((geo.bs, padk, MLA_KV_LORA), BF16)], axis=1).reshape(geo.bs * geo.kpad, -1)
    kr_all = jnp.concatenate([jnp.pad(cache_kr, ((0, 0), (0, 0), (0, LANES - MLA_ROPE))).astype(BF16),
                              kr_b[mp:].reshape(geo.bs, geo.ts, -1),
                              jnp.zeros((geo.bs, padk, LANES), BF16)], axis=1).reshape(geo.bs * geo.kpad, -1)
    (kn_all,) = _mm([(ckv_all, w_k)], tm=3 * LANES, tn=None, epilogue=lambda acc: (acc,), outs=[(None, BF16)])
    vt_all = _mm_t(w_vt, ckv_all, nb=geo.bs, t=geo.kpad, row0=0)
    o_s = _flash([qn, qr], [(kn_all, 0, True), (kr_all, 0, False)], vt_all,
                 nb=geo.bs, tq_total=geo.ts, tk_total=geo.kpad, q_row0=mp, q_off=geo.past, tq=geo.ts, tk=geo.tk_s,
                 chunked=True, nh=MLA_HEADS)
    o_mla = jnp.concatenate([o_p, o_s], axis=0)

    outs_p = _rwkv_prep(rwc, jnp.zeros((geo.bp, 1, RWKV_PCOLS), F32), rw, nb=geo.bp, t=geo.tp, row0=0)
    outs_s = _rwkv_prep(rwc, _rwkv_cols_layout(shift_prev)[:, None, :], rw, nb=geo.bs, t=geo.ts, row0=mp)
    y_p, wkv_p = _rwkv_scan(*outs_p[:7], jnp.zeros((geo.bp, RWKV_HEADS, RWKV_HD, RWKV_HD), F32), rw, nb=geo.bp, t=geo.tp)
    y_s, wkv_s = _rwkv_scan(*outs_s[:7], wkv0, rw, nb=geo.bs, t=geo.ts)
    o_rwkv = jnp.concatenate([y_p, y_s], axis=0)
    states = dict(
        ckv_p=ckv[:mp].reshape(geo.bp, geo.tp, -1), ckv_s=ckv[mp:].reshape(geo.bs, geo.ts, -1),
        kr_p=kr[:mp, :MLA_ROPE].reshape(geo.bp, geo.tp, -1), kr_s=kr[mp:, :MLA_ROPE].reshape(geo.bs, geo.ts, -1),
        sh_p=_rwkv_cols_unlayout(outs_p[7][:, 0]), sh_s=_rwkv_cols_unlayout(outs_s[7][:, 0]),
        rwkv_p=wkv_p, rwkv_s=wkv_s)
    return o_mla, o_rwkv, states


def _head_rms(x, g, nh, scale=1.0):
    outs = []
    for h in range(nh):
        seg = x[:, h * LANES:(h + 1) * LANES]
        outs.append(seg * lax.rsqrt(jnp.mean(seg * seg, axis=-1, keepdims=True) + RMS_EPS) * (g * scale))
    return jnp.concatenate(outs, axis=1)


def _head_l2(x, nh, scale=1.0):
    outs = []
    for h in range(nh):
        seg = x[:, h * LANES:(h + 1) * LANES]
        outs.append(seg * (lax.rsqrt(jnp.sum(seg * seg, axis=-1, keepdims=True) + 1e-6) * scale))
    return jnp.concatenate(outs, axis=1)


def _cumsum_time(x):
    nb, t, w = x.shape

    def body(x_ref, o_ref):
        o_ref[0] = _cumsum_rows(x_ref[0], t)

    spec = pl.BlockSpec((1, t, w), lambda b: (b, 0, 0))
    return pl.pallas_call(body, grid=(nb,), in_specs=[spec], out_specs=spec, name="cumsum_time",
                          out_shape=jax.ShapeDtypeStruct(x.shape, F32), compiler_params=_cp("parallel"))(x)


def _gdn_prep(cols, conv_prev, conv_w, *, nb, t, row0):
    tt = min(256, t)
    assert t % tt == 0 and row0 % tt == 0
    nt = t // tt
    rb0 = row0 // tt
    w = GDN_W

    def body(c_ref, cp_ref, cw_ref, q_o, k_o, v_o, carry):
        ti = pl.program_id(1)

        @pl.when(ti == 0)
        def _():
            carry[...] = cp_ref[0]

        c = c_ref[...]
        ext = jnp.concatenate([carry[...], c], axis=0)
        carry[...] = c[tt - 8:tt, :]
        cw = cw_ref[...]
        acc = c * cw[GDN_CONV - 1:GDN_CONV, :]
        for j in range(GDN_CONV - 1):
            back = GDN_CONV - 1 - j
            acc = acc + ext[8 - back:8 - back + tt, :] * cw[j:j + 1, :]
        qkv = _silu(acc)
        q_o[...] = _head_l2(qkv[:, :w], GDN_HEADS, GDN_HD ** -0.5)
        k_o[...] = _head_l2(qkv[:, w:2 * w], GDN_HEADS)
        v_o[...] = qkv[:, 2 * w:]

    tok = pl.BlockSpec((tt, w), lambda b, ti: (b * nt + ti, 0))
    return pl.pallas_call(
        body, grid=(nb, nt), name="gdn_prep",
        in_specs=[pl.BlockSpec((tt, GDN_QKV), lambda b, ti: (rb0 + b * nt + ti, 0)),
                  pl.BlockSpec((1, 8, GDN_QKV), lambda b, ti: (b, 0, 0)),
                  pl.BlockSpec((8, GDN_QKV), lambda b, ti: (0, 0))],
        out_specs=[tok] * 3,
        out_shape=[jax.ShapeDtypeStruct((nb * t, w), F32)] * 3,
        scratch_shapes=[pltpu.VMEM((8, GDN_QKV), F32)],
        compiler_params=_cp("parallel", "arbitrary"))(cols, conv_prev, conv_w)


def _gdn_scan(q, k, v, gb, z, s0, norm_g, *, nb, t, gb_row0, z_row0):
    c = min(SCAN_CHUNK, t)
    assert t % c == 0 and gb_row0 % c == 0 and z_row0 % c == 0
    nc = t // c
    nh, hd = GDN_HEADS, GDN_HD

    def body(q_ref, k_ref, v_ref, gb_ref, z_ref, s0_ref, ng_ref, o_ref, sf_ref, st):
        ci = pl.program_id(1)

        @pl.when(ci == 0)
        def _():
            st[...] = s0_ref[0]

        gbv = gb_ref[...]
        gc = _cumsum_rows(gbv, c)
        gct = gc.T
        ri = lax.broadcasted_iota(jnp.int32, (c, c), 0)
        cj = lax.broadcasted_iota(jnp.int32, (c, c), 1)
        tril, strict = ri >= cj, ri > cj
        hs = lambda a: _heads(a, nh, hd)
        k_h, q_h, v_h = hs(k_ref[...]), hs(q_ref[...]), hs(v_ref[...])
        gcol = jnp.stack([gc[:, h:h + 1] for h in range(nh)], axis=0)
        grow = gct[:nh][:, None, :]
        bcol = jnp.stack([gbv[:, nh + h:nh + h + 1] for h in range(nh)], axis=0)
        decay = jnp.where(tril, jnp.exp(jnp.where(tril, gcol - grow, 0.0)), 0.0)
        pp = GDN_PASSES
        kb = k_h * bcol
        prods = _bmm(jnp.concatenate([kb, q_h], axis=1), k_h, 2, 2, pp['a'])
        low = jnp.where(strict, prods[:, :c] * decay, 0.0)
        a_qk = jnp.where(tril, prods[:, c:] * decay, 0.0)
        tinv = _unit_lower_inverse(-low, c, pp['inv'])
        e_g = jnp.exp(gcol)
        uw = _bmm(tinv, jnp.concatenate([v_h * bcol, kb * e_g], axis=2), 2, 1, pp['u'])
        s_all = st[...]
        ws_qs = _bmm(jnp.concatenate([uw[:, :, hd:], q_h * e_g], axis=1), s_all, 2, 1, pp['x'])
        v_new = uw[:, :, :hd] - ws_qs[:, :c]
        o = ws_qs[:, c:] + _bmm(a_qk, v_new, 2, 1, pp['y'])
        g_last = gcol[:, c - 1:c, :]
        st[...] = s_all * jnp.exp(g_last) + _bmm(k_h * jnp.exp(g_last - gcol), v_new, 1, 1, pp['s'])
        o = o * lax.rsqrt(jnp.mean(o * o, axis=-1, keepdims=True) + RMS_EPS) * ng_ref[...]
        o_ref[...] = (_unheads(o) * _silu(z_ref[...])).astype(o_ref.dtype)

        @pl.when(ci == nc - 1)
        def _():
            sf_ref[0] = st[...]

    tok = pl.BlockSpec((c, GDN_W), lambda b, ci: (b * nc + ci, 0))
    stt = pl.BlockSpec((1, nh, hd, hd), lambda b, ci: (b, 0, 0, 0))
    return pl.pallas_call(
        body, grid=(nb, nc), name="gdn_scan",
        in_specs=[tok] * 3 + [pl.BlockSpec((c, LANES), lambda b, ci: (gb_row0 // c + b * nc + ci, 0)),
                              pl.BlockSpec((c, GDN_W), lambda b, ci: (z_row0 // c + b * nc + ci, 0)),
                              stt, pl.BlockSpec((1, hd), lambda b, ci: (0, 0))],
        out_specs=[tok, stt],
        out_shape=[jax.ShapeDtypeStruct((nb * t, GDN_W), BF16), jax.ShapeDtypeStruct((nb, nh, hd, hd), F32)],
        scratch_shapes=[pltpu.VMEM((nh, hd, hd), F32)],
        compiler_params=_cp("parallel", "arbitrary"))(q, k, v, gb, z, s0, norm_g)


def _odd_mixer(geo, xb, w_in, fx, gd, cache_k, cache_v, cache_logf, conv_prev, gdn0):
    tm, m, mp = geo.tm, geo.m, geo.mp
    fw = FOX_W
    o_f = 4 * fw
    o_g = o_f + FOX_HEADS
    w_q, w_k, w_v, w_gate = (w_in[:, i * fw:(i + 1) * fw].astype(BF16) for i in range(4))
    w_f = _pad_cols(w_in[:, o_f:o_g], LANES).astype(BF16)
    w_qkv = w_in[:, o_g:o_g + GDN_QKV].astype(BF16)
    w_ab = _pad_cols(w_in[:, o_g + GDN_QKV:o_g + GDN_QKV + 2 * GDN_HEADS], LANES).astype(BF16)
    w_z = w_in[:, o_g + GDN_QKV + 2 * GDN_HEADS:].astype(BF16)
    qn, kn = fx['q_norm'].reshape(1, -1), fx['k_norm'].reshape(1, -1)
    f_bias = jnp.pad(fx['f_bias'], (0, LANES - FOX_HEADS)).reshape(1, -1)
    scale = FOX_HD ** -0.5

    (q,) = _mm([(xb, w_q)], tm=tm, tn=fw, aux=[(qn, 'row0')],
               epilogue=lambda acc, g: (_head_rms(acc, g, FOX_HEADS, scale),), outs=[(fw, BF16)])
    k, k_b = _mm([(xb, w_k)], tm=tm, tn=fw, aux=[(kn, 'row0')],
                 epilogue=lambda acc, g: (_head_rms(acc, g, FOX_HEADS),) * 2, outs=[(fw, F32), (fw, BF16)])
    (v,) = _mm([(xb, w_v)], tm=tm, tn=None, epilogue=lambda acc: (acc,), outs=[(None, F32)])
    w_vt = w_v.T
    vt_p = _mm_t(w_vt, xb, nb=geo.bp, t=geo.tp, row0=0)
    vt_s = _mm_t(w_vt, xb, nb=geo.bs, t=geo.ts, row0=mp)
    (gate,) = _mm([(xb, w_gate)], tm=tm, tn=None, epilogue=lambda acc: (_sigmoid(acc),), outs=[(None, F32)])
    (logf,) = _mm([(xb, w_f)], tm=tm, tn=LANES, aux=[(f_bias, 'col')],
                  epilogue=lambda acc, fb: (-_softplus(-(acc + fb)),), outs=[(LANES, F32)])

    cum_p = _cumsum_time(logf[:mp].reshape(geo.bp, geo.tp, LANES))[:, :, :FOX_HEADS]
    bias_p = (jnp.transpose(cum_p, (0, 2, 1))[:, :, None, :], jnp.transpose(cum_p, (0, 2, 1))[..., None])
    o_p = _flash([q], [(k_b, 0, True)], vt_p, nb=geo.bp, tq_total=geo.tp, tk_total=geo.tp, q_row0=0, q_off=0,
                 tq=geo.tq_p, tk=geo.tq_p, chunked=False, nh=FOX_HEADS, bias=bias_p, gate=gate)
    padk = geo.kpad - geo.past - geo.ts
    lf_all = jnp.concatenate([jnp.pad(cache_logf, ((0, 0), (0, 0), (0, LANES - FOX_HEADS))),
                              logf[mp:].reshape(geo.bs, geo.ts, LANES), jnp.zeros((geo.bs, padk, LANES), F32)], axis=1)
    cum_s = jnp.transpose(_cumsum_time(lf_all)[:, :, :FOX_HEADS], (0, 2, 1))
    bias_s = (cum_s[:, :, None, geo.past:geo.past + geo.ts], cum_s[..., None])
    k_all = jnp.concatenate([cache_k.reshape(geo.bs, geo.past, fw).astype(BF16), k_b[mp:].reshape(geo.bs, geo.ts, fw),
                             jnp.zeros((geo.bs, padk, fw), BF16)], axis=1).reshape(geo.bs * geo.kpad, fw)
    vt_all = jnp.concatenate([jnp.transpose(cache_v.reshape(geo.bs, geo.past, fw), (0, 2, 1)).astype(BF16),
                              vt_s.reshape(geo.bs, fw, geo.ts), jnp.zeros((geo.bs, fw, padk), BF16)],
                             axis=2).reshape(geo.bs * fw, geo.kpad)
    o_s = _flash([q], [(k_all, 0, True)], vt_all, nb=geo.bs, tq_total=geo.ts, tk_total=geo.kpad, q_row0=mp,
                 q_off=geo.past, tq=geo.ts, tk=geo.tk_s, chunked=False, nh=FOX_HEADS, bias=bias_s, gate=gate)
    o_fox = jnp.concatenate([o_p, o_s], axis=0)

    (qkv_raw,) = _mm([(xb, w_qkv)], tm=tm, tn=None, epilogue=lambda acc: (acc,), outs=[(None, F32)])
    lane = jnp.arange(LANES)
    neg_a = jnp.where(lane < GDN_HEADS, -jnp.exp(jnp.pad(gd['a_log'], (0, LANES - GDN_HEADS))), 0.0).reshape(1, -1)
    dtb = jnp.pad(gd['dt_bias'], (0, LANES - GDN_HEADS)).reshape(1, -1)
    is_g = (lane < GDN_HEADS).astype(F32).reshape(1, -1)
    (gb,) = _mm([(xb, w_ab)], tm=tm, tn=LANES, aux=[(neg_a, 'col'), (dtb, 'col'), (is_g, 'col')],
                epilogue=lambda acc, na, db, ig: (jnp.where(ig > 0.5, na * _softplus(acc + db), _sigmoid(acc)),),
                outs=[(LANES, F32)])
    (z,) = _mm([(xb, w_z)], tm=tm, tn=None, epilogue=lambda acc: (acc,), outs=[(None, F32)])
    conv_w = jnp.pad(gd['conv_w'], ((0, 8 - GDN_CONV), (0, 0)))
    norm_g = gd['norm'].reshape(1, -1)
    prev8 = lambda a: jnp.pad(a, ((0, 0), (8 - (GDN_CONV - 1), 0), (0, 0)))
    q_p, k_p, v_p = _gdn_prep(qkv_raw, jnp.zeros((geo.bp, 8, GDN_QKV), F32), conv_w, nb=geo.bp, t=geo.tp, row0=0)
    q_s, k_s, v_s = _gdn_prep(qkv_raw, prev8(conv_prev), conv_w, nb=geo.bs, t=geo.ts, row0=mp)
    y_p, s_p = _gdn_scan(q_p, k_p, v_p, gb, z, jnp.zeros((geo.bp, GDN_HEADS, GDN_HD, GDN_HD), F32), norm_g,
                         nb=geo.bp, t=geo.tp, gb_row0=0, z_row0=0)
    y_s, s_s = _gdn_scan(q_s, k_s, v_s, gb, z, gdn0, norm_g, nb=geo.bs, t=geo.ts, gb_row0=mp, z_row0=mp)
    o_gdn = jnp.concatenate([y_p, y_s], axis=0)
    nconv = GDN_CONV - 1
    states = dict(
        fk_p=k[:mp].reshape(geo.bp, geo.tp, FOX_HEADS, FOX_HD), fk_s=k[mp:].reshape(geo.bs, geo.ts, FOX_HEADS, FOX_HD),
        fv_p=v[:mp].reshape(geo.bp, geo.tp, FOX_HEADS, FOX_HD), fv_s=v[mp:].reshape(geo.bs, geo.ts, FOX_HEADS, FOX_HD),
        fl_p=logf[:mp, :FOX_HEADS].reshape(geo.bp, geo.tp, FOX_HEADS),
        fl_s=logf[mp:, :FOX_HEADS].reshape(geo.bs, geo.ts, FOX_HEADS),
        cv_p=qkv_raw[:mp].reshape(geo.bp, geo.tp, GDN_QKV)[:, geo.tp - nconv:],
        cv_s=qkv_raw[mp:].reshape(geo.bs, geo.ts, GDN_QKV)[:, geo.ts - nconv:],
        gdn_p=s_p, gdn_s=s_s)
    return o_fox, o_gdn, states


MOE_BLOCK = 256
MOE_COMBINE_ROWS = 128


def _moe_route(x, router, router_bias, *, tm):
    m = x.shape[0]
    assert m % tm == 0
    per_group = N_EXPERTS // N_GROUPS

    def lane_max(v):
        return jnp.max(v, axis=-1, keepdims=True)

    def first_lane(mask, lane):
        return jnp.min(jnp.where(mask, lane.astype(F32), float(LANES)), axis=-1, keepdims=True).astype(jnp.int32)

    def group_all(v, lane, op):
        sh = 1
        while sh < per_group:
            partner = jnp.where((lane & sh) == 0, pltpu.roll(v, LANES - sh, 1), pltpu.roll(v, sh, 1))
            v = op(v, partner)
            sh *= 2
        return v

    def body(x_ref, r_ref, b_ref, e_o, g_o, k_o, cnt_o, carry):
        i = pl.program_id(0)

        @pl.when(i == 0)
        def _():
            carry[...] = jnp.zeros(carry.shape, F32)

        lane = lax.broadcasted_iota(jnp.int32, (tm, LANES), 1)
        valid = lane < N_EXPERTS
        neg = -jnp.inf
        scores = _sigmoid(_dot(x_ref[...], r_ref[...], HP))
        biased = jnp.where(valid, scores + b_ref[...], neg)
        m1 = group_all(biased, lane, jnp.maximum)
        first = group_all(jnp.where(biased == m1, lane, LANES), lane, jnp.minimum)
        m2 = group_all(jnp.where(lane == first, neg, biased), lane, jnp.maximum)
        grp = jnp.where(valid & ((lane & (per_group - 1)) == 0), m1 + m2, neg)
        emask = jnp.zeros((tm, LANES), jnp.bool_)
        for _ in range(TOPK_GROUPS):
            idx = first_lane(grp == lane_max(grp), lane)
            emask = emask | ((lane - idx >= 0) & (lane - idx < per_group))
            grp = jnp.where(lane == idx, neg, grp)
        cur = jnp.where(emask & valid, biased, neg)
        picks = []
        sel = jnp.zeros((tm, LANES), jnp.bool_)
        for _ in range(TOP_K):
            idx = first_lane(cur == lane_max(cur), lane)
            pick = lane == idx
            picks.append((idx, pick))
            sel = sel | pick
            cur = jnp.where(pick, neg, cur)
        selw = jnp.where(sel, scores, 0.0)
        gates_dense = selw / jnp.sum(selw, axis=-1, keepdims=True) * ROUTED_SCALE
        ri = lax.broadcasted_iota(jnp.int32, (tm, tm), 0)
        cj = lax.broadcasted_iota(jnp.int32, (tm, tm), 1)
        p01 = jnp.where(sel, 1.0, 0.0)
        before = _dot(jnp.where(ri > cj, 1.0, 0.0).astype(BF16), p01.astype(BF16)) + carry[...]
        carry[...] = carry[...] + jnp.sum(p01, axis=0, keepdims=True)
        cnt_o[...] = carry[...]
        e_out = jnp.zeros((tm, LANES), jnp.int32)
        g_out = jnp.zeros((tm, LANES), F32)
        k_out = jnp.zeros((tm, LANES), jnp.int32)
        for j, (idx, pick) in enumerate(picks):
            e_out = jnp.where(lane == j, idx, e_out)
            g_out = jnp.where(lane == j, jnp.sum(jnp.where(pick, gates_dense, 0.0), axis=-1, keepdims=True), g_out)
            rank = jnp.sum(jnp.where(pick, before, 0.0), axis=-1, keepdims=True)
            k_out = jnp.where(lane == j, rank.astype(jnp.int32), k_out)
        e_o[...] = e_out
        g_o[...] = g_out
        k_o[...] = k_out

    tok = pl.BlockSpec((tm, LANES), lambda i: (i, 0))
    one = pl.BlockSpec((1, LANES), lambda i: (0, 0))
    return pl.pallas_call(
        body, grid=(m // tm,), name="moe_route",
        in_specs=[pl.BlockSpec((tm, x.shape[1]), lambda i: (i, 0)), pl.BlockSpec(router.shape, lambda i: (0, 0)), one],
        out_specs=[tok, tok, tok, one],
        out_shape=[jax.ShapeDtypeStruct((m, LANES), jnp.int32), jax.ShapeDtypeStruct((m, LANES), F32),
                   jax.ShapeDtypeStruct((m, LANES), jnp.int32), jax.ShapeDtypeStruct((1, LANES), F32)],
        scratch_shapes=[pltpu.VMEM((1, LANES), F32)],
        compiler_params=_cp("arbitrary"))(x, router, router_bias)


def _moe_dispatch(x_words, dest, cap, *, tb):
    m, wd = x_words.shape
    nt = m // tb

    def body(dest_ref, x_ref, out_ref, sem):
        def row_copy(i, j):
            return pltpu.make_async_copy(x_ref.at[pl.ds(i, 1)], out_ref.at[pl.ds(dest_ref[0, 0, i * TOP_K + j], 1)], sem)

        def issue(i, carry):
            for j in range(TOP_K):
                row_copy(i, j).start()
            return carry

        lax.fori_loop(0, tb, issue, 0, unroll=8)
        for _ in range(TOP_K):
            pltpu.make_async_copy(x_ref, out_ref.at[pl.ds(0, tb)], sem).wait()

    return pl.pallas_call(
        body, grid=(nt,), name="moe_dispatch",
        in_specs=[pl.BlockSpec((1, 1, tb * TOP_K), lambda i: (i, 0, 0), memory_space=pltpu.SMEM),
                  pl.BlockSpec((tb, wd), lambda i: (i, 0))],
        out_specs=pl.BlockSpec(memory_space=pl.ANY),
        out_shape=jax.ShapeDtypeStruct((cap, wd), x_words.dtype),
        scratch_shapes=[pltpu.SemaphoreType.DMA(())],
        compiler_params=_cp("arbitrary"))(dest, x_words)


def _pack_halves(y):
    n = y.shape[1] // 2
    bits = lambda a: lax.bitcast_convert_type(a.astype(BF16).astype(F32), jnp.uint32)
    return (bits(y[:, :n]) >> 16) | (bits(y[:, n:]) & jnp.uint32(0xFFFF0000))


def _unpack_halves(w):
    lo = lax.bitcast_convert_type(w << 16, F32).astype(BF16)
    hi = lax.bitcast_convert_type(w & jnp.uint32(0xFFFF0000), F32).astype(BF16)
    return lo, hi


def _ep_ln_packed(acc, x, g, b):
    y, _ = _ep_ln(acc, x, g, b)
    return y, y, _pack_halves(y)


def _moe_experts(xs_words, blk_exp, blk_rows, n_used, w_gate, w_up, w_down):
    cap, half = xs_words.shape
    d = 2 * half
    nblk = cap // MOE_BLOCK
    ed = w_gate.shape[2]

    def body(be_ref, br_ref, nu_ref, x_ref, wg_ref, wu_ref, wd_ref, o_ref, wg_b, wu_b, wd_b):
        i = pl.program_id(0)

        @pl.when(i < nu_ref[0])
        def _():
            @pl.when((i == 0) | (be_ref[i] != be_ref[jnp.maximum(i - 1, 0)]))
            def _():
                wg_b[...] = wg_ref[0].astype(BF16)
                wu_b[...] = wu_ref[0].astype(BF16)
                wd_b[...] = wd_ref[0].astype(BF16)

            row = lax.broadcasted_iota(jnp.int32, (MOE_BLOCK, 1), 0)
            lo, hi = _unpack_halves(jnp.where(row < br_ref[i], x_ref[...], jnp.uint32(0)))
            hg = _dot(lo, wg_b[:half]) + _dot(hi, wg_b[half:])
            hu = _dot(lo, wu_b[:half]) + _dot(hi, wu_b[half:])
            o_ref[...] = _pack_halves(_dot((_silu(hg) * hu).astype(BF16), wd_b[...]))

    blk = lambda i, be, br, nu: (jnp.minimum(i, nu[0] - 1), 0)
    wmap = lambda i, be, br, nu: (be[jnp.minimum(i, nu[0] - 1)], 0, 0)
    grid_spec = pltpu.PrefetchScalarGridSpec(
        num_scalar_prefetch=3, grid=(nblk,),
        in_specs=[pl.BlockSpec((MOE_BLOCK, half), blk), pl.BlockSpec((1, d, ed), wmap), pl.BlockSpec((1, d, ed), wmap),
                  pl.BlockSpec((1, ed, d), wmap)],
        out_specs=pl.BlockSpec((MOE_BLOCK, half), blk),
        scratch_shapes=[pltpu.VMEM((d, ed), BF16), pltpu.VMEM((d, ed), BF16), pltpu.VMEM((ed, d), BF16)])
    return pl.pallas_call(body, grid_spec=grid_spec, out_shape=jax.ShapeDtypeStruct((cap, half), jnp.uint32), name="moe_experts",
                          compiler_params=_cp("arbitrary"))(blk_exp, blk_rows, n_used, xs_words, w_gate, w_up, w_down)


def _moe_combine(ys_words, dest, gates, *, tc):
    cap, half = ys_words.shape
    d = 2 * half
    m = gates.shape[0]
    nt = m // tc

    def body(dest0_ref, destn_ref, g_ref, y_ref, o_ref, buf, sem):
        i = pl.program_id(0)

        def issue(dref, slot):
            def one(r, carry):
                for j in range(TOP_K):
                    pltpu.make_async_copy(y_ref.at[pl.ds(dref[0, 0, r * TOP_K + j], 1)],
                                          buf.at[slot, j, pl.ds(r, 1)], sem.at[slot]).start()
                return carry
            lax.fori_loop(0, tc, one, 0, unroll=8)

        @pl.when(i == 0)
        def _():
            issue(dest0_ref, 0)

        @pl.when(i + 1 < nt)
        def _():
            issue(destn_ref, (i + 1) % 2)

        slot = i % 2
        for j in range(TOP_K):
            pltpu.make_async_copy(y_ref.at[pl.ds(0, tc)], buf.at[slot, j], sem.at[slot]).wait()
        g = g_ref[...]
        acc_lo = acc_hi = None
        for j in range(TOP_K):
            lo, hi = _unpack_halves(buf[slot, j])
            gj = g[:, j:j + 1]
            acc_lo = lo.astype(F32) * gj if acc_lo is None else acc_lo + lo.astype(F32) * gj
            acc_hi = hi.astype(F32) * gj if acc_hi is None else acc_hi + hi.astype(F32) * gj
        o_ref[:, :half] = acc_lo
        o_ref[:, half:] = acc_hi

    return pl.pallas_call(
        body, grid=(nt,), name="moe_combine",
        in_specs=[pl.BlockSpec((1, 1, tc * TOP_K), lambda i: (0, 0, 0), memory_space=pltpu.SMEM),
                  pl.BlockSpec((1, 1, tc * TOP_K), lambda i: (jnp.minimum(i + 1, nt - 1), 0, 0), memory_space=pltpu.SMEM),
                  pl.BlockSpec((tc, LANES), lambda i: (i, 0)),
                  pl.BlockSpec(memory_space=pl.ANY)],
        out_specs=pl.BlockSpec((tc, d), lambda i: (i, 0)),
        out_shape=jax.ShapeDtypeStruct((m, d), F32),
        scratch_shapes=[pltpu.VMEM((2, TOP_K, tc, half), jnp.uint32), pltpu.SemaphoreType.DMA((2,))],
        compiler_params=_cp("arbitrary"))(dest, dest, gates, ys_words)


def _moe_layer(geo, x, xb, x_words, router, router_bias, w_gate, w_up, w_down, ws_gate, ws_up, ws_down, ln_g, ln_b):
    m, d = x.shape
    tm = geo.tm
    eidx, gates, rank, counts = _moe_route(x, _pad_cols(router, LANES), jnp.pad(router_bias, (0, LANES - N_EXPERTS)).reshape(1, -1), tm=tm)
    cnt = counts[0, :N_EXPERTS].astype(jnp.int32)
    padded = (cnt + MOE_BLOCK - 1) // MOE_BLOCK * MOE_BLOCK
    pad_end = jnp.cumsum(padded)
    pad_start = pad_end - padded
    n_blocks = -(-(m * TOP_K) // MOE_BLOCK) + N_EXPERTS
    cap = n_blocks * MOE_BLOCK
    blk_row0 = jnp.arange(n_blocks, dtype=jnp.int32) * MOE_BLOCK
    blk_exp = jnp.minimum(jnp.sum((pad_end[None, :] <= blk_row0[:, None]).astype(jnp.int32), axis=1), N_EXPERTS - 1)
    blk_rows = jnp.clip(cnt[blk_exp] - (blk_row0 - pad_start[blk_exp]), 0, MOE_BLOCK).astype(jnp.int32)
    n_used = (pad_end[-1:] // MOE_BLOCK).astype(jnp.int32)
    dest = pad_start[eidx[:, :TOP_K]] + rank[:, :TOP_K]

    tb = tm
    xs_words = _moe_dispatch(x_words, dest.reshape(m // tb, 1, tb * TOP_K), cap, tb=tb)
    ys = _moe_experts(xs_words, blk_exp, blk_rows, n_used, w_gate, w_up, w_down)
    tc = min(MOE_COMBINE_ROWS, tm)
    routed = _moe_combine(ys, dest.reshape(m // tc, 1, tc * TOP_K), gates, tc=tc)

    sd = ws_gate.shape[1]
    w_sh = jnp.concatenate([ws_gate, ws_up], axis=1).astype(BF16)
    (hs,) = _mm([(xb, w_sh)], tm=tm, tn=2 * sd, epilogue=lambda acc: (_silu(acc[:, :sd]) * acc[:, sd:],), outs=[(sd, BF16)])
    return _mm([(hs, ws_down.astype(BF16))], tm=tm // 2, tn=d,
               aux=[(x, 'rowcol'), (routed, 'rowcol'), (ln_g.reshape(1, -1), 'col'), (ln_b.reshape(1, -1), 'col')],
               epilogue=lambda acc, xr, rt, g, b: _ep_ln(acc + rt, xr, g, b), outs=[(d, F32), (d, BF16)])


def _rwkv_params(li, mu, w0, w2, a0, a2, g2, k_k, k_a, r_k, ln_g, ln_b):
    row = lambda a: a[li].reshape(1, -1)
    padr = lambda a, n: jnp.pad(a, ((0, n - a.shape[0]), (0, 0)))
    return dict(mu=_rwkv_cols_layout(mu[li])[None, :], w0=row(w0), a0=row(a0), k_k=row(k_k), k_a=row(k_a),
                w2=padr(w2[li], LANES), a2=padr(a2[li], LANES), g2=g2[li], r_k=row(r_k), ln_g=row(ln_g), ln_b=row(ln_b))


def _mix_out(geo, x, o_a, o_b, w_out, ln_g, ln_b):
    d = x.shape[1]
    ka = o_a.shape[1]
    return _mm([(o_a, w_out[:ka].astype(BF16)), (o_b, w_out[ka:].astype(BF16))], tm=geo.tm // 2, tn=d,
               aux=[(x, 'rowcol'), (ln_g.reshape(1, -1), 'col'), (ln_b.reshape(1, -1), 'col')],
               epilogue=_ep_ln_packed,
               outs=[(d, F32), (d, BF16), (d // 2, jnp.uint32)], name="mix_out")


def kernel(x_prompt, x_sample, cache_mla_ckv, cache_mla_krope, state_rwkv_shift, state_rwkv_wkv, cache_fox_k, cache_fox_v, cache_fox_logf, state_gdn_conv, state_gdn_wkv, ln1_g, ln1_b, ln2_g, ln2_b, ev_w_in, ev_w_out, mla_kv_norm, mla_w_uk, mla_w_uv, rwkv_mu, rwkv_w0, rwkv_w2, rwkv_a0, rwkv_a2, rwkv_g2, rwkv_k_k, rwkv_k_a, rwkv_r_k, rwkv_ln_g, rwkv_ln_b, od_w_in, od_w_out, fox_q_norm, fox_k_norm, fox_f_bias, gdn_conv_w, gdn_a_log, gdn_dt_bias, gdn_norm, moe_router, moe_router_bias, moe_w_gate, moe_w_up, moe_w_down, moe_ws_gate, moe_ws_up, moe_ws_down):
    bp, tp, d = x_prompt.shape
    bs, ts, _ = x_sample.shape
    geo = _Geom(bp, tp, bs, ts, cache_mla_ckv.shape[2])
    x = jnp.concatenate([x_prompt.reshape(bp * tp, d), x_sample.reshape(bs * ts, d)], axis=0)
    xb = x.astype(BF16)
    rope_tab = _rope_table(geo)
    st = {}
    for layer in range(ln1_g.shape[0]):
        li = layer // 2
        if layer % 2 == 0:
            rw = _rwkv_params(li, rwkv_mu, rwkv_w0, rwkv_w2, rwkv_a0, rwkv_a2, rwkv_g2, rwkv_k_k, rwkv_k_a, rwkv_r_k,
                              rwkv_ln_g, rwkv_ln_b)
            o_a, o_b, new = _even_mixer(geo, xb, rope_tab, ev_w_in[li], mla_kv_norm[li], mla_w_uk[li], mla_w_uv[li], rw,
                                        cache_mla_ckv[li], cache_mla_krope[li], state_rwkv_shift[li], state_rwkv_wkv[li])
            w_out = ev_w_out[li]
        else:
            fx = {'q_norm': fox_q_norm[li], 'k_norm': fox_k_norm[li], 'f_bias': fox_f_bias[li]}
            gd = {'conv_w': gdn_conv_w[li], 'a_log': gdn_a_log[li], 'dt_bias': gdn_dt_bias[li], 'norm': gdn_norm[li]}
            o_a, o_b, new = _odd_mixer(geo, xb, od_w_in[li], fx, gd, cache_fox_k[li], cache_fox_v[li], cache_fox_logf[li],
                                       state_gdn_conv[li], state_gdn_wkv[li])
            w_out = od_w_out[li]
        for name, val in new.items():
            st.setdefault(name, []).append(val)
        x, xb, x_words = _mix_out(geo, x, o_a, o_b, w_out, ln1_g[layer], ln1_b[layer])
        x, xb = _moe_layer(geo, x, xb, x_words, moe_router[layer], moe_router_bias[layer], moe_w_gate[layer], moe_w_up[layer],
                           moe_w_down[layer], moe_ws_gate[layer], moe_ws_up[layer], moe_ws_down[layer],
                           ln2_g[layer], ln2_b[layer])
    names = ('ckv', 'kr', 'sh', 'rwkv', 'fk', 'fv', 'fl', 'cv', 'gdn')
    return ((x[:geo.mp].reshape(bp, tp, d), x[geo.mp:].reshape(bs, ts, d))
            + tuple(jnp.stack(st[n + '_p']) for n in names) + tuple(jnp.stack(st[n + '_s']) for n in names))
```

```python
import functools
import math

import jax
import jax.numpy as jnp
from jax import lax
from jax.experimental import pallas as pl
from jax.experimental.pallas import tpu as pltpu

F32 = jnp.float32
BF16 = jnp.bfloat16
HP = lax.Precision.HIGHEST

D_MODEL = 2048
DEPTH = 2
DN_ALPHA = float((2.0 * DEPTH) ** 0.25)
LN_EPS = 1e-5
RMS_EPS = 1e-6
CHUNK = 64
LANES = 128

MLA_HEADS, MLA_NOPE, MLA_ROPE, MLA_VDIM, MLA_KV_LORA = 8, 128, 64, 128, 512
MLA_QK = MLA_NOPE + MLA_ROPE
ROPE_THETA = 10000.0
RWKV_HEADS, RWKV_HD = 16, 64
RWKV_W = RWKV_HEADS * RWKV_HD
RWKV_DECAY_LORA, RWKV_A_LORA, RWKV_G_LORA = 96, 96, 256
RWKV_GN_EPS = 64e-5
RWKV_PCOLS = 3 * RWKV_W + 2 * LANES + RWKV_G_LORA
FOX_HEADS, FOX_HD = 8, 128
FOX_W = FOX_HEADS * FOX_HD
GDN_HEADS, GDN_HD, GDN_CONV = 8, 128, 4
GDN_W = GDN_HEADS * GDN_HD
GDN_QKV = 3 * GDN_W
N_EXPERTS, TOP_K, N_GROUPS, TOPK_GROUPS = 64, 6, 8, 4
EXPERT_DIM, SHARED_DIM = 512, 512
ROUTED_SCALE = 2.5
SCAN_CHUNK = 64
FLASH_KEY_SUBTILE = 1024
NEG_BIG = -1e30
RWKV_PASSES = dict(a=1, x=3, inv=1, u=1, y=1, s=3)
GDN_PASSES = dict(a=1, x=1, inv=1, u=1, y=1, s=1)

VMEM_LIMIT_BYTES = 56 * 1024 * 1024
MM_MAX_COLS = 1792


def _cp(*sem):
    return pltpu.CompilerParams(dimension_semantics=("arbitrary",) * len(sem), vmem_limit_bytes=VMEM_LIMIT_BYTES)


def _dot(a, b, prec=None):
    return jnp.dot(a, b, precision=prec, preferred_element_type=F32)


def _dot_nt(a, b, prec=None):
    return lax.dot_general(a, b, (((1,), (1,)), ((), ())), precision=prec, preferred_element_type=F32)


def _dot_tn(a, b, prec=None):
    return lax.dot_general(a, b, (((0,), (0,)), ((), ())), precision=prec, preferred_element_type=F32)


def _sigmoid(x):
    return 1.0 / (1.0 + jnp.exp(-x))


def _softplus(x):
    return jnp.maximum(x, 0.0) + jnp.log(1.0 + jnp.exp(-jnp.abs(x)))


def _silu(x):
    return x * _sigmoid(x)


def _mm(pairs, *, tm, tn, epilogue, outs, aux=(), name="mm"):
    m, n = pairs[0][0].shape[0], pairs[0][1].shape[1]
    if tn is None:
        tn = n if n <= MM_MAX_COLS else max(c for c in range(LANES, MM_MAX_COLS + 1, LANES) if n % c == 0)
    outs = [(tn if ow is None else ow, dt) for ow, dt in outs]
    assert m % tm == 0 and n % tn == 0, (m, n, tm, tn)
    nj, ni = n // tn, m // tm
    in_specs, args = [], []
    for a, w in pairs:
        k = a.shape[1]
        in_specs += [pl.BlockSpec((tm, k), lambda j, i: (i, 0)), pl.BlockSpec((k, tn), lambda j, i: (0, j))]
        args += [a, w]
    for arr, kind in aux:
        if kind == 'row':
            in_specs.append(pl.BlockSpec((tm, arr.shape[1]), lambda j, i: (i, 0)))
        elif kind == 'rowcol':
            in_specs.append(pl.BlockSpec((tm, tn), lambda j, i: (i, j)))
        elif kind == 'row0':
            in_specs.append(pl.BlockSpec(arr.shape, lambda j, i: (0, 0)))
        else:
            in_specs.append(pl.BlockSpec((1, tn), lambda j, i: (0, j)))
        args.append(arr)
    out_shape = [jax.ShapeDtypeStruct((m, ow * nj), dt) for ow, dt in outs]
    out_specs = [pl.BlockSpec((tm, ow), lambda j, i: (i, j)) for ow, dt in outs]
    n_pairs, n_aux = len(pairs), len(aux)

    def body(*refs):
        acc = None
        for p in range(n_pairs):
            d = _dot(refs[2 * p][...], refs[2 * p + 1][...])
            acc = d if acc is None else acc + d
        res = epilogue(acc, *[r[...] for r in refs[2 * n_pairs:2 * n_pairs + n_aux]])
        for o_ref, val in zip(refs[2 * n_pairs + n_aux:], res):
            o_ref[...] = val.astype(o_ref.dtype)

    res = pl.pallas_call(body, grid=(nj, ni), in_specs=in_specs, out_specs=out_specs, out_shape=out_shape, name=name,
                         compiler_params=_cp("parallel", "parallel"))(*args)
    return res


def _mm_t(w_t, a, *, nb, t, row0, name="mm_t"):
    n, k = w_t.shape
    tm = next((c for c in (512, 384, 256, 128) if t % c == 0), t)
    assert t % tm == 0 and row0 % tm == 0
    nt = t // tm

    def body(w_ref, a_ref, o_ref):
        o_ref[...] = _dot_nt(w_ref[...], a_ref[...]).astype(o_ref.dtype)

    return pl.pallas_call(
        body, grid=(nb, nt), name=name,
        in_specs=[pl.BlockSpec((n, k), lambda b, i: (0, 0)), pl.BlockSpec((tm, k), lambda b, i: (row0 // tm + b * nt + i, 0))],
        out_specs=pl.BlockSpec((n, tm), lambda b, i: (b, i)),
        out_shape=jax.ShapeDtypeStruct((nb * n, t), BF16),
        compiler_params=_cp("parallel", "parallel"))(w_t, a)


def _rope_lanes(x, tab):
    c, s1, s2 = tab[:, :LANES], tab[:, LANES:2 * LANES], tab[:, 2 * LANES:]
    n = x.shape[1]
    lo = pltpu.roll(x, n - MLA_ROPE // 2, 1)
    hi = pltpu.roll(x, MLA_ROPE // 2, 1)
    if n > LANES:
        reps = n // LANES
        c, s1, s2 = (jnp.concatenate([t] * reps, axis=1) for t in (c, s1, s2))
    return x * c + lo * s1 + hi * s2


def _ep_rms(acc, g):
    y = acc * lax.rsqrt(jnp.mean(acc * acc, axis=-1, keepdims=True) + RMS_EPS) * g
    return y, y


def _ep_ln(acc, x, g, b):
    z = DN_ALPHA * x.astype(F32) + acc
    mu = jnp.mean(z, axis=-1, keepdims=True)
    zc = z - mu
    var = jnp.mean(zc * zc, axis=-1, keepdims=True)
    return zc * lax.rsqrt(var + LN_EPS) * g + b


def _flash(qs, ks, vt, *, nb, tq_total, tk_total, q_row0, q_off, tq, tk, chunked, nh, bias=None, gate=None,
           out_dtype=BF16, name="flash"):
    assert tq_total % tq == 0 and tk_total % tk == 0 and q_row0 % tq == 0
    nq, nk = tq_total // tq, tk_total // tk
    qrb0 = q_row0 // tq
    ts = tk if tk % FLASH_KEY_SUBTILE else min(tk, FLASH_KEY_SUBTILE)
    for qi_s in range(nq):
        last_s = min(nk - 1, (q_off + qi_s * tq + tq - 1) // tk)
        assert last_s * tk <= q_off + qi_s * tq + 1, "key tiles before the last needed one must be fully visible"
    n_q, n_k = len(qs), len(ks)
    has_bias, has_gate = bias is not None, gate is not None

    def last_k(qi):
        return jnp.minimum(nk - 1, (q_off + qi * tq + tq - 1) // tk)

    kblk = lambda qi, ki: jnp.minimum(ki, last_k(qi))
    q_map = lambda b, h, qi, ki: (qrb0 + b * nq + qi, h)
    in_specs, args = [], []
    for q in qs:
        in_specs.append(pl.BlockSpec((tq, LANES), q_map))
        args.append(q)
    for arr, c0, per_head in ks:
        if per_head:
            in_specs.append(pl.BlockSpec((tk, LANES), lambda b, h, qi, ki, c0=c0: (b * nk + kblk(qi, ki), c0 + h)))
        else:
            in_specs.append(pl.BlockSpec((tk, LANES), lambda b, h, qi, ki, c0=c0: (b * nk + kblk(qi, ki), c0)))
        args.append(arr)
    in_specs.append(pl.BlockSpec((LANES, tk), lambda b, h, qi, ki: (b * nh + h, kblk(qi, ki))))
    args.append(vt)
    if has_bias:
        in_specs.append(pl.BlockSpec((1, 1, 1, tq), lambda b, h, qi, ki: (b, h, 0, qi)))
        in_specs.append(pl.BlockSpec((1, 1, tk, 1), lambda b, h, qi, ki: (b, h, kblk(qi, ki), 0)))
        args += list(bias)
    if has_gate:
        in_specs.append(pl.BlockSpec((tq, LANES), q_map))
        args.append(gate)

    def body(*refs):
        q_refs = refs[:n_q]
        k_refs = refs[n_q:n_q + n_k]
        vt_ref = refs[n_q + n_k]
        pos = n_q + n_k + 1
        if has_bias:
            qb_ref, kb_ref = refs[pos], refs[pos + 1]
            pos += 2
        if has_gate:
            gate_ref = refs[pos]
            pos += 1
        o_ref, m_ref, l_ref, acc_ref = refs[pos:pos + 4]
        qi, ki = pl.program_id(2), pl.program_id(3)

        @pl.when(ki == 0)
        def _():
            m_ref[...] = jnp.full(m_ref.shape, NEG_BIG, F32)
            l_ref[...] = jnp.zeros(l_ref.shape, F32)
            acc_ref[...] = jnp.zeros(acc_ref.shape, F32)

        def step(masked):
            q_all = jnp.concatenate([q_ref[...] for q_ref in q_refs], axis=1) if n_q > 1 else q_refs[0][...]
            qpos = q_off + qi * tq + lax.broadcasted_iota(jnp.int32, (1, tq), 1)
            qlim = (qpos | (CHUNK - 1)) if chunked else qpos
            subs = [slice(j * ts, (j + 1) * ts) for j in range(tk // ts)]
            ss = []
            for j, rows in enumerate(subs):
                k_all = jnp.concatenate([k_ref[rows, :] for k_ref in k_refs], axis=1) if n_k > 1 else k_refs[0][rows, :]
                s = _dot_nt(k_all, q_all)
                if has_bias:
                    s = s + (qb_ref[0, 0] - kb_ref[0, 0, rows, :])
                if masked:
                    kpos = ki * tk + j * ts + lax.broadcasted_iota(jnp.int32, (ts, 1), 0)
                    s = jnp.where(kpos <= qlim, s, NEG_BIG)
                ss.append(s)
            m_old = m_ref[...]
            m_new = m_old
            for s in ss:
                m_new = jnp.maximum(m_new, jnp.max(s, axis=0, keepdims=True))
            alpha = jnp.exp(m_old - m_new)
            l_new = alpha * l_ref[...]
            acc_new = alpha * acc_ref[...]
            for rows, s in zip(subs, ss):
                p = jnp.exp(s - m_new)
                l_new = l_new + jnp.sum(p, axis=0, keepdims=True)
                acc_new = acc_new + _dot(vt_ref[:, rows], p.astype(BF16))
            m_ref[...] = m_new
            l_ref[...] = l_new
            acc_ref[...] = acc_new

        @pl.when(ki < last_k(qi))
        def _():
            step(False)

        @pl.when(ki == last_k(qi))
        def _():
            step(True)

        @pl.when(ki == nk - 1)
        def _():
            o = (acc_ref[...] / l_ref[...]).T
            if has_gate:
                o = o * gate_ref[...]
            o_ref[...] = o.astype(o_ref.dtype)

    return pl.pallas_call(
        body, grid=(nb, nh, nq, nk), in_specs=in_specs, name=name,
        out_specs=pl.BlockSpec((tq, LANES), lambda b, h, qi, ki: (b * nq + qi, h)),
        out_shape=jax.ShapeDtypeStruct((nb * tq_total, nh * LANES), out_dtype),
        scratch_shapes=[pltpu.VMEM((1, tq), F32), pltpu.VMEM((1, tq), F32), pltpu.VMEM((LANES, tq), F32)],
        compiler_params=_cp("parallel", "parallel", "parallel", "arbitrary"))(*args)


def _head_sum_matrices(n_heads, hd):
    lane = jnp.arange(n_heads * hd)[:, None] // hd
    e = (lane == jnp.arange(LANES)[None, :]).astype(F32)
    return e, e.T


def _rwkv_prep(cols, shift_prev, prm, *, nb, t, row0):
    tt = min(256, t)
    assert t % tt == 0 and row0 % tt == 0
    nt = t // tt
    rb0 = row0 // tt
    w = RWKV_W

    def body(c_ref, sp_ref, mu_ref, w0_ref, a0_ref, kk_ref, ka_ref, w2_ref, a2_ref, g2_ref, e_ref, et_ref,
             r_o, k_o, v_o, kk_o, b_o, lw_o, g_o, sh_o, carry):
        ti = pl.program_id(1)

        @pl.when(ti == 0)
        def _():
            carry[...] = sp_ref[0]

        c = c_ref[...]
        row = lax.broadcasted_iota(jnp.int32, (tt, 1), 0)
        prev = jnp.where(row == 0, carry[...], pltpu.roll(c, 1, 0))
        carry[...] = c[tt - 1:tt, :]
        sh_o[0] = c[tt - 1:tt, :]
        xs = c + (prev - c) * mu_ref[...]
        r, k, v = xs[:, :w], xs[:, w:2 * w], xs[:, 2 * w:3 * w]
        o1 = 3 * w
        lora = lambda u, w_ref: _dot(u.astype(BF16), w_ref[...].astype(BF16))
        w_raw = w0_ref[...] + lora(jnp.tanh(xs[:, o1:o1 + LANES]), w2_ref)
        log_w = -jnp.exp(-_softplus(-w_raw) - 0.5)
        a = _sigmoid(a0_ref[...] + lora(xs[:, o1 + LANES:o1 + 2 * LANES], a2_ref))
        g = lora(_sigmoid(xs[:, o1 + 2 * LANES:]), g2_ref)

        def head_dot(u, sel_ref):
            hi, lo = _split_bf16(u)
            sel = sel_ref[...].astype(BF16)
            return _dot(hi.astype(BF16), sel) + _dot(lo.astype(BF16), sel)

        kk = k * kk_ref[...]
        kk = kk * head_dot(lax.rsqrt(head_dot(kk * kk, e_ref) + 1e-6), et_ref)
        r_o[...] = r
        k_o[...] = k * (1.0 + (a - 1.0) * ka_ref[...])
        v_o[...] = v
        kk_o[...] = kk
        b_o[...] = kk * a
        lw_o[...] = log_w
        g_o[...] = g

    e, et = _head_sum_matrices(RWKV_HEADS, RWKV_HD)
    tok = pl.BlockSpec((tt, w), lambda b, ti: (b * nt + ti, 0))
    full = lambda arr: pl.BlockSpec(arr.shape, lambda b, ti: (0,) * arr.ndim)
    params = [prm['mu'], prm['w0'], prm['a0'], prm['k_k'], prm['k_a'], prm['w2'], prm['a2'], prm['g2'], e, et]
    outs = pl.pallas_call(
        body, grid=(nb, nt), name="rwkv_prep",
        in_specs=[pl.BlockSpec((tt, RWKV_PCOLS), lambda b, ti: (rb0 + b * nt + ti, 0)),
                  pl.BlockSpec((1, 1, RWKV_PCOLS), lambda b, ti: (b, 0, 0))] + [full(p) for p in params],
        out_specs=[tok] * 7 + [pl.BlockSpec((1, 1, RWKV_PCOLS), lambda b, ti: (b, 0, 0))],
        out_shape=[jax.ShapeDtypeStruct((nb * t, w), F32)] * 7 + [jax.ShapeDtypeStruct((nb, 1, RWKV_PCOLS), F32)],
        scratch_shapes=[pltpu.VMEM((1, RWKV_PCOLS), F32)],
        compiler_params=_cp("parallel", "arbitrary"))(cols, shift_prev, *params)
    return outs


def _cumsum_rows(x, n):
    row = lax.broadcasted_iota(jnp.int32, (n, 1), 0)
    sh = 1
    while sh < n:
        x = x + jnp.where(row >= sh, pltpu.roll(x, sh, 0), 0.0)
        sh *= 2
    return x


def _split_bf16(x):
    hi = x.astype(BF16).astype(F32)
    return hi, x - hi


def _bmm(a, b, ca, cb, passes):
    dn = (((ca,), (cb,)), ((0,), (0,)))
    if passes == 6:
        return lax.dot_general(a, b, dn, precision=HP, preferred_element_type=F32)
    if passes == 3:
        ah, al = _split_bf16(a)
        bh, bl = _split_bf16(b)
        a = jnp.concatenate([ah, ah, al], axis=ca)
        b = jnp.concatenate([bh, bl, bh], axis=cb)
    return lax.dot_general(a.astype(BF16), b.astype(BF16), dn, preferred_element_type=F32)


def _heads(x, nh, hd):
    return jnp.stack([x[:, h * hd:(h + 1) * hd] for h in range(nh)], axis=0)


def _unheads(x):
    return jnp.concatenate([x[h] for h in range(x.shape[0])], axis=1)


def _unit_lower_inverse(nmat, n, passes):
    eye = (lax.broadcasted_iota(jnp.int32, (n, n), 0) == lax.broadcasted_iota(jnp.int32, (n, n), 1)).astype(F32)
    t = eye + nmat
    p = nmat
    steps = int(math.ceil(math.log2(n))) - 1
    for _ in range(steps):
        p = _bmm(p, p, 2, 1, passes)
        t = t + _bmm(p, t, 2, 1, passes)
    return t


def _rwkv_scan(r, k, v, kk, bvec, logw, g, s0, prm, *, nb, t):
    c = min(SCAN_CHUNK, t)
    assert t % c == 0
    nc = t // c
    hd, nh = RWKV_HD, RWKV_HEADS

    def body(r_ref, k_ref, v_ref, kk_ref, b_ref, lw_ref, g_ref, s0_ref, rk_ref, lg_ref, lb_ref, o_ref, sf_ref, st):
        ci = pl.program_id(1)

        @pl.when(ci == 0)
        def _():
            st[...] = s0_ref[0]

        lw = lw_ref[...]
        lc = _cumsum_rows(lw, c)
        rr, kx, vv, bb = r_ref[...], k_ref[...], v_ref[...], b_ref[...]
        g_inv = jnp.exp(-lc)
        g_end = jnp.exp(lc[c - 1:c, :] - lc)
        hs = lambda a: _heads(a, nh, hd)
        at, rt = hs(-kk_ref[...] * jnp.exp(lc - lw)), hs(rr * jnp.exp(lc))
        bt, kt = hs(bb * g_inv), hs(kx * g_inv)
        bg, kg = hs(bb * g_end), hs(kx * g_end)
        w_end = hs(jnp.exp(lc[c - 1:c, :]))
        v_h = hs(vv)
        ri = lax.broadcasted_iota(jnp.int32, (c, c), 0)
        cj = lax.broadcasted_iota(jnp.int32, (c, c), 1)
        strict = ri > cj
        incl2 = lax.broadcasted_iota(jnp.int32, (c, 2 * c), 0) >= (lax.broadcasted_iota(jnp.int32, (c, 2 * c), 1) & (c - 1))
        s_all = st[...]
        pp = RWKV_PASSES
        lhs = jnp.concatenate([at, rt], axis=1)
        a_all = _bmm(lhs, jnp.concatenate([bt, kt], axis=1), 2, 2, pp['a'])
        x_all = _bmm(lhs, s_all, 2, 2, pp['x'])
        tinv = _unit_lower_inverse(jnp.where(strict, a_all[:, :c, :c], 0.0), c, pp['inv'])
        a_ak = jnp.where(strict, a_all[:, :c, c:], 0.0)
        u = _bmm(tinv, x_all[:, :c] + _bmm(a_ak, v_h, 2, 1, pp['u']), 2, 1, pp['u'])
        uv = jnp.concatenate([u, v_h], axis=1)
        y = x_all[:, c:] + _bmm(jnp.where(incl2, a_all[:, c:, :], 0.0), uv, 2, 1, pp['y'])
        st[...] = s_all * w_end + _bmm(uv, jnp.concatenate([bg, kg], axis=1), 1, 1, pp['s'])
        mu = jnp.mean(y, axis=-1, keepdims=True)
        yc = y - mu
        var = jnp.mean(yc * yc, axis=-1, keepdims=True)
        yn = yc * lax.rsqrt(var + RWKV_GN_EPS) * hs(lg_ref[...]) + hs(lb_ref[...])
        bonus = jnp.sum(hs(rr * kx * rk_ref[...]), axis=-1, keepdims=True) * v_h
        o_ref[...] = (_unheads(yn + bonus) * g_ref[...]).astype(o_ref.dtype)

        @pl.when(ci == nc - 1)
        def _():
            sf_ref[0] = st[...]

    tok = pl.BlockSpec((c, RWKV_W), lambda b, ci: (b * nc + ci, 0))
    stt = pl.BlockSpec((1, nh, hd, hd), lambda b, ci: (b, 0, 0, 0))
    par = pl.BlockSpec((1, RWKV_W), lambda b, ci: (0, 0))
    return pl.pallas_call(
        body, grid=(nb, nc), name="rwkv_scan",
        in_specs=[tok] * 7 + [stt, par, par, par],
        out_specs=[tok, stt],
        out_shape=[jax.ShapeDtypeStruct((nb * t, RWKV_W), BF16), jax.ShapeDtypeStruct((nb, nh, hd, hd), F32)],
        scratch_shapes=[pltpu.VMEM((nh, hd, hd), F32)],
        compiler_params=_cp("parallel", "arbitrary"))(r, k, v, kk, bvec, logw, g, s0, prm['r_k'], prm['ln_g'], prm['ln_b'])


class _Geom:
    def __init__(self, bp, tp, bs, ts, past):
        self.bp, self.tp, self.bs, self.ts, self.past = bp, tp, bs, ts, past
        self.mp, self.ms = bp * tp, bs * ts
        self.m = self.mp + self.ms
        self.tm = math.gcd(self.m, 512)
        assert self.tm % 16 == 0 and tp % CHUNK == 0 and past % CHUNK == 0 and ts == CHUNK
        self.tq_p = min(1024, tp)
        self.kpad = -(-(past + ts) // (3 * LANES)) * (3 * LANES)
        self.tk_s = self.kpad // 3


def _pad_cols(w, n):
    return jnp.pad(w, ((0, 0), (0, n - w.shape[1])))


def _rope_table(geo):
    half = MLA_ROPE // 2
    pos = jnp.concatenate([jnp.tile(jnp.arange(geo.tp, dtype=jnp.int32), geo.bp),
                           geo.past + jnp.tile(jnp.arange(geo.ts, dtype=jnp.int32), geo.bs)])
    inv_freq = ROPE_THETA ** (-jnp.arange(half, dtype=F32) / half)
    ang = pos.astype(F32)[:, None] * inv_freq[None, :]
    cos, sin = jnp.cos(ang), jnp.sin(ang)
    z = jnp.zeros_like(cos)
    zz = jnp.zeros((pos.shape[0], LANES - MLA_ROPE), F32)
    return jnp.concatenate([cos, cos, zz, -sin, z, zz, z, sin, zz], axis=1)


def _rwkv_cols_layout(a):
    o1 = 3 * RWKV_W
    o2 = o1 + RWKV_DECAY_LORA
    o3 = o2 + RWKV_A_LORA
    pad = [(0, 0)] * (a.ndim - 1)
    return jnp.concatenate([a[..., :o1],
                            jnp.pad(a[..., o1:o2], pad + [(0, LANES - RWKV_DECAY_LORA)]),
                            jnp.pad(a[..., o2:o3], pad + [(0, LANES - RWKV_A_LORA)]),
                            a[..., o3:]], axis=-1)


def _rwkv_cols_unlayout(a):
    o1 = 3 * RWKV_W
    return jnp.concatenate([a[..., :o1], a[..., o1:o1 + RWKV_DECAY_LORA],
                            a[..., o1 + LANES:o1 + LANES + RWKV_A_LORA], a[..., o1 + 2 * LANES:]], axis=-1)


def _even_mixer(geo, xb, rope_tab, w_in, kv_norm, w_uk, w_uv, rw, cache_ckv, cache_kr, shift_prev, wkv0):
    tm, m, mp = geo.tm, geo.m, geo.mp
    d = w_in.shape[0]
    nq = MLA_HEADS * MLA_QK
    wq = w_in[:, :nq].reshape(d, MLA_HEADS, MLA_QK)
    w_qn = wq[:, :, :MLA_NOPE].reshape(d, MLA_HEADS * MLA_NOPE).astype(BF16)
    w_qr = jnp.pad(wq[:, :, MLA_NOPE:], ((0, 0), (0, 0), (0, LANES - MLA_ROPE))).reshape(d, MLA_HEADS * LANES).astype(BF16)
    w_ckv = w_in[:, nq:nq + MLA_KV_LORA].astype(BF16)
    w_kr = _pad_cols(w_in[:, nq + MLA_KV_LORA:nq + MLA_KV_LORA + MLA_ROPE], LANES).astype(BF16)
    w_rw = _rwkv_cols_layout(w_in[:, nq + MLA_KV_LORA + MLA_ROPE:]).astype(BF16)
    scale = MLA_QK ** -0.5

    (qn,) = _mm([(xb, w_qn)], tm=tm, tn=None, epilogue=lambda acc: (acc * scale,), outs=[(None, BF16)])
    (qr,) = _mm([(xb, w_qr)], tm=tm, tn=MLA_HEADS * LANES, aux=[(rope_tab, 'row')],
                epilogue=lambda acc, tab: (_rope_lanes(acc, tab) * scale,), outs=[(MLA_HEADS * LANES, BF16)])
    ckv, ckv_b = _mm([(xb, w_ckv)], tm=tm, tn=MLA_KV_LORA, aux=[(kv_norm.reshape(1, -1), 'col')],
                     epilogue=_ep_rms, outs=[(MLA_KV_LORA, F32), (MLA_KV_LORA, BF16)])
    kr, kr_b = _mm([(xb, w_kr)], tm=tm, tn=LANES, aux=[(rope_tab, 'row')],
                   epilogue=lambda acc, tab: (_rope_lanes(acc, tab),) * 2, outs=[(LANES, F32), (LANES, BF16)])
    (rwc,) = _mm([(xb, w_rw)], tm=tm, tn=None, epilogue=lambda acc: (acc,), outs=[(None, F32)])

    w_k = w_uk.reshape(MLA_KV_LORA, -1).astype(BF16)
    w_vt = w_uv.reshape(MLA_KV_LORA, -1).T.astype(BF16)
    (kn_new,) = _mm([(ckv_b, w_k)], tm=tm, tn=None, epilogue=lambda acc: (acc,), outs=[(None, BF16)])
    vt_p = _mm_t(w_vt, ckv_b, nb=geo.bp, t=geo.tp, row0=0)
    o_p = _flash([qn, qr], [(kn_new, 0, True), (kr_b, 0, False)], vt_p,
                 nb=geo.bp, tq_total=geo.tp, tk_total=geo.tp, q_row0=0, q_off=0, tq=geo.tq_p, tk=geo.tq_p,
                 chunked=True, nh=MLA_HEADS)
    padk = geo.kpad - geo.past - geo.ts
    ckv_all = jnp.concatenate([cache_ckv.astype(BF16), ckv_b[mp:].reshape(geo.bs, geo.ts, -1),
                               jnp.zeros((geo.bs, padk, MLA_KV_LORA), BF16)], axis=1).reshape(geo.bs * geo.kpad, -1)
    kr_all = jnp.concatenate([jnp.pad(cache_kr, ((0, 0), (0, 0), (0, LANES - MLA_ROPE))).astype(BF16),
                              kr_b[mp:].reshape(geo.bs, geo.ts, -1),
                              jnp.zeros((geo.bs, padk, LANES), BF16)], axis=1).reshape(geo.bs * geo.kpad, -1)
    (kn_all,) = _mm([(ckv_all, w_k)], tm=3 * LANES, tn=None, epilogue=lambda acc: (acc,), outs=[(None, BF16)])
    vt_all = _mm_t(w_vt, ckv_all, nb=geo.bs, t=geo.kpad, row0=0)
    o_s = _flash([qn, qr], [(kn_all, 0, True), (kr_all, 0, False)], vt_all,
                 nb=geo.bs, tq_total=geo.ts, tk_total=geo.kpad, q_row0=mp, q_off=geo.past, tq=geo.ts, tk=geo.tk_s,
                 chunked=True, nh=MLA_HEADS)
    o_mla = jnp.concatenate([o_p, o_s], axis=0)

    outs_p = _rwkv_prep(rwc, jnp.zeros((geo.bp, 1, RWKV_PCOLS), F32), rw, nb=geo.bp, t=geo.tp, row0=0)
    outs_s = _rwkv_prep(rwc, _rwkv_cols_layout(shift_prev)[:, None, :], rw, nb=geo.bs, t=geo.ts, row0=mp)
    y_p, wkv_p = _rwkv_scan(*outs_p[:7], jnp.zeros((geo.bp, RWKV_HEADS, RWKV_HD, RWKV_HD), F32), rw, nb=geo.bp, t=geo.tp)
    y_s, wkv_s = _rwkv_scan(*outs_s[:7], wkv0, rw, nb=geo.bs, t=geo.ts)
    o_rwkv = jnp.concatenate([y_p, y_s], axis=0)
    states = dict(
        ckv_p=ckv[:mp].reshape(geo.bp, geo.tp, -1), ckv_s=ckv[mp:].reshape(geo.bs, geo.ts, -1),
        kr_p=kr[:mp, :MLA_ROPE].reshape(geo.bp, geo.tp, -1), kr_s=kr[mp:, :MLA_ROPE].reshape(geo.bs, geo.ts, -1),
        sh_p=_rwkv_cols_unlayout(outs_p[7][:, 0]), sh_s=_rwkv_cols_unlayout(outs_s[7][:, 0]),
        rwkv_p=wkv_p, rwkv_s=wkv_s)
    return o_mla, o_rwkv, states


def _head_rms(x, g, nh, scale=1.0):
    outs = []
    for h in range(nh):
        seg = x[:, h * LANES:(h + 1) * LANES]
        outs.append(seg * lax.rsqrt(jnp.mean(seg * seg, axis=-1, keepdims=True) + RMS_EPS) * (g * scale))
    return jnp.concatenate(outs, axis=1)


def _head_l2(x, nh, scale=1.0):
    outs = []
    for h in range(nh):
        seg = x[:, h * LANES:(h + 1) * LANES]
        outs.append(seg * (lax.rsqrt(jnp.sum(seg * seg, axis=-1, keepdims=True) + 1e-6) * scale))
    return jnp.concatenate(outs, axis=1)


def _cumsum_time(x):
    nb, t, w = x.shape

    def body(x_ref, o_ref):
        o_ref[0] = _cumsum_rows(x_ref[0], t)

    spec = pl.BlockSpec((1, t, w), lambda b: (b, 0, 0))
    return pl.pallas_call(body, grid=(nb,), in_specs=[spec], out_specs=spec, name="cumsum_time",
                          out_shape=jax.ShapeDtypeStruct(x.shape, F32), compiler_params=_cp("parallel"))(x)


def _gdn_prep(cols, conv_prev, conv_w, *, nb, t, row0):
    tt = min(256, t)
    assert t % tt == 0 and row0 % tt == 0
    nt = t // tt
    rb0 = row0 // tt
    w = GDN_W

    def body(c_ref, cp_ref, cw_ref, q_o, k_o, v_o, carry):
        ti = pl.program_id(1)

        @pl.when(ti == 0)
        def _():
            carry[...] = cp_ref[0]

        c = c_ref[...]
        ext = jnp.concatenate([carry[...], c], axis=0)
        carry[...] = c[tt - 8:tt, :]
        cw = cw_ref[...]
        acc = c * cw[GDN_CONV - 1:GDN_CONV, :]
        for j in range(GDN_CONV - 1):
            back = GDN_CONV - 1 - j
            acc = acc + ext[8 - back:8 - back + tt, :] * cw[j:j + 1, :]
        qkv = _silu(acc)
        q_o[...] = _head_l2(qkv[:, :w], GDN_HEADS, GDN_HD ** -0.5)
        k_o[...] = _head_l2(qkv[:, w:2 * w], GDN_HEADS)
        v_o[...] = qkv[:, 2 * w:]

    tok = pl.BlockSpec((tt, w), lambda b, ti: (b * nt + ti, 0))
    return pl.pallas_call(
        body, grid=(nb, nt), name="gdn_prep",
        in_specs=[pl.BlockSpec((tt, GDN_QKV), lambda b, ti: (rb0 + b * nt + ti, 0)),
                  pl.BlockSpec((1, 8, GDN_QKV), lambda b, ti: (b, 0, 0)),
                  pl.BlockSpec((8, GDN_QKV), lambda b, ti: (0, 0))],
        out_specs=[tok] * 3,
        out_shape=[jax.ShapeDtypeStruct((nb * t, w), F32)] * 3,
        scratch_shapes=[pltpu.VMEM((8, GDN_QKV), F32)],
        compiler_params=_cp("parallel", "arbitrary"))(cols, conv_prev, conv_w)


def _gdn_scan(q, k, v, gb, z, s0, norm_g, *, nb, t, gb_row0, z_row0):
    c = min(SCAN_CHUNK, t)
    assert t % c == 0 and gb_row0 % c == 0 and z_row0 % c == 0
    nc = t // c
    nh, hd = GDN_HEADS, GDN_HD

    def body(q_ref, k_ref, v_ref, gb_ref, z_ref, s0_ref, ng_ref, o_ref, sf_ref, st):
        ci = pl.program_id(1)

        @pl.when(ci == 0)
        def _():
            st[...] = s0_ref[0]

        gbv = gb_ref[...]
        gc = _cumsum_rows(gbv, c)
        gct = gc.T
        ri = lax.broadcasted_iota(jnp.int32, (c, c), 0)
        cj = lax.broadcasted_iota(jnp.int32, (c, c), 1)
        tril, strict = ri >= cj, ri > cj
        hs = lambda a: _heads(a, nh, hd)
        k_h, q_h, v_h = hs(k_ref[...]), hs(q_ref[...]), hs(v_ref[...])
        gcol = jnp.stack([gc[:, h:h + 1] for h in range(nh)], axis=0)
        grow = gct[:nh][:, None, :]
        bcol = jnp.stack([gbv[:, nh + h:nh + h + 1] for h in range(nh)], axis=0)
        decay = jnp.where(tril, jnp.exp(jnp.where(tril, gcol - grow, 0.0)), 0.0)
        pp = GDN_PASSES
        kb = k_h * bcol
        prods = _bmm(jnp.concatenate([kb, q_h], axis=1), k_h, 2, 2, pp['a'])
        low = jnp.where(strict, prods[:, :c] * decay, 0.0)
        a_qk = jnp.where(tril, prods[:, c:] * decay, 0.0)
        tinv = _unit_lower_inverse(-low, c, pp['inv'])
        e_g = jnp.exp(gcol)
        uw = _bmm(tinv, jnp.concatenate([v_h * bcol, kb * e_g], axis=2), 2, 1, pp['u'])
        s_all = st[...]
        ws_qs = _bmm(jnp.concatenate([uw[:, :, hd:], q_h * e_g], axis=1), s_all, 2, 1, pp['x'])
        v_new = uw[:, :, :hd] - ws_qs[:, :c]
        o = ws_qs[:, c:] + _bmm(a_qk, v_new, 2, 1, pp['y'])
        g_last = gcol[:, c - 1:c, :]
        st[...] = s_all * jnp.exp(g_last) + _bmm(k_h * jnp.exp(g_last - gcol), v_new, 1, 1, pp['s'])
        o = o * lax.rsqrt(jnp.mean(o * o, axis=-1, keepdims=True) + RMS_EPS) * ng_ref[...]
        o_ref[...] = (_unheads(o) * _silu(z_ref[...])).astype(o_ref.dtype)

        @pl.when(ci == nc - 1)
        def _():
            sf_ref[0] = st[...]

    tok = pl.BlockSpec((c, GDN_W), lambda b, ci: (b * nc + ci, 0))
    stt = pl.BlockSpec((1, nh, hd, hd), lambda b, ci: (b, 0, 0, 0))
    return pl.pallas_call(
        body, grid=(nb, nc), name="gdn_scan",
        in_specs=[tok] * 3 + [pl.BlockSpec((c, LANES), lambda b, ci: (gb_row0 // c + b * nc + ci, 0)),
                              pl.BlockSpec((c, GDN_W), lambda b, ci: (z_row0 // c + b * nc + ci, 0)),
                              stt, pl.BlockSpec((1, hd), lambda b, ci: (0, 0))],
        out_specs=[tok, stt],
        out_shape=[jax.ShapeDtypeStruct((nb * t, GDN_W), BF16), jax.ShapeDtypeStruct((nb, nh, hd, hd), F32)],
        scratch_shapes=[pltpu.VMEM((nh, hd, hd), F32)],
        compiler_params=_cp("parallel", "arbitrary"))(q, k, v, gb, z, s0, norm_g)


def _odd_mixer(geo, xb, w_in, fx, gd, cache_k, cache_v, cache_logf, conv_prev, gdn0):
    tm, m, mp = geo.tm, geo.m, geo.mp
    fw = FOX_W
    o_f = 4 * fw
    o_g = o_f + FOX_HEADS
    w_q, w_k, w_v, w_gate = (w_in[:, i * fw:(i + 1) * fw].astype(BF16) for i in range(4))
    w_f = _pad_cols(w_in[:, o_f:o_g], LANES).astype(BF16)
    w_qkv = w_in[:, o_g:o_g + GDN_QKV].astype(BF16)
    w_ab = _pad_cols(w_in[:, o_g + GDN_QKV:o_g + GDN_QKV + 2 * GDN_HEADS], LANES).astype(BF16)
    w_z = w_in[:, o_g + GDN_QKV + 2 * GDN_HEADS:].astype(BF16)
    qn, kn = fx['q_norm'].reshape(1, -1), fx['k_norm'].reshape(1, -1)
    f_bias = jnp.pad(fx['f_bias'], (0, LANES - FOX_HEADS)).reshape(1, -1)
    scale = FOX_HD ** -0.5

    (q,) = _mm([(xb, w_q)], tm=tm, tn=fw, aux=[(qn, 'row0')],
               epilogue=lambda acc, g: (_head_rms(acc, g, FOX_HEADS, scale),), outs=[(fw, BF16)])
    k, k_b = _mm([(xb, w_k)], tm=tm, tn=fw, aux=[(kn, 'row0')],
                 epilogue=lambda acc, g: (_head_rms(acc, g, FOX_HEADS),) * 2, outs=[(fw, F32), (fw, BF16)])
    (v,) = _mm([(xb, w_v)], tm=tm, tn=None, epilogue=lambda acc: (acc,), outs=[(None, F32)])
    w_vt = w_v.T
    vt_p = _mm_t(w_vt, xb, nb=geo.bp, t=geo.tp, row0=0)
    vt_s = _mm_t(w_vt, xb, nb=geo.bs, t=geo.ts, row0=mp)
    (gate,) = _mm([(xb, w_gate)], tm=tm, tn=None, epilogue=lambda acc: (_sigmoid(acc),), outs=[(None, F32)])
    (logf,) = _mm([(xb, w_f)], tm=tm, tn=LANES, aux=[(f_bias, 'col')],
                  epilogue=lambda acc, fb: (-_softplus(-(acc + fb)),), outs=[(LANES, F32)])

    cum_p = _cumsum_time(logf[:mp].reshape(geo.bp, geo.tp, LANES))[:, :, :FOX_HEADS]
    bias_p = (jnp.transpose(cum_p, (0, 2, 1))[:, :, None, :], jnp.transpose(cum_p, (0, 2, 1))[..., None])
    o_p = _flash([q], [(k_b, 0, True)], vt_p, nb=geo.bp, tq_total=geo.tp, tk_total=geo.tp, q_row0=0, q_off=0,
                 tq=geo.tq_p, tk=geo.tq_p, chunked=False, nh=FOX_HEADS, bias=bias_p, gate=gate)
    padk = geo.kpad - geo.past - geo.ts
    lf_all = jnp.concatenate([jnp.pad(cache_logf, ((0, 0), (0, 0), (0, LANES - FOX_HEADS))),
                              logf[mp:].reshape(geo.bs, geo.ts, LANES), jnp.zeros((geo.bs, padk, LANES), F32)], axis=1)
    cum_s = jnp.transpose(_cumsum_time(lf_all)[:, :, :FOX_HEADS], (0, 2, 1))
    bias_s = (cum_s[:, :, None, geo.past:geo.past + geo.ts], cum_s[..., None])
    k_all = jnp.concatenate([cache_k.reshape(geo.bs, geo.past, fw).astype(BF16), k_b[mp:].reshape(geo.bs, geo.ts, fw),
                             jnp.zeros((geo.bs, padk, fw), BF16)], axis=1).reshape(geo.bs * geo.kpad, fw)
    vt_all = jnp.concatenate([jnp.transpose(cache_v.reshape(geo.bs, geo.past, fw), (0, 2, 1)).astype(BF16),
                              vt_s.reshape(geo.bs, fw, geo.ts), jnp.zeros((geo.bs, fw, padk), BF16)],
                             axis=2).reshape(geo.bs * fw, geo.kpad)
    o_s = _flash([q], [(k_all, 0, True)], vt_all, nb=geo.bs, tq_total=geo.ts, tk_total=geo.kpad, q_row0=mp,
                 q_off=geo.past, tq=geo.ts, tk=geo.tk_s, chunked=False, nh=FOX_HEADS, bias=bias_s, gate=gate)
    o_fox = jnp.concatenate([o_p, o_s], axis=0)

    (qkv_raw,) = _mm([(xb, w_qkv)], tm=tm, tn=None, epilogue=lambda acc: (acc,), outs=[(None, F32)])
    lane = jnp.arange(LANES)
    neg_a = jnp.where(lane < GDN_HEADS, -jnp.exp(jnp.pad(gd['a_log'], (0, LANES - GDN_HEADS))), 0.0).reshape(1, -1)
    dtb = jnp.pad(gd['dt_bias'], (0, LANES - GDN_HEADS)).reshape(1, -1)
    is_g = (lane < GDN_HEADS).astype(F32).reshape(1, -1)
    (gb,) = _mm([(xb, w_ab)], tm=tm, tn=LANES, aux=[(neg_a, 'col'), (dtb, 'col'), (is_g, 'col')],
                epilogue=lambda acc, na, db, ig: (jnp.where(ig > 0.5, na * _softplus(acc + db), _sigmoid(acc)),),
                outs=[(LANES, F32)])
    (z,) = _mm([(xb, w_z)], tm=tm, tn=None, epilogue=lambda acc: (acc,), outs=[(None, F32)])
    conv_w = jnp.pad(gd['conv_w'], ((0, 8 - GDN_CONV), (0, 0)))
    norm_g = gd['norm'].reshape(1, -1)
    prev8 = lambda a: jnp.pad(a, ((0, 0), (8 - (GDN_CONV - 1), 0), (0, 0)))
    q_p, k_p, v_p = _gdn_prep(qkv_raw, jnp.zeros((geo.bp, 8, GDN_QKV), F32), conv_w, nb=geo.bp, t=geo.tp, row0=0)
    q_s, k_s, v_s = _gdn_prep(qkv_raw, prev8(conv_prev), conv_w, nb=geo.bs, t=geo.ts, row0=mp)
    y_p, s_p = _gdn_scan(q_p, k_p, v_p, gb, z, jnp.zeros((geo.bp, GDN_HEADS, GDN_HD, GDN_HD), F32), norm_g,
                         nb=geo.bp, t=geo.tp, gb_row0=0, z_row0=0)
    y_s, s_s = _gdn_scan(q_s, k_s, v_s, gb, z, gdn0, norm_g, nb=geo.bs, t=geo.ts, gb_row0=mp, z_row0=mp)
    o_gdn = jnp.concatenate([y_p, y_s], axis=0)
    nconv = GDN_CONV - 1
    last_rows = lambda row0, nb, t: (row0 + jnp.arange(nb)[:, None] * t + (t - nconv + jnp.arange(nconv))[None, :]).reshape(-1)
    states = dict(
        fk_p=k[:mp].reshape(geo.bp, geo.tp, FOX_HEADS, FOX_HD), fk_s=k[mp:].reshape(geo.bs, geo.ts, FOX_HEADS, FOX_HD),
        fv_p=v[:mp].reshape(geo.bp, geo.tp, FOX_HEADS, FOX_HD), fv_s=v[mp:].reshape(geo.bs, geo.ts, FOX_HEADS, FOX_HD),
        fl_p=logf[:mp, :FOX_HEADS].reshape(geo.bp, geo.tp, FOX_HEADS),
        fl_s=logf[mp:, :FOX_HEADS].reshape(geo.bs, geo.ts, FOX_HEADS),
        cv_p=jnp.take(qkv_raw, last_rows(0, geo.bp, geo.tp), axis=0).reshape(geo.bp, nconv, GDN_QKV),
        cv_s=jnp.take(qkv_raw, last_rows(mp, geo.bs, geo.ts), axis=0).reshape(geo.bs, nconv, GDN_QKV),
        gdn_p=s_p, gdn_s=s_s)
    return o_fox, o_gdn, states


MOE_BLOCK = 256
MOE_COMBINE_ROWS = 128


def _moe_route(x, router, router_bias, *, tm):
    m = x.shape[0]
    assert m % tm == 0
    per_group = N_EXPERTS // N_GROUPS

    def lane_max(v):
        return jnp.max(v, axis=-1, keepdims=True)

    def first_lane(mask, lane):
        return jnp.min(jnp.where(mask, lane.astype(F32), float(LANES)), axis=-1, keepdims=True).astype(jnp.int32)

    def group_all(v, lane, op):
        sh = 1
        while sh < per_group:
            partner = jnp.where((lane & sh) == 0, pltpu.roll(v, LANES - sh, 1), pltpu.roll(v, sh, 1))
            v = op(v, partner)
            sh *= 2
        return v

    def body(x_ref, r_ref, b_ref, e_o, g_o, k_o, cnt_o, carry):
        i = pl.program_id(0)

        @pl.when(i == 0)
        def _():
            carry[...] = jnp.zeros(carry.shape, F32)

        lane = lax.broadcasted_iota(jnp.int32, (tm, LANES), 1)
        valid = lane < N_EXPERTS
        neg = -jnp.inf
        scores = _sigmoid(_dot(x_ref[...], r_ref[...]))
        biased = jnp.where(valid, scores + b_ref[...], neg)
        m1 = group_all(biased, lane, jnp.maximum)
        first = group_all(jnp.where(biased == m1, lane, LANES), lane, jnp.minimum)
        m2 = group_all(jnp.where(lane == first, neg, biased), lane, jnp.maximum)
        grp = jnp.where(valid & ((lane & (per_group - 1)) == 0), m1 + m2, neg)
        emask = jnp.zeros((tm, LANES), jnp.bool_)
        for _ in range(TOPK_GROUPS):
            idx = first_lane(grp == lane_max(grp), lane)
            emask = emask | ((lane - idx >= 0) & (lane - idx < per_group))
            grp = jnp.where(lane == idx, neg, grp)
        cur = jnp.where(emask & valid, biased, neg)
        picks = []
        sel = jnp.zeros((tm, LANES), jnp.bool_)
        for _ in range(TOP_K):
            idx = first_lane(cur == lane_max(cur), lane)
            pick = lane == idx
            picks.append((idx, pick))
            sel = sel | pick
            cur = jnp.where(pick, neg, cur)
        selw = jnp.where(sel, scores, 0.0)
        gates_dense = selw / jnp.sum(selw, axis=-1, keepdims=True) * ROUTED_SCALE
        ri = lax.broadcasted_iota(jnp.int32, (tm, tm), 0)
        cj = lax.broadcasted_iota(jnp.int32, (tm, tm), 1)
        p01 = jnp.where(sel, 1.0, 0.0)
        before = _dot(jnp.where(ri > cj, 1.0, 0.0).astype(BF16), p01.astype(BF16)) + carry[...]
        carry[...] = carry[...] + jnp.sum(p01, axis=0, keepdims=True)
        cnt_o[...] = carry[...]
        e_out = jnp.zeros((tm, LANES), jnp.int32)
        g_out = jnp.zeros((tm, LANES), F32)
        k_out = jnp.zeros((tm, LANES), jnp.int32)
        for j, (idx, pick) in enumerate(picks):
            e_out = jnp.where(lane == j, idx, e_out)
            g_out = jnp.where(lane == j, jnp.sum(jnp.where(pick, gates_dense, 0.0), axis=-1, keepdims=True), g_out)
            rank = jnp.sum(jnp.where(pick, before, 0.0), axis=-1, keepdims=True)
            k_out = jnp.where(lane == j, rank.astype(jnp.int32), k_out)
        e_o[...] = e_out
        g_o[...] = g_out
        k_o[...] = k_out

    tok = pl.BlockSpec((tm, LANES), lambda i: (i, 0))
    one = pl.BlockSpec((1, LANES), lambda i: (0, 0))
    return pl.pallas_call(
        body, grid=(m // tm,), name="moe_route",
        in_specs=[pl.BlockSpec((tm, x.shape[1]), lambda i: (i, 0)), pl.BlockSpec(router.shape, lambda i: (0, 0)), one],
        out_specs=[tok, tok, tok, one],
        out_shape=[jax.ShapeDtypeStruct((m, LANES), jnp.int32), jax.ShapeDtypeStruct((m, LANES), F32),
                   jax.ShapeDtypeStruct((m, LANES), jnp.int32), jax.ShapeDtypeStruct((1, LANES), F32)],
        scratch_shapes=[pltpu.VMEM((1, LANES), F32)],
        compiler_params=_cp("arbitrary"))(x, router, router_bias)


def _moe_dispatch(x_words, dest, cap, *, tb):
    m, wd = x_words.shape
    nt = m // tb

    def body(dest_ref, x_ref, out_ref, sem):
        def row_copy(i, j):
            return pltpu.make_async_copy(x_ref.at[pl.ds(i, 1)], out_ref.at[pl.ds(dest_ref[0, 0, i * TOP_K + j], 1)], sem)

        def issue(i, carry):
            for j in range(TOP_K):
                row_copy(i, j).start()
            return carry

        lax.fori_loop(0, tb, issue, 0, unroll=8)
        for _ in range(TOP_K):
            pltpu.make_async_copy(x_ref, out_ref.at[pl.ds(0, tb)], sem).wait()

    return pl.pallas_call(
        body, grid=(nt,), name="moe_dispatch",
        in_specs=[pl.BlockSpec((1, 1, tb * TOP_K), lambda i: (i, 0, 0), memory_space=pltpu.SMEM),
                  pl.BlockSpec((tb, wd), lambda i: (i, 0))],
        out_specs=pl.BlockSpec(memory_space=pl.ANY),
        out_shape=jax.ShapeDtypeStruct((cap, wd), x_words.dtype),
        scratch_shapes=[pltpu.SemaphoreType.DMA(())],
        compiler_params=_cp("arbitrary"))(dest, x_words)


def _pack_halves(y):
    n = y.shape[1] // 2
    bits = lambda a: lax.bitcast_convert_type(a.astype(BF16).astype(F32), jnp.uint32)
    return (bits(y[:, :n]) >> 16) | (bits(y[:, n:]) & jnp.uint32(0xFFFF0000))


def _unpack_halves(w):
    lo = lax.bitcast_convert_type(w << 16, F32).astype(BF16)
    hi = lax.bitcast_convert_type(w & jnp.uint32(0xFFFF0000), F32).astype(BF16)
    return lo, hi


def _ep_ln_packed(acc, x, g, b):
    y = _ep_ln(acc, x, g, b)
    return y, _pack_halves(y)


def _moe_experts(xs_words, blk_exp, blk_next, blk_rows, n_used, w_gate, w_up, w_down, layer):
    cap, half = xs_words.shape
    d = 2 * half
    nblk = cap // MOE_BLOCK
    ed = w_gate.shape[3]

    def body(be_ref, bn_ref, br_ref, nu_ref, x_ref, wg_hbm, wu_hbm, wd_hbm, o_ref, wg_f, wu_f, wd_f, wg_b, wu_b, wd_b,
             slot_ref, sem):
        i = pl.program_id(0)

        def fetch(e, slot):
            return [pltpu.make_async_copy(src.at[layer, e], dst.at[slot], sem.at[slot])
                    for src, dst in ((wg_hbm, wg_f), (wu_hbm, wu_f), (wd_hbm, wd_f))]

        @pl.when(i < nu_ref[0])
        def _():
            first = i == 0
            e = be_ref[i]

            @pl.when(first)
            def _():
                slot_ref[0] = 0
                for cp in fetch(e, 0):
                    cp.start()

            @pl.when(first | (e != be_ref[jnp.maximum(i - 1, 0)]))
            def _():
                slot = jnp.where(first, 0, 1 - slot_ref[0])
                slot_ref[0] = slot
                for cp in fetch(e, slot):
                    cp.wait()
                wg_b[...] = wg_f[slot].astype(BF16)
                wu_b[...] = wu_f[slot].astype(BF16)
                wd_b[...] = wd_f[slot].astype(BF16)

                @pl.when(bn_ref[i] != e)
                def _():
                    for cp in fetch(bn_ref[i], 1 - slot):
                        cp.start()

            row = lax.broadcasted_iota(jnp.int32, (MOE_BLOCK, 1), 0)
            lo, hi = _unpack_halves(jnp.where(row < br_ref[i], x_ref[...], jnp.uint32(0)))
            hg = _dot(lo, wg_b[:half]) + _dot(hi, wg_b[half:])
            hu = _dot(lo, wu_b[:half]) + _dot(hi, wu_b[half:])
            o_ref[...] = _pack_halves(_dot((_silu(hg) * hu).astype(BF16), wd_b[...]))

    blk = lambda i, be, bn, br, nu: (jnp.minimum(i, nu[0] - 1), 0)
    hbm = pl.BlockSpec(memory_space=pl.ANY)
    grid_spec = pltpu.PrefetchScalarGridSpec(
        num_scalar_prefetch=4, grid=(nblk,),
        in_specs=[pl.BlockSpec((MOE_BLOCK, half), blk), hbm, hbm, hbm],
        out_specs=pl.BlockSpec((MOE_BLOCK, half), blk),
        scratch_shapes=[pltpu.VMEM((2, d, ed), F32), pltpu.VMEM((2, d, ed), F32), pltpu.VMEM((2, ed, d), F32),
                        pltpu.VMEM((d, ed), BF16), pltpu.VMEM((d, ed), BF16), pltpu.VMEM((ed, d), BF16),
                        pltpu.SMEM((1,), jnp.int32), pltpu.SemaphoreType.DMA((2,))])
    return pl.pallas_call(body, grid_spec=grid_spec, out_shape=jax.ShapeDtypeStruct((cap, half), jnp.uint32), name="moe_experts",
                          compiler_params=_cp("arbitrary"))(blk_exp, blk_next, blk_rows, n_used, xs_words, w_gate, w_up, w_down)


def _moe_combine(ys_words, dest, gates, *, tc):
    cap, half = ys_words.shape
    d = 2 * half
    m = gates.shape[0]
    nt = m // tc

    def body(dest0_ref, destn_ref, g_ref, y_ref, o_ref, buf, sem):
        i = pl.program_id(0)

        def issue(dref, slot):
            def one(r, carry):
                for j in range(TOP_K):
                    pltpu.make_async_copy(y_ref.at[pl.ds(dref[0, 0, r * TOP_K + j], 1)],
                                          buf.at[slot, j, pl.ds(r, 1)], sem.at[slot]).start()
                return carry
            lax.fori_loop(0, tc, one, 0, unroll=8)

        @pl.when(i == 0)
        def _():
            issue(dest0_ref, 0)

        @pl.when(i + 1 < nt)
        def _():
            issue(destn_ref, (i + 1) % 2)

        slot = i % 2
        for j in range(TOP_K):
            pltpu.make_async_copy(y_ref.at[pl.ds(0, tc)], buf.at[slot, j], sem.at[slot]).wait()
        g = g_ref[...]
        acc_lo = acc_hi = None
        for j in range(TOP_K):
            lo, hi = _unpack_halves(buf[slot, j])
            gj = g[:, j:j + 1]
            acc_lo = lo.astype(F32) * gj if acc_lo is None else acc_lo + lo.astype(F32) * gj
            acc_hi = hi.astype(F32) * gj if acc_hi is None else acc_hi + hi.astype(F32) * gj
        o_ref[:, :half] = acc_lo
        o_ref[:, half:] = acc_hi

    return pl.pallas_call(
        body, grid=(nt,), name="moe_combine",
        in_specs=[pl.BlockSpec((1, 1, tc * TOP_K), lambda i: (0, 0, 0), memory_space=pltpu.SMEM),
                  pl.BlockSpec((1, 1, tc * TOP_K), lambda i: (jnp.minimum(i + 1, nt - 1), 0, 0), memory_space=pltpu.SMEM),
                  pl.BlockSpec((tc, LANES), lambda i: (i, 0)),
                  pl.BlockSpec(memory_space=pl.ANY)],
        out_specs=pl.BlockSpec((tc, d), lambda i: (i, 0)),
        out_shape=jax.ShapeDtypeStruct((m, d), F32),
        scratch_shapes=[pltpu.VMEM((2, TOP_K, tc, half), jnp.uint32), pltpu.SemaphoreType.DMA((2,))],
        compiler_params=_cp("arbitrary"))(dest, dest, gates, ys_words)


def _moe_layer(geo, x_res, xb, x_words, layer, router, router_bias, w_gate, w_up, w_down, ws_gate, ws_up, ws_down,
               ln_g, ln_b, out_dtype):
    m, d = x_res.shape
    tm = geo.tm
    eidx, gates, rank, counts = _moe_route(xb, _pad_cols(router, LANES).astype(BF16),
                                           jnp.pad(router_bias, (0, LANES - N_EXPERTS)).reshape(1, -1), tm=tm)
    cnt = counts[0, :N_EXPERTS].astype(jnp.int32)
    padded = (cnt + MOE_BLOCK - 1) // MOE_BLOCK * MOE_BLOCK
    pad_end = jnp.cumsum(padded)
    pad_start = pad_end - padded
    n_blocks = -(-(m * TOP_K) // MOE_BLOCK) + N_EXPERTS
    cap = n_blocks * MOE_BLOCK
    blk_row0 = jnp.arange(n_blocks, dtype=jnp.int32) * MOE_BLOCK
    blk_exp = jnp.minimum(jnp.sum((pad_end[None, :] <= blk_row0[:, None]).astype(jnp.int32), axis=1), N_EXPERTS - 1)
    blk_rows = jnp.clip(cnt[blk_exp] - (blk_row0 - pad_start[blk_exp]), 0, MOE_BLOCK).astype(jnp.int32)
    n_used = (pad_end[-1:] // MOE_BLOCK).astype(jnp.int32)
    eid = jnp.arange(N_EXPERTS, dtype=jnp.int32)
    later = jnp.where((eid[None, :] > eid[:, None]) & (cnt[None, :] > 0), eid[None, :], N_EXPERTS)
    next_e = jnp.min(later, axis=1)
    next_e = jnp.where(next_e == N_EXPERTS, eid, next_e)
    blk_next = next_e[blk_exp].astype(jnp.int32)
    dest = pad_start[eidx[:, :TOP_K]] + rank[:, :TOP_K]

    tb = tm
    xs_words = _moe_dispatch(x_words, dest.reshape(m // tb, 1, tb * TOP_K), cap, tb=tb)
    ys = _moe_experts(xs_words, blk_exp, blk_next, blk_rows, n_used, w_gate, w_up, w_down, layer)
    tc = min(MOE_COMBINE_ROWS, tm)
    routed = _moe_combine(ys, dest.reshape(m // tc, 1, tc * TOP_K), gates, tc=tc)

    sd = ws_gate.shape[1]
    w_sh = jnp.concatenate([ws_gate, ws_up], axis=1).astype(BF16)
    (hs,) = _mm([(xb, w_sh)], tm=tm, tn=2 * sd, epilogue=lambda acc: (_silu(acc[:, :sd]) * acc[:, sd:],), outs=[(sd, BF16)])
    (y,) = _mm([(hs, ws_down.astype(BF16))], tm=tm // 2, tn=d,
               aux=[(x_res, 'rowcol'), (routed, 'rowcol'), (ln_g.reshape(1, -1), 'col'), (ln_b.reshape(1, -1), 'col')],
               epilogue=lambda acc, xr, rt, g, b: (_ep_ln(acc + rt, xr, g, b),), outs=[(d, out_dtype)], name="moe_out")
    return y


def _rwkv_params(li, mu, w0, w2, a0, a2, g2, k_k, k_a, r_k, ln_g, ln_b):
    row = lambda a: a[li].reshape(1, -1)
    padr = lambda a, n: jnp.pad(a, ((0, n - a.shape[0]), (0, 0)))
    return dict(mu=_rwkv_cols_layout(mu[li])[None, :], w0=row(w0), a0=row(a0), k_k=row(k_k), k_a=row(k_a),
                w2=padr(w2[li], LANES), a2=padr(a2[li], LANES), g2=g2[li], r_k=row(r_k), ln_g=row(ln_g), ln_b=row(ln_b))


def _mix_out(geo, x_res, o_a, o_b, w_out, ln_g, ln_b):
    d = x_res.shape[1]
    ka = o_a.shape[1]
    return _mm([(o_a, w_out[:ka].astype(BF16)), (o_b, w_out[ka:].astype(BF16))], tm=geo.tm // 2, tn=d,
               aux=[(x_res, 'rowcol'), (ln_g.reshape(1, -1), 'col'), (ln_b.reshape(1, -1), 'col')],
               epilogue=_ep_ln_packed, outs=[(d, BF16), (d // 2, jnp.uint32)], name="mix_out")


def kernel(x_prompt, x_sample, cache_mla_ckv, cache_mla_krope, state_rwkv_shift, state_rwkv_wkv, cache_fox_k, cache_fox_v, cache_fox_logf, state_gdn_conv, state_gdn_wkv, ln1_g, ln1_b, ln2_g, ln2_b, ev_w_in, ev_w_out, mla_kv_norm, mla_w_uk, mla_w_uv, rwkv_mu, rwkv_w0, rwkv_w2, rwkv_a0, rwkv_a2, rwkv_g2, rwkv_k_k, rwkv_k_a, rwkv_r_k, rwkv_ln_g, rwkv_ln_b, od_w_in, od_w_out, fox_q_norm, fox_k_norm, fox_f_bias, gdn_conv_w, gdn_a_log, gdn_dt_bias, gdn_norm, moe_router, moe_router_bias, moe_w_gate, moe_w_up, moe_w_down, moe_ws_gate, moe_ws_up, moe_ws_down):
    bp, tp, d = x_prompt.shape
    bs, ts, _ = x_sample.shape
    geo = _Geom(bp, tp, bs, ts, cache_mla_ckv.shape[2])
    x = jnp.concatenate([x_prompt.reshape(bp * tp, d), x_sample.reshape(bs * ts, d)], axis=0)
    xb = x.astype(BF16)
    rope_tab = _rope_table(geo)
    st = {}
    n_layers = ln1_g.shape[0]
    for layer in range(n_layers):
        li = layer // 2
        if layer % 2 == 0:
            rw = _rwkv_params(li, rwkv_mu, rwkv_w0, rwkv_w2, rwkv_a0, rwkv_a2, rwkv_g2, rwkv_k_k, rwkv_k_a, rwkv_r_k,
                              rwkv_ln_g, rwkv_ln_b)
            o_a, o_b, new = _even_mixer(geo, xb, rope_tab, ev_w_in[li], mla_kv_norm[li], mla_w_uk[li], mla_w_uv[li], rw,
                                        cache_mla_ckv[li], cache_mla_krope[li], state_rwkv_shift[li], state_rwkv_wkv[li])
            w_out = ev_w_out[li]
        else:
            fx = {'q_norm': fox_q_norm[li], 'k_norm': fox_k_norm[li], 'f_bias': fox_f_bias[li]}
            gd = {'conv_w': gdn_conv_w[li], 'a_log': gdn_a_log[li], 'dt_bias': gdn_dt_bias[li], 'norm': gdn_norm[li]}
            o_a, o_b, new = _odd_mixer(geo, xb, od_w_in[li], fx, gd, cache_fox_k[li], cache_fox_v[li], cache_fox_logf[li],
                                       state_gdn_conv[li], state_gdn_wkv[li])
            w_out = od_w_out[li]
        for name, val in new.items():
            st.setdefault(name, []).append(val)
        xb, x_words = _mix_out(geo, x, o_a, o_b, w_out, ln1_g[layer], ln1_b[layer])
        last = layer == n_layers - 1
        x = _moe_layer(geo, xb, xb, x_words, layer, moe_router[layer], moe_router_bias[layer], moe_w_gate, moe_w_up,
                       moe_w_down, moe_ws_gate[layer], moe_ws_up[layer], moe_ws_down[layer], ln2_g[layer], ln2_b[layer],
                       F32 if last else BF16)
        xb = x
    names = ('ckv', 'kr', 'sh', 'rwkv', 'fk', 'fv', 'fl', 'cv', 'gdn')
    return ((x[:geo.mp].reshape(bp, tp, d), x[geo.mp:].reshape(bs, ts, d))
            + tuple(jnp.stack(st[n + '_p']) for n in names) + tuple(jnp.stack(st[n + '_s']) for n in names))
```

```python
import functools
import math

import jax
import jax.numpy as jnp
from jax import lax
from jax.experimental import pallas as pl
from jax.experimental.pallas import tpu as pltpu

F32 = jnp.float32
BF16 = jnp.bfloat16
HP = lax.Precision.HIGHEST

D_MODEL = 2048
DEPTH = 2
DN_ALPHA = float((2.0 * DEPTH) ** 0.25)
LN_EPS = 1e-5
RMS_EPS = 1e-6
CHUNK = 64
LANES = 128

MLA_HEADS, MLA_NOPE, MLA_ROPE, MLA_VDIM, MLA_KV_LORA = 8, 128, 64, 128, 512
MLA_QK = MLA_NOPE + MLA_ROPE
ROPE_THETA = 10000.0
RWKV_HEADS, RWKV_HD = 16, 64
RWKV_W = RWKV_HEADS * RWKV_HD
RWKV_DECAY_LORA, RWKV_A_LORA, RWKV_G_LORA = 96, 96, 256
RWKV_GN_EPS = 64e-5
RWKV_PCOLS = 3 * RWKV_W + 2 * LANES + RWKV_G_LORA
FOX_HEADS, FOX_HD = 8, 128
FOX_W = FOX_HEADS * FOX_HD
GDN_HEADS, GDN_HD, GDN_CONV = 8, 128, 4
GDN_W = GDN_HEADS * GDN_HD
GDN_QKV = 3 * GDN_W
N_EXPERTS, TOP_K, N_GROUPS, TOPK_GROUPS = 64, 6, 8, 4
EXPERT_DIM, SHARED_DIM = 512, 512
ROUTED_SCALE = 2.5
SCAN_CHUNK = 64
FLASH_KEY_SUBTILE = 1024
NEG_BIG = -1e30
RWKV_PASSES = dict(a=1, x=3, inv=1, u=1, y=1, s=3)
GDN_PASSES = dict(a=1, x=1, inv=1, u=1, y=1, s=1)

VMEM_LIMIT_BYTES = 56 * 1024 * 1024
MM_MAX_COLS = 1792


def _cp(*sem):
    return pltpu.CompilerParams(dimension_semantics=("arbitrary",) * len(sem), vmem_limit_bytes=VMEM_LIMIT_BYTES)


def _dot(a, b, prec=None):
    return jnp.dot(a, b, precision=prec, preferred_element_type=F32)


def _dot_nt(a, b, prec=None):
    return lax.dot_general(a, b, (((1,), (1,)), ((), ())), precision=prec, preferred_element_type=F32)


def _dot_tn(a, b, prec=None):
    return lax.dot_general(a, b, (((0,), (0,)), ((), ())), precision=prec, preferred_element_type=F32)


def _sigmoid(x):
    return 1.0 / (1.0 + jnp.exp(-x))


def _softplus(x):
    return jnp.maximum(x, 0.0) + jnp.log(1.0 + jnp.exp(-jnp.abs(x)))


def _silu(x):
    return x * _sigmoid(x)


def _mm(pairs, *, tm, tn, epilogue, outs, aux=(), name="mm"):
    m, n = pairs[0][0].shape[0], pairs[0][1].shape[1]
    if tn is None:
        tn = n if n <= MM_MAX_COLS else max(c for c in range(LANES, MM_MAX_COLS + 1, LANES) if n % c == 0)
    outs = [(tn if ow is None else ow, dt) for ow, dt in outs]
    assert m % tm == 0 and n % tn == 0, (m, n, tm, tn)
    nj, ni = n // tn, m // tm
    in_specs, args = [], []
    for a, w in pairs:
        k = a.shape[1]
        in_specs += [pl.BlockSpec((tm, k), lambda j, i: (i, 0)), pl.BlockSpec((k, tn), lambda j, i: (0, j))]
        args += [a, w]
    for arr, kind in aux:
        if kind == 'row':
            in_specs.append(pl.BlockSpec((tm, arr.shape[1]), lambda j, i: (i, 0)))
        elif kind == 'rowcol':
            in_specs.append(pl.BlockSpec((tm, tn), lambda j, i: (i, j)))
        elif kind == 'row0':
            in_specs.append(pl.BlockSpec(arr.shape, lambda j, i: (0, 0)))
        else:
            in_specs.append(pl.BlockSpec((1, tn), lambda j, i: (0, j)))
        args.append(arr)
    out_shape = [jax.ShapeDtypeStruct((m, ow * nj), dt) for ow, dt in outs]
    out_specs = [pl.BlockSpec((tm, ow), lambda j, i: (i, j)) for ow, dt in outs]
    n_pairs, n_aux = len(pairs), len(aux)

    def body(*refs):
        acc = None
        for p in range(n_pairs):
            d = _dot(refs[2 * p][...], refs[2 * p + 1][...])
            acc = d if acc is None else acc + d
        res = epilogue(acc, *[r[...] for r in refs[2 * n_pairs:2 * n_pairs + n_aux]])
        for o_ref, val in zip(refs[2 * n_pairs + n_aux:], res):
            o_ref[...] = val.astype(o_ref.dtype)

    res = pl.pallas_call(body, grid=(nj, ni), in_specs=in_specs, out_specs=out_specs, out_shape=out_shape, name=name,
                         compiler_params=_cp("parallel", "parallel"))(*args)
    return res


def _mm_t(w_t, a, *, nb, t, row0, name="mm_t"):
    n, k = w_t.shape
    tm = next((c for c in (512, 384, 256, 128) if t % c == 0), t)
    assert t % tm == 0 and row0 % tm == 0
    nt = t // tm

    def body(w_ref, a_ref, o_ref):
        o_ref[...] = _dot_nt(w_ref[...], a_ref[...]).astype(o_ref.dtype)

    return pl.pallas_call(
        body, grid=(nb, nt), name=name,
        in_specs=[pl.BlockSpec((n, k), lambda b, i: (0, 0)), pl.BlockSpec((tm, k), lambda b, i: (row0 // tm + b * nt + i, 0))],
        out_specs=pl.BlockSpec((n, tm), lambda b, i: (b, i)),
        out_shape=jax.ShapeDtypeStruct((nb * n, t), BF16),
        compiler_params=_cp("parallel", "parallel"))(w_t, a)


def _rope_lanes(x, tab):
    c, s1, s2 = tab[:, :LANES], tab[:, LANES:2 * LANES], tab[:, 2 * LANES:]
    n = x.shape[1]
    lo = pltpu.roll(x, n - MLA_ROPE // 2, 1)
    hi = pltpu.roll(x, MLA_ROPE // 2, 1)
    if n > LANES:
        reps = n // LANES
        c, s1, s2 = (jnp.concatenate([t] * reps, axis=1) for t in (c, s1, s2))
    return x * c + lo * s1 + hi * s2


def _ep_rms(acc, g):
    y = acc * lax.rsqrt(jnp.mean(acc * acc, axis=-1, keepdims=True) + RMS_EPS) * g
    return y, y


def _ep_ln(acc, x, g, b):
    z = DN_ALPHA * x.astype(F32) + acc
    mu = jnp.mean(z, axis=-1, keepdims=True)
    zc = z - mu
    var = jnp.mean(zc * zc, axis=-1, keepdims=True)
    return zc * lax.rsqrt(var + LN_EPS) * g + b


def _flash(qs, ks, vt, *, nb, tq_total, tk_total, q_row0, q_off, tq, tk, chunked, nh, bias=None, gate=None,
           out_dtype=BF16, name="flash"):
    assert tq_total % tq == 0 and tk_total % tk == 0 and q_row0 % tq == 0
    nq, nk = tq_total // tq, tk_total // tk
    qrb0 = q_row0 // tq
    ts = tk if tk % FLASH_KEY_SUBTILE else min(tk, FLASH_KEY_SUBTILE)
    for qi_s in range(nq):
        last_s = min(nk - 1, (q_off + qi_s * tq + tq - 1) // tk)
        assert last_s * tk <= q_off + qi_s * tq + 1, "key tiles before the last needed one must be fully visible"
    n_q, n_k = len(qs), len(ks)
    has_bias, has_gate = bias is not None, gate is not None

    def last_k(qi):
        return jnp.minimum(nk - 1, (q_off + qi * tq + tq - 1) // tk)

    kblk = lambda qi, ki: jnp.minimum(ki, last_k(qi))
    q_map = lambda b, h, qi, ki: (qrb0 + b * nq + qi, h)
    in_specs, args = [], []
    for q in qs:
        in_specs.append(pl.BlockSpec((tq, LANES), q_map))
        args.append(q)
    for arr, c0, per_head in ks:
        if per_head:
            in_specs.append(pl.BlockSpec((tk, LANES), lambda b, h, qi, ki, c0=c0: (b * nk + kblk(qi, ki), c0 + h)))
        else:
            in_specs.append(pl.BlockSpec((tk, LANES), lambda b, h, qi, ki, c0=c0: (b * nk + kblk(qi, ki), c0)))
        args.append(arr)
    in_specs.append(pl.BlockSpec((LANES, tk), lambda b, h, qi, ki: (b * nh + h, kblk(qi, ki))))
    args.append(vt)
    if has_bias:
        in_specs.append(pl.BlockSpec((1, 1, 1, tq), lambda b, h, qi, ki: (b, h, 0, qi)))
        in_specs.append(pl.BlockSpec((1, 1, tk, 1), lambda b, h, qi, ki: (b, h, kblk(qi, ki), 0)))
        args += list(bias)
    if has_gate:
        in_specs.append(pl.BlockSpec((tq, LANES), q_map))
        args.append(gate)

    def body(*refs):
        q_refs = refs[:n_q]
        k_refs = refs[n_q:n_q + n_k]
        vt_ref = refs[n_q + n_k]
        pos = n_q + n_k + 1
        if has_bias:
            qb_ref, kb_ref = refs[pos], refs[pos + 1]
            pos += 2
        if has_gate:
            gate_ref = refs[pos]
            pos += 1
        o_ref, m_ref, l_ref, acc_ref = refs[pos:pos + 4]
        qi, ki = pl.program_id(2), pl.program_id(3)

        @pl.when(ki == 0)
        def _():
            m_ref[...] = jnp.full(m_ref.shape, NEG_BIG, F32)
            l_ref[...] = jnp.zeros(l_ref.shape, F32)
            acc_ref[...] = jnp.zeros(acc_ref.shape, F32)

        def step(masked):
            q_all = jnp.concatenate([q_ref[...] for q_ref in q_refs], axis=1) if n_q > 1 else q_refs[0][...]
            qpos = q_off + qi * tq + lax.broadcasted_iota(jnp.int32, (1, tq), 1)
            qlim = (qpos | (CHUNK - 1)) if chunked else qpos
            subs = [slice(j * ts, (j + 1) * ts) for j in range(tk // ts)]
            ss = []
            for j, rows in enumerate(subs):
                k_all = jnp.concatenate([k_ref[rows, :] for k_ref in k_refs], axis=1) if n_k > 1 else k_refs[0][rows, :]
                s = _dot_nt(k_all, q_all)
                if has_bias:
                    s = s + (qb_ref[0, 0] - kb_ref[0, 0, rows, :])
                if masked:
                    kpos = ki * tk + j * ts + lax.broadcasted_iota(jnp.int32, (ts, 1), 0)
                    s = jnp.where(kpos <= qlim, s, NEG_BIG)
                ss.append(s)
            m_old = m_ref[...]
            m_new = m_old
            for s in ss:
                m_new = jnp.maximum(m_new, jnp.max(s, axis=0, keepdims=True))
            alpha = jnp.exp(m_old - m_new)
            l_new = alpha * l_ref[...]
            acc_new = alpha * acc_ref[...]
            for rows, s in zip(subs, ss):
                p = jnp.exp(s - m_new)
                l_new = l_new + jnp.sum(p, axis=0, keepdims=True)
                acc_new = acc_new + _dot(vt_ref[:, rows], p.astype(BF16))
            m_ref[...] = m_new
            l_ref[...] = l_new
            acc_ref[...] = acc_new

        @pl.when(ki < last_k(qi))
        def _():
            step(False)

        @pl.when(ki == last_k(qi))
        def _():
            step(True)

        @pl.when(ki == nk - 1)
        def _():
            o = (acc_ref[...] / l_ref[...]).T
            if has_gate:
                o = o * gate_ref[...]
            o_ref[...] = o.astype(o_ref.dtype)

    return pl.pallas_call(
        body, grid=(nb, nh, nq, nk), in_specs=in_specs, name=name,
        out_specs=pl.BlockSpec((tq, LANES), lambda b, h, qi, ki: (b * nq + qi, h)),
        out_shape=jax.ShapeDtypeStruct((nb * tq_total, nh * LANES), out_dtype),
        scratch_shapes=[pltpu.VMEM((1, tq), F32), pltpu.VMEM((1, tq), F32), pltpu.VMEM((LANES, tq), F32)],
        compiler_params=_cp("parallel", "parallel", "parallel", "arbitrary"))(*args)


def _head_sum_matrices(n_heads, hd):
    lane = jnp.arange(n_heads * hd)[:, None] // hd
    e = (lane == jnp.arange(LANES)[None, :]).astype(F32)
    return e, e.T


def _rwkv_prep(cols, shift_prev, prm, *, nb, t, row0):
    tt = min(256, t)
    assert t % tt == 0 and row0 % tt == 0
    nt = t // tt
    rb0 = row0 // tt
    w = RWKV_W

    def body(c_ref, sp_ref, mu_ref, w0_ref, a0_ref, kk_ref, ka_ref, w2_ref, a2_ref, g2_ref, e_ref, et_ref,
             r_o, k_o, v_o, kk_o, b_o, lw_o, g_o, sh_o, carry):
        ti = pl.program_id(1)

        @pl.when(ti == 0)
        def _():
            carry[...] = sp_ref[0]

        c = c_ref[...]
        row = lax.broadcasted_iota(jnp.int32, (tt, 1), 0)
        prev = jnp.where(row == 0, carry[...], pltpu.roll(c, 1, 0))
        carry[...] = c[tt - 1:tt, :]
        sh_o[0] = c[tt - 1:tt, :]
        xs = c + (prev - c) * mu_ref[...]
        r, k, v = xs[:, :w], xs[:, w:2 * w], xs[:, 2 * w:3 * w]
        o1 = 3 * w
        lora = lambda u, w_ref: _dot(u.astype(BF16), w_ref[...].astype(BF16))
        w_raw = w0_ref[...] + lora(jnp.tanh(xs[:, o1:o1 + LANES]), w2_ref)
        log_w = -jnp.exp(-_softplus(-w_raw) - 0.5)
        a = _sigmoid(a0_ref[...] + lora(xs[:, o1 + LANES:o1 + 2 * LANES], a2_ref))
        g = lora(_sigmoid(xs[:, o1 + 2 * LANES:]), g2_ref)

        def head_dot(u, sel_ref):
            hi, lo = _split_bf16(u)
            sel = sel_ref[...].astype(BF16)
            return _dot(hi.astype(BF16), sel) + _dot(lo.astype(BF16), sel)

        kk = k * kk_ref[...]
        kk = kk * head_dot(lax.rsqrt(head_dot(kk * kk, e_ref) + 1e-6), et_ref)
        r_o[...] = r
        k_o[...] = k * (1.0 + (a - 1.0) * ka_ref[...])
        v_o[...] = v
        kk_o[...] = kk
        b_o[...] = kk * a
        lw_o[...] = log_w
        g_o[...] = g

    e, et = _head_sum_matrices(RWKV_HEADS, RWKV_HD)
    tok = pl.BlockSpec((tt, w), lambda b, ti: (b * nt + ti, 0))
    full = lambda arr: pl.BlockSpec(arr.shape, lambda b, ti: (0,) * arr.ndim)
    params = [prm['mu'], prm['w0'], prm['a0'], prm['k_k'], prm['k_a'], prm['w2'], prm['a2'], prm['g2'], e, et]
    outs = pl.pallas_call(
        body, grid=(nb, nt), name="rwkv_prep",
        in_specs=[pl.BlockSpec((tt, RWKV_PCOLS), lambda b, ti: (rb0 + b * nt + ti, 0)),
                  pl.BlockSpec((1, 1, RWKV_PCOLS), lambda b, ti: (b, 0, 0))] + [full(p) for p in params],
        out_specs=[tok] * 7 + [pl.BlockSpec((1, 1, RWKV_PCOLS), lambda b, ti: (b, 0, 0))],
        out_shape=[jax.ShapeDtypeStruct((nb * t, w), F32)] * 7 + [jax.ShapeDtypeStruct((nb, 1, RWKV_PCOLS), F32)],
        scratch_shapes=[pltpu.VMEM((1, RWKV_PCOLS), F32)],
        compiler_params=_cp("parallel", "arbitrary"))(cols, shift_prev, *params)
    return outs


def _cumsum_rows(x, n):
    row = lax.broadcasted_iota(jnp.int32, (n, 1), 0)
    sh = 1
    while sh < n:
        x = x + jnp.where(row >= sh, pltpu.roll(x, sh, 0), 0.0)
        sh *= 2
    return x


def _split_bf16(x):
    hi = x.astype(BF16).astype(F32)
    return hi, x - hi


def _bmm(a, b, ca, cb, passes):
    dn = (((ca,), (cb,)), ((0,), (0,)))
    if passes == 6:
        return lax.dot_general(a, b, dn, precision=HP, preferred_element_type=F32)
    if passes == 3:
        ah, al = _split_bf16(a)
        bh, bl = _split_bf16(b)
        a = jnp.concatenate([ah, ah, al], axis=ca)
        b = jnp.concatenate([bh, bl, bh], axis=cb)
    return lax.dot_general(a.astype(BF16), b.astype(BF16), dn, preferred_element_type=F32)


def _heads(x, nh, hd):
    return jnp.stack([x[:, h * hd:(h + 1) * hd] for h in range(nh)], axis=0)


def _unheads(x):
    return jnp.concatenate([x[h] for h in range(x.shape[0])], axis=1)


def _unit_lower_inverse(nmat, n, passes):
    eye = (lax.broadcasted_iota(jnp.int32, (n, n), 0) == lax.broadcasted_iota(jnp.int32, (n, n), 1)).astype(F32)
    t = eye + nmat
    p = nmat
    steps = int(math.ceil(math.log2(n))) - 1
    for _ in range(steps):
        p = _bmm(p, p, 2, 1, passes)
        t = t + _bmm(p, t, 2, 1, passes)
    return t


def _rwkv_scan(r, k, v, kk, bvec, logw, g, s0, prm, *, nb, t):
    c = min(SCAN_CHUNK, t)
    assert t % c == 0
    nc = t // c
    hd, nh = RWKV_HD, RWKV_HEADS

    def body(r_ref, k_ref, v_ref, kk_ref, b_ref, lw_ref, g_ref, s0_ref, rk_ref, lg_ref, lb_ref, o_ref, sf_ref, st):
        ci = pl.program_id(1)

        @pl.when(ci == 0)
        def _():
            st[...] = s0_ref[0]

        lw = lw_ref[...]
        lc = _cumsum_rows(lw, c)
        rr, kx, vv, bb = r_ref[...], k_ref[...], v_ref[...], b_ref[...]
        g_inv = jnp.exp(-lc)
        g_end = jnp.exp(lc[c - 1:c, :] - lc)
        hs = lambda a: _heads(a, nh, hd)
        at, rt = hs(-kk_ref[...] * jnp.exp(lc - lw)), hs(rr * jnp.exp(lc))
        bt, kt = hs(bb * g_inv), hs(kx * g_inv)
        bg, kg = hs(bb * g_end), hs(kx * g_end)
        w_end = hs(jnp.exp(lc[c - 1:c, :]))
        v_h = hs(vv)
        ri = lax.broadcasted_iota(jnp.int32, (c, c), 0)
        cj = lax.broadcasted_iota(jnp.int32, (c, c), 1)
        strict = ri > cj
        incl2 = lax.broadcasted_iota(jnp.int32, (c, 2 * c), 0) >= (lax.broadcasted_iota(jnp.int32, (c, 2 * c), 1) & (c - 1))
        s_all = st[...]
        pp = RWKV_PASSES
        lhs = jnp.concatenate([at, rt], axis=1)
        a_all = _bmm(lhs, jnp.concatenate([bt, kt], axis=1), 2, 2, pp['a'])
        x_all = _bmm(lhs, s_all, 2, 2, pp['x'])
        tinv = _unit_lower_inverse(jnp.where(strict, a_all[:, :c, :c], 0.0), c, pp['inv'])
        a_ak = jnp.where(strict, a_all[:, :c, c:], 0.0)
        u = _bmm(tinv, x_all[:, :c] + _bmm(a_ak, v_h, 2, 1, pp['u']), 2, 1, pp['u'])
        uv = jnp.concatenate([u, v_h], axis=1)
        y = x_all[:, c:] + _bmm(jnp.where(incl2, a_all[:, c:, :], 0.0), uv, 2, 1, pp['y'])
        st[...] = s_all * w_end + _bmm(uv, jnp.concatenate([bg, kg], axis=1), 1, 1, pp['s'])
        mu = jnp.mean(y, axis=-1, keepdims=True)
        yc = y - mu
        var = jnp.mean(yc * yc, axis=-1, keepdims=True)
        yn = yc * lax.rsqrt(var + RWKV_GN_EPS) * hs(lg_ref[...]) + hs(lb_ref[...])
        bonus = jnp.sum(hs(rr * kx * rk_ref[...]), axis=-1, keepdims=True) * v_h
        o_ref[...] = (_unheads(yn + bonus) * g_ref[...]).astype(o_ref.dtype)

        @pl.when(ci == nc - 1)
        def _():
            sf_ref[0] = st[...]

    tok = pl.BlockSpec((c, RWKV_W), lambda b, ci: (b * nc + ci, 0))
    stt = pl.BlockSpec((1, nh, hd, hd), lambda b, ci: (b, 0, 0, 0))
    par = pl.BlockSpec((1, RWKV_W), lambda b, ci: (0, 0))
    return pl.pallas_call(
        body, grid=(nb, nc), name="rwkv_scan",
        in_specs=[tok] * 7 + [stt, par, par, par],
        out_specs=[tok, stt],
        out_shape=[jax.ShapeDtypeStruct((nb * t, RWKV_W), BF16), jax.ShapeDtypeStruct((nb, nh, hd, hd), F32)],
        scratch_shapes=[pltpu.VMEM((nh, hd, hd), F32)],
        compiler_params=_cp("parallel", "arbitrary"))(r, k, v, kk, bvec, logw, g, s0, prm['r_k'], prm['ln_g'], prm['ln_b'])


class _Geom:
    def __init__(self, bp, tp, bs, ts, past):
        self.bp, self.tp, self.bs, self.ts, self.past = bp, tp, bs, ts, past
        self.mp, self.ms = bp * tp, bs * ts
        self.m = self.mp + self.ms
        self.tm = math.gcd(self.m, 512)
        assert self.tm % 16 == 0 and tp % CHUNK == 0 and past % CHUNK == 0 and ts == CHUNK
        self.tq_p = min(1024, tp)
        self.kpad = -(-(past + ts) // (3 * LANES)) * (3 * LANES)
        self.tk_s = self.kpad


def _pad_cols(w, n):
    return jnp.pad(w, ((0, 0), (0, n - w.shape[1])))


def _rope_table(geo):
    half = MLA_ROPE // 2
    pos = jnp.concatenate([jnp.tile(jnp.arange(geo.tp, dtype=jnp.int32), geo.bp),
                           geo.past + jnp.tile(jnp.arange(geo.ts, dtype=jnp.int32), geo.bs)])
    inv_freq = ROPE_THETA ** (-jnp.arange(half, dtype=F32) / half)
    ang = pos.astype(F32)[:, None] * inv_freq[None, :]
    cos, sin = jnp.cos(ang), jnp.sin(ang)
    z = jnp.zeros_like(cos)
    zz = jnp.zeros((pos.shape[0], LANES - MLA_ROPE), F32)
    return jnp.concatenate([cos, cos, zz, -sin, z, zz, z, sin, zz], axis=1)


def _rwkv_cols_layout(a):
    o1 = 3 * RWKV_W
    o2 = o1 + RWKV_DECAY_LORA
    o3 = o2 + RWKV_A_LORA
    pad = [(0, 0)] * (a.ndim - 1)
    return jnp.concatenate([a[..., :o1],
                            jnp.pad(a[..., o1:o2], pad + [(0, LANES - RWKV_DECAY_LORA)]),
                            jnp.pad(a[..., o2:o3], pad + [(0, LANES - RWKV_A_LORA)]),
                            a[..., o3:]], axis=-1)


def _rwkv_cols_unlayout(a):
    o1 = 3 * RWKV_W
    return jnp.concatenate([a[..., :o1], a[..., o1:o1 + RWKV_DECAY_LORA],
                            a[..., o1 + LANES:o1 + LANES + RWKV_A_LORA], a[..., o1 + 2 * LANES:]], axis=-1)


def _even_mixer(geo, xb, rope_tab, w_in, kv_norm, w_uk, w_uv, rw, cache_ckv, cache_kr, shift_prev, wkv0):
    tm, m, mp = geo.tm, geo.m, geo.mp
    d = w_in.shape[0]
    nq = MLA_HEADS * MLA_QK
    wq = w_in[:, :nq].reshape(d, MLA_HEADS, MLA_QK)
    w_qn = wq[:, :, :MLA_NOPE].reshape(d, MLA_HEADS * MLA_NOPE).astype(BF16)
    w_qr = jnp.pad(wq[:, :, MLA_NOPE:], ((0, 0), (0, 0), (0, LANES - MLA_ROPE))).reshape(d, MLA_HEADS * LANES).astype(BF16)
    w_ckv = w_in[:, nq:nq + MLA_KV_LORA].astype(BF16)
    w_kr = _pad_cols(w_in[:, nq + MLA_KV_LORA:nq + MLA_KV_LORA + MLA_ROPE], LANES).astype(BF16)
    w_rw = _rwkv_cols_layout(w_in[:, nq + MLA_KV_LORA + MLA_ROPE:]).astype(BF16)
    scale = MLA_QK ** -0.5

    (qn,) = _mm([(xb, w_qn)], tm=tm, tn=None, epilogue=lambda acc: (acc * scale,), outs=[(None, BF16)])
    (qr,) = _mm([(xb, w_qr)], tm=tm, tn=MLA_HEADS * LANES, aux=[(rope_tab, 'row')],
                epilogue=lambda acc, tab: (_rope_lanes(acc, tab) * scale,), outs=[(MLA_HEADS * LANES, BF16)])
    ckv, ckv_b = _mm([(xb, w_ckv)], tm=tm, tn=MLA_KV_LORA, aux=[(kv_norm.reshape(1, -1), 'col')],
                     epilogue=_ep_rms, outs=[(MLA_KV_LORA, F32), (MLA_KV_LORA, BF16)])
    kr, kr_b = _mm([(xb, w_kr)], tm=tm, tn=LANES, aux=[(rope_tab, 'row')],
                   epilogue=lambda acc, tab: (_rope_lanes(acc, tab),) * 2, outs=[(LANES, F32), (LANES, BF16)])
    (rwc,) = _mm([(xb, w_rw)], tm=tm, tn=None, epilogue=lambda acc: (acc,), outs=[(None, F32)])

    w_k = w_uk.reshape(MLA_KV_LORA, -1).astype(BF16)
    w_vt = w_uv.reshape(MLA_KV_LORA, -1).T.astype(BF16)
    (kn_new,) = _mm([(ckv_b, w_k)], tm=tm, tn=None, epilogue=lambda acc: (acc,), outs=[(None, BF16)])
    vt_p = _mm_t(w_vt, ckv_b, nb=geo.bp, t=geo.tp, row0=0)
    o_p = _flash([qn, qr], [(kn_new, 0, True), (kr_b, 0, False)], vt_p,
                 nb=geo.bp, tq_total=geo.tp, tk_total=geo.tp, q_row0=0, q_off=0, tq=geo.tq_p, tk=geo.tq_p,
                 chunked=True, nh=MLA_HEADS)
    padk = geo.kpad - geo.past - geo.ts
    ckv_all = jnp.concatenate([cache_ckv.astype(BF16), ckv_b[mp:].reshape(geo.bs, geo.ts, -1),
                               jnp.zeros((geo.bs, padk, MLA_KV_LORA), BF16)], axis=1).reshape(geo.bs * geo.kpad, -1)
    kr_all = jnp.concatenate([jnp.pad(cache_kr, ((0, 0), (0, 0), (0, LANES - MLA_ROPE))).astype(BF16),
                              kr_b[mp:].reshape(geo.bs, geo.ts, -1),
                              jnp.zeros((geo.bs, padk, LANES), BF16)], axis=1).reshape(geo.bs * geo.kpad, -1)
    (kn_all,) = _mm([(ckv_all, w_k)], tm=3 * LANES, tn=None, epilogue=lambda acc: (acc,), outs=[(None, BF16)])
    vt_all = _mm_t(w_vt, ckv_all, nb=geo.bs, t=geo.kpad, row0=0)
    o_s = _flash([qn, qr], [(kn_all, 0, True), (kr_all, 0, False)], vt_all,
                 nb=geo.bs, tq_total=geo.ts, tk_total=geo.kpad, q_row0=mp, q_off=geo.past, tq=geo.ts, tk=geo.tk_s,
                 chunked=True, nh=MLA_HEADS)
    o_mla = jnp.concatenate([o_p, o_s], axis=0)

    outs_p = _rwkv_prep(rwc, jnp.zeros((geo.bp, 1, RWKV_PCOLS), F32), rw, nb=geo.bp, t=geo.tp, row0=0)
    outs_s = _rwkv_prep(rwc, _rwkv_cols_layout(shift_prev)[:, None, :], rw, nb=geo.bs, t=geo.ts, row0=mp)
    y_p, wkv_p = _rwkv_scan(*outs_p[:7], jnp.zeros((geo.bp, RWKV_HEADS, RWKV_HD, RWKV_HD), F32), rw, nb=geo.bp, t=geo.tp)
    y_s, wkv_s = _rwkv_scan(*outs_s[:7], wkv0, rw, nb=geo.bs, t=geo.ts)
    o_rwkv = jnp.concatenate([y_p, y_s], axis=0)
    states = dict(
        ckv_p=ckv[:mp].reshape(geo.bp, geo.tp, -1), ckv_s=ckv[mp:].reshape(geo.bs, geo.ts, -1),
        kr_p=kr[:mp, :MLA_ROPE].reshape(geo.bp, geo.tp, -1), kr_s=kr[mp:, :MLA_ROPE].reshape(geo.bs, geo.ts, -1),
        sh_p=_rwkv_cols_unlayout(outs_p[7][:, 0]), sh_s=_rwkv_cols_unlayout(outs_s[7][:, 0]),
        rwkv_p=wkv_p, rwkv_s=wkv_s)
    return o_mla, o_rwkv, states


def _head_rms(x, g, nh, scale=1.0):
    outs = []
    for h in range(nh):
        seg = x[:, h * LANES:(h + 1) * LANES]
        outs.append(seg * lax.rsqrt(jnp.mean(seg * seg, axis=-1, keepdims=True) + RMS_EPS) * (g * scale))
    return jnp.concatenate(outs, axis=1)


def _head_l2(x, nh, scale=1.0):
    outs = []
    for h in range(nh):
        seg = x[:, h * LANES:(h + 1) * LANES]
        outs.append(seg * (lax.rsqrt(jnp.sum(seg * seg, axis=-1, keepdims=True) + 1e-6) * scale))
    return jnp.concatenate(outs, axis=1)


def _cumsum_time(x):
    nb, t, w = x.shape

    def body(x_ref, o_ref):
        o_ref[0] = _cumsum_rows(x_ref[0], t)

    spec = pl.BlockSpec((1, t, w), lambda b: (b, 0, 0))
    return pl.pallas_call(body, grid=(nb,), in_specs=[spec], out_specs=spec, name="cumsum_time",
                          out_shape=jax.ShapeDtypeStruct(x.shape, F32), compiler_params=_cp("parallel"))(x)


def _gdn_prep(cols, conv_prev, conv_w, *, nb, t, row0):
    tt = min(256, t)
    assert t % tt == 0 and row0 % tt == 0
    nt = t // tt
    rb0 = row0 // tt
    w = GDN_W

    def body(c_ref, cp_ref, cw_ref, q_o, k_o, v_o, carry):
        ti = pl.program_id(1)

        @pl.when(ti == 0)
        def _():
            carry[...] = cp_ref[0]

        c = c_ref[...]
        ext = jnp.concatenate([carry[...], c], axis=0)
        carry[...] = c[tt - 8:tt, :]
        cw = cw_ref[...]
        acc = c * cw[GDN_CONV - 1:GDN_CONV, :]
        for j in range(GDN_CONV - 1):
            back = GDN_CONV - 1 - j
            acc = acc + ext[8 - back:8 - back + tt, :] * cw[j:j + 1, :]
        qkv = _silu(acc)
        q_o[...] = _head_l2(qkv[:, :w], GDN_HEADS, GDN_HD ** -0.5)
        k_o[...] = _head_l2(qkv[:, w:2 * w], GDN_HEADS)
        v_o[...] = qkv[:, 2 * w:]

    tok = pl.BlockSpec((tt, w), lambda b, ti: (b * nt + ti, 0))
    return pl.pallas_call(
        body, grid=(nb, nt), name="gdn_prep",
        in_specs=[pl.BlockSpec((tt, GDN_QKV), lambda b, ti: (rb0 + b * nt + ti, 0)),
                  pl.BlockSpec((1, 8, GDN_QKV), lambda b, ti: (b, 0, 0)),
                  pl.BlockSpec((8, GDN_QKV), lambda b, ti: (0, 0))],
        out_specs=[tok] * 3,
        out_shape=[jax.ShapeDtypeStruct((nb * t, w), F32)] * 3,
        scratch_shapes=[pltpu.VMEM((8, GDN_QKV), F32)],
        compiler_params=_cp("parallel", "arbitrary"))(cols, conv_prev, conv_w)


def _gdn_scan(q, k, v, gb, z, s0, norm_g, *, nb, t, gb_row0, z_row0):
    c = min(SCAN_CHUNK, t)
    assert t % c == 0 and gb_row0 % c == 0 and z_row0 % c == 0
    nc = t // c
    nh, hd = GDN_HEADS, GDN_HD

    def body(q_ref, k_ref, v_ref, gb_ref, z_ref, s0_ref, ng_ref, o_ref, sf_ref, st):
        ci = pl.program_id(1)

        @pl.when(ci == 0)
        def _():
            st[...] = s0_ref[0]

        gbv = gb_ref[...]
        gc = _cumsum_rows(gbv, c)
        gct = gc.T
        ri = lax.broadcasted_iota(jnp.int32, (c, c), 0)
        cj = lax.broadcasted_iota(jnp.int32, (c, c), 1)
        tril, strict = ri >= cj, ri > cj
        hs = lambda a: _heads(a, nh, hd)
        k_h, q_h, v_h = hs(k_ref[...]), hs(q_ref[...]), hs(v_ref[...])
        gcol = jnp.stack([gc[:, h:h + 1] for h in range(nh)], axis=0)
        grow = gct[:nh][:, None, :]
        bcol = jnp.stack([gbv[:, nh + h:nh + h + 1] for h in range(nh)], axis=0)
        decay = jnp.where(tril, jnp.exp(jnp.where(tril, gcol - grow, 0.0)), 0.0)
        pp = GDN_PASSES
        kb = k_h * bcol
        prods = _bmm(jnp.concatenate([kb, q_h], axis=1), k_h, 2, 2, pp['a'])
        low = jnp.where(strict, prods[:, :c] * decay, 0.0)
        a_qk = jnp.where(tril, prods[:, c:] * decay, 0.0)
        tinv = _unit_lower_inverse(-low, c, pp['inv'])
        e_g = jnp.exp(gcol)
        uw = _bmm(tinv, jnp.concatenate([v_h * bcol, kb * e_g], axis=2), 2, 1, pp['u'])
        s_all = st[...]
        ws_qs = _bmm(jnp.concatenate([uw[:, :, hd:], q_h * e_g], axis=1), s_all, 2, 1, pp['x'])
        v_new = uw[:, :, :hd] - ws_qs[:, :c]
        o = ws_qs[:, c:] + _bmm(a_qk, v_new, 2, 1, pp['y'])
        g_last = gcol[:, c - 1:c, :]
        st[...] = s_all * jnp.exp(g_last) + _bmm(k_h * jnp.exp(g_last - gcol), v_new, 1, 1, pp['s'])
        o = o * lax.rsqrt(jnp.mean(o * o, axis=-1, keepdims=True) + RMS_EPS) * ng_ref[...]
        o_ref[...] = (_unheads(o) * _silu(z_ref[...])).astype(o_ref.dtype)

        @pl.when(ci == nc - 1)
        def _():
            sf_ref[0] = st[...]

    tok = pl.BlockSpec((c, GDN_W), lambda b, ci: (b * nc + ci, 0))
    stt = pl.BlockSpec((1, nh, hd, hd), lambda b, ci: (b, 0, 0, 0))
    return pl.pallas_call(
        body, grid=(nb, nc), name="gdn_scan",
        in_specs=[tok] * 3 + [pl.BlockSpec((c, LANES), lambda b, ci: (gb_row0 // c + b * nc + ci, 0)),
                              pl.BlockSpec((c, GDN_W), lambda b, ci: (z_row0 // c + b * nc + ci, 0)),
                              stt, pl.BlockSpec((1, hd), lambda b, ci: (0, 0))],
        out_specs=[tok, stt],
        out_shape=[jax.ShapeDtypeStruct((nb * t, GDN_W), BF16), jax.ShapeDtypeStruct((nb, nh, hd, hd), F32)],
        scratch_shapes=[pltpu.VMEM((nh, hd, hd), F32)],
        compiler_params=_cp("parallel", "arbitrary"))(q, k, v, gb, z, s0, norm_g)


def _odd_mixer(geo, xb, w_in, fx, gd, cache_k, cache_v, cache_logf, conv_prev, gdn0):
    tm, m, mp = geo.tm, geo.m, geo.mp
    fw = FOX_W
    o_f = 4 * fw
    o_g = o_f + FOX_HEADS
    w_q, w_k, w_v, w_gate = (w_in[:, i * fw:(i + 1) * fw].astype(BF16) for i in range(4))
    w_f = _pad_cols(w_in[:, o_f:o_g], LANES).astype(BF16)
    w_qkv = w_in[:, o_g:o_g + GDN_QKV].astype(BF16)
    w_ab = _pad_cols(w_in[:, o_g + GDN_QKV:o_g + GDN_QKV + 2 * GDN_HEADS], LANES).astype(BF16)
    w_z = w_in[:, o_g + GDN_QKV + 2 * GDN_HEADS:].astype(BF16)
    qn, kn = fx['q_norm'].reshape(1, -1), fx['k_norm'].reshape(1, -1)
    f_bias = jnp.pad(fx['f_bias'], (0, LANES - FOX_HEADS)).reshape(1, -1)
    scale = FOX_HD ** -0.5

    (q,) = _mm([(xb, w_q)], tm=tm, tn=fw, aux=[(qn, 'row0')],
               epilogue=lambda acc, g: (_head_rms(acc, g, FOX_HEADS, scale),), outs=[(fw, BF16)])
    k, k_b = _mm([(xb, w_k)], tm=tm, tn=fw, aux=[(kn, 'row0')],
                 epilogue=lambda acc, g: (_head_rms(acc, g, FOX_HEADS),) * 2, outs=[(fw, F32), (fw, BF16)])
    (v,) = _mm([(xb, w_v)], tm=tm, tn=None, epilogue=lambda acc: (acc,), outs=[(None, F32)])
    w_vt = w_v.T
    vt_p = _mm_t(w_vt, xb, nb=geo.bp, t=geo.tp, row0=0)
    vt_s = _mm_t(w_vt, xb, nb=geo.bs, t=geo.ts, row0=mp)
    (gate,) = _mm([(xb, w_gate)], tm=tm, tn=None, epilogue=lambda acc: (_sigmoid(acc),), outs=[(None, F32)])
    (logf,) = _mm([(xb, w_f)], tm=tm, tn=LANES, aux=[(f_bias, 'col')],
                  epilogue=lambda acc, fb: (-_softplus(-(acc + fb)),), outs=[(LANES, F32)])

    cum_p = _cumsum_time(logf[:mp].reshape(geo.bp, geo.tp, LANES))[:, :, :FOX_HEADS]
    bias_p = (jnp.transpose(cum_p, (0, 2, 1))[:, :, None, :], jnp.transpose(cum_p, (0, 2, 1))[..., None])
    o_p = _flash([q], [(k_b, 0, True)], vt_p, nb=geo.bp, tq_total=geo.tp, tk_total=geo.tp, q_row0=0, q_off=0,
                 tq=geo.tq_p, tk=geo.tq_p, chunked=False, nh=FOX_HEADS, bias=bias_p, gate=gate)
    padk = geo.kpad - geo.past - geo.ts
    lf_all = jnp.concatenate([jnp.pad(cache_logf, ((0, 0), (0, 0), (0, LANES - FOX_HEADS))),
                              logf[mp:].reshape(geo.bs, geo.ts, LANES), jnp.zeros((geo.bs, padk, LANES), F32)], axis=1)
    cum_s = jnp.transpose(_cumsum_time(lf_all)[:, :, :FOX_HEADS], (0, 2, 1))
    bias_s = (cum_s[:, :, None, geo.past:geo.past + geo.ts], cum_s[..., None])
    k_all = jnp.concatenate([cache_k.reshape(geo.bs, geo.past, fw).astype(BF16), k_b[mp:].reshape(geo.bs, geo.ts, fw),
                             jnp.zeros((geo.bs, padk, fw), BF16)], axis=1).reshape(geo.bs * geo.kpad, fw)
    vt_all = jnp.concatenate([jnp.transpose(cache_v.reshape(geo.bs, geo.past, fw), (0, 2, 1)).astype(BF16),
                              vt_s.reshape(geo.bs, fw, geo.ts), jnp.zeros((geo.bs, fw, padk), BF16)],
                             axis=2).reshape(geo.bs * fw, geo.kpad)
    o_s = _flash([q], [(k_all, 0, True)], vt_all, nb=geo.bs, tq_total=geo.ts, tk_total=geo.kpad, q_row0=mp,
                 q_off=geo.past, tq=geo.ts, tk=geo.tk_s, chunked=False, nh=FOX_HEADS, bias=bias_s, gate=gate)
    o_fox = jnp.concatenate([o_p, o_s], axis=0)

    (qkv_raw,) = _mm([(xb, w_qkv)], tm=tm, tn=None, epilogue=lambda acc: (acc,), outs=[(None, F32)])
    lane = jnp.arange(LANES)
    neg_a = jnp.where(lane < GDN_HEADS, -jnp.exp(jnp.pad(gd['a_log'], (0, LANES - GDN_HEADS))), 0.0).reshape(1, -1)
    dtb = jnp.pad(gd['dt_bias'], (0, LANES - GDN_HEADS)).reshape(1, -1)
    is_g = (lane < GDN_HEADS).astype(F32).reshape(1, -1)
    (gb,) = _mm([(xb, w_ab)], tm=tm, tn=LANES, aux=[(neg_a, 'col'), (dtb, 'col'), (is_g, 'col')],
                epilogue=lambda acc, na, db, ig: (jnp.where(ig > 0.5, na * _softplus(acc + db), _sigmoid(acc)),),
                outs=[(LANES, F32)])
    (z,) = _mm([(xb, w_z)], tm=tm, tn=None, epilogue=lambda acc: (acc,), outs=[(None, F32)])
    conv_w = jnp.pad(gd['conv_w'], ((0, 8 - GDN_CONV), (0, 0)))
    norm_g = gd['norm'].reshape(1, -1)
    prev8 = lambda a: jnp.pad(a, ((0, 0), (8 - (GDN_CONV - 1), 0), (0, 0)))
    q_p, k_p, v_p = _gdn_prep(qkv_raw, jnp.zeros((geo.bp, 8, GDN_QKV), F32), conv_w, nb=geo.bp, t=geo.tp, row0=0)
    q_s, k_s, v_s = _gdn_prep(qkv_raw, prev8(conv_prev), conv_w, nb=geo.bs, t=geo.ts, row0=mp)
    y_p, s_p = _gdn_scan(q_p, k_p, v_p, gb, z, jnp.zeros((geo.bp, GDN_HEADS, GDN_HD, GDN_HD), F32), norm_g,
                         nb=geo.bp, t=geo.tp, gb_row0=0, z_row0=0)
    y_s, s_s = _gdn_scan(q_s, k_s, v_s, gb, z, gdn0, norm_g, nb=geo.bs, t=geo.ts, gb_row0=mp, z_row0=mp)
    o_gdn = jnp.concatenate([y_p, y_s], axis=0)
    nconv = GDN_CONV - 1
    last_rows = lambda row0, nb, t: (row0 + jnp.arange(nb)[:, None] * t + (t - nconv + jnp.arange(nconv))[None, :]).reshape(-1)
    states = dict(
        fk_p=k[:mp].reshape(geo.bp, geo.tp, FOX_HEADS, FOX_HD), fk_s=k[mp:].reshape(geo.bs, geo.ts, FOX_HEADS, FOX_HD),
        fv_p=v[:mp].reshape(geo.bp, geo.tp, FOX_HEADS, FOX_HD), fv_s=v[mp:].reshape(geo.bs, geo.ts, FOX_HEADS, FOX_HD),
        fl_p=logf[:mp, :FOX_HEADS].reshape(geo.bp, geo.tp, FOX_HEADS),
        fl_s=logf[mp:, :FOX_HEADS].reshape(geo.bs, geo.ts, FOX_HEADS),
        cv_p=jnp.take(qkv_raw, last_rows(0, geo.bp, geo.tp), axis=0).reshape(geo.bp, nconv, GDN_QKV),
        cv_s=jnp.take(qkv_raw, last_rows(mp, geo.bs, geo.ts), axis=0).reshape(geo.bs, nconv, GDN_QKV),
        gdn_p=s_p, gdn_s=s_s)
    return o_fox, o_gdn, states


MOE_BLOCK = 512
MOE_COMBINE_ROWS = 256


def _moe_route(x, router, router_bias, *, tm):
    m = x.shape[0]
    assert m % tm == 0
    per_group = N_EXPERTS // N_GROUPS

    def lane_max(v):
        return jnp.max(v, axis=-1, keepdims=True)

    def first_lane(mask, lane):
        return jnp.min(jnp.where(mask, lane.astype(F32), float(LANES)), axis=-1, keepdims=True).astype(jnp.int32)

    def group_all(v, lane, op):
        sh = 1
        while sh < per_group:
            partner = jnp.where((lane & sh) == 0, pltpu.roll(v, LANES - sh, 1), pltpu.roll(v, sh, 1))
            v = op(v, partner)
            sh *= 2
        return v

    def body(x_ref, r_ref, b_ref, e_o, g_o, k_o, cnt_o, carry):
        i = pl.program_id(0)

        @pl.when(i == 0)
        def _():
            carry[...] = jnp.zeros(carry.shape, F32)

        lane = lax.broadcasted_iota(jnp.int32, (tm, LANES), 1)
        valid = lane < N_EXPERTS
        neg = -jnp.inf
        scores = _sigmoid(_dot(x_ref[...], r_ref[...]))
        biased = jnp.where(valid, scores + b_ref[...], neg)
        m1 = group_all(biased, lane, jnp.maximum)
        first = group_all(jnp.where(biased == m1, lane, LANES), lane, jnp.minimum)
        m2 = group_all(jnp.where(lane == first, neg, biased), lane, jnp.maximum)
        grp = jnp.where(valid & ((lane & (per_group - 1)) == 0), m1 + m2, neg)
        emask = jnp.zeros((tm, LANES), jnp.bool_)
        for _ in range(TOPK_GROUPS):
            idx = first_lane(grp == lane_max(grp), lane)
            emask = emask | ((lane - idx >= 0) & (lane - idx < per_group))
            grp = jnp.where(lane == idx, neg, grp)
        cur = jnp.where(emask & valid, biased, neg)
        picks = []
        sel = jnp.zeros((tm, LANES), jnp.bool_)
        for _ in range(TOP_K):
            idx = first_lane(cur == lane_max(cur), lane)
            pick = lane == idx
            picks.append((idx, pick))
            sel = sel | pick
            cur = jnp.where(pick, neg, cur)
        selw = jnp.where(sel, scores, 0.0)
        gates_dense = selw / jnp.sum(selw, axis=-1, keepdims=True) * ROUTED_SCALE
        ri = lax.broadcasted_iota(jnp.int32, (tm, tm), 0)
        cj = lax.broadcasted_iota(jnp.int32, (tm, tm), 1)
        p01 = jnp.where(sel, 1.0, 0.0)
        before = _dot(jnp.where(ri > cj, 1.0, 0.0).astype(BF16), p01.astype(BF16)) + carry[...]
        carry[...] = carry[...] + jnp.sum(p01, axis=0, keepdims=True)
        cnt_o[...] = carry[...]
        e_out = jnp.zeros((tm, LANES), jnp.int32)
        g_out = jnp.zeros((tm, LANES), F32)
        k_out = jnp.zeros((tm, LANES), jnp.int32)
        for j, (idx, pick) in enumerate(picks):
            e_out = jnp.where(lane == j, idx, e_out)
            g_out = jnp.where(lane == j, jnp.sum(jnp.where(pick, gates_dense, 0.0), axis=-1, keepdims=True), g_out)
            rank = jnp.sum(jnp.where(pick, before, 0.0), axis=-1, keepdims=True)
            k_out = jnp.where(lane == j, rank.astype(jnp.int32), k_out)
        e_o[...] = e_out
        g_o[...] = g_out
        k_o[...] = k_out

    tok = pl.BlockSpec((tm, LANES), lambda i: (i, 0))
    one = pl.BlockSpec((1, LANES), lambda i: (0, 0))
    return pl.pallas_call(
        body, grid=(m // tm,), name="moe_route",
        in_specs=[pl.BlockSpec((tm, x.shape[1]), lambda i: (i, 0)), pl.BlockSpec(router.shape, lambda i: (0, 0)), one],
        out_specs=[tok, tok, tok, one],
        out_shape=[jax.ShapeDtypeStruct((m, LANES), jnp.int32), jax.ShapeDtypeStruct((m, LANES), F32),
                   jax.ShapeDtypeStruct((m, LANES), jnp.int32), jax.ShapeDtypeStruct((1, LANES), F32)],
        scratch_shapes=[pltpu.VMEM((1, LANES), F32)],
        compiler_params=_cp("arbitrary"))(x, router, router_bias)


def _moe_dispatch(x_words, dest, cap, *, tb):
    m, wd = x_words.shape
    nt = m // tb

    def body(dest_ref, x_ref, out_ref, sem):
        def row_copy(i, j):
            return pltpu.make_async_copy(x_ref.at[pl.ds(i, 1)], out_ref.at[pl.ds(dest_ref[0, 0, i * TOP_K + j], 1)], sem)

        def issue(i, carry):
            for j in range(TOP_K):
                row_copy(i, j).start()
            return carry

        lax.fori_loop(0, tb, issue, 0, unroll=8)
        for _ in range(TOP_K):
            pltpu.make_async_copy(x_ref, out_ref.at[pl.ds(0, tb)], sem).wait()

    return pl.pallas_call(
        body, grid=(nt,), name="moe_dispatch",
        in_specs=[pl.BlockSpec((1, 1, tb * TOP_K), lambda i: (i, 0, 0), memory_space=pltpu.SMEM),
                  pl.BlockSpec((tb, wd), lambda i: (i, 0))],
        out_specs=pl.BlockSpec(memory_space=pl.ANY),
        out_shape=jax.ShapeDtypeStruct((cap, wd), x_words.dtype),
        scratch_shapes=[pltpu.SemaphoreType.DMA(())],
        compiler_params=_cp("arbitrary"))(dest, x_words)


def _pack_halves(y):
    n = y.shape[1] // 2
    bits = lambda a: lax.bitcast_convert_type(a.astype(BF16).astype(F32), jnp.uint32)
    return (bits(y[:, :n]) >> 16) | (bits(y[:, n:]) & jnp.uint32(0xFFFF0000))


def _unpack_halves(w):
    lo = lax.bitcast_convert_type(w << 16, F32).astype(BF16)
    hi = lax.bitcast_convert_type(w & jnp.uint32(0xFFFF0000), F32).astype(BF16)
    return lo, hi


def _ep_ln_packed(acc, x, g, b):
    y = _ep_ln(acc, x, g, b)
    return y, _pack_halves(y)


def _moe_experts(xs_words, blk_exp, blk_next, blk_rows, n_used, w_gate, w_up, w_down, layer):
    cap, half = xs_words.shape
    d = 2 * half
    nblk = cap // MOE_BLOCK
    ed = w_gate.shape[3]

    def body(be_ref, bn_ref, br_ref, nu_ref, x_ref, wg_hbm, wu_hbm, wd_hbm, o_ref, wg_f, wu_f, wd_f, wg_b, wu_b, wd_b,
             slot_ref, sem):
        i = pl.program_id(0)

        def fetch(e, slot):
            return [pltpu.make_async_copy(src.at[layer, e], dst.at[slot], sem.at[slot])
                    for src, dst in ((wg_hbm, wg_f), (wu_hbm, wu_f), (wd_hbm, wd_f))]

        @pl.when(i < nu_ref[0])
        def _():
            first = i == 0
            e = be_ref[i]

            @pl.when(first)
            def _():
                slot_ref[0] = 0
                for cp in fetch(e, 0):
                    cp.start()

            @pl.when(first | (e != be_ref[jnp.maximum(i - 1, 0)]))
            def _():
                slot = jnp.where(first, 0, 1 - slot_ref[0])
                slot_ref[0] = slot
                for cp in fetch(e, slot):
                    cp.wait()
                wg_b[...] = wg_f[slot].astype(BF16)
                wu_b[...] = wu_f[slot].astype(BF16)
                wd_b[...] = wd_f[slot].astype(BF16)

                @pl.when(bn_ref[i] != e)
                def _():
                    for cp in fetch(bn_ref[i], 1 - slot):
                        cp.start()

            row = lax.broadcasted_iota(jnp.int32, (MOE_BLOCK, 1), 0)
            lo, hi = _unpack_halves(jnp.where(row < br_ref[i], x_ref[...], jnp.uint32(0)))
            hg = _dot(lo, wg_b[:half]) + _dot(hi, wg_b[half:])
            hu = _dot(lo, wu_b[:half]) + _dot(hi, wu_b[half:])
            o_ref[...] = _pack_halves(_dot((_silu(hg) * hu).astype(BF16), wd_b[...]))

    blk = lambda i, be, bn, br, nu: (jnp.minimum(i, nu[0] - 1), 0)
    hbm = pl.BlockSpec(memory_space=pl.ANY)
    grid_spec = pltpu.PrefetchScalarGridSpec(
        num_scalar_prefetch=4, grid=(nblk,),
        in_specs=[pl.BlockSpec((MOE_BLOCK, half), blk), hbm, hbm, hbm],
        out_specs=pl.BlockSpec((MOE_BLOCK, half), blk),
        scratch_shapes=[pltpu.VMEM((2, d, ed), F32), pltpu.VMEM((2, d, ed), F32), pltpu.VMEM((2, ed, d), F32),
                        pltpu.VMEM((d, ed), BF16), pltpu.VMEM((d, ed), BF16), pltpu.VMEM((ed, d), BF16),
                        pltpu.SMEM((1,), jnp.int32), pltpu.SemaphoreType.DMA((2,))])
    return pl.pallas_call(body, grid_spec=grid_spec, out_shape=jax.ShapeDtypeStruct((cap, half), jnp.uint32), name="moe_experts",
                          compiler_params=_cp("arbitrary"))(blk_exp, blk_next, blk_rows, n_used, xs_words, w_gate, w_up, w_down)


def _moe_combine(ys_words, dest, gates, *, tc):
    cap, half = ys_words.shape
    d = 2 * half
    m = gates.shape[0]
    nt = m // tc

    def body(dest0_ref, destn_ref, g_ref, y_ref, o_ref, buf, sem):
        i = pl.program_id(0)

        def issue(dref, slot):
            def one(r, carry):
                for j in range(TOP_K):
                    pltpu.make_async_copy(y_ref.at[pl.ds(dref[0, 0, r * TOP_K + j], 1)],
                                          buf.at[slot, j, pl.ds(r, 1)], sem.at[slot]).start()
                return carry
            lax.fori_loop(0, tc, one, 0, unroll=8)

        @pl.when(i == 0)
        def _():
            issue(dest0_ref, 0)

        @pl.when(i + 1 < nt)
        def _():
            issue(destn_ref, (i + 1) % 2)

        slot = i % 2
        for j in range(TOP_K):
            pltpu.make_async_copy(y_ref.at[pl.ds(0, tc)], buf.at[slot, j], sem.at[slot]).wait()
        g = g_ref[...]
        acc_lo = acc_hi = None
        for j in range(TOP_K):
            lo, hi = _unpack_halves(buf[slot, j])
            gj = g[:, j:j + 1]
            acc_lo = lo.astype(F32) * gj if acc_lo is None else acc_lo + lo.astype(F32) * gj
            acc_hi = hi.astype(F32) * gj if acc_hi is None else acc_hi + hi.astype(F32) * gj
        o_ref[:, :half] = acc_lo
        o_ref[:, half:] = acc_hi

    return pl.pallas_call(
        body, grid=(nt,), name="moe_combine",
        in_specs=[pl.BlockSpec((1, 1, tc * TOP_K), lambda i: (0, 0, 0), memory_space=pltpu.SMEM),
                  pl.BlockSpec((1, 1, tc * TOP_K), lambda i: (jnp.minimum(i + 1, nt - 1), 0, 0), memory_space=pltpu.SMEM),
                  pl.BlockSpec((tc, LANES), lambda i: (i, 0)),
                  pl.BlockSpec(memory_space=pl.ANY)],
        out_specs=pl.BlockSpec((tc, d), lambda i: (i, 0)),
        out_shape=jax.ShapeDtypeStruct((m, d), F32),
        scratch_shapes=[pltpu.VMEM((2, TOP_K, tc, half), jnp.uint32), pltpu.SemaphoreType.DMA((2,))],
        compiler_params=_cp("arbitrary"))(dest, dest, gates, ys_words)


def _moe_layer(geo, x_res, xb, x_words, layer, router, router_bias, w_gate, w_up, w_down, ws_gate, ws_up, ws_down,
               ln_g, ln_b, out_dtype):
    m, d = x_res.shape
    tm = geo.tm
    eidx, gates, rank, counts = _moe_route(xb, _pad_cols(router, LANES).astype(BF16),
                                           jnp.pad(router_bias, (0, LANES - N_EXPERTS)).reshape(1, -1), tm=tm)
    cnt = counts[0, :N_EXPERTS].astype(jnp.int32)
    padded = (cnt + MOE_BLOCK - 1) // MOE_BLOCK * MOE_BLOCK
    pad_end = jnp.cumsum(padded)
    pad_start = pad_end - padded
    n_blocks = -(-(m * TOP_K) // MOE_BLOCK) + N_EXPERTS
    cap = n_blocks * MOE_BLOCK
    blk_row0 = jnp.arange(n_blocks, dtype=jnp.int32) * MOE_BLOCK
    blk_exp = jnp.minimum(jnp.sum((pad_end[None, :] <= blk_row0[:, None]).astype(jnp.int32), axis=1), N_EXPERTS - 1)
    blk_rows = jnp.clip(cnt[blk_exp] - (blk_row0 - pad_start[blk_exp]), 0, MOE_BLOCK).astype(jnp.int32)
    n_used = (pad_end[-1:] // MOE_BLOCK).astype(jnp.int32)
    eid = jnp.arange(N_EXPERTS, dtype=jnp.int32)
    later = jnp.where((eid[None, :] > eid[:, None]) & (cnt[None, :] > 0), eid[None, :], N_EXPERTS)
    next_e = jnp.min(later, axis=1)
    next_e = jnp.where(next_e == N_EXPERTS, eid, next_e)
    blk_next = next_e[blk_exp].astype(jnp.int32)
    dest = jnp.sum(jnp.where(eidx[:, :TOP_K, None] == eid, pad_start, 0), axis=-1) + rank[:, :TOP_K]

    tb = tm
    xs_words = _moe_dispatch(x_words, dest.reshape(m // tb, 1, tb * TOP_K), cap, tb=tb)
    ys = _moe_experts(xs_words, blk_exp, blk_next, blk_rows, n_used, w_gate, w_up, w_down, layer)
    tc = min(MOE_COMBINE_ROWS, tm)
    routed = _moe_combine(ys, dest.reshape(m // tc, 1, tc * TOP_K), gates, tc=tc)

    sd = ws_gate.shape[1]
    w_sh = jnp.concatenate([ws_gate, ws_up], axis=1).astype(BF16)
    (hs,) = _mm([(xb, w_sh)], tm=tm, tn=2 * sd, epilogue=lambda acc: (_silu(acc[:, :sd]) * acc[:, sd:],), outs=[(sd, BF16)])
    (y,) = _mm([(hs, ws_down.astype(BF16))], tm=tm, tn=d,
               aux=[(x_res, 'rowcol'), (routed, 'rowcol'), (ln_g.reshape(1, -1), 'col'), (ln_b.reshape(1, -1), 'col')],
               epilogue=lambda acc, xr, rt, g, b: (_ep_ln(acc + rt, xr, g, b),), outs=[(d, out_dtype)], name="moe_out")
    return y


def _rwkv_params(li, mu, w0, w2, a0, a2, g2, k_k, k_a, r_k, ln_g, ln_b):
    row = lambda a: a[li].reshape(1, -1)
    padr = lambda a, n: jnp.pad(a, ((0, n - a.shape[0]), (0, 0)))
    return dict(mu=_rwkv_cols_layout(mu[li])[None, :], w0=row(w0), a0=row(a0), k_k=row(k_k), k_a=row(k_a),
                w2=padr(w2[li], LANES), a2=padr(a2[li], LANES), g2=g2[li], r_k=row(r_k), ln_g=row(ln_g), ln_b=row(ln_b))


def _mix_out(geo, x_res, o_a, o_b, w_out, ln_g, ln_b):
    d = x_res.shape[1]
    ka = o_a.shape[1]
    return _mm([(o_a, w_out[:ka].astype(BF16)), (o_b, w_out[ka:].astype(BF16))], tm=geo.tm, tn=d,
               aux=[(x_res, 'rowcol'), (ln_g.reshape(1, -1), 'col'), (ln_b.reshape(1, -1), 'col')],
               epilogue=_ep_ln_packed, outs=[(d, BF16), (d // 2, jnp.uint32)], name="mix_out")


def kernel(x_prompt, x_sample, cache_mla_ckv, cache_mla_krope, state_rwkv_shift, state_rwkv_wkv, cache_fox_k, cache_fox_v, cache_fox_logf, state_gdn_conv, state_gdn_wkv, ln1_g, ln1_b, ln2_g, ln2_b, ev_w_in, ev_w_out, mla_kv_norm, mla_w_uk, mla_w_uv, rwkv_mu, rwkv_w0, rwkv_w2, rwkv_a0, rwkv_a2, rwkv_g2, rwkv_k_k, rwkv_k_a, rwkv_r_k, rwkv_ln_g, rwkv_ln_b, od_w_in, od_w_out, fox_q_norm, fox_k_norm, fox_f_bias, gdn_conv_w, gdn_a_log, gdn_dt_bias, gdn_norm, moe_router, moe_router_bias, moe_w_gate, moe_w_up, moe_w_down, moe_ws_gate, moe_ws_up, moe_ws_down):
    bp, tp, d = x_prompt.shape
    bs, ts, _ = x_sample.shape
    geo = _Geom(bp, tp, bs, ts, cache_mla_ckv.shape[2])
    x = jnp.concatenate([x_prompt.reshape(bp * tp, d), x_sample.reshape(bs * ts, d)], axis=0)
    xb = x.astype(BF16)
    rope_tab = _rope_table(geo)
    st = {}
    n_layers = ln1_g.shape[0]
    for layer in range(n_layers):
        li = layer // 2
        if layer % 2 == 0:
            rw = _rwkv_params(li, rwkv_mu, rwkv_w0, rwkv_w2, rwkv_a0, rwkv_a2, rwkv_g2, rwkv_k_k, rwkv_k_a, rwkv_r_k,
                              rwkv_ln_g, rwkv_ln_b)
            o_a, o_b, new = _even_mixer(geo, xb, rope_tab, ev_w_in[li], mla_kv_norm[li], mla_w_uk[li], mla_w_uv[li], rw,
                                        cache_mla_ckv[li], cache_mla_krope[li], state_rwkv_shift[li], state_rwkv_wkv[li])
            w_out = ev_w_out[li]
        else:
            fx = {'q_norm': fox_q_norm[li], 'k_norm': fox_k_norm[li], 'f_bias': fox_f_bias[li]}
            gd = {'conv_w': gdn_conv_w[li], 'a_log': gdn_a_log[li], 'dt_bias': gdn_dt_bias[li], 'norm': gdn_norm[li]}
            o_a, o_b, new = _odd_mixer(geo, xb, od_w_in[li], fx, gd, cache_fox_k[li], cache_fox_v[li], cache_fox_logf[li],
                                       state_gdn_conv[li], state_gdn_wkv[li])
            w_out = od_w_out[li]
        for name, val in new.items():
            st.setdefault(name, []).append(val)
        xb, x_words = _mix_out(geo, x, o_a, o_b, w_out, ln1_g[layer], ln1_b[layer])
        last = layer == n_layers - 1
        x = _moe_layer(geo, xb, xb, x_words, layer, moe_router[layer], moe_router_bias[layer], moe_w_gate, moe_w_up,
                       moe_w_down, moe_ws_gate[layer], moe_ws_up[layer], moe_ws_down[layer], ln2_g[layer], ln2_b[layer],
                       F32 if last else BF16)
        xb = x
    names = ('ckv', 'kr', 'sh', 'rwkv', 'fk', 'fv', 'fl', 'cv', 'gdn')
    return ((x[:geo.mp].reshape(bp, tp, d), x[geo.mp:].reshape(bs, ts, d))
            + tuple(jnp.stack(st[n + '_p']) for n in names) + tuple(jnp.stack(st[n + '_s']) for n in names))
```

```python
import functools
import math

import jax
import jax.numpy as jnp
from jax import lax
from jax.experimental import pallas as pl
from jax.experimental.pallas import tpu as pltpu

F32 = jnp.float32
BF16 = jnp.bfloat16
HP = lax.Precision.HIGHEST

D_MODEL = 2048
DEPTH = 2
DN_ALPHA = float((2.0 * DEPTH) ** 0.25)
LN_EPS = 1e-5
RMS_EPS = 1e-6
CHUNK = 64
LANES = 128

MLA_HEADS, MLA_NOPE, MLA_ROPE, MLA_VDIM, MLA_KV_LORA = 8, 128, 64, 128, 512
MLA_QK = MLA_NOPE + MLA_ROPE
ROPE_THETA = 10000.0
RWKV_HEADS, RWKV_HD = 16, 64
RWKV_W = RWKV_HEADS * RWKV_HD
RWKV_DECAY_LORA, RWKV_A_LORA, RWKV_G_LORA = 96, 96, 256
RWKV_GN_EPS = 64e-5
RWKV_PCOLS = 3 * RWKV_W + 2 * LANES + RWKV_G_LORA
FOX_HEADS, FOX_HD = 8, 128
FOX_W = FOX_HEADS * FOX_HD
GDN_HEADS, GDN_HD, GDN_CONV = 8, 128, 4
GDN_W = GDN_HEADS * GDN_HD
GDN_QKV = 3 * GDN_W
N_EXPERTS, TOP_K, N_GROUPS, TOPK_GROUPS = 64, 6, 8, 4
EXPERT_DIM, SHARED_DIM = 512, 512
ROUTED_SCALE = 2.5
SCAN_CHUNK = 64
FLASH_KEY_SUBTILE = 1024
NEG_BIG = -1e30
RWKV_PASSES = dict(a=1, x=1, inv=1, u=1, y=1, s=1)
GDN_PASSES = dict(a=1, x=1, inv=1, u=1, y=1, s=1)

VMEM_LIMIT_BYTES = 56 * 1024 * 1024
MM_MAX_COLS = 1792


def _cp(*sem):
    return pltpu.CompilerParams(dimension_semantics=("arbitrary",) * len(sem), vmem_limit_bytes=VMEM_LIMIT_BYTES)


def _dot(a, b, prec=None):
    return jnp.dot(a, b, precision=prec, preferred_element_type=F32)


def _dot_nt(a, b, prec=None):
    return lax.dot_general(a, b, (((1,), (1,)), ((), ())), precision=prec, preferred_element_type=F32)


def _dot_tn(a, b, prec=None):
    return lax.dot_general(a, b, (((0,), (0,)), ((), ())), precision=prec, preferred_element_type=F32)


def _sigmoid(x):
    return 1.0 / (1.0 + jnp.exp(-x))


def _softplus(x):
    return jnp.maximum(x, 0.0) + jnp.log(1.0 + jnp.exp(-jnp.abs(x)))


def _silu(x):
    return x * _sigmoid(x)


def _mm(pairs, *, tm, tn, epilogue, outs, aux=(), name="mm"):
    m, n = pairs[0][0].shape[0], pairs[0][1].shape[1]
    if tn is None:
        tn = n if n <= MM_MAX_COLS else max(c for c in range(LANES, MM_MAX_COLS + 1, LANES) if n % c == 0)
    outs = [(tn if ow is None else ow, dt) for ow, dt in outs]
    assert m % tm == 0 and n % tn == 0, (m, n, tm, tn)
    nj, ni = n // tn, m // tm
    in_specs, args = [], []
    for a, w in pairs:
        k = a.shape[1]
        in_specs += [pl.BlockSpec((tm, k), lambda j, i: (i, 0)), pl.BlockSpec((k, tn), lambda j, i: (0, j))]
        args += [a, w]
    for arr, kind in aux:
        if kind == 'row':
            in_specs.append(pl.BlockSpec((tm, arr.shape[1]), lambda j, i: (i, 0)))
        elif kind == 'rowcol':
            in_specs.append(pl.BlockSpec((tm, tn), lambda j, i: (i, j)))
        elif kind == 'row0':
            in_specs.append(pl.BlockSpec(arr.shape, lambda j, i: (0, 0)))
        else:
            in_specs.append(pl.BlockSpec((1, tn), lambda j, i: (0, j)))
        args.append(arr)
    out_shape = [jax.ShapeDtypeStruct((m, ow * nj), dt) for ow, dt in outs]
    out_specs = [pl.BlockSpec((tm, ow), lambda j, i: (i, j)) for ow, dt in outs]
    n_pairs, n_aux = len(pairs), len(aux)

    def body(*refs):
        acc = None
        for p in range(n_pairs):
            d = _dot(refs[2 * p][...], refs[2 * p + 1][...])
            acc = d if acc is None else acc + d
        res = epilogue(acc, *[r[...] for r in refs[2 * n_pairs:2 * n_pairs + n_aux]])
        for o_ref, val in zip(refs[2 * n_pairs + n_aux:], res):
            o_ref[...] = val.astype(o_ref.dtype)

    res = pl.pallas_call(body, grid=(nj, ni), in_specs=in_specs, out_specs=out_specs, out_shape=out_shape, name=name,
                         compiler_params=_cp("parallel", "parallel"))(*args)
    return res


def _mm_t(w_t, a, *, nb, t, row0, name="mm_t"):
    n, k = w_t.shape
    tm = next((c for c in (512, 384, 256, 128) if t % c == 0), t)
    assert t % tm == 0 and row0 % tm == 0
    nt = t // tm

    def body(w_ref, a_ref, o_ref):
        o_ref[...] = _dot_nt(w_ref[...], a_ref[...]).astype(o_ref.dtype)

    return pl.pallas_call(
        body, grid=(nb, nt), name=name,
        in_specs=[pl.BlockSpec((n, k), lambda b, i: (0, 0)), pl.BlockSpec((tm, k), lambda b, i: (row0 // tm + b * nt + i, 0))],
        out_specs=pl.BlockSpec((n, tm), lambda b, i: (b, i)),
        out_shape=jax.ShapeDtypeStruct((nb * n, t), BF16),
        compiler_params=_cp("parallel", "parallel"))(w_t, a)


def _rope_lanes(x, tab):
    c, s1, s2 = tab[:, :LANES], tab[:, LANES:2 * LANES], tab[:, 2 * LANES:]
    n = x.shape[1]
    lo = pltpu.roll(x, n - MLA_ROPE // 2, 1)
    hi = pltpu.roll(x, MLA_ROPE // 2, 1)
    if n > LANES:
        reps = n // LANES
        c, s1, s2 = (jnp.concatenate([t] * reps, axis=1) for t in (c, s1, s2))
    return x * c + lo * s1 + hi * s2


def _ep_rms(acc, g):
    y = acc * lax.rsqrt(jnp.mean(acc * acc, axis=-1, keepdims=True) + RMS_EPS) * g
    return y, y


def _ep_ln(acc, x, g, b):
    z = DN_ALPHA * x.astype(F32) + acc
    mu = jnp.mean(z, axis=-1, keepdims=True)
    zc = z - mu
    var = jnp.mean(zc * zc, axis=-1, keepdims=True)
    return zc * lax.rsqrt(var + LN_EPS) * g + b


def _flash(qs, ks, vt, *, nb, tq_total, tk_total, q_row0, q_off, tq, tk, chunked, nh, bias=None, gate=None,
           out_dtype=BF16, name="flash"):
    assert tq_total % tq == 0 and tk_total % tk == 0 and q_row0 % tq == 0
    nq, nk = tq_total // tq, tk_total // tk
    qrb0 = q_row0 // tq
    ts = tk if tk % FLASH_KEY_SUBTILE else min(tk, FLASH_KEY_SUBTILE)
    for qi_s in range(nq):
        last_s = min(nk - 1, (q_off + qi_s * tq + tq - 1) // tk)
        assert last_s * tk <= q_off + qi_s * tq + 1, "key tiles before the last needed one must be fully visible"
    n_q, n_k = len(qs), len(ks)
    has_bias, has_gate = bias is not None, gate is not None

    def last_k(qi):
        return jnp.minimum(nk - 1, (q_off + qi * tq + tq - 1) // tk)

    kblk = lambda qi, ki: jnp.minimum(ki, last_k(qi))
    q_map = lambda b, h, qi, ki: (qrb0 + b * nq + qi, h)
    in_specs, args = [], []
    for q in qs:
        in_specs.append(pl.BlockSpec((tq, LANES), q_map))
        args.append(q)
    for arr, c0, per_head in ks:
        if per_head:
            in_specs.append(pl.BlockSpec((tk, LANES), lambda b, h, qi, ki, c0=c0: (b * nk + kblk(qi, ki), c0 + h)))
        else:
            in_specs.append(pl.BlockSpec((tk, LANES), lambda b, h, qi, ki, c0=c0: (b * nk + kblk(qi, ki), c0)))
        args.append(arr)
    in_specs.append(pl.BlockSpec((LANES, tk), lambda b, h, qi, ki: (b * nh + h, kblk(qi, ki))))
    args.append(vt)
    if has_bias:
        in_specs.append(pl.BlockSpec((1, 1, 1, tq), lambda b, h, qi, ki: (b, h, 0, qi)))
        in_specs.append(pl.BlockSpec((1, 1, tk, 1), lambda b, h, qi, ki: (b, h, kblk(qi, ki), 0)))
        args += list(bias)
    if has_gate:
        in_specs.append(pl.BlockSpec((tq, LANES), q_map))
        args.append(gate)

    def body(*refs):
        q_refs = refs[:n_q]
        k_refs = refs[n_q:n_q + n_k]
        vt_ref = refs[n_q + n_k]
        pos = n_q + n_k + 1
        if has_bias:
            qb_ref, kb_ref = refs[pos], refs[pos + 1]
            pos += 2
        if has_gate:
            gate_ref = refs[pos]
            pos += 1
        o_ref, m_ref, l_ref, acc_ref = refs[pos:pos + 4]
        qi, ki = pl.program_id(2), pl.program_id(3)

        @pl.when(ki == 0)
        def _():
            m_ref[...] = jnp.full(m_ref.shape, NEG_BIG, F32)
            l_ref[...] = jnp.zeros(l_ref.shape, F32)
            acc_ref[...] = jnp.zeros(acc_ref.shape, F32)

        def step(masked):
            q_all = jnp.concatenate([q_ref[...] for q_ref in q_refs], axis=1) if n_q > 1 else q_refs[0][...]
            qpos = q_off + qi * tq + lax.broadcasted_iota(jnp.int32, (1, tq), 1)
            qlim = (qpos | (CHUNK - 1)) if chunked else qpos
            subs = [slice(j * ts, (j + 1) * ts) for j in range(tk // ts)]
            ss = []
            for j, rows in enumerate(subs):
                k_all = jnp.concatenate([k_ref[rows, :] for k_ref in k_refs], axis=1) if n_k > 1 else k_refs[0][rows, :]
                s = _dot_nt(k_all, q_all)
                if has_bias:
                    s = s + (qb_ref[0, 0] - kb_ref[0, 0, rows, :])
                if masked:
                    kpos = ki * tk + j * ts + lax.broadcasted_iota(jnp.int32, (ts, 1), 0)
                    s = jnp.where(kpos <= qlim, s, NEG_BIG)
                ss.append(s)
            m_old = m_ref[...]
            m_new = m_old
            for s in ss:
                m_new = jnp.maximum(m_new, jnp.max(s, axis=0, keepdims=True))
            alpha = jnp.exp(m_old - m_new)
            l_new = alpha * l_ref[...]
            acc_new = alpha * acc_ref[...]
            for rows, s in zip(subs, ss):
                p = jnp.exp(s - m_new)
                l_new = l_new + jnp.sum(p, axis=0, keepdims=True)
                acc_new = acc_new + _dot(vt_ref[:, rows], p.astype(BF16))
            m_ref[...] = m_new
            l_ref[...] = l_new
            acc_ref[...] = acc_new

        @pl.when(ki < last_k(qi))
        def _():
            step(False)

        @pl.when(ki == last_k(qi))
        def _():
            step(True)

        @pl.when(ki == nk - 1)
        def _():
            o = (acc_ref[...] / l_ref[...]).T
            if has_gate:
                o = o * gate_ref[...].astype(F32)
            o_ref[...] = o.astype(o_ref.dtype)

    return pl.pallas_call(
        body, grid=(nb, nh, nq, nk), in_specs=in_specs, name=name,
        out_specs=pl.BlockSpec((tq, LANES), lambda b, h, qi, ki: (b * nq + qi, h)),
        out_shape=jax.ShapeDtypeStruct((nb * tq_total, nh * LANES), out_dtype),
        scratch_shapes=[pltpu.VMEM((1, tq), F32), pltpu.VMEM((1, tq), F32), pltpu.VMEM((LANES, tq), F32)],
        compiler_params=_cp("parallel", "parallel", "parallel", "arbitrary"))(*args)


def _head_sum_matrices(n_heads, hd):
    lane = jnp.arange(n_heads * hd)[:, None] // hd
    e = (lane == jnp.arange(LANES)[None, :]).astype(F32)
    return e, e.T


def _rwkv_prep(cols, shift_prev, prm, *, nb, t, row0):
    tt = min(256, t)
    assert t % tt == 0 and row0 % tt == 0
    nt = t // tt
    rb0 = row0 // tt
    w = RWKV_W

    def body(c_ref, sp_ref, mu_ref, w0_ref, a0_ref, kk_ref, ka_ref, w2_ref, a2_ref, g2_ref, e_ref, et_ref,
             r_o, k_o, v_o, kk_o, b_o, lw_o, g_o, sh_o, carry):
        ti = pl.program_id(1)

        @pl.when(ti == 0)
        def _():
            carry[...] = sp_ref[0]

        c = c_ref[...]
        row = lax.broadcasted_iota(jnp.int32, (tt, 1), 0)
        prev = jnp.where(row == 0, carry[...], pltpu.roll(c, 1, 0))
        carry[...] = c[tt - 1:tt, :]
        sh_o[0] = c[tt - 1:tt, :]
        xs = c + (prev - c) * mu_ref[...]
        r, k, v = xs[:, :w], xs[:, w:2 * w], xs[:, 2 * w:3 * w]
        o1 = 3 * w
        lora = lambda u, w_ref: _dot(u.astype(BF16), w_ref[...].astype(BF16))
        w_raw = w0_ref[...] + lora(jnp.tanh(xs[:, o1:o1 + LANES]), w2_ref)
        log_w = -jnp.exp(-_softplus(-w_raw) - 0.5)
        a = _sigmoid(a0_ref[...] + lora(xs[:, o1 + LANES:o1 + 2 * LANES], a2_ref))
        g = lora(_sigmoid(xs[:, o1 + 2 * LANES:]), g2_ref)

        def head_dot(u, sel_ref):
            hi, lo = _split_bf16(u)
            sel = sel_ref[...].astype(BF16)
            return _dot(hi.astype(BF16), sel) + _dot(lo.astype(BF16), sel)

        kk = k * kk_ref[...]
        kk = kk * head_dot(lax.rsqrt(head_dot(kk * kk, e_ref) + 1e-6), et_ref)
        r_o[...] = r.astype(r_o.dtype)
        k_o[...] = (k * (1.0 + (a - 1.0) * ka_ref[...])).astype(k_o.dtype)
        v_o[...] = v.astype(v_o.dtype)
        kk_o[...] = kk.astype(kk_o.dtype)
        b_o[...] = (kk * a).astype(b_o.dtype)
        lw_o[...] = log_w
        g_o[...] = g.astype(g_o.dtype)

    e, et = _head_sum_matrices(RWKV_HEADS, RWKV_HD)
    tok = pl.BlockSpec((tt, w), lambda b, ti: (b * nt + ti, 0))
    full = lambda arr: pl.BlockSpec(arr.shape, lambda b, ti: (0,) * arr.ndim)
    params = [prm['mu'], prm['w0'], prm['a0'], prm['k_k'], prm['k_a'], prm['w2'], prm['a2'], prm['g2'], e, et]
    outs = pl.pallas_call(
        body, grid=(nb, nt), name="rwkv_prep",
        in_specs=[pl.BlockSpec((tt, RWKV_PCOLS), lambda b, ti: (rb0 + b * nt + ti, 0)),
                  pl.BlockSpec((1, 1, RWKV_PCOLS), lambda b, ti: (b, 0, 0))] + [full(p) for p in params],
        out_specs=[tok] * 7 + [pl.BlockSpec((1, 1, RWKV_PCOLS), lambda b, ti: (b, 0, 0))],
        out_shape=[jax.ShapeDtypeStruct((nb * t, w), dt) for dt in (BF16, BF16, BF16, BF16, BF16, F32, BF16)]
        + [jax.ShapeDtypeStruct((nb, 1, RWKV_PCOLS), F32)],
        scratch_shapes=[pltpu.VMEM((1, RWKV_PCOLS), F32)],
        compiler_params=_cp("parallel", "arbitrary"))(cols, shift_prev, *params)
    return outs


def _cumsum_rows(x, n):
    row = lax.broadcasted_iota(jnp.int32, (n, 1), 0)
    sh = 1
    while sh < n:
        x = x + jnp.where(row >= sh, pltpu.roll(x, sh, 0), 0.0)
        sh *= 2
    return x


def _split_bf16(x):
    hi = x.astype(BF16).astype(F32)
    return hi, x - hi


def _bmm(a, b, ca, cb, passes):
    dn = (((ca,), (cb,)), ((0,), (0,)))
    if passes == 6:
        return lax.dot_general(a, b, dn, precision=HP, preferred_element_type=F32)
    if passes == 3:
        ah, al = _split_bf16(a)
        bh, bl = _split_bf16(b)
        a = jnp.concatenate([ah, ah, al], axis=ca)
        b = jnp.concatenate([bh, bl, bh], axis=cb)
    return lax.dot_general(a.astype(BF16), b.astype(BF16), dn, preferred_element_type=F32)


def _heads(x, nh, hd):
    return jnp.stack([x[:, h * hd:(h + 1) * hd] for h in range(nh)], axis=0)


def _unheads(x):
    return jnp.concatenate([x[h] for h in range(x.shape[0])], axis=1)


def _unit_lower_inverse(nmat, n, passes):
    eye = (lax.broadcasted_iota(jnp.int32, (n, n), 0) == lax.broadcasted_iota(jnp.int32, (n, n), 1)).astype(F32)
    t = eye + nmat
    p = nmat
    steps = int(math.ceil(math.log2(n))) - 1
    for _ in range(steps):
        p = _bmm(p, p, 2, 1, passes)
        t = t + _bmm(p, t, 2, 1, passes)
    return t


def _rwkv_scan(r, k, v, kk, bvec, logw, g, s0, prm, *, nb, t):
    c = min(SCAN_CHUNK, t)
    assert t % c == 0
    nc = t // c
    hd, nh = RWKV_HD, RWKV_HEADS

    def body(r_ref, k_ref, v_ref, kk_ref, b_ref, lw_ref, g_ref, s0_ref, rk_ref, lg_ref, lb_ref, o_ref, sf_ref, st):
        ci = pl.program_id(1)

        @pl.when(ci == 0)
        def _():
            st[...] = s0_ref[0]

        lw = lw_ref[...]
        lc = _cumsum_rows(lw, c)
        rr, kx, vv, bb = (ref[...].astype(F32) for ref in (r_ref, k_ref, v_ref, b_ref))
        g_inv = jnp.exp(-lc)
        g_end = jnp.exp(lc[c - 1:c, :] - lc)
        hs = lambda a: _heads(a, nh, hd)
        at, rt = hs(-kk_ref[...].astype(F32) * jnp.exp(lc - lw)), hs(rr * jnp.exp(lc))
        bt, kt = hs(bb * g_inv), hs(kx * g_inv)
        bg, kg = hs(bb * g_end), hs(kx * g_end)
        w_end = hs(jnp.exp(lc[c - 1:c, :]))
        v_h = hs(vv)
        ri = lax.broadcasted_iota(jnp.int32, (c, c), 0)
        cj = lax.broadcasted_iota(jnp.int32, (c, c), 1)
        strict = ri > cj
        incl2 = lax.broadcasted_iota(jnp.int32, (c, 2 * c), 0) >= (lax.broadcasted_iota(jnp.int32, (c, 2 * c), 1) & (c - 1))
        s_all = st[...]
        pp = RWKV_PASSES
        lhs = jnp.concatenate([at, rt], axis=1)
        a_all = _bmm(lhs, jnp.concatenate([bt, kt], axis=1), 2, 2, pp['a'])
        x_all = _bmm(lhs, s_all, 2, 2, pp['x'])
        tinv = _unit_lower_inverse(jnp.where(strict, a_all[:, :c, :c], 0.0), c, pp['inv'])
        a_ak = jnp.where(strict, a_all[:, :c, c:], 0.0)
        u = _bmm(tinv, x_all[:, :c] + _bmm(a_ak, v_h, 2, 1, pp['u']), 2, 1, pp['u'])
        uv = jnp.concatenate([u, v_h], axis=1)
        y = x_all[:, c:] + _bmm(jnp.where(incl2, a_all[:, c:, :], 0.0), uv, 2, 1, pp['y'])
        st[...] = s_all * w_end + _bmm(uv, jnp.concatenate([bg, kg], axis=1), 1, 1, pp['s'])
        mu = jnp.mean(y, axis=-1, keepdims=True)
        yc = y - mu
        var = jnp.mean(yc * yc, axis=-1, keepdims=True)
        yn = yc * lax.rsqrt(var + RWKV_GN_EPS) * hs(lg_ref[...]) + hs(lb_ref[...])
        bonus = jnp.sum(hs(rr * kx * rk_ref[...]), axis=-1, keepdims=True) * v_h
        o_ref[...] = (_unheads(yn + bonus) * g_ref[...].astype(F32)).astype(o_ref.dtype)

        @pl.when(ci == nc - 1)
        def _():
            sf_ref[0] = st[...]

    tok = pl.BlockSpec((c, RWKV_W), lambda b, ci: (b * nc + ci, 0))
    stt = pl.BlockSpec((1, nh, hd, hd), lambda b, ci: (b, 0, 0, 0))
    par = pl.BlockSpec((1, RWKV_W), lambda b, ci: (0, 0))
    return pl.pallas_call(
        body, grid=(nb, nc), name="rwkv_scan",
        in_specs=[tok] * 7 + [stt, par, par, par],
        out_specs=[tok, stt],
        out_shape=[jax.ShapeDtypeStruct((nb * t, RWKV_W), BF16), jax.ShapeDtypeStruct((nb, nh, hd, hd), F32)],
        scratch_shapes=[pltpu.VMEM((nh, hd, hd), F32)],
        compiler_params=_cp("parallel", "arbitrary"))(r, k, v, kk, bvec, logw, g, s0, prm['r_k'], prm['ln_g'], prm['ln_b'])


class _Geom:
    def __init__(self, bp, tp, bs, ts, past):
        self.bp, self.tp, self.bs, self.ts, self.past = bp, tp, bs, ts, past
        self.mp, self.ms = bp * tp, bs * ts
        self.m = self.mp + self.ms
        self.tm = math.gcd(self.m, 512)
        assert self.tm % 16 == 0 and tp % CHUNK == 0 and past % CHUNK == 0 and ts == CHUNK
        self.tq_p = min(1024, tp)
        self.kpad = -(-(past + ts) // (3 * LANES)) * (3 * LANES)
        self.tk_s = self.kpad


def _pad_cols(w, n):
    return jnp.pad(w, ((0, 0), (0, n - w.shape[1])))


def _rope_table(geo):
    half = MLA_ROPE // 2
    pos = jnp.concatenate([jnp.tile(jnp.arange(geo.tp, dtype=jnp.int32), geo.bp),
                           geo.past + jnp.tile(jnp.arange(geo.ts, dtype=jnp.int32), geo.bs)])
    inv_freq = ROPE_THETA ** (-jnp.arange(half, dtype=F32) / half)
    ang = pos.astype(F32)[:, None] * inv_freq[None, :]
    cos, sin = jnp.cos(ang), jnp.sin(ang)
    z = jnp.zeros_like(cos)
    zz = jnp.zeros((pos.shape[0], LANES - MLA_ROPE), F32)
    return jnp.concatenate([cos, cos, zz, -sin, z, zz, z, sin, zz], axis=1)


def _rwkv_cols_layout(a):
    o1 = 3 * RWKV_W
    o2 = o1 + RWKV_DECAY_LORA
    o3 = o2 + RWKV_A_LORA
    pad = [(0, 0)] * (a.ndim - 1)
    return jnp.concatenate([a[..., :o1],
                            jnp.pad(a[..., o1:o2], pad + [(0, LANES - RWKV_DECAY_LORA)]),
                            jnp.pad(a[..., o2:o3], pad + [(0, LANES - RWKV_A_LORA)]),
                            a[..., o3:]], axis=-1)


def _rwkv_cols_unlayout(a):
    o1 = 3 * RWKV_W
    return jnp.concatenate([a[..., :o1], a[..., o1:o1 + RWKV_DECAY_LORA],
                            a[..., o1 + LANES:o1 + LANES + RWKV_A_LORA], a[..., o1 + 2 * LANES:]], axis=-1)


def _even_mixer(geo, xb, rope_tab, w_in, kv_norm, w_uk, w_uv, rw, cache_ckv, cache_kr, shift_prev, wkv0):
    tm, m, mp = geo.tm, geo.m, geo.mp
    d = w_in.shape[0]
    nq = MLA_HEADS * MLA_QK
    wq = w_in[:, :nq].reshape(d, MLA_HEADS, MLA_QK)
    w_qn = wq[:, :, :MLA_NOPE].reshape(d, MLA_HEADS * MLA_NOPE).astype(BF16)
    w_qr = jnp.pad(wq[:, :, MLA_NOPE:], ((0, 0), (0, 0), (0, LANES - MLA_ROPE))).reshape(d, MLA_HEADS * LANES).astype(BF16)
    w_ckv = w_in[:, nq:nq + MLA_KV_LORA].astype(BF16)
    w_kr = _pad_cols(w_in[:, nq + MLA_KV_LORA:nq + MLA_KV_LORA + MLA_ROPE], LANES).astype(BF16)
    w_rw = _rwkv_cols_layout(w_in[:, nq + MLA_KV_LORA + MLA_ROPE:]).astype(BF16)
    scale = MLA_QK ** -0.5

    (qn,) = _mm([(xb, w_qn)], tm=tm, tn=None, epilogue=lambda acc: (acc * scale,), outs=[(None, BF16)])
    (qr,) = _mm([(xb, w_qr)], tm=tm, tn=MLA_HEADS * LANES, aux=[(rope_tab, 'row')],
                epilogue=lambda acc, tab: (_rope_lanes(acc, tab) * scale,), outs=[(MLA_HEADS * LANES, BF16)])
    ckv, ckv_b = _mm([(xb, w_ckv)], tm=tm, tn=MLA_KV_LORA, aux=[(kv_norm.reshape(1, -1), 'col')],
                     epilogue=_ep_rms, outs=[(MLA_KV_LORA, F32), (MLA_KV_LORA, BF16)])
    kr, kr_b = _mm([(xb, w_kr)], tm=tm, tn=LANES, aux=[(rope_tab, 'row')],
                   epilogue=lambda acc, tab: (_rope_lanes(acc, tab),) * 2, outs=[(LANES, F32), (LANES, BF16)])
    (rwc,) = _mm([(xb, w_rw)], tm=tm, tn=None, epilogue=lambda acc: (acc,), outs=[(None, F32)])

    w_k = w_uk.reshape(MLA_KV_LORA, -1).astype(BF16)
    w_vt = w_uv.reshape(MLA_KV_LORA, -1).T.astype(BF16)
    (kn_new,) = _mm([(ckv_b, w_k)], tm=tm, tn=None, epilogue=lambda acc: (acc,), outs=[(None, BF16)])
    vt_p = _mm_t(w_vt, ckv_b, nb=geo.bp, t=geo.tp, row0=0)
    o_p = _flash([qn, qr], [(kn_new, 0, True), (kr_b, 0, False)], vt_p,
                 nb=geo.bp, tq_total=geo.tp, tk_total=geo.tp, q_row0=0, q_off=0, tq=geo.tq_p, tk=geo.tq_p,
                 chunked=True, nh=MLA_HEADS)
    padk = geo.kpad - geo.past - geo.ts
    ckv_all = jnp.concatenate([cache_ckv.astype(BF16), ckv_b[mp:].reshape(geo.bs, geo.ts, -1),
                               jnp.zeros((geo.bs, padk, MLA_KV_LORA), BF16)], axis=1).reshape(geo.bs * geo.kpad, -1)
    kr_all = jnp.concatenate([jnp.pad(cache_kr, ((0, 0), (0, 0), (0, LANES - MLA_ROPE))).astype(BF16),
                              kr_b[mp:].reshape(geo.bs, geo.ts, -1),
                              jnp.zeros((geo.bs, padk, LANES), BF16)], axis=1).reshape(geo.bs * geo.kpad, -1)
    (kn_all,) = _mm([(ckv_all, w_k)], tm=3 * LANES, tn=None, epilogue=lambda acc: (acc,), outs=[(None, BF16)])
    vt_all = _mm_t(w_vt, ckv_all, nb=geo.bs, t=geo.kpad, row0=0)
    o_s = _flash([qn, qr], [(kn_all, 0, True), (kr_all, 0, False)], vt_all,
                 nb=geo.bs, tq_total=geo.ts, tk_total=geo.kpad, q_row0=mp, q_off=geo.past, tq=geo.ts, tk=geo.tk_s,
                 chunked=True, nh=MLA_HEADS)
    o_mla = jnp.concatenate([o_p, o_s], axis=0)

    outs_p = _rwkv_prep(rwc, jnp.zeros((geo.bp, 1, RWKV_PCOLS), F32), rw, nb=geo.bp, t=geo.tp, row0=0)
    outs_s = _rwkv_prep(rwc, _rwkv_cols_layout(shift_prev)[:, None, :], rw, nb=geo.bs, t=geo.ts, row0=mp)
    y_p, wkv_p = _rwkv_scan(*outs_p[:7], jnp.zeros((geo.bp, RWKV_HEADS, RWKV_HD, RWKV_HD), F32), rw, nb=geo.bp, t=geo.tp)
    y_s, wkv_s = _rwkv_scan(*outs_s[:7], wkv0, rw, nb=geo.bs, t=geo.ts)
    o_rwkv = jnp.concatenate([y_p, y_s], axis=0)
    states = dict(
        ckv_p=ckv[:mp].reshape(geo.bp, geo.tp, -1), ckv_s=ckv[mp:].reshape(geo.bs, geo.ts, -1),
        kr_p=kr[:mp, :MLA_ROPE].reshape(geo.bp, geo.tp, -1), kr_s=kr[mp:, :MLA_ROPE].reshape(geo.bs, geo.ts, -1),
        sh_p=_rwkv_cols_unlayout(outs_p[7][:, 0]), sh_s=_rwkv_cols_unlayout(outs_s[7][:, 0]),
        rwkv_p=wkv_p, rwkv_s=wkv_s)
    return o_mla, o_rwkv, states


def _head_rms(x, g, nh, scale=1.0):
    outs = []
    for h in range(nh):
        seg = x[:, h * LANES:(h + 1) * LANES]
        outs.append(seg * lax.rsqrt(jnp.mean(seg * seg, axis=-1, keepdims=True) + RMS_EPS) * (g * scale))
    return jnp.concatenate(outs, axis=1)


def _head_l2(x, nh, scale=1.0):
    outs = []
    for h in range(nh):
        seg = x[:, h * LANES:(h + 1) * LANES]
        outs.append(seg * (lax.rsqrt(jnp.sum(seg * seg, axis=-1, keepdims=True) + 1e-6) * scale))
    return jnp.concatenate(outs, axis=1)


def _cumsum_time(x):
    nb, t, w = x.shape

    def body(x_ref, o_ref):
        o_ref[0] = _cumsum_rows(x_ref[0], t)

    spec = pl.BlockSpec((1, t, w), lambda b: (b, 0, 0))
    return pl.pallas_call(body, grid=(nb,), in_specs=[spec], out_specs=spec, name="cumsum_time",
                          out_shape=jax.ShapeDtypeStruct(x.shape, F32), compiler_params=_cp("parallel"))(x)


def _gdn_prep(cols, conv_prev, conv_w, *, nb, t, row0):
    tt = min(256, t)
    assert t % tt == 0 and row0 % tt == 0
    nt = t // tt
    rb0 = row0 // tt
    w = GDN_W

    def body(c_ref, cp_ref, cw_ref, q_o, k_o, v_o, carry):
        ti = pl.program_id(1)

        @pl.when(ti == 0)
        def _():
            carry[...] = cp_ref[0]

        c = c_ref[...]
        ext = jnp.concatenate([carry[...], c], axis=0)
        carry[...] = c[tt - 8:tt, :]
        cw = cw_ref[...]
        acc = c * cw[GDN_CONV - 1:GDN_CONV, :]
        for j in range(GDN_CONV - 1):
            back = GDN_CONV - 1 - j
            acc = acc + ext[8 - back:8 - back + tt, :] * cw[j:j + 1, :]
        qkv = _silu(acc)
        q_o[...] = _head_l2(qkv[:, :w], GDN_HEADS, GDN_HD ** -0.5).astype(q_o.dtype)
        k_o[...] = _head_l2(qkv[:, w:2 * w], GDN_HEADS).astype(k_o.dtype)
        v_o[...] = qkv[:, 2 * w:].astype(v_o.dtype)

    tok = pl.BlockSpec((tt, w), lambda b, ti: (b * nt + ti, 0))
    return pl.pallas_call(
        body, grid=(nb, nt), name="gdn_prep",
        in_specs=[pl.BlockSpec((tt, GDN_QKV), lambda b, ti: (rb0 + b * nt + ti, 0)),
                  pl.BlockSpec((1, 8, GDN_QKV), lambda b, ti: (b, 0, 0)),
                  pl.BlockSpec((8, GDN_QKV), lambda b, ti: (0, 0))],
        out_specs=[tok] * 3,
        out_shape=[jax.ShapeDtypeStruct((nb * t, w), BF16)] * 3,
        scratch_shapes=[pltpu.VMEM((8, GDN_QKV), F32)],
        compiler_params=_cp("parallel", "arbitrary"))(cols, conv_prev, conv_w)


def _gdn_scan(q, k, v, gb, z, s0, norm_g, *, nb, t, gb_row0, z_row0):
    c = min(SCAN_CHUNK, t)
    assert t % c == 0 and gb_row0 % c == 0 and z_row0 % c == 0
    nc = t // c
    nh, hd = GDN_HEADS, GDN_HD

    def body(q_ref, k_ref, v_ref, gb_ref, z_ref, s0_ref, ng_ref, o_ref, sf_ref, st):
        ci = pl.program_id(1)

        @pl.when(ci == 0)
        def _():
            st[...] = s0_ref[0]

        gbv = gb_ref[...]
        gc = _cumsum_rows(gbv, c)
        gct = gc.T
        ri = lax.broadcasted_iota(jnp.int32, (c, c), 0)
        cj = lax.broadcasted_iota(jnp.int32, (c, c), 1)
        tril, strict = ri >= cj, ri > cj
        hs = lambda a: _heads(a, nh, hd)
        k_h, q_h, v_h = (hs(ref[...].astype(F32)) for ref in (k_ref, q_ref, v_ref))
        gcol = jnp.stack([gc[:, h:h + 1] for h in range(nh)], axis=0)
        grow = gct[:nh][:, None, :]
        bcol = jnp.stack([gbv[:, nh + h:nh + h + 1] for h in range(nh)], axis=0)
        decay = jnp.where(tril, jnp.exp(jnp.where(tril, gcol - grow, 0.0)), 0.0)
        pp = GDN_PASSES
        kb = k_h * bcol
        prods = _bmm(jnp.concatenate([kb, q_h], axis=1), k_h, 2, 2, pp['a'])
        low = jnp.where(strict, prods[:, :c] * decay, 0.0)
        a_qk = jnp.where(tril, prods[:, c:] * decay, 0.0)
        tinv = _unit_lower_inverse(-low, c, pp['inv'])
        e_g = jnp.exp(gcol)
        uw = _bmm(tinv, jnp.concatenate([v_h * bcol, kb * e_g], axis=2), 2, 1, pp['u'])
        s_all = st[...]
        ws_qs = _bmm(jnp.concatenate([uw[:, :, hd:], q_h * e_g], axis=1), s_all, 2, 1, pp['x'])
        v_new = uw[:, :, :hd] - ws_qs[:, :c]
        o = ws_qs[:, c:] + _bmm(a_qk, v_new, 2, 1, pp['y'])
        g_last = gcol[:, c - 1:c, :]
        st[...] = s_all * jnp.exp(g_last) + _bmm(k_h * jnp.exp(g_last - gcol), v_new, 1, 1, pp['s'])
        o = o * lax.rsqrt(jnp.mean(o * o, axis=-1, keepdims=True) + RMS_EPS) * ng_ref[...]
        o_ref[...] = (_unheads(o) * _silu(z_ref[...].astype(F32))).astype(o_ref.dtype)

        @pl.when(ci == nc - 1)
        def _():
            sf_ref[0] = st[...]

    tok = pl.BlockSpec((c, GDN_W), lambda b, ci: (b * nc + ci, 0))
    stt = pl.BlockSpec((1, nh, hd, hd), lambda b, ci: (b, 0, 0, 0))
    return pl.pallas_call(
        body, grid=(nb, nc), name="gdn_scan",
        in_specs=[tok] * 3 + [pl.BlockSpec((c, LANES), lambda b, ci: (gb_row0 // c + b * nc + ci, 0)),
                              pl.BlockSpec((c, GDN_W), lambda b, ci: (z_row0 // c + b * nc + ci, 0)),
                              stt, pl.BlockSpec((1, hd), lambda b, ci: (0, 0))],
        out_specs=[tok, stt],
        out_shape=[jax.ShapeDtypeStruct((nb * t, GDN_W), BF16), jax.ShapeDtypeStruct((nb, nh, hd, hd), F32)],
        scratch_shapes=[pltpu.VMEM((nh, hd, hd), F32)],
        compiler_params=_cp("parallel", "arbitrary"))(q, k, v, gb, z, s0, norm_g)


def _odd_mixer(geo, xb, w_in, fx, gd, cache_k, cache_v, cache_logf, conv_prev, gdn0):
    tm, m, mp = geo.tm, geo.m, geo.mp
    fw = FOX_W
    o_f = 4 * fw
    o_g = o_f + FOX_HEADS
    w_q, w_k, w_v, w_gate = (w_in[:, i * fw:(i + 1) * fw].astype(BF16) for i in range(4))
    w_f = _pad_cols(w_in[:, o_f:o_g], LANES).astype(BF16)
    w_qkv = w_in[:, o_g:o_g + GDN_QKV].astype(BF16)
    w_ab = _pad_cols(w_in[:, o_g + GDN_QKV:o_g + GDN_QKV + 2 * GDN_HEADS], LANES).astype(BF16)
    w_z = w_in[:, o_g + GDN_QKV + 2 * GDN_HEADS:].astype(BF16)
    qn, kn = fx['q_norm'].reshape(1, -1), fx['k_norm'].reshape(1, -1)
    f_bias = jnp.pad(fx['f_bias'], (0, LANES - FOX_HEADS)).reshape(1, -1)
    scale = FOX_HD ** -0.5

    (q,) = _mm([(xb, w_q)], tm=tm, tn=fw, aux=[(qn, 'row0')],
               epilogue=lambda acc, g: (_head_rms(acc, g, FOX_HEADS, scale),), outs=[(fw, BF16)])
    k, k_b = _mm([(xb, w_k)], tm=tm, tn=fw, aux=[(kn, 'row0')],
                 epilogue=lambda acc, g: (_head_rms(acc, g, FOX_HEADS),) * 2, outs=[(fw, F32), (fw, BF16)])
    (v,) = _mm([(xb, w_v)], tm=tm, tn=None, epilogue=lambda acc: (acc,), outs=[(None, F32)])
    w_vt = w_v.T
    vt_p = _mm_t(w_vt, xb, nb=geo.bp, t=geo.tp, row0=0)
    vt_s = _mm_t(w_vt, xb, nb=geo.bs, t=geo.ts, row0=mp)
    (gate,) = _mm([(xb, w_gate)], tm=tm, tn=None, epilogue=lambda acc: (_sigmoid(acc),), outs=[(None, BF16)])
    (logf,) = _mm([(xb, w_f)], tm=tm, tn=LANES, aux=[(f_bias, 'col')],
                  epilogue=lambda acc, fb: (-_softplus(-(acc + fb)),), outs=[(LANES, F32)])

    cum_p = _cumsum_time(logf[:mp].reshape(geo.bp, geo.tp, LANES))[:, :, :FOX_HEADS]
    bias_p = (jnp.transpose(cum_p, (0, 2, 1))[:, :, None, :], jnp.transpose(cum_p, (0, 2, 1))[..., None])
    o_p = _flash([q], [(k_b, 0, True)], vt_p, nb=geo.bp, tq_total=geo.tp, tk_total=geo.tp, q_row0=0, q_off=0,
                 tq=geo.tq_p, tk=geo.tq_p, chunked=False, nh=FOX_HEADS, bias=bias_p, gate=gate)
    padk = geo.kpad - geo.past - geo.ts
    lf_all = jnp.concatenate([jnp.pad(cache_logf, ((0, 0), (0, 0), (0, LANES - FOX_HEADS))),
                              logf[mp:].reshape(geo.bs, geo.ts, LANES), jnp.zeros((geo.bs, padk, LANES), F32)], axis=1)
    cum_s = jnp.transpose(_cumsum_time(lf_all)[:, :, :FOX_HEADS], (0, 2, 1))
    bias_s = (cum_s[:, :, None, geo.past:geo.past + geo.ts], cum_s[..., None])
    k_all = jnp.concatenate([cache_k.reshape(geo.bs, geo.past, fw).astype(BF16), k_b[mp:].reshape(geo.bs, geo.ts, fw),
                             jnp.zeros((geo.bs, padk, fw), BF16)], axis=1).reshape(geo.bs * geo.kpad, fw)
    vt_all = jnp.concatenate([jnp.transpose(cache_v.reshape(geo.bs, geo.past, fw), (0, 2, 1)).astype(BF16),
                              vt_s.reshape(geo.bs, fw, geo.ts), jnp.zeros((geo.bs, fw, padk), BF16)],
                             axis=2).reshape(geo.bs * fw, geo.kpad)
    o_s = _flash([q], [(k_all, 0, True)], vt_all, nb=geo.bs, tq_total=geo.ts, tk_total=geo.kpad, q_row0=mp,
                 q_off=geo.past, tq=geo.ts, tk=geo.tk_s, chunked=False, nh=FOX_HEADS, bias=bias_s, gate=gate)
    o_fox = jnp.concatenate([o_p, o_s], axis=0)

    (qkv_raw,) = _mm([(xb, w_qkv)], tm=tm, tn=None, epilogue=lambda acc: (acc,), outs=[(None, F32)])
    lane = jnp.arange(LANES)
    neg_a = jnp.where(lane < GDN_HEADS, -jnp.exp(jnp.pad(gd['a_log'], (0, LANES - GDN_HEADS))), 0.0).reshape(1, -1)
    dtb = jnp.pad(gd['dt_bias'], (0, LANES - GDN_HEADS)).reshape(1, -1)
    is_g = (lane < GDN_HEADS).astype(F32).reshape(1, -1)
    (gb,) = _mm([(xb, w_ab)], tm=tm, tn=LANES, aux=[(neg_a, 'col'), (dtb, 'col'), (is_g, 'col')],
                epilogue=lambda acc, na, db, ig: (jnp.where(ig > 0.5, na * _softplus(acc + db), _sigmoid(acc)),),
                outs=[(LANES, F32)])
    (z,) = _mm([(xb, w_z)], tm=tm, tn=None, epilogue=lambda acc: (acc,), outs=[(None, BF16)])
    conv_w = jnp.pad(gd['conv_w'], ((0, 8 - GDN_CONV), (0, 0)))
    norm_g = gd['norm'].reshape(1, -1)
    prev8 = lambda a: jnp.pad(a, ((0, 0), (8 - (GDN_CONV - 1), 0), (0, 0)))
    q_p, k_p, v_p = _gdn_prep(qkv_raw, jnp.zeros((geo.bp, 8, GDN_QKV), F32), conv_w, nb=geo.bp, t=geo.tp, row0=0)
    q_s, k_s, v_s = _gdn_prep(qkv_raw, prev8(conv_prev), conv_w, nb=geo.bs, t=geo.ts, row0=mp)
    y_p, s_p = _gdn_scan(q_p, k_p, v_p, gb, z, jnp.zeros((geo.bp, GDN_HEADS, GDN_HD, GDN_HD), F32), norm_g,
                         nb=geo.bp, t=geo.tp, gb_row0=0, z_row0=0)
    y_s, s_s = _gdn_scan(q_s, k_s, v_s, gb, z, gdn0, norm_g, nb=geo.bs, t=geo.ts, gb_row0=mp, z_row0=mp)
    o_gdn = jnp.concatenate([y_p, y_s], axis=0)
    nconv = GDN_CONV - 1
    last_rows = lambda row0, nb, t: (row0 + jnp.arange(nb)[:, None] * t + (t - nconv + jnp.arange(nconv))[None, :]).reshape(-1)
    states = dict(
        fk_p=k[:mp].reshape(geo.bp, geo.tp, FOX_HEADS, FOX_HD), fk_s=k[mp:].reshape(geo.bs, geo.ts, FOX_HEADS, FOX_HD),
        fv_p=v[:mp].reshape(geo.bp, geo.tp, FOX_HEADS, FOX_HD), fv_s=v[mp:].reshape(geo.bs, geo.ts, FOX_HEADS, FOX_HD),
        fl_p=logf[:mp, :FOX_HEADS].reshape(geo.bp, geo.tp, FOX_HEADS),
        fl_s=logf[mp:, :FOX_HEADS].reshape(geo.bs, geo.ts, FOX_HEADS),
        cv_p=jnp.take(qkv_raw, last_rows(0, geo.bp, geo.tp), axis=0).reshape(geo.bp, nconv, GDN_QKV),
        cv_s=jnp.take(qkv_raw, last_rows(mp, geo.bs, geo.ts), axis=0).reshape(geo.bs, nconv, GDN_QKV),
        gdn_p=s_p, gdn_s=s_s)
    return o_fox, o_gdn, states


MOE_BLOCK = 512
MOE_COMBINE_ROWS = 256


def _moe_route(x, router, router_bias, *, tm):
    m = x.shape[0]
    assert m % tm == 0
    per_group = N_EXPERTS // N_GROUPS

    def lane_max(v):
        return jnp.max(v, axis=-1, keepdims=True)

    def first_lane(mask, lane):
        return jnp.min(jnp.where(mask, lane.astype(F32), float(LANES)), axis=-1, keepdims=True).astype(jnp.int32)

    def group_all(v, lane, op):
        sh = 1
        while sh < per_group:
            partner = jnp.where((lane & sh) == 0, pltpu.roll(v, LANES - sh, 1), pltpu.roll(v, sh, 1))
            v = op(v, partner)
            sh *= 2
        return v

    def body(x_ref, r_ref, b_ref, e_o, g_o, k_o, cnt_o, carry):
        i = pl.program_id(0)

        @pl.when(i == 0)
        def _():
            carry[...] = jnp.zeros(carry.shape, F32)

        lane = lax.broadcasted_iota(jnp.int32, (tm, LANES), 1)
        valid = lane < N_EXPERTS
        neg = -jnp.inf
        scores = _sigmoid(_dot(x_ref[...], r_ref[...]))
        biased = jnp.where(valid, scores + b_ref[...], neg)
        m1 = group_all(biased, lane, jnp.maximum)
        first = group_all(jnp.where(biased == m1, lane, LANES), lane, jnp.minimum)
        m2 = group_all(jnp.where(lane == first, neg, biased), lane, jnp.maximum)
        grp = jnp.where(valid & ((lane & (per_group - 1)) == 0), m1 + m2, neg)
        emask = jnp.zeros((tm, LANES), jnp.bool_)
        for _ in range(TOPK_GROUPS):
            idx = first_lane(grp == lane_max(grp), lane)
            emask = emask | ((lane - idx >= 0) & (lane - idx < per_group))
            grp = jnp.where(lane == idx, neg, grp)
        cur = jnp.where(emask & valid, biased, neg)
        picks = []
        sel = jnp.zeros((tm, LANES), jnp.bool_)
        for _ in range(TOP_K):
            idx = first_lane(cur == lane_max(cur), lane)
            pick = lane == idx
            picks.append((idx, pick))
            sel = sel | pick
            cur = jnp.where(pick, neg, cur)
        selw = jnp.where(sel, scores, 0.0)
        gates_dense = selw / jnp.sum(selw, axis=-1, keepdims=True) * ROUTED_SCALE
        ri = lax.broadcasted_iota(jnp.int32, (tm, tm), 0)
        cj = lax.broadcasted_iota(jnp.int32, (tm, tm), 1)
        p01 = jnp.where(sel, 1.0, 0.0)
        before = _dot(jnp.where(ri > cj, 1.0, 0.0).astype(BF16), p01.astype(BF16)) + carry[...]
        carry[...] = carry[...] + jnp.sum(p01, axis=0, keepdims=True)
        cnt_o[...] = carry[...]
        e_out = jnp.zeros((tm, LANES), jnp.int32)
        g_out = jnp.zeros((tm, LANES), F32)
        k_out = jnp.zeros((tm, LANES), jnp.int32)
        for j, (idx, pick) in enumerate(picks):
            e_out = jnp.where(lane == j, idx, e_out)
            g_out = jnp.where(lane == j, jnp.sum(jnp.where(pick, gates_dense, 0.0), axis=-1, keepdims=True), g_out)
            rank = jnp.sum(jnp.where(pick, before, 0.0), axis=-1, keepdims=True)
            k_out = jnp.where(lane == j, rank.astype(jnp.int32), k_out)
        e_o[...] = e_out
        g_o[...] = g_out
        k_o[...] = k_out

    tok = pl.BlockSpec((tm, LANES), lambda i: (i, 0))
    one = pl.BlockSpec((1, LANES), lambda i: (0, 0))
    return pl.pallas_call(
        body, grid=(m // tm,), name="moe_route",
        in_specs=[pl.BlockSpec((tm, x.shape[1]), lambda i: (i, 0)), pl.BlockSpec(router.shape, lambda i: (0, 0)), one],
        out_specs=[tok, tok, tok, one],
        out_shape=[jax.ShapeDtypeStruct((m, LANES), jnp.int32), jax.ShapeDtypeStruct((m, LANES), F32),
                   jax.ShapeDtypeStruct((m, LANES), jnp.int32), jax.ShapeDtypeStruct((1, LANES), F32)],
        scratch_shapes=[pltpu.VMEM((1, LANES), F32)],
        compiler_params=_cp("arbitrary"))(x, router, router_bias)


def _moe_dispatch(x_words, dest, cap, *, tb):
    m, wd = x_words.shape
    nt = m // tb

    def body(dest_ref, x_ref, out_ref, sem):
        def row_copy(i, j):
            return pltpu.make_async_copy(x_ref.at[pl.ds(i, 1)], out_ref.at[pl.ds(dest_ref[0, 0, i * TOP_K + j], 1)], sem)

        def issue(i, carry):
            for j in range(TOP_K):
                row_copy(i, j).start()
            return carry

        lax.fori_loop(0, tb, issue, 0, unroll=8)
        for _ in range(TOP_K):
            pltpu.make_async_copy(x_ref, out_ref.at[pl.ds(0, tb)], sem).wait()

    return pl.pallas_call(
        body, grid=(nt,), name="moe_dispatch",
        in_specs=[pl.BlockSpec((1, 1, tb * TOP_K), lambda i: (i, 0, 0), memory_space=pltpu.SMEM),
                  pl.BlockSpec((tb, wd), lambda i: (i, 0))],
        out_specs=pl.BlockSpec(memory_space=pl.ANY),
        out_shape=jax.ShapeDtypeStruct((cap, wd), x_words.dtype),
        scratch_shapes=[pltpu.SemaphoreType.DMA(())],
        compiler_params=_cp("arbitrary"))(dest, x_words)


def _pack_halves(y):
    n = y.shape[1] // 2
    bits = lambda a: lax.bitcast_convert_type(a.astype(BF16).astype(F32), jnp.uint32)
    return (bits(y[:, :n]) >> 16) | (bits(y[:, n:]) & jnp.uint32(0xFFFF0000))


def _unpack_halves(w):
    lo = lax.bitcast_convert_type(w << 16, F32).astype(BF16)
    hi = lax.bitcast_convert_type(w & jnp.uint32(0xFFFF0000), F32).astype(BF16)
    return lo, hi


def _ep_ln_packed(acc, x, g, b):
    y = _ep_ln(acc, x, g, b)
    return y, _pack_halves(y)


def _moe_experts(xs_words, blk_exp, blk_next, blk_rows, n_used, w_gate, w_up, w_down, layer):
    cap, half = xs_words.shape
    d = 2 * half
    nblk = cap // MOE_BLOCK
    ed = w_gate.shape[3]

    def body(be_ref, bn_ref, br_ref, nu_ref, x_ref, wg_hbm, wu_hbm, wd_hbm, o_ref, wg_f, wu_f, wd_f, wg_b, wu_b, wd_b,
             slot_ref, sem):
        i = pl.program_id(0)

        def fetch(e, slot):
            return [pltpu.make_async_copy(src.at[layer, e], dst.at[slot], sem.at[slot])
                    for src, dst in ((wg_hbm, wg_f), (wu_hbm, wu_f), (wd_hbm, wd_f))]

        @pl.when(i < nu_ref[0])
        def _():
            first = i == 0
            e = be_ref[i]

            @pl.when(first)
            def _():
                slot_ref[0] = 0
                for cp in fetch(e, 0):
                    cp.start()

            @pl.when(first | (e != be_ref[jnp.maximum(i - 1, 0)]))
            def _():
                slot = jnp.where(first, 0, 1 - slot_ref[0])
                slot_ref[0] = slot
                for cp in fetch(e, slot):
                    cp.wait()
                wg_b[...] = wg_f[slot].astype(BF16)
                wu_b[...] = wu_f[slot].astype(BF16)
                wd_b[...] = wd_f[slot].astype(BF16)

                @pl.when(bn_ref[i] != e)
                def _():
                    for cp in fetch(bn_ref[i], 1 - slot):
                        cp.start()

            row = lax.broadcasted_iota(jnp.int32, (MOE_BLOCK, 1), 0)
            lo, hi = _unpack_halves(jnp.where(row < br_ref[i], x_ref[...], jnp.uint32(0)))
            hg = _dot(lo, wg_b[:half]) + _dot(hi, wg_b[half:])
            hu = _dot(lo, wu_b[:half]) + _dot(hi, wu_b[half:])
            o_ref[...] = _pack_halves(_dot((_silu(hg) * hu).astype(BF16), wd_b[...]))

    blk = lambda i, be, bn, br, nu: (jnp.minimum(i, nu[0] - 1), 0)
    hbm = pl.BlockSpec(memory_space=pl.ANY)
    grid_spec = pltpu.PrefetchScalarGridSpec(
        num_scalar_prefetch=4, grid=(nblk,),
        in_specs=[pl.BlockSpec((MOE_BLOCK, half), blk), hbm, hbm, hbm],
        out_specs=pl.BlockSpec((MOE_BLOCK, half), blk),
        scratch_shapes=[pltpu.VMEM((2, d, ed), F32), pltpu.VMEM((2, d, ed), F32), pltpu.VMEM((2, ed, d), F32),
                        pltpu.VMEM((d, ed), BF16), pltpu.VMEM((d, ed), BF16), pltpu.VMEM((ed, d), BF16),
                        pltpu.SMEM((1,), jnp.int32), pltpu.SemaphoreType.DMA((2,))])
    return pl.pallas_call(body, grid_spec=grid_spec, out_shape=jax.ShapeDtypeStruct((cap, half), jnp.uint32), name="moe_experts",
                          compiler_params=_cp("arbitrary"))(blk_exp, blk_next, blk_rows, n_used, xs_words, w_gate, w_up, w_down)


def _moe_combine(ys_words, dest, gates, *, tc):
    cap, half = ys_words.shape
    d = 2 * half
    m = gates.shape[0]
    nt = m // tc

    def body(dest0_ref, destn_ref, g_ref, y_ref, o_ref, buf, sem):
        i = pl.program_id(0)

        def issue(dref, slot):
            def one(r, carry):
                for j in range(TOP_K):
                    pltpu.make_async_copy(y_ref.at[pl.ds(dref[0, 0, r * TOP_K + j], 1)],
                                          buf.at[slot, j, pl.ds(r, 1)], sem.at[slot]).start()
                return carry
            lax.fori_loop(0, tc, one, 0, unroll=8)

        @pl.when(i == 0)
        def _():
            issue(dest0_ref, 0)

        @pl.when(i + 1 < nt)
        def _():
            issue(destn_ref, (i + 1) % 2)

        slot = i % 2
        for j in range(TOP_K):
            pltpu.make_async_copy(y_ref.at[pl.ds(0, tc)], buf.at[slot, j], sem.at[slot]).wait()
        g = g_ref[...]
        acc_lo = acc_hi = None
        for j in range(TOP_K):
            lo, hi = _unpack_halves(buf[slot, j])
            gj = g[:, j:j + 1]
            acc_lo = lo.astype(F32) * gj if acc_lo is None else acc_lo + lo.astype(F32) * gj
            acc_hi = hi.astype(F32) * gj if acc_hi is None else acc_hi + hi.astype(F32) * gj
        o_ref[:, :half] = acc_lo
        o_ref[:, half:] = acc_hi

    return pl.pallas_call(
        body, grid=(nt,), name="moe_combine",
        in_specs=[pl.BlockSpec((1, 1, tc * TOP_K), lambda i: (0, 0, 0), memory_space=pltpu.SMEM),
                  pl.BlockSpec((1, 1, tc * TOP_K), lambda i: (jnp.minimum(i + 1, nt - 1), 0, 0), memory_space=pltpu.SMEM),
                  pl.BlockSpec((tc, LANES), lambda i: (i, 0)),
                  pl.BlockSpec(memory_space=pl.ANY)],
        out_specs=pl.BlockSpec((tc, d), lambda i: (i, 0)),
        out_shape=jax.ShapeDtypeStruct((m, d), F32),
        scratch_shapes=[pltpu.VMEM((2, TOP_K, tc, half), jnp.uint32), pltpu.SemaphoreType.DMA((2,))],
        compiler_params=_cp("arbitrary"))(dest, dest, gates, ys_words)


def _moe_layer(geo, x_res, xb, x_words, layer, router, router_bias, w_gate, w_up, w_down, ws_gate, ws_up, ws_down,
               ln_g, ln_b, out_dtype):
    m, d = x_res.shape
    tm = geo.tm
    eidx, gates, rank, counts = _moe_route(xb, _pad_cols(router, LANES).astype(BF16),
                                           jnp.pad(router_bias, (0, LANES - N_EXPERTS)).reshape(1, -1), tm=tm)
    cnt = counts[0, :N_EXPERTS].astype(jnp.int32)
    padded = (cnt + MOE_BLOCK - 1) // MOE_BLOCK * MOE_BLOCK
    pad_end = jnp.cumsum(padded)
    pad_start = pad_end - padded
    n_blocks = -(-(m * TOP_K) // MOE_BLOCK) + N_EXPERTS
    cap = n_blocks * MOE_BLOCK
    blk_row0 = jnp.arange(n_blocks, dtype=jnp.int32) * MOE_BLOCK
    blk_exp = jnp.minimum(jnp.sum((pad_end[None, :] <= blk_row0[:, None]).astype(jnp.int32), axis=1), N_EXPERTS - 1)
    blk_rows = jnp.clip(cnt[blk_exp] - (blk_row0 - pad_start[blk_exp]), 0, MOE_BLOCK).astype(jnp.int32)
    n_used = (pad_end[-1:] // MOE_BLOCK).astype(jnp.int32)
    eid = jnp.arange(N_EXPERTS, dtype=jnp.int32)
    later = jnp.where((eid[None, :] > eid[:, None]) & (cnt[None, :] > 0), eid[None, :], N_EXPERTS)
    next_e = jnp.min(later, axis=1)
    next_e = jnp.where(next_e == N_EXPERTS, eid, next_e)
    blk_next = next_e[blk_exp].astype(jnp.int32)
    dest = jnp.sum(jnp.where(eidx[:, :TOP_K, None] == eid, pad_start, 0), axis=-1) + rank[:, :TOP_K]

    tb = tm
    xs_words = _moe_dispatch(x_words, dest.reshape(m // tb, 1, tb * TOP_K), cap, tb=tb)
    ys = _moe_experts(xs_words, blk_exp, blk_next, blk_rows, n_used, w_gate, w_up, w_down, layer)
    tc = min(MOE_COMBINE_ROWS, tm)
    routed = _moe_combine(ys, dest.reshape(m // tc, 1, tc * TOP_K), gates, tc=tc)

    sd = ws_gate.shape[1]
    w_sh = jnp.concatenate([ws_gate, ws_up], axis=1).astype(BF16)
    (hs,) = _mm([(xb, w_sh)], tm=tm, tn=2 * sd, epilogue=lambda acc: (_silu(acc[:, :sd]) * acc[:, sd:],), outs=[(sd, BF16)])
    (y,) = _mm([(hs, ws_down.astype(BF16))], tm=tm, tn=d,
               aux=[(x_res, 'rowcol'), (routed, 'rowcol'), (ln_g.reshape(1, -1), 'col'), (ln_b.reshape(1, -1), 'col')],
               epilogue=lambda acc, xr, rt, g, b: (_ep_ln(acc + rt, xr, g, b),), outs=[(d, out_dtype)], name="moe_out")
    return y


def _rwkv_params(li, mu, w0, w2, a0, a2, g2, k_k, k_a, r_k, ln_g, ln_b):
    row = lambda a: a[li].reshape(1, -1)
    padr = lambda a, n: jnp.pad(a, ((0, n - a.shape[0]), (0, 0)))
    return dict(mu=_rwkv_cols_layout(mu[li])[None, :], w0=row(w0), a0=row(a0), k_k=row(k_k), k_a=row(k_a),
                w2=padr(w2[li], LANES), a2=padr(a2[li], LANES), g2=g2[li], r_k=row(r_k), ln_g=row(ln_g), ln_b=row(ln_b))


def _mix_out(geo, x_res, o_a, o_b, w_out, ln_g, ln_b):
    d = x_res.shape[1]
    ka = o_a.shape[1]
    return _mm([(o_a, w_out[:ka].astype(BF16)), (o_b, w_out[ka:].astype(BF16))], tm=geo.tm, tn=d,
               aux=[(x_res, 'rowcol'), (ln_g.reshape(1, -1), 'col'), (ln_b.reshape(1, -1), 'col')],
               epilogue=_ep_ln_packed, outs=[(d, BF16), (d // 2, jnp.uint32)], name="mix_out")


def kernel(x_prompt, x_sample, cache_mla_ckv, cache_mla_krope, state_rwkv_shift, state_rwkv_wkv, cache_fox_k, cache_fox_v, cache_fox_logf, state_gdn_conv, state_gdn_wkv, ln1_g, ln1_b, ln2_g, ln2_b, ev_w_in, ev_w_out, mla_kv_norm, mla_w_uk, mla_w_uv, rwkv_mu, rwkv_w0, rwkv_w2, rwkv_a0, rwkv_a2, rwkv_g2, rwkv_k_k, rwkv_k_a, rwkv_r_k, rwkv_ln_g, rwkv_ln_b, od_w_in, od_w_out, fox_q_norm, fox_k_norm, fox_f_bias, gdn_conv_w, gdn_a_log, gdn_dt_bias, gdn_norm, moe_router, moe_router_bias, moe_w_gate, moe_w_up, moe_w_down, moe_ws_gate, moe_ws_up, moe_ws_down):
    bp, tp, d = x_prompt.shape
    bs, ts, _ = x_sample.shape
    geo = _Geom(bp, tp, bs, ts, cache_mla_ckv.shape[2])
    x = jnp.concatenate([x_prompt.reshape(bp * tp, d), x_sample.reshape(bs * ts, d)], axis=0)
    xb = x.astype(BF16)
    rope_tab = _rope_table(geo)
    st = {}
    n_layers = ln1_g.shape[0]
    for layer in range(n_layers):
        li = layer // 2
        if layer % 2 == 0:
            rw = _rwkv_params(li, rwkv_mu, rwkv_w0, rwkv_w2, rwkv_a0, rwkv_a2, rwkv_g2, rwkv_k_k, rwkv_k_a, rwkv_r_k,
                              rwkv_ln_g, rwkv_ln_b)
            o_a, o_b, new = _even_mixer(geo, xb, rope_tab, ev_w_in[li], mla_kv_norm[li], mla_w_uk[li], mla_w_uv[li], rw,
                                        cache_mla_ckv[li], cache_mla_krope[li], state_rwkv_shift[li], state_rwkv_wkv[li])
            w_out = ev_w_out[li]
        else:
            fx = {'q_norm': fox_q_norm[li], 'k_norm': fox_k_norm[li], 'f_bias': fox_f_bias[li]}
            gd = {'conv_w': gdn_conv_w[li], 'a_log': gdn_a_log[li], 'dt_bias': gdn_dt_bias[li], 'norm': gdn_norm[li]}
            o_a, o_b, new = _odd_mixer(geo, xb, od_w_in[li], fx, gd, cache_fox_k[li], cache_fox_v[li], cache_fox_logf[li],
                                       state_gdn_conv[li], state_gdn_wkv[li])
            w_out = od_w_out[li]
        for name, val in new.items():
            st.setdefault(name, []).append(val)
        xb, x_words = _mix_out(geo, x, o_a, o_b, w_out, ln1_g[layer], ln1_b[layer])
        last = layer == n_layers - 1
        x = _moe_layer(geo, xb, xb, x_words, layer, moe_router[layer], moe_router_bias[layer], moe_w_gate, moe_w_up,
                       moe_w_down, moe_ws_gate[layer], moe_ws_up[layer], moe_ws_down[layer], ln2_g[layer], ln2_b[layer],
                       F32 if last else BF16)
        xb = x
    names = ('ckv', 'kr', 'sh', 'rwkv', 'fk', 'fv', 'fl', 'cv', 'gdn')
    return ((x[:geo.mp].reshape(bp, tp, d), x[geo.mp:].reshape(bs, ts, d))
            + tuple(jnp.stack(st[n + '_p']) for n in names) + tuple(jnp.stack(st[n + '_s']) for n in names))
```

```python
import functools
import math

import jax
import jax.numpy as jnp
from jax import lax
from jax.experimental import pallas as pl
from jax.experimental.pallas import tpu as pltpu

F32 = jnp.float32
BF16 = jnp.bfloat16
HP = lax.Precision.HIGHEST

D_MODEL = 2048
DEPTH = 2
DN_ALPHA = float((2.0 * DEPTH) ** 0.25)
LN_EPS = 1e-5
RMS_EPS = 1e-6
CHUNK = 64
LANES = 128

MLA_HEADS, MLA_NOPE, MLA_ROPE, MLA_VDIM, MLA_KV_LORA = 8, 128, 64, 128, 512
MLA_QK = MLA_NOPE + MLA_ROPE
ROPE_THETA = 10000.0
RWKV_HEADS, RWKV_HD = 16, 64
RWKV_W = RWKV_HEADS * RWKV_HD
RWKV_DECAY_LORA, RWKV_A_LORA, RWKV_G_LORA = 96, 96, 256
RWKV_GN_EPS = 64e-5
RWKV_PCOLS = 3 * RWKV_W + 2 * LANES + RWKV_G_LORA
FOX_HEADS, FOX_HD = 8, 128
FOX_W = FOX_HEADS * FOX_HD
GDN_HEADS, GDN_HD, GDN_CONV = 8, 128, 4
GDN_W = GDN_HEADS * GDN_HD
GDN_QKV = 3 * GDN_W
N_EXPERTS, TOP_K, N_GROUPS, TOPK_GROUPS = 64, 6, 8, 4
EXPERT_DIM, SHARED_DIM = 512, 512
ROUTED_SCALE = 2.5
SCAN_CHUNK = 64
FLASH_KEY_SUBTILE = 1024
NEG_BIG = -1e30
RWKV_PASSES = dict(a=1, x=1, inv=1, u=1, y=1, s=1)
GDN_PASSES = dict(a=1, x=1, inv=1, u=1, y=1, s=1)

VMEM_LIMIT_BYTES = 56 * 1024 * 1024
MM_MAX_COLS = 1792


def _cp(*sem):
    return pltpu.CompilerParams(dimension_semantics=("arbitrary",) * len(sem), vmem_limit_bytes=VMEM_LIMIT_BYTES)


def _dot(a, b, prec=None):
    return jnp.dot(a, b, precision=prec, preferred_element_type=F32)


def _dot_nt(a, b, prec=None):
    return lax.dot_general(a, b, (((1,), (1,)), ((), ())), precision=prec, preferred_element_type=F32)


def _dot_tn(a, b, prec=None):
    return lax.dot_general(a, b, (((0,), (0,)), ((), ())), precision=prec, preferred_element_type=F32)


def _sigmoid(x):
    return 1.0 / (1.0 + jnp.exp(-x))


def _softplus(x):
    return jnp.maximum(x, 0.0) + jnp.log(1.0 + jnp.exp(-jnp.abs(x)))


def _silu(x):
    return x * _sigmoid(x)


def _mm(pairs, *, tm, tn, epilogue, outs, aux=(), rows=None, name="mm"):
    row0, m = rows if rows is not None else (0, pairs[0][0].shape[0])
    n = pairs[0][1].shape[1]
    if tn is None:
        tn = n if n <= MM_MAX_COLS else max(c for c in range(LANES, MM_MAX_COLS + 1, LANES) if n % c == 0)
    outs = [(tn if ow is None else ow, dt) for ow, dt in outs]
    assert m % tm == 0 and row0 % tm == 0 and n % tn == 0, (m, row0, n, tm, tn)
    nj, ni = n // tn, m // tm
    rb0 = row0 // tm
    in_specs, args = [], []
    for a, w in pairs:
        k = a.shape[1]
        in_specs += [pl.BlockSpec((tm, k), lambda j, i: (rb0 + i, 0)), pl.BlockSpec((k, tn), lambda j, i: (0, j))]
        args += [a, w]
    for arr, kind in aux:
        if kind == 'row':
            in_specs.append(pl.BlockSpec((tm, arr.shape[1]), lambda j, i: (rb0 + i, 0)))
        elif kind == 'rowcol':
            in_specs.append(pl.BlockSpec((tm, tn), lambda j, i: (rb0 + i, j)))
        elif kind == 'row0':
            in_specs.append(pl.BlockSpec(arr.shape, lambda j, i: (0, 0)))
        else:
            in_specs.append(pl.BlockSpec((1, tn), lambda j, i: (0, j)))
        args.append(arr)
    out_shape = [jax.ShapeDtypeStruct((m, ow * nj), dt) for ow, dt in outs]
    out_specs = [pl.BlockSpec((tm, ow), lambda j, i: (i, j)) for ow, dt in outs]
    n_pairs, n_aux = len(pairs), len(aux)

    def body(*refs):
        acc = None
        for p in range(n_pairs):
            d = _dot(refs[2 * p][...], refs[2 * p + 1][...])
            acc = d if acc is None else acc + d
        res = epilogue(acc, *[r[...] for r in refs[2 * n_pairs:2 * n_pairs + n_aux]])
        for o_ref, val in zip(refs[2 * n_pairs + n_aux:], res):
            o_ref[...] = val.astype(o_ref.dtype)

    res = pl.pallas_call(body, grid=(nj, ni), in_specs=in_specs, out_specs=out_specs, out_shape=out_shape, name=name,
                         compiler_params=_cp("parallel", "parallel"))(*args)
    return res


def _mm_t(w_t, a, *, nb, t, row0, name="mm_t"):
    n, k = w_t.shape
    tm = next((c for c in (512, 384, 256, 128) if t % c == 0), t)
    assert t % tm == 0 and row0 % tm == 0
    nt = t // tm

    def body(w_ref, a_ref, o_ref):
        o_ref[...] = _dot_nt(w_ref[...], a_ref[...]).astype(o_ref.dtype)

    return pl.pallas_call(
        body, grid=(nb, nt), name=name,
        in_specs=[pl.BlockSpec((n, k), lambda b, i: (0, 0)), pl.BlockSpec((tm, k), lambda b, i: (row0 // tm + b * nt + i, 0))],
        out_specs=pl.BlockSpec((n, tm), lambda b, i: (b, i)),
        out_shape=jax.ShapeDtypeStruct((nb * n, t), BF16),
        compiler_params=_cp("parallel", "parallel"))(w_t, a)


def _rope_lanes(x, tab):
    c, s1, s2 = tab[:, :LANES], tab[:, LANES:2 * LANES], tab[:, 2 * LANES:]
    n = x.shape[1]
    lo = pltpu.roll(x, n - MLA_ROPE // 2, 1)
    hi = pltpu.roll(x, MLA_ROPE // 2, 1)
    if n > LANES:
        reps = n // LANES
        c, s1, s2 = (jnp.concatenate([t] * reps, axis=1) for t in (c, s1, s2))
    return x * c + lo * s1 + hi * s2


def _ep_rms(acc, g):
    y = acc * lax.rsqrt(jnp.mean(acc * acc, axis=-1, keepdims=True) + RMS_EPS) * g
    return y, y


def _ep_ln(acc, x, g, b):
    z = DN_ALPHA * x.astype(F32) + acc
    mu = jnp.mean(z, axis=-1, keepdims=True)
    zc = z - mu
    var = jnp.mean(zc * zc, axis=-1, keepdims=True)
    return zc * lax.rsqrt(var + LN_EPS) * g + b


def _flash(qs, ks, vt, *, nb, tq_total, tk_total, q_row0, q_off, tq, tk, chunked, nh, bias=None, gate=None,
           out_dtype=BF16, name="flash"):
    assert tq_total % tq == 0 and tk_total % tk == 0 and q_row0 % tq == 0
    nq, nk = tq_total // tq, tk_total // tk
    qrb0 = q_row0 // tq
    ts = tk if tk % FLASH_KEY_SUBTILE else min(tk, FLASH_KEY_SUBTILE)
    for qi_s in range(nq):
        last_s = min(nk - 1, (q_off + qi_s * tq + tq - 1) // tk)
        assert last_s * tk <= q_off + qi_s * tq + 1, "key tiles before the last needed one must be fully visible"
    n_q, n_k = len(qs), len(ks)
    has_bias, has_gate = bias is not None, gate is not None

    def last_k(qi):
        return jnp.minimum(nk - 1, (q_off + qi * tq + tq - 1) // tk)

    kblk = lambda qi, ki: jnp.minimum(ki, last_k(qi))
    q_map = lambda b, h, qi, ki: (qrb0 + b * nq + qi, h)
    in_specs, args = [], []
    for q in qs:
        in_specs.append(pl.BlockSpec((tq, LANES), q_map))
        args.append(q)
    for arr, c0, per_head in ks:
        if per_head:
            in_specs.append(pl.BlockSpec((tk, LANES), lambda b, h, qi, ki, c0=c0: (b * nk + kblk(qi, ki), c0 + h)))
        else:
            in_specs.append(pl.BlockSpec((tk, LANES), lambda b, h, qi, ki, c0=c0: (b * nk + kblk(qi, ki), c0)))
        args.append(arr)
    in_specs.append(pl.BlockSpec((LANES, tk), lambda b, h, qi, ki: (b * nh + h, kblk(qi, ki))))
    args.append(vt)
    if has_bias:
        in_specs.append(pl.BlockSpec((1, 1, 1, tq), lambda b, h, qi, ki: (b, h, 0, qi)))
        in_specs.append(pl.BlockSpec((1, 1, tk, 1), lambda b, h, qi, ki: (b, h, kblk(qi, ki), 0)))
        args += list(bias)
    if has_gate:
        in_specs.append(pl.BlockSpec((tq, LANES), q_map))
        args.append(gate)

    def body(*refs):
        q_refs = refs[:n_q]
        k_refs = refs[n_q:n_q + n_k]
        vt_ref = refs[n_q + n_k]
        pos = n_q + n_k + 1
        if has_bias:
            qb_ref, kb_ref = refs[pos], refs[pos + 1]
            pos += 2
        if has_gate:
            gate_ref = refs[pos]
            pos += 1
        o_ref, m_ref, l_ref, acc_ref = refs[pos:pos + 4]
        qi, ki = pl.program_id(2), pl.program_id(3)

        @pl.when(ki == 0)
        def _():
            m_ref[...] = jnp.full(m_ref.shape, NEG_BIG, F32)
            l_ref[...] = jnp.zeros(l_ref.shape, F32)
            acc_ref[...] = jnp.zeros(acc_ref.shape, F32)

        def step(masked):
            q_all = jnp.concatenate([q_ref[...] for q_ref in q_refs], axis=1) if n_q > 1 else q_refs[0][...]
            qpos = q_off + qi * tq + lax.broadcasted_iota(jnp.int32, (1, tq), 1)
            qlim = (qpos | (CHUNK - 1)) if chunked else qpos
            subs = [slice(j * ts, (j + 1) * ts) for j in range(tk // ts)]
            ss = []
            for j, rows in enumerate(subs):
                k_all = jnp.concatenate([k_ref[rows, :] for k_ref in k_refs], axis=1) if n_k > 1 else k_refs[0][rows, :]
                s = _dot_nt(k_all, q_all)
                if has_bias:
                    s = s + (qb_ref[0, 0] - kb_ref[0, 0, rows, :])
                if masked:
                    kpos = ki * tk + j * ts + lax.broadcasted_iota(jnp.int32, (ts, 1), 0)
                    s = jnp.where(kpos <= qlim, s, NEG_BIG)
                ss.append(s)
            m_old = m_ref[...]
            m_new = m_old
            for s in ss:
                m_new = jnp.maximum(m_new, jnp.max(s, axis=0, keepdims=True))
            alpha = jnp.exp(m_old - m_new)
            l_new = alpha * l_ref[...]
            acc_new = alpha * acc_ref[...]
            for rows, s in zip(subs, ss):
                p = jnp.exp(s - m_new)
                l_new = l_new + jnp.sum(p, axis=0, keepdims=True)
                acc_new = acc_new + _dot(vt_ref[:, rows], p.astype(BF16))
            m_ref[...] = m_new
            l_ref[...] = l_new
            acc_ref[...] = acc_new

        @pl.when(ki < last_k(qi))
        def _():
            step(False)

        @pl.when(ki == last_k(qi))
        def _():
            step(True)

        @pl.when(ki == nk - 1)
        def _():
            o = (acc_ref[...] / l_ref[...]).T
            if has_gate:
                o = o * gate_ref[...].astype(F32)
            o_ref[...] = o.astype(o_ref.dtype)

    return pl.pallas_call(
        body, grid=(nb, nh, nq, nk), in_specs=in_specs, name=name,
        out_specs=pl.BlockSpec((tq, LANES), lambda b, h, qi, ki: (b * nq + qi, h)),
        out_shape=jax.ShapeDtypeStruct((nb * tq_total, nh * LANES), out_dtype),
        scratch_shapes=[pltpu.VMEM((1, tq), F32), pltpu.VMEM((1, tq), F32), pltpu.VMEM((LANES, tq), F32)],
        compiler_params=_cp("parallel", "parallel", "parallel", "arbitrary"))(*args)


def _head_sum_matrices(n_heads, hd):
    lane = jnp.arange(n_heads * hd)[:, None] // hd
    e = (lane == jnp.arange(LANES)[None, :]).astype(F32)
    return e, e.T


def _rwkv_prep(cols, shift_prev, prm, *, nb, t, row0):
    tt = min(256, t)
    assert t % tt == 0 and row0 % tt == 0
    nt = t // tt
    rb0 = row0 // tt
    w = RWKV_W

    def body(c_ref, sp_ref, mu_ref, w0_ref, a0_ref, kk_ref, ka_ref, w2_ref, a2_ref, g2_ref, e_ref, et_ref,
             r_o, k_o, v_o, kk_o, b_o, lw_o, g_o, sh_o, carry):
        ti = pl.program_id(1)

        @pl.when(ti == 0)
        def _():
            carry[...] = sp_ref[0]

        c = c_ref[...]
        row = lax.broadcasted_iota(jnp.int32, (tt, 1), 0)
        prev = jnp.where(row == 0, carry[...], pltpu.roll(c, 1, 0))
        carry[...] = c[tt - 1:tt, :]
        sh_o[0] = c[tt - 1:tt, :]
        xs = c + (prev - c) * mu_ref[...]
        r, k, v = xs[:, :w], xs[:, w:2 * w], xs[:, 2 * w:3 * w]
        o1 = 3 * w
        lora = lambda u, w_ref: _dot(u.astype(BF16), w_ref[...].astype(BF16))
        w_raw = w0_ref[...] + lora(jnp.tanh(xs[:, o1:o1 + LANES]), w2_ref)
        log_w = -jnp.exp(-_softplus(-w_raw) - 0.5)
        a = _sigmoid(a0_ref[...] + lora(xs[:, o1 + LANES:o1 + 2 * LANES], a2_ref))
        g = lora(_sigmoid(xs[:, o1 + 2 * LANES:]), g2_ref)

        def head_dot(u, sel_ref):
            hi, lo = _split_bf16(u)
            sel = sel_ref[...].astype(BF16)
            return _dot(hi.astype(BF16), sel) + _dot(lo.astype(BF16), sel)

        kk = k * kk_ref[...]
        kk = kk * head_dot(lax.rsqrt(head_dot(kk * kk, e_ref) + 1e-6), et_ref)
        r_o[...] = r.astype(r_o.dtype)
        k_o[...] = (k * (1.0 + (a - 1.0) * ka_ref[...])).astype(k_o.dtype)
        v_o[...] = v.astype(v_o.dtype)
        kk_o[...] = kk.astype(kk_o.dtype)
        b_o[...] = (kk * a).astype(b_o.dtype)
        lw_o[...] = log_w
        g_o[...] = g.astype(g_o.dtype)

    e, et = _head_sum_matrices(RWKV_HEADS, RWKV_HD)
    tok = pl.BlockSpec((tt, w), lambda b, ti: (b * nt + ti, 0))
    full = lambda arr: pl.BlockSpec(arr.shape, lambda b, ti: (0,) * arr.ndim)
    params = [prm['mu'], prm['w0'], prm['a0'], prm['k_k'], prm['k_a'], prm['w2'], prm['a2'], prm['g2'], e, et]
    outs = pl.pallas_call(
        body, grid=(nb, nt), name="rwkv_prep",
        in_specs=[pl.BlockSpec((tt, RWKV_PCOLS), lambda b, ti: (rb0 + b * nt + ti, 0)),
                  pl.BlockSpec((1, 1, RWKV_PCOLS), lambda b, ti: (b, 0, 0))] + [full(p) for p in params],
        out_specs=[tok] * 7 + [pl.BlockSpec((1, 1, RWKV_PCOLS), lambda b, ti: (b, 0, 0))],
        out_shape=[jax.ShapeDtypeStruct((nb * t, w), dt) for dt in (BF16, BF16, BF16, BF16, BF16, F32, BF16)]
        + [jax.ShapeDtypeStruct((nb, 1, RWKV_PCOLS), F32)],
        scratch_shapes=[pltpu.VMEM((1, RWKV_PCOLS), F32)],
        compiler_params=_cp("parallel", "arbitrary"))(cols, shift_prev, *params)
    return outs


def _cumsum_rows(x, n):
    row = lax.broadcasted_iota(jnp.int32, (n, 1), 0)
    sh = 1
    while sh < n:
        x = x + jnp.where(row >= sh, pltpu.roll(x, sh, 0), 0.0)
        sh *= 2
    return x


def _split_bf16(x):
    hi = x.astype(BF16).astype(F32)
    return hi, x - hi


def _bmm(a, b, ca, cb, passes):
    dn = (((ca,), (cb,)), ((0,), (0,)))
    if passes == 6:
        return lax.dot_general(a, b, dn, precision=HP, preferred_element_type=F32)
    if passes == 3:
        ah, al = _split_bf16(a)
        bh, bl = _split_bf16(b)
        a = jnp.concatenate([ah, ah, al], axis=ca)
        b = jnp.concatenate([bh, bl, bh], axis=cb)
    return lax.dot_general(a.astype(BF16), b.astype(BF16), dn, preferred_element_type=F32)


def _heads(x, nh, hd):
    return jnp.stack([x[:, h * hd:(h + 1) * hd] for h in range(nh)], axis=0)


def _unheads(x):
    return jnp.concatenate([x[h] for h in range(x.shape[0])], axis=1)


def _unit_lower_inverse(nmat, n, passes):
    eye = (lax.broadcasted_iota(jnp.int32, (n, n), 0) == lax.broadcasted_iota(jnp.int32, (n, n), 1)).astype(F32)
    t = eye + nmat
    p = nmat
    steps = int(math.ceil(math.log2(n))) - 1
    for _ in range(steps):
        p = _bmm(p, p, 2, 1, passes)
        t = t + _bmm(p, t, 2, 1, passes)
    return t


def _rwkv_scan(r, k, v, kk, bvec, logw, g, s0, prm, *, nb, t):
    c = min(SCAN_CHUNK, t)
    assert t % c == 0
    nc = t // c
    hd, nh = RWKV_HD, RWKV_HEADS

    def body(r_ref, k_ref, v_ref, kk_ref, b_ref, lw_ref, g_ref, s0_ref, rk_ref, lg_ref, lb_ref, o_ref, sf_ref, st):
        ci = pl.program_id(1)

        @pl.when(ci == 0)
        def _():
            st[...] = s0_ref[0]

        lw = lw_ref[...]
        lc = _cumsum_rows(lw, c)
        rr, kx, vv, bb = (ref[...].astype(F32) for ref in (r_ref, k_ref, v_ref, b_ref))
        g_inv = jnp.exp(-lc)
        g_end = jnp.exp(lc[c - 1:c, :] - lc)
        hs = lambda a: _heads(a, nh, hd)
        at, rt = hs(-kk_ref[...].astype(F32) * jnp.exp(lc - lw)), hs(rr * jnp.exp(lc))
        bt, kt = hs(bb * g_inv), hs(kx * g_inv)
        bg, kg = hs(bb * g_end), hs(kx * g_end)
        w_end = hs(jnp.exp(lc[c - 1:c, :]))
        v_h = hs(vv)
        ri = lax.broadcasted_iota(jnp.int32, (c, c), 0)
        cj = lax.broadcasted_iota(jnp.int32, (c, c), 1)
        strict = ri > cj
        incl2 = lax.broadcasted_iota(jnp.int32, (c, 2 * c), 0) >= (lax.broadcasted_iota(jnp.int32, (c, 2 * c), 1) & (c - 1))
        s_all = st[...]
        pp = RWKV_PASSES
        lhs = jnp.concatenate([at, rt], axis=1)
        a_all = _bmm(lhs, jnp.concatenate([bt, kt], axis=1), 2, 2, pp['a'])
        x_all = _bmm(lhs, s_all, 2, 2, pp['x'])
        tinv = _unit_lower_inverse(jnp.where(strict, a_all[:, :c, :c], 0.0), c, pp['inv'])
        a_ak = jnp.where(strict, a_all[:, :c, c:], 0.0)
        u = _bmm(tinv, x_all[:, :c] + _bmm(a_ak, v_h, 2, 1, pp['u']), 2, 1, pp['u'])
        uv = jnp.concatenate([u, v_h], axis=1)
        y = x_all[:, c:] + _bmm(jnp.where(incl2, a_all[:, c:, :], 0.0), uv, 2, 1, pp['y'])
        st[...] = s_all * w_end + _bmm(uv, jnp.concatenate([bg, kg], axis=1), 1, 1, pp['s'])
        mu = jnp.mean(y, axis=-1, keepdims=True)
        yc = y - mu
        var = jnp.mean(yc * yc, axis=-1, keepdims=True)
        yn = yc * lax.rsqrt(var + RWKV_GN_EPS) * hs(lg_ref[...]) + hs(lb_ref[...])
        bonus = jnp.sum(hs(rr * kx * rk_ref[...]), axis=-1, keepdims=True) * v_h
        o_ref[...] = (_unheads(yn + bonus) * g_ref[...].astype(F32)).astype(o_ref.dtype)

        @pl.when(ci == nc - 1)
        def _():
            sf_ref[0] = st[...]

    tok = pl.BlockSpec((c, RWKV_W), lambda b, ci: (b * nc + ci, 0))
    stt = pl.BlockSpec((1, nh, hd, hd), lambda b, ci: (b, 0, 0, 0))
    par = pl.BlockSpec((1, RWKV_W), lambda b, ci: (0, 0))
    return pl.pallas_call(
        body, grid=(nb, nc), name="rwkv_scan",
        in_specs=[tok] * 7 + [stt, par, par, par],
        out_specs=[tok, stt],
        out_shape=[jax.ShapeDtypeStruct((nb * t, RWKV_W), BF16), jax.ShapeDtypeStruct((nb, nh, hd, hd), F32)],
        scratch_shapes=[pltpu.VMEM((nh, hd, hd), F32)],
        compiler_params=_cp("parallel", "arbitrary"))(r, k, v, kk, bvec, logw, g, s0, prm['r_k'], prm['ln_g'], prm['ln_b'])


class _Geom:
    def __init__(self, bp, tp, bs, ts, past):
        self.bp, self.tp, self.bs, self.ts, self.past = bp, tp, bs, ts, past
        self.mp, self.ms = bp * tp, bs * ts
        self.m = self.mp + self.ms
        self.tm = math.gcd(self.m, 512)
        assert self.tm % 16 == 0 and tp % CHUNK == 0 and past % CHUNK == 0 and ts == CHUNK
        self.tq_p = min(1024, tp)
        self.kpad = -(-(past + ts) // (3 * LANES)) * (3 * LANES)
        self.tk_s = self.kpad


def _pad_cols(w, n):
    return jnp.pad(w, ((0, 0), (0, n - w.shape[1])))


def _rope_table(geo):
    half = MLA_ROPE // 2
    pos = jnp.concatenate([jnp.tile(jnp.arange(geo.tp, dtype=jnp.int32), geo.bp),
                           geo.past + jnp.tile(jnp.arange(geo.ts, dtype=jnp.int32), geo.bs)])
    inv_freq = ROPE_THETA ** (-jnp.arange(half, dtype=F32) / half)
    ang = pos.astype(F32)[:, None] * inv_freq[None, :]
    cos, sin = jnp.cos(ang), jnp.sin(ang)
    z = jnp.zeros_like(cos)
    zz = jnp.zeros((pos.shape[0], LANES - MLA_ROPE), F32)
    return jnp.concatenate([cos, cos, zz, -sin, z, zz, z, sin, zz], axis=1)


def _rwkv_cols_layout(a):
    o1 = 3 * RWKV_W
    o2 = o1 + RWKV_DECAY_LORA
    o3 = o2 + RWKV_A_LORA
    pad = [(0, 0)] * (a.ndim - 1)
    return jnp.concatenate([a[..., :o1],
                            jnp.pad(a[..., o1:o2], pad + [(0, LANES - RWKV_DECAY_LORA)]),
                            jnp.pad(a[..., o2:o3], pad + [(0, LANES - RWKV_A_LORA)]),
                            a[..., o3:]], axis=-1)


def _rwkv_cols_unlayout(a):
    o1 = 3 * RWKV_W
    return jnp.concatenate([a[..., :o1], a[..., o1:o1 + RWKV_DECAY_LORA],
                            a[..., o1 + LANES:o1 + LANES + RWKV_A_LORA], a[..., o1 + 2 * LANES:]], axis=-1)


def _even_mixer(geo, xb, rope_tab, w_in, kv_norm, w_uk, w_uv, rw, cache_ckv, cache_kr, shift_prev, wkv0):
    tm, m, mp = geo.tm, geo.m, geo.mp
    d = w_in.shape[0]
    nq = MLA_HEADS * MLA_QK
    wq = w_in[:, :nq].reshape(d, MLA_HEADS, MLA_QK)
    w_qn = wq[:, :, :MLA_NOPE].reshape(d, MLA_HEADS * MLA_NOPE).astype(BF16)
    w_qr = jnp.pad(wq[:, :, MLA_NOPE:], ((0, 0), (0, 0), (0, LANES - MLA_ROPE))).reshape(d, MLA_HEADS * LANES).astype(BF16)
    w_ckv = w_in[:, nq:nq + MLA_KV_LORA].astype(BF16)
    w_kr = _pad_cols(w_in[:, nq + MLA_KV_LORA:nq + MLA_KV_LORA + MLA_ROPE], LANES).astype(BF16)
    w_rw = _rwkv_cols_layout(w_in[:, nq + MLA_KV_LORA + MLA_ROPE:]).astype(BF16)
    scale = MLA_QK ** -0.5

    (qn,) = _mm([(xb, w_qn)], tm=tm, tn=None, epilogue=lambda acc: (acc * scale,), outs=[(None, BF16)])
    (qr,) = _mm([(xb, w_qr)], tm=tm, tn=MLA_HEADS * LANES, aux=[(rope_tab, 'row')],
                epilogue=lambda acc, tab: (_rope_lanes(acc, tab) * scale,), outs=[(MLA_HEADS * LANES, BF16)])
    groups = ((0, mp), (mp, geo.ms))
    (ckv_p, ckvb_p), (ckv_s, ckvb_s) = (
        _mm([(xb, w_ckv)], tm=tm, tn=MLA_KV_LORA, aux=[(kv_norm.reshape(1, -1), 'col')], rows=g,
            epilogue=_ep_rms, outs=[(MLA_KV_LORA, F32), (MLA_KV_LORA, BF16)]) for g in groups)
    (kr_p, krb_p), (kr_s, krb_s) = (
        _mm([(xb, w_kr)], tm=tm, tn=LANES, aux=[(rope_tab, 'row')], rows=g,
            epilogue=lambda acc, tab: (_rope_lanes(acc, tab),) * 2, outs=[(LANES, F32), (LANES, BF16)]) for g in groups)
    (rwc,) = _mm([(xb, w_rw)], tm=tm, tn=None, epilogue=lambda acc: (acc,), outs=[(None, F32)])

    w_k = w_uk.reshape(MLA_KV_LORA, -1).astype(BF16)
    w_vt = w_uv.reshape(MLA_KV_LORA, -1).T.astype(BF16)
    (kn_p,) = _mm([(ckvb_p, w_k)], tm=tm, tn=None, epilogue=lambda acc: (acc,), outs=[(None, BF16)])
    vt_p = _mm_t(w_vt, ckvb_p, nb=geo.bp, t=geo.tp, row0=0)
    o_p = _flash([qn, qr], [(kn_p, 0, True), (krb_p, 0, False)], vt_p,
                 nb=geo.bp, tq_total=geo.tp, tk_total=geo.tp, q_row0=0, q_off=0, tq=geo.tq_p, tk=geo.tq_p,
                 chunked=True, nh=MLA_HEADS)
    padk = geo.kpad - geo.past - geo.ts
    ckv_all = jnp.concatenate([cache_ckv.astype(BF16), ckvb_s.reshape(geo.bs, geo.ts, -1),
                               jnp.zeros((geo.bs, padk, MLA_KV_LORA), BF16)], axis=1).reshape(geo.bs * geo.kpad, -1)
    kr_all = jnp.concatenate([jnp.pad(cache_kr, ((0, 0), (0, 0), (0, LANES - MLA_ROPE))).astype(BF16),
                              krb_s.reshape(geo.bs, geo.ts, -1),
                              jnp.zeros((geo.bs, padk, LANES), BF16)], axis=1).reshape(geo.bs * geo.kpad, -1)
    (kn_all,) = _mm([(ckv_all, w_k)], tm=3 * LANES, tn=None, epilogue=lambda acc: (acc,), outs=[(None, BF16)])
    vt_all = _mm_t(w_vt, ckv_all, nb=geo.bs, t=geo.kpad, row0=0)
    o_s = _flash([qn, qr], [(kn_all, 0, True), (kr_all, 0, False)], vt_all,
                 nb=geo.bs, tq_total=geo.ts, tk_total=geo.kpad, q_row0=mp, q_off=geo.past, tq=geo.ts, tk=geo.tk_s,
                 chunked=True, nh=MLA_HEADS)
    o_mla = jnp.concatenate([o_p, o_s], axis=0)

    outs_p = _rwkv_prep(rwc, jnp.zeros((geo.bp, 1, RWKV_PCOLS), F32), rw, nb=geo.bp, t=geo.tp, row0=0)
    outs_s = _rwkv_prep(rwc, _rwkv_cols_layout(shift_prev)[:, None, :], rw, nb=geo.bs, t=geo.ts, row0=mp)
    y_p, wkv_p = _rwkv_scan(*outs_p[:7], jnp.zeros((geo.bp, RWKV_HEADS, RWKV_HD, RWKV_HD), F32), rw, nb=geo.bp, t=geo.tp)
    y_s, wkv_s = _rwkv_scan(*outs_s[:7], wkv0, rw, nb=geo.bs, t=geo.ts)
    o_rwkv = jnp.concatenate([y_p, y_s], axis=0)
    states = dict(
        ckv_p=ckv_p.reshape(geo.bp, geo.tp, -1), ckv_s=ckv_s.reshape(geo.bs, geo.ts, -1),
        kr_p=kr_p[:, :MLA_ROPE].reshape(geo.bp, geo.tp, -1), kr_s=kr_s[:, :MLA_ROPE].reshape(geo.bs, geo.ts, -1),
        sh_p=_rwkv_cols_unlayout(outs_p[7][:, 0]), sh_s=_rwkv_cols_unlayout(outs_s[7][:, 0]),
        rwkv_p=wkv_p, rwkv_s=wkv_s)
    return o_mla, o_rwkv, states


def _head_rms(x, g, nh, scale=1.0):
    outs = []
    for h in range(nh):
        seg = x[:, h * LANES:(h + 1) * LANES]
        outs.append(seg * lax.rsqrt(jnp.mean(seg * seg, axis=-1, keepdims=True) + RMS_EPS) * (g * scale))
    return jnp.concatenate(outs, axis=1)


def _head_l2(x, nh, scale=1.0):
    outs = []
    for h in range(nh):
        seg = x[:, h * LANES:(h + 1) * LANES]
        outs.append(seg * (lax.rsqrt(jnp.sum(seg * seg, axis=-1, keepdims=True) + 1e-6) * scale))
    return jnp.concatenate(outs, axis=1)


def _cumsum_time(x):
    nb, t, w = x.shape

    def body(x_ref, o_ref):
        o_ref[0] = _cumsum_rows(x_ref[0], t)

    spec = pl.BlockSpec((1, t, w), lambda b: (b, 0, 0))
    return pl.pallas_call(body, grid=(nb,), in_specs=[spec], out_specs=spec, name="cumsum_time",
                          out_shape=jax.ShapeDtypeStruct(x.shape, F32), compiler_params=_cp("parallel"))(x)


def _gdn_prep(cols, conv_prev, conv_w, *, nb, t, row0):
    tt = min(256, t)
    assert t % tt == 0 and row0 % tt == 0
    nt = t // tt
    rb0 = row0 // tt
    w = GDN_W

    def body(c_ref, cp_ref, cw_ref, q_o, k_o, v_o, carry):
        ti = pl.program_id(1)

        @pl.when(ti == 0)
        def _():
            carry[...] = cp_ref[0]

        c = c_ref[...]
        ext = jnp.concatenate([carry[...], c], axis=0)
        carry[...] = c[tt - 8:tt, :]
        cw = cw_ref[...]
        acc = c * cw[GDN_CONV - 1:GDN_CONV, :]
        for j in range(GDN_CONV - 1):
            back = GDN_CONV - 1 - j
            acc = acc + ext[8 - back:8 - back + tt, :] * cw[j:j + 1, :]
        qkv = _silu(acc)
        q_o[...] = _head_l2(qkv[:, :w], GDN_HEADS, GDN_HD ** -0.5).astype(q_o.dtype)
        k_o[...] = _head_l2(qkv[:, w:2 * w], GDN_HEADS).astype(k_o.dtype)
        v_o[...] = qkv[:, 2 * w:].astype(v_o.dtype)

    tok = pl.BlockSpec((tt, w), lambda b, ti: (b * nt + ti, 0))
    return pl.pallas_call(
        body, grid=(nb, nt), name="gdn_prep",
        in_specs=[pl.BlockSpec((tt, GDN_QKV), lambda b, ti: (rb0 + b * nt + ti, 0)),
                  pl.BlockSpec((1, 8, GDN_QKV), lambda b, ti: (b, 0, 0)),
                  pl.BlockSpec((8, GDN_QKV), lambda b, ti: (0, 0))],
        out_specs=[tok] * 3,
        out_shape=[jax.ShapeDtypeStruct((nb * t, w), BF16)] * 3,
        scratch_shapes=[pltpu.VMEM((8, GDN_QKV), F32)],
        compiler_params=_cp("parallel", "arbitrary"))(cols, conv_prev, conv_w)


def _gdn_scan(q, k, v, gb, z, s0, norm_g, *, nb, t, gb_row0, z_row0):
    c = min(SCAN_CHUNK, t)
    assert t % c == 0 and gb_row0 % c == 0 and z_row0 % c == 0
    nc = t // c
    nh, hd = GDN_HEADS, GDN_HD

    def body(q_ref, k_ref, v_ref, gb_ref, z_ref, s0_ref, ng_ref, o_ref, sf_ref, st):
        ci = pl.program_id(1)

        @pl.when(ci == 0)
        def _():
            st[...] = s0_ref[0]

        gbv = gb_ref[...]
        gc = _cumsum_rows(gbv, c)
        gct = gc.T
        ri = lax.broadcasted_iota(jnp.int32, (c, c), 0)
        cj = lax.broadcasted_iota(jnp.int32, (c, c), 1)
        tril, strict = ri >= cj, ri > cj
        hs = lambda a: _heads(a, nh, hd)
        k_h, q_h, v_h = (hs(ref[...].astype(F32)) for ref in (k_ref, q_ref, v_ref))
        gcol = jnp.stack([gc[:, h:h + 1] for h in range(nh)], axis=0)
        grow = gct[:nh][:, None, :]
        bcol = jnp.stack([gbv[:, nh + h:nh + h + 1] for h in range(nh)], axis=0)
        decay = jnp.where(tril, jnp.exp(jnp.where(tril, gcol - grow, 0.0)), 0.0)
        pp = GDN_PASSES
        kb = k_h * bcol
        prods = _bmm(jnp.concatenate([kb, q_h], axis=1), k_h, 2, 2, pp['a'])
        low = jnp.where(strict, prods[:, :c] * decay, 0.0)
        a_qk = jnp.where(tril, prods[:, c:] * decay, 0.0)
        tinv = _unit_lower_inverse(-low, c, pp['inv'])
        e_g = jnp.exp(gcol)
        uw = _bmm(tinv, jnp.concatenate([v_h * bcol, kb * e_g], axis=2), 2, 1, pp['u'])
        s_all = st[...]
        ws_qs = _bmm(jnp.concatenate([uw[:, :, hd:], q_h * e_g], axis=1), s_all, 2, 1, pp['x'])
        v_new = uw[:, :, :hd] - ws_qs[:, :c]
        o = ws_qs[:, c:] + _bmm(a_qk, v_new, 2, 1, pp['y'])
        g_last = gcol[:, c - 1:c, :]
        st[...] = s_all * jnp.exp(g_last) + _bmm(k_h * jnp.exp(g_last - gcol), v_new, 1, 1, pp['s'])
        o = o * lax.rsqrt(jnp.mean(o * o, axis=-1, keepdims=True) + RMS_EPS) * ng_ref[...]
        o_ref[...] = (_unheads(o) * _silu(z_ref[...].astype(F32))).astype(o_ref.dtype)

        @pl.when(ci == nc - 1)
        def _():
            sf_ref[0] = st[...]

    tok = pl.BlockSpec((c, GDN_W), lambda b, ci: (b * nc + ci, 0))
    stt = pl.BlockSpec((1, nh, hd, hd), lambda b, ci: (b, 0, 0, 0))
    return pl.pallas_call(
        body, grid=(nb, nc), name="gdn_scan",
        in_specs=[tok] * 3 + [pl.BlockSpec((c, LANES), lambda b, ci: (gb_row0 // c + b * nc + ci, 0)),
                              pl.BlockSpec((c, GDN_W), lambda b, ci: (z_row0 // c + b * nc + ci, 0)),
                              stt, pl.BlockSpec((1, hd), lambda b, ci: (0, 0))],
        out_specs=[tok, stt],
        out_shape=[jax.ShapeDtypeStruct((nb * t, GDN_W), BF16), jax.ShapeDtypeStruct((nb, nh, hd, hd), F32)],
        scratch_shapes=[pltpu.VMEM((nh, hd, hd), F32)],
        compiler_params=_cp("parallel", "arbitrary"))(q, k, v, gb, z, s0, norm_g)


def _odd_mixer(geo, xb, w_in, fx, gd, cache_k, cache_v, cache_logf, conv_prev, gdn0):
    tm, m, mp = geo.tm, geo.m, geo.mp
    fw = FOX_W
    o_f = 4 * fw
    o_g = o_f + FOX_HEADS
    w_q, w_k, w_v, w_gate = (w_in[:, i * fw:(i + 1) * fw].astype(BF16) for i in range(4))
    w_f = _pad_cols(w_in[:, o_f:o_g], LANES).astype(BF16)
    w_qkv = w_in[:, o_g:o_g + GDN_QKV].astype(BF16)
    w_ab = _pad_cols(w_in[:, o_g + GDN_QKV:o_g + GDN_QKV + 2 * GDN_HEADS], LANES).astype(BF16)
    w_z = w_in[:, o_g + GDN_QKV + 2 * GDN_HEADS:].astype(BF16)
    qn, kn = fx['q_norm'].reshape(1, -1), fx['k_norm'].reshape(1, -1)
    f_bias = jnp.pad(fx['f_bias'], (0, LANES - FOX_HEADS)).reshape(1, -1)
    scale = FOX_HD ** -0.5

    (q,) = _mm([(xb, w_q)], tm=tm, tn=fw, aux=[(qn, 'row0')],
               epilogue=lambda acc, g: (_head_rms(acc, g, FOX_HEADS, scale),), outs=[(fw, BF16)])
    groups = ((0, mp), (mp, geo.ms))
    (fk_p, kb_p), (fk_s, kb_s) = (
        _mm([(xb, w_k)], tm=tm, tn=fw, aux=[(kn, 'row0')], rows=g,
            epilogue=lambda acc, g_: (_head_rms(acc, g_, FOX_HEADS),) * 2, outs=[(fw, F32), (fw, BF16)]) for g in groups)
    (fv_p,), (fv_s,) = (_mm([(xb, w_v)], tm=tm, tn=None, rows=g, epilogue=lambda acc: (acc,), outs=[(None, F32)])
                      for g in groups)
    w_vt = w_v.T
    vt_p = _mm_t(w_vt, xb, nb=geo.bp, t=geo.tp, row0=0)
    vt_s = _mm_t(w_vt, xb, nb=geo.bs, t=geo.ts, row0=mp)
    (gate,) = _mm([(xb, w_gate)], tm=tm, tn=None, epilogue=lambda acc: (_sigmoid(acc),), outs=[(None, BF16)])
    (logf,) = _mm([(xb, w_f)], tm=tm, tn=LANES, aux=[(f_bias, 'col')],
                  epilogue=lambda acc, fb: (-_softplus(-(acc + fb)),), outs=[(LANES, F32)])

    cum_p = _cumsum_time(logf[:mp].reshape(geo.bp, geo.tp, LANES))[:, :, :FOX_HEADS]
    bias_p = (jnp.transpose(cum_p, (0, 2, 1))[:, :, None, :], jnp.transpose(cum_p, (0, 2, 1))[..., None])
    o_p = _flash([q], [(kb_p, 0, True)], vt_p, nb=geo.bp, tq_total=geo.tp, tk_total=geo.tp, q_row0=0, q_off=0,
                 tq=geo.tq_p, tk=geo.tq_p, chunked=False, nh=FOX_HEADS, bias=bias_p, gate=gate)
    padk = geo.kpad - geo.past - geo.ts
    lf_all = jnp.concatenate([jnp.pad(cache_logf, ((0, 0), (0, 0), (0, LANES - FOX_HEADS))),
                              logf[mp:].reshape(geo.bs, geo.ts, LANES), jnp.zeros((geo.bs, padk, LANES), F32)], axis=1)
    cum_s = jnp.transpose(_cumsum_time(lf_all)[:, :, :FOX_HEADS], (0, 2, 1))
    bias_s = (cum_s[:, :, None, geo.past:geo.past + geo.ts], cum_s[..., None])
    k_all = jnp.concatenate([cache_k.reshape(geo.bs, geo.past, fw).astype(BF16), kb_s.reshape(geo.bs, geo.ts, fw),
                             jnp.zeros((geo.bs, padk, fw), BF16)], axis=1).reshape(geo.bs * geo.kpad, fw)
    vt_all = jnp.concatenate([jnp.transpose(cache_v.reshape(geo.bs, geo.past, fw), (0, 2, 1)).astype(BF16),
                              vt_s.reshape(geo.bs, fw, geo.ts), jnp.zeros((geo.bs, fw, padk), BF16)],
                             axis=2).reshape(geo.bs * fw, geo.kpad)
    o_s = _flash([q], [(k_all, 0, True)], vt_all, nb=geo.bs, tq_total=geo.ts, tk_total=geo.kpad, q_row0=mp,
                 q_off=geo.past, tq=geo.ts, tk=geo.tk_s, chunked=False, nh=FOX_HEADS, bias=bias_s, gate=gate)
    o_fox = jnp.concatenate([o_p, o_s], axis=0)

    (qkv_raw,) = _mm([(xb, w_qkv)], tm=tm, tn=None, epilogue=lambda acc: (acc,), outs=[(None, F32)])
    lane = jnp.arange(LANES)
    neg_a = jnp.where(lane < GDN_HEADS, -jnp.exp(jnp.pad(gd['a_log'], (0, LANES - GDN_HEADS))), 0.0).reshape(1, -1)
    dtb = jnp.pad(gd['dt_bias'], (0, LANES - GDN_HEADS)).reshape(1, -1)
    is_g = (lane < GDN_HEADS).astype(F32).reshape(1, -1)
    (gb,) = _mm([(xb, w_ab)], tm=tm, tn=LANES, aux=[(neg_a, 'col'), (dtb, 'col'), (is_g, 'col')],
                epilogue=lambda acc, na, db, ig: (jnp.where(ig > 0.5, na * _softplus(acc + db), _sigmoid(acc)),),
                outs=[(LANES, F32)])
    (z,) = _mm([(xb, w_z)], tm=tm, tn=None, epilogue=lambda acc: (acc,), outs=[(None, BF16)])
    conv_w = jnp.pad(gd['conv_w'], ((0, 8 - GDN_CONV), (0, 0)))
    norm_g = gd['norm'].reshape(1, -1)
    prev8 = lambda a: jnp.pad(a, ((0, 0), (8 - (GDN_CONV - 1), 0), (0, 0)))
    q_p, k_p, v_p = _gdn_prep(qkv_raw, jnp.zeros((geo.bp, 8, GDN_QKV), F32), conv_w, nb=geo.bp, t=geo.tp, row0=0)
    q_s, k_s, v_s = _gdn_prep(qkv_raw, prev8(conv_prev), conv_w, nb=geo.bs, t=geo.ts, row0=mp)
    y_p, s_p = _gdn_scan(q_p, k_p, v_p, gb, z, jnp.zeros((geo.bp, GDN_HEADS, GDN_HD, GDN_HD), F32), norm_g,
                         nb=geo.bp, t=geo.tp, gb_row0=0, z_row0=0)
    y_s, s_s = _gdn_scan(q_s, k_s, v_s, gb, z, gdn0, norm_g, nb=geo.bs, t=geo.ts, gb_row0=mp, z_row0=mp)
    o_gdn = jnp.concatenate([y_p, y_s], axis=0)
    nconv = GDN_CONV - 1
    last_rows = lambda row0, nb, t: (row0 + jnp.arange(nb)[:, None] * t + (t - nconv + jnp.arange(nconv))[None, :]).reshape(-1)
    states = dict(
        fk_p=fk_p.reshape(geo.bp, geo.tp, FOX_HEADS, FOX_HD), fk_s=fk_s.reshape(geo.bs, geo.ts, FOX_HEADS, FOX_HD),
        fv_p=fv_p.reshape(geo.bp, geo.tp, FOX_HEADS, FOX_HD), fv_s=fv_s.reshape(geo.bs, geo.ts, FOX_HEADS, FOX_HD),
        fl_p=logf[:mp, :FOX_HEADS].reshape(geo.bp, geo.tp, FOX_HEADS),
        fl_s=logf[mp:, :FOX_HEADS].reshape(geo.bs, geo.ts, FOX_HEADS),
        cv_p=jnp.take(qkv_raw, last_rows(0, geo.bp, geo.tp), axis=0).reshape(geo.bp, nconv, GDN_QKV),
        cv_s=jnp.take(qkv_raw, last_rows(mp, geo.bs, geo.ts), axis=0).reshape(geo.bs, nconv, GDN_QKV),
        gdn_p=s_p, gdn_s=s_s)
    return o_fox, o_gdn, states


MOE_BLOCK = 512
MOE_COMBINE_ROWS = 256


def _moe_route(x, router, router_bias, *, tm):
    m = x.shape[0]
    assert m % tm == 0
    per_group = N_EXPERTS // N_GROUPS

    def lane_max(v):
        return jnp.max(v, axis=-1, keepdims=True)

    def first_lane(mask, lane):
        return jnp.min(jnp.where(mask, lane.astype(F32), float(LANES)), axis=-1, keepdims=True).astype(jnp.int32)

    def group_all(v, lane, op):
        sh = 1
        while sh < per_group:
            partner = jnp.where((lane & sh) == 0, pltpu.roll(v, LANES - sh, 1), pltpu.roll(v, sh, 1))
            v = op(v, partner)
            sh *= 2
        return v

    def body(x_ref, r_ref, b_ref, e_o, g_o, k_o, cnt_o, carry):
        i = pl.program_id(0)

        @pl.when(i == 0)
        def _():
            carry[...] = jnp.zeros(carry.shape, F32)

        lane = lax.broadcasted_iota(jnp.int32, (tm, LANES), 1)
        valid = lane < N_EXPERTS
        neg = -jnp.inf
        scores = _sigmoid(_dot(x_ref[...], r_ref[...]))
        biased = jnp.where(valid, scores + b_ref[...], neg)
        m1 = group_all(biased, lane, jnp.maximum)
        first = group_all(jnp.where(biased == m1, lane, LANES), lane, jnp.minimum)
        m2 = group_all(jnp.where(lane == first, neg, biased), lane, jnp.maximum)
        grp = jnp.where(valid & ((lane & (per_group - 1)) == 0), m1 + m2, neg)
        emask = jnp.zeros((tm, LANES), jnp.bool_)
        for _ in range(TOPK_GROUPS):
            idx = first_lane(grp == lane_max(grp), lane)
            emask = emask | ((lane - idx >= 0) & (lane - idx < per_group))
            grp = jnp.where(lane == idx, neg, grp)
        cur = jnp.where(emask & valid, biased, neg)
        picks = []
        sel = jnp.zeros((tm, LANES), jnp.bool_)
        for _ in range(TOP_K):
            idx = first_lane(cur == lane_max(cur), lane)
            pick = lane == idx
            picks.append((idx, pick))
            sel = sel | pick
            cur = jnp.where(pick, neg, cur)
        selw = jnp.where(sel, scores, 0.0)
        gates_dense = selw / jnp.sum(selw, axis=-1, keepdims=True) * ROUTED_SCALE
        ri = lax.broadcasted_iota(jnp.int32, (tm, tm), 0)
        cj = lax.broadcasted_iota(jnp.int32, (tm, tm), 1)
        p01 = jnp.where(sel, 1.0, 0.0)
        before = _dot(jnp.where(ri > cj, 1.0, 0.0).astype(BF16), p01.astype(BF16)) + carry[...]
        carry[...] = carry[...] + jnp.sum(p01, axis=0, keepdims=True)
        cnt_o[...] = carry[...]
        e_out = jnp.zeros((tm, LANES), jnp.int32)
        g_out = jnp.zeros((tm, LANES), F32)
        k_out = jnp.zeros((tm, LANES), jnp.int32)
        for j, (idx, pick) in enumerate(picks):
            e_out = jnp.where(lane == j, idx, e_out)
            g_out = jnp.where(lane == j, jnp.sum(jnp.where(pick, gates_dense, 0.0), axis=-1, keepdims=True), g_out)
            rank = jnp.sum(jnp.where(pick, before, 0.0), axis=-1, keepdims=True)
            k_out = jnp.where(lane == j, rank.astype(jnp.int32), k_out)
        e_o[...] = e_out
        g_o[...] = g_out
        k_o[...] = k_out

    tok = pl.BlockSpec((tm, LANES), lambda i: (i, 0))
    one = pl.BlockSpec((1, LANES), lambda i: (0, 0))
    return pl.pallas_call(
        body, grid=(m // tm,), name="moe_route",
        in_specs=[pl.BlockSpec((tm, x.shape[1]), lambda i: (i, 0)), pl.BlockSpec(router.shape, lambda i: (0, 0)), one],
        out_specs=[tok, tok, tok, one],
        out_shape=[jax.ShapeDtypeStruct((m, LANES), jnp.int32), jax.ShapeDtypeStruct((m, LANES), F32),
                   jax.ShapeDtypeStruct((m, LANES), jnp.int32), jax.ShapeDtypeStruct((1, LANES), F32)],
        scratch_shapes=[pltpu.VMEM((1, LANES), F32)],
        compiler_params=_cp("arbitrary"))(x, router, router_bias)


def _moe_dispatch(x_words, dest, cap, *, tb):
    m, wd = x_words.shape
    nt = m // tb

    def body(dest_ref, x_ref, out_ref, sem):
        def row_copy(i, j):
            return pltpu.make_async_copy(x_ref.at[pl.ds(i, 1)], out_ref.at[pl.ds(dest_ref[0, 0, i * TOP_K + j], 1)], sem)

        def issue(i, carry):
            for j in range(TOP_K):
                row_copy(i, j).start()
            return carry

        lax.fori_loop(0, tb, issue, 0, unroll=8)
        for _ in range(TOP_K):
            pltpu.make_async_copy(x_ref, out_ref.at[pl.ds(0, tb)], sem).wait()

    return pl.pallas_call(
        body, grid=(nt,), name="moe_dispatch",
        in_specs=[pl.BlockSpec((1, 1, tb * TOP_K), lambda i: (i, 0, 0), memory_space=pltpu.SMEM),
                  pl.BlockSpec((tb, wd), lambda i: (i, 0))],
        out_specs=pl.BlockSpec(memory_space=pl.ANY),
        out_shape=jax.ShapeDtypeStruct((cap, wd), x_words.dtype),
        scratch_shapes=[pltpu.SemaphoreType.DMA(())],
        compiler_params=_cp("arbitrary"))(dest, x_words)


def _pack_halves(y):
    n = y.shape[1] // 2
    bits = lambda a: lax.bitcast_convert_type(a.astype(BF16).astype(F32), jnp.uint32)
    return (bits(y[:, :n]) >> 16) | (bits(y[:, n:]) & jnp.uint32(0xFFFF0000))


def _unpack_halves(w):
    lo = lax.bitcast_convert_type(w << 16, F32).astype(BF16)
    hi = lax.bitcast_convert_type(w & jnp.uint32(0xFFFF0000), F32).astype(BF16)
    return lo, hi


def _ep_ln_packed(acc, x, g, b):
    y = _ep_ln(acc, x, g, b)
    return y, _pack_halves(y)


def _moe_experts(xs_words, blk_exp, blk_next, blk_rows, n_used, w_gate, w_up, w_down, layer):
    cap, half = xs_words.shape
    d = 2 * half
    nblk = cap // MOE_BLOCK
    ed = w_gate.shape[3]

    def body(be_ref, bn_ref, br_ref, nu_ref, x_ref, wg_hbm, wu_hbm, wd_hbm, o_ref, wg_f, wu_f, wd_f, wg_b, wu_b, wd_b,
             slot_ref, sem):
        i = pl.program_id(0)

        def fetch(e, slot):
            return [pltpu.make_async_copy(src.at[layer, e], dst.at[slot], sem.at[slot])
                    for src, dst in ((wg_hbm, wg_f), (wu_hbm, wu_f), (wd_hbm, wd_f))]

        @pl.when(i < nu_ref[0])
        def _():
            first = i == 0
            e = be_ref[i]

            @pl.when(first)
            def _():
                slot_ref[0] = 0
                for cp in fetch(e, 0):
                    cp.start()

            @pl.when(first | (e != be_ref[jnp.maximum(i - 1, 0)]))
            def _():
                slot = jnp.where(first, 0, 1 - slot_ref[0])
                slot_ref[0] = slot
                for cp in fetch(e, slot):
                    cp.wait()
                wg_b[...] = wg_f[slot].astype(BF16)
                wu_b[...] = wu_f[slot].astype(BF16)
                wd_b[...] = wd_f[slot].astype(BF16)

                @pl.when(bn_ref[i] != e)
                def _():
                    for cp in fetch(bn_ref[i], 1 - slot):
                        cp.start()

            row = lax.broadcasted_iota(jnp.int32, (MOE_BLOCK, 1), 0)
            lo, hi = _unpack_halves(jnp.where(row < br_ref[i], x_ref[...], jnp.uint32(0)))
            hg = _dot(lo, wg_b[:half]) + _dot(hi, wg_b[half:])
            hu = _dot(lo, wu_b[:half]) + _dot(hi, wu_b[half:])
            o_ref[...] = _pack_halves(_dot((_silu(hg) * hu).astype(BF16), wd_b[...]))

    blk = lambda i, be, bn, br, nu: (jnp.minimum(i, nu[0] - 1), 0)
    hbm = pl.BlockSpec(memory_space=pl.ANY)
    grid_spec = pltpu.PrefetchScalarGridSpec(
        num_scalar_prefetch=4, grid=(nblk,),
        in_specs=[pl.BlockSpec((MOE_BLOCK, half), blk), hbm, hbm, hbm],
        out_specs=pl.BlockSpec((MOE_BLOCK, half), blk),
        scratch_shapes=[pltpu.VMEM((2, d, ed), F32), pltpu.VMEM((2, d, ed), F32), pltpu.VMEM((2, ed, d), F32),
                        pltpu.VMEM((d, ed), BF16), pltpu.VMEM((d, ed), BF16), pltpu.VMEM((ed, d), BF16),
                        pltpu.SMEM((1,), jnp.int32), pltpu.SemaphoreType.DMA((2,))])
    return pl.pallas_call(body, grid_spec=grid_spec, out_shape=jax.ShapeDtypeStruct((cap, half), jnp.uint32), name="moe_experts",
                          compiler_params=_cp("arbitrary"))(blk_exp, blk_next, blk_rows, n_used, xs_words, w_gate, w_up, w_down)


def _moe_combine(ys_words, dest, gates, *, tc):
    cap, half = ys_words.shape
    d = 2 * half
    m = gates.shape[0]
    nt = m // tc

    def body(dest0_ref, destn_ref, g_ref, y_ref, o_ref, buf, sem):
        i = pl.program_id(0)

        def issue(dref, slot):
            def one(r, carry):
                for j in range(TOP_K):
                    pltpu.make_async_copy(y_ref.at[pl.ds(dref[0, 0, r * TOP_K + j], 1)],
                                          buf.at[slot, j, pl.ds(r, 1)], sem.at[slot]).start()
                return carry
            lax.fori_loop(0, tc, one, 0, unroll=8)

        @pl.when(i == 0)
        def _():
            issue(dest0_ref, 0)

        @pl.when(i + 1 < nt)
        def _():
            issue(destn_ref, (i + 1) % 2)

        slot = i % 2
        for j in range(TOP_K):
            pltpu.make_async_copy(y_ref.at[pl.ds(0, tc)], buf.at[slot, j], sem.at[slot]).wait()
        g = g_ref[...]
        acc_lo = acc_hi = None
        for j in range(TOP_K):
            lo, hi = _unpack_halves(buf[slot, j])
            gj = g[:, j:j + 1]
            acc_lo = lo.astype(F32) * gj if acc_lo is None else acc_lo + lo.astype(F32) * gj
            acc_hi = hi.astype(F32) * gj if acc_hi is None else acc_hi + hi.astype(F32) * gj
        o_ref[:, :half] = acc_lo
        o_ref[:, half:] = acc_hi

    return pl.pallas_call(
        body, grid=(nt,), name="moe_combine",
        in_specs=[pl.BlockSpec((1, 1, tc * TOP_K), lambda i: (0, 0, 0), memory_space=pltpu.SMEM),
                  pl.BlockSpec((1, 1, tc * TOP_K), lambda i: (jnp.minimum(i + 1, nt - 1), 0, 0), memory_space=pltpu.SMEM),
                  pl.BlockSpec((tc, LANES), lambda i: (i, 0)),
                  pl.BlockSpec(memory_space=pl.ANY)],
        out_specs=pl.BlockSpec((tc, d), lambda i: (i, 0)),
        out_shape=jax.ShapeDtypeStruct((m, d), F32),
        scratch_shapes=[pltpu.VMEM((2, TOP_K, tc, half), jnp.uint32), pltpu.SemaphoreType.DMA((2,))],
        compiler_params=_cp("arbitrary"))(dest, dest, gates, ys_words)


def _moe_layer(geo, x_res, xb, x_words, layer, router, router_bias, w_gate, w_up, w_down, ws_gate, ws_up, ws_down,
               ln_g, ln_b, out_dtype, out_rows):
    m, d = x_res.shape
    tm = geo.tm
    eidx, gates, rank, counts = _moe_route(xb, _pad_cols(router, LANES).astype(BF16),
                                           jnp.pad(router_bias, (0, LANES - N_EXPERTS)).reshape(1, -1), tm=tm)
    cnt = counts[0, :N_EXPERTS].astype(jnp.int32)
    padded = (cnt + MOE_BLOCK - 1) // MOE_BLOCK * MOE_BLOCK
    pad_end = jnp.cumsum(padded)
    pad_start = pad_end - padded
    n_blocks = -(-(m * TOP_K) // MOE_BLOCK) + N_EXPERTS
    cap = n_blocks * MOE_BLOCK
    blk_row0 = jnp.arange(n_blocks, dtype=jnp.int32) * MOE_BLOCK
    blk_exp = jnp.minimum(jnp.sum((pad_end[None, :] <= blk_row0[:, None]).astype(jnp.int32), axis=1), N_EXPERTS - 1)
    blk_rows = jnp.clip(cnt[blk_exp] - (blk_row0 - pad_start[blk_exp]), 0, MOE_BLOCK).astype(jnp.int32)
    n_used = (pad_end[-1:] // MOE_BLOCK).astype(jnp.int32)
    eid = jnp.arange(N_EXPERTS, dtype=jnp.int32)
    later = jnp.where((eid[None, :] > eid[:, None]) & (cnt[None, :] > 0), eid[None, :], N_EXPERTS)
    next_e = jnp.min(later, axis=1)
    next_e = jnp.where(next_e == N_EXPERTS, eid, next_e)
    blk_next = next_e[blk_exp].astype(jnp.int32)
    dest = jnp.sum(jnp.where(eidx[:, :TOP_K, None] == eid, pad_start, 0), axis=-1) + rank[:, :TOP_K]

    tb = tm
    xs_words = _moe_dispatch(x_words, dest.reshape(m // tb, 1, tb * TOP_K), cap, tb=tb)
    ys = _moe_experts(xs_words, blk_exp, blk_next, blk_rows, n_used, w_gate, w_up, w_down, layer)
    tc = min(MOE_COMBINE_ROWS, tm)
    routed = _moe_combine(ys, dest.reshape(m // tc, 1, tc * TOP_K), gates, tc=tc)

    sd = ws_gate.shape[1]
    w_sh = jnp.concatenate([ws_gate, ws_up], axis=1).astype(BF16)
    (hs,) = _mm([(xb, w_sh)], tm=tm, tn=2 * sd, epilogue=lambda acc: (_silu(acc[:, :sd]) * acc[:, sd:],), outs=[(sd, BF16)])
    ys = [_mm([(hs, ws_down.astype(BF16))], tm=tm, tn=d, rows=g,
              aux=[(x_res, 'rowcol'), (routed, 'rowcol'), (ln_g.reshape(1, -1), 'col'), (ln_b.reshape(1, -1), 'col')],
              epilogue=lambda acc, xr, rt, g_, b: (_ep_ln(acc + rt, xr, g_, b),), outs=[(d, out_dtype)], name="moe_out")[0]
          for g in out_rows]
    return ys


def _rwkv_params(li, mu, w0, w2, a0, a2, g2, k_k, k_a, r_k, ln_g, ln_b):
    row = lambda a: a[li].reshape(1, -1)
    padr = lambda a, n: jnp.pad(a, ((0, n - a.shape[0]), (0, 0)))
    return dict(mu=_rwkv_cols_layout(mu[li])[None, :], w0=row(w0), a0=row(a0), k_k=row(k_k), k_a=row(k_a),
                w2=padr(w2[li], LANES), a2=padr(a2[li], LANES), g2=g2[li], r_k=row(r_k), ln_g=row(ln_g), ln_b=row(ln_b))


def _mix_out(geo, x_res, o_a, o_b, w_out, ln_g, ln_b):
    d = x_res.shape[1]
    ka = o_a.shape[1]
    return _mm([(o_a, w_out[:ka].astype(BF16)), (o_b, w_out[ka:].astype(BF16))], tm=geo.tm, tn=d,
               aux=[(x_res, 'rowcol'), (ln_g.reshape(1, -1), 'col'), (ln_b.reshape(1, -1), 'col')],
               epilogue=_ep_ln_packed, outs=[(d, BF16), (d // 2, jnp.uint32)], name="mix_out")


def kernel(x_prompt, x_sample, cache_mla_ckv, cache_mla_krope, state_rwkv_shift, state_rwkv_wkv, cache_fox_k, cache_fox_v, cache_fox_logf, state_gdn_conv, state_gdn_wkv, ln1_g, ln1_b, ln2_g, ln2_b, ev_w_in, ev_w_out, mla_kv_norm, mla_w_uk, mla_w_uv, rwkv_mu, rwkv_w0, rwkv_w2, rwkv_a0, rwkv_a2, rwkv_g2, rwkv_k_k, rwkv_k_a, rwkv_r_k, rwkv_ln_g, rwkv_ln_b, od_w_in, od_w_out, fox_q_norm, fox_k_norm, fox_f_bias, gdn_conv_w, gdn_a_log, gdn_dt_bias, gdn_norm, moe_router, moe_router_bias, moe_w_gate, moe_w_up, moe_w_down, moe_ws_gate, moe_ws_up, moe_ws_down):
    bp, tp, d = x_prompt.shape
    bs, ts, _ = x_sample.shape
    geo = _Geom(bp, tp, bs, ts, cache_mla_ckv.shape[2])
    x = xb = jnp.concatenate([x_prompt.reshape(bp * tp, d).astype(BF16), x_sample.reshape(bs * ts, d).astype(BF16)], axis=0)
    rope_tab = _rope_table(geo)
    st = {}
    n_layers = ln1_g.shape[0]
    for layer in range(n_layers):
        li = layer // 2
        if layer % 2 == 0:
            rw = _rwkv_params(li, rwkv_mu, rwkv_w0, rwkv_w2, rwkv_a0, rwkv_a2, rwkv_g2, rwkv_k_k, rwkv_k_a, rwkv_r_k,
                              rwkv_ln_g, rwkv_ln_b)
            o_a, o_b, new = _even_mixer(geo, xb, rope_tab, ev_w_in[li], mla_kv_norm[li], mla_w_uk[li], mla_w_uv[li], rw,
                                        cache_mla_ckv[li], cache_mla_krope[li], state_rwkv_shift[li], state_rwkv_wkv[li])
            w_out = ev_w_out[li]
        else:
            fx = {'q_norm': fox_q_norm[li], 'k_norm': fox_k_norm[li], 'f_bias': fox_f_bias[li]}
            gd = {'conv_w': gdn_conv_w[li], 'a_log': gdn_a_log[li], 'dt_bias': gdn_dt_bias[li], 'norm': gdn_norm[li]}
            o_a, o_b, new = _odd_mixer(geo, xb, od_w_in[li], fx, gd, cache_fox_k[li], cache_fox_v[li], cache_fox_logf[li],
                                       state_gdn_conv[li], state_gdn_wkv[li])
            w_out = od_w_out[li]
        for name, val in new.items():
            st.setdefault(name, []).append(val)
        xb, x_words = _mix_out(geo, x, o_a, o_b, w_out, ln1_g[layer], ln1_b[layer])
        last = layer == n_layers - 1
        ys = _moe_layer(geo, xb, xb, x_words, layer, moe_router[layer], moe_router_bias[layer], moe_w_gate, moe_w_up,
                        moe_w_down, moe_ws_gate[layer], moe_ws_up[layer], moe_ws_down[layer], ln2_g[layer], ln2_b[layer],
                        F32 if last else BF16, ((0, geo.mp), (geo.mp, geo.ms)) if last else ((0, geo.m),))
        x = xb = ys[0]
    names = ('ckv', 'kr', 'sh', 'rwkv', 'fk', 'fv', 'fl', 'cv', 'gdn')
    return ((ys[0].reshape(bp, tp, d), ys[1].reshape(bs, ts, d))
            + tuple(jnp.stack(st[n + '_p']) for n in names) + tuple(jnp.stack(st[n + '_s']) for n in names))
```

```python
import functools
import math

import jax
import jax.numpy as jnp
from jax import lax
from jax.experimental import pallas as pl
from jax.experimental.pallas import tpu as pltpu

F32 = jnp.float32
BF16 = jnp.bfloat16
HP = lax.Precision.HIGHEST

D_MODEL = 2048
DEPTH = 2
DN_ALPHA = float((2.0 * DEPTH) ** 0.25)
LN_EPS = 1e-5
RMS_EPS = 1e-6
CHUNK = 64
LANES = 128

MLA_HEADS, MLA_NOPE, MLA_ROPE, MLA_VDIM, MLA_KV_LORA = 8, 128, 64, 128, 512
MLA_QK = MLA_NOPE + MLA_ROPE
ROPE_THETA = 10000.0
RWKV_HEADS, RWKV_HD = 16, 64
RWKV_W = RWKV_HEADS * RWKV_HD
RWKV_DECAY_LORA, RWKV_A_LORA, RWKV_G_LORA = 96, 96, 256
RWKV_GN_EPS = 64e-5
RWKV_PCOLS = 3 * RWKV_W + 2 * LANES + RWKV_G_LORA
FOX_HEADS, FOX_HD = 8, 128
FOX_W = FOX_HEADS * FOX_HD
GDN_HEADS, GDN_HD, GDN_CONV = 8, 128, 4
GDN_W = GDN_HEADS * GDN_HD
GDN_QKV = 3 * GDN_W
N_EXPERTS, TOP_K, N_GROUPS, TOPK_GROUPS = 64, 6, 8, 4
EXPERT_DIM, SHARED_DIM = 512, 512
ROUTED_SCALE = 2.5
SCAN_CHUNK = 64
GDN_STREAMS = 4
FLASH_KEY_SUBTILE = 1024
NEG_BIG = -1e30
RWKV_PASSES = dict(a=1, x=1, inv=1, u=1, y=1, s=1)
GDN_PASSES = dict(a=1, x=1, inv=1, u=1, y=1, s=1)

VMEM_LIMIT_BYTES = 56 * 1024 * 1024
MM_MAX_COLS = 1792


def _cp(*sem):
    return pltpu.CompilerParams(dimension_semantics=("arbitrary",) * len(sem), vmem_limit_bytes=VMEM_LIMIT_BYTES)


def _dot(a, b, prec=None):
    return jnp.dot(a, b, precision=prec, preferred_element_type=F32)


def _dot_nt(a, b, prec=None):
    return lax.dot_general(a, b, (((1,), (1,)), ((), ())), precision=prec, preferred_element_type=F32)


def _dot_tn(a, b, prec=None):
    return lax.dot_general(a, b, (((0,), (0,)), ((), ())), precision=prec, preferred_element_type=F32)


def _sigmoid(x):
    return 1.0 / (1.0 + jnp.exp(-x))


def _softplus(x):
    return jnp.maximum(x, 0.0) + jnp.log(1.0 + jnp.exp(-jnp.abs(x)))


def _silu(x):
    return x * _sigmoid(x)


def _mm(pairs, *, tm, tn, epilogue, outs, aux=(), rows=None, name="mm"):
    row0, m = rows if rows is not None else (0, pairs[0][0].shape[0])
    n = pairs[0][1].shape[1]
    if tn is None:
        tn = n if n <= MM_MAX_COLS else max(c for c in range(LANES, MM_MAX_COLS + 1, LANES) if n % c == 0)
    outs = [(tn if ow is None else ow, dt) for ow, dt in outs]
    assert m % tm == 0 and row0 % tm == 0 and n % tn == 0, (m, row0, n, tm, tn)
    nj, ni = n // tn, m // tm
    rb0 = row0 // tm
    in_specs, args = [], []
    for a, w in pairs:
        k = a.shape[1]
        in_specs += [pl.BlockSpec((tm, k), lambda j, i: (rb0 + i, 0)), pl.BlockSpec((k, tn), lambda j, i: (0, j))]
        args += [a, w]
    for arr, kind in aux:
        if kind == 'row':
            in_specs.append(pl.BlockSpec((tm, arr.shape[1]), lambda j, i: (rb0 + i, 0)))
        elif kind == 'rowcol':
            in_specs.append(pl.BlockSpec((tm, tn), lambda j, i: (rb0 + i, j)))
        elif kind == 'row0':
            in_specs.append(pl.BlockSpec(arr.shape, lambda j, i: (0, 0)))
        else:
            in_specs.append(pl.BlockSpec((1, tn), lambda j, i: (0, j)))
        args.append(arr)
    out_shape = [jax.ShapeDtypeStruct((m, ow * nj), dt) for ow, dt in outs]
    out_specs = [pl.BlockSpec((tm, ow), lambda j, i: (i, j)) for ow, dt in outs]
    n_pairs, n_aux = len(pairs), len(aux)

    def body(*refs):
        acc = None
        for p in range(n_pairs):
            d = _dot(refs[2 * p][...], refs[2 * p + 1][...])
            acc = d if acc is None else acc + d
        res = epilogue(acc, *[r[...] for r in refs[2 * n_pairs:2 * n_pairs + n_aux]])
        for o_ref, val in zip(refs[2 * n_pairs + n_aux:], res):
            o_ref[...] = val.astype(o_ref.dtype)

    res = pl.pallas_call(body, grid=(nj, ni), in_specs=in_specs, out_specs=out_specs, out_shape=out_shape, name=name,
                         compiler_params=_cp("parallel", "parallel"))(*args)
    return res


def _mm_t(w_t, a, *, nb, t, row0, name="mm_t"):
    n, k = w_t.shape
    tm = next((c for c in (512, 384, 256, 128) if t % c == 0), t)
    assert t % tm == 0 and row0 % tm == 0
    nt = t // tm

    def body(w_ref, a_ref, o_ref):
        o_ref[...] = _dot_nt(w_ref[...], a_ref[...]).astype(o_ref.dtype)

    return pl.pallas_call(
        body, grid=(nb, nt), name=name,
        in_specs=[pl.BlockSpec((n, k), lambda b, i: (0, 0)), pl.BlockSpec((tm, k), lambda b, i: (row0 // tm + b * nt + i, 0))],
        out_specs=pl.BlockSpec((n, tm), lambda b, i: (b, i)),
        out_shape=jax.ShapeDtypeStruct((nb * n, t), BF16),
        compiler_params=_cp("parallel", "parallel"))(w_t, a)


def _rope_lanes(x, tab):
    c, s1, s2 = tab[:, :LANES], tab[:, LANES:2 * LANES], tab[:, 2 * LANES:]
    n = x.shape[1]
    lo = pltpu.roll(x, n - MLA_ROPE // 2, 1)
    hi = pltpu.roll(x, MLA_ROPE // 2, 1)
    if n > LANES:
        reps = n // LANES
        c, s1, s2 = (jnp.concatenate([t] * reps, axis=1) for t in (c, s1, s2))
    return x * c + lo * s1 + hi * s2


def _ep_rms(acc, g):
    y = acc * lax.rsqrt(jnp.mean(acc * acc, axis=-1, keepdims=True) + RMS_EPS) * g
    return y, y


def _ep_ln(acc, x, g, b):
    z = DN_ALPHA * x.astype(F32) + acc
    mu = jnp.mean(z, axis=-1, keepdims=True)
    zc = z - mu
    var = jnp.mean(zc * zc, axis=-1, keepdims=True)
    return zc * lax.rsqrt(var + LN_EPS) * g + b


def _flash(qs, ks, vt, *, nb, tq_total, tk_total, q_row0, q_off, tq, tk, chunked, nh, bias=None, gate=None,
           out_dtype=BF16, name="flash"):
    assert tq_total % tq == 0 and tk_total % tk == 0 and q_row0 % tq == 0
    nq, nk = tq_total // tq, tk_total // tk
    qrb0 = q_row0 // tq
    ts = tk if tk % FLASH_KEY_SUBTILE else min(tk, FLASH_KEY_SUBTILE)
    for qi_s in range(nq):
        last_s = min(nk - 1, (q_off + qi_s * tq + tq - 1) // tk)
        assert last_s * tk <= q_off + qi_s * tq + 1, "key tiles before the last needed one must be fully visible"
    n_q, n_k = len(qs), len(ks)
    has_bias, has_gate = bias is not None, gate is not None

    def last_k(qi):
        return jnp.minimum(nk - 1, (q_off + qi * tq + tq - 1) // tk)

    kblk = lambda qi, ki: jnp.minimum(ki, last_k(qi))
    q_map = lambda b, h, qi, ki: (qrb0 + b * nq + qi, h)
    in_specs, args = [], []
    for q in qs:
        in_specs.append(pl.BlockSpec((tq, LANES), q_map))
        args.append(q)
    for arr, c0, per_head in ks:
        if per_head:
            in_specs.append(pl.BlockSpec((tk, LANES), lambda b, h, qi, ki, c0=c0: (b * nk + kblk(qi, ki), c0 + h)))
        else:
            in_specs.append(pl.BlockSpec((tk, LANES), lambda b, h, qi, ki, c0=c0: (b * nk + kblk(qi, ki), c0)))
        args.append(arr)
    in_specs.append(pl.BlockSpec((LANES, tk), lambda b, h, qi, ki: (b * nh + h, kblk(qi, ki))))
    args.append(vt)
    if has_bias:
        in_specs.append(pl.BlockSpec((1, 1, 1, tq), lambda b, h, qi, ki: (b, h, 0, qi)))
        in_specs.append(pl.BlockSpec((1, 1, tk, 1), lambda b, h, qi, ki: (b, h, kblk(qi, ki), 0)))
        args += list(bias)
    if has_gate:
        in_specs.append(pl.BlockSpec((tq, LANES), q_map))
        args.append(gate)

    def body(*refs):
        q_refs = refs[:n_q]
        k_refs = refs[n_q:n_q + n_k]
        vt_ref = refs[n_q + n_k]
        pos = n_q + n_k + 1
        if has_bias:
            qb_ref, kb_ref = refs[pos], refs[pos + 1]
            pos += 2
        if has_gate:
            gate_ref = refs[pos]
            pos += 1
        o_ref, m_ref, l_ref, acc_ref = refs[pos:pos + 4]
        qi, ki = pl.program_id(2), pl.program_id(3)

        @pl.when(ki == 0)
        def _():
            m_ref[...] = jnp.full(m_ref.shape, NEG_BIG, F32)
            l_ref[...] = jnp.zeros(l_ref.shape, F32)
            acc_ref[...] = jnp.zeros(acc_ref.shape, F32)

        def step(masked):
            q_all = jnp.concatenate([q_ref[...] for q_ref in q_refs], axis=1) if n_q > 1 else q_refs[0][...]
            qpos = q_off + qi * tq + lax.broadcasted_iota(jnp.int32, (1, tq), 1)
            qlim = (qpos | (CHUNK - 1)) if chunked else qpos
            subs = [slice(j * ts, (j + 1) * ts) for j in range(tk // ts)]
            ss = []
            for j, rows in enumerate(subs):
                k_all = jnp.concatenate([k_ref[rows, :] for k_ref in k_refs], axis=1) if n_k > 1 else k_refs[0][rows, :]
                s = _dot_nt(k_all, q_all)
                if has_bias:
                    s = s + (qb_ref[0, 0] - kb_ref[0, 0, rows, :])
                if masked:
                    kpos = ki * tk + j * ts + lax.broadcasted_iota(jnp.int32, (ts, 1), 0)
                    s = jnp.where(kpos <= qlim, s, NEG_BIG)
                ss.append(s)
            m_old = m_ref[...]
            m_new = m_old
            for s in ss:
                m_new = jnp.maximum(m_new, jnp.max(s, axis=0, keepdims=True))
            alpha = jnp.exp(m_old - m_new)
            l_new = alpha * l_ref[...]
            acc_new = alpha * acc_ref[...]
            for rows, s in zip(subs, ss):
                p = jnp.exp(s - m_new)
                l_new = l_new + jnp.sum(p, axis=0, keepdims=True)
                acc_new = acc_new + _dot(vt_ref[:, rows], p.astype(BF16))
            m_ref[...] = m_new
            l_ref[...] = l_new
            acc_ref[...] = acc_new

        @pl.when(ki < last_k(qi))
        def _():
            step(False)

        @pl.when(ki == last_k(qi))
        def _():
            step(True)

        @pl.when(ki == nk - 1)
        def _():
            o = (acc_ref[...] / l_ref[...]).T
            if has_gate:
                o = o * gate_ref[...].astype(F32)
            o_ref[...] = o.astype(o_ref.dtype)

    return pl.pallas_call(
        body, grid=(nb, nh, nq, nk), in_specs=in_specs, name=name,
        out_specs=pl.BlockSpec((tq, LANES), lambda b, h, qi, ki: (b * nq + qi, h)),
        out_shape=jax.ShapeDtypeStruct((nb * tq_total, nh * LANES), out_dtype),
        scratch_shapes=[pltpu.VMEM((1, tq), F32), pltpu.VMEM((1, tq), F32), pltpu.VMEM((LANES, tq), F32)],
        compiler_params=_cp("parallel", "parallel", "parallel", "arbitrary"))(*args)


def _head_sum_matrices(n_heads, hd):
    lane = jnp.arange(n_heads * hd)[:, None] // hd
    e = (lane == jnp.arange(LANES)[None, :]).astype(F32)
    return e, e.T


def _rwkv_prep(cols, shift_prev, prm, *, nb, t, row0):
    tt = min(256, t)
    assert t % tt == 0 and row0 % tt == 0
    nt = t // tt
    rb0 = row0 // tt
    w = RWKV_W

    def body(c_ref, sp_ref, mu_ref, w0_ref, a0_ref, kk_ref, ka_ref, w2_ref, a2_ref, g2_ref, e_ref, et_ref,
             r_o, k_o, v_o, kk_o, b_o, lw_o, g_o, sh_o, carry):
        ti = pl.program_id(1)

        @pl.when(ti == 0)
        def _():
            carry[...] = sp_ref[0]

        c = c_ref[...]
        row = lax.broadcasted_iota(jnp.int32, (tt, 1), 0)
        prev = jnp.where(row == 0, carry[...], pltpu.roll(c, 1, 0))
        carry[...] = c[tt - 1:tt, :]
        sh_o[0] = c[tt - 1:tt, :]
        xs = c + (prev - c) * mu_ref[...]
        r, k, v = xs[:, :w], xs[:, w:2 * w], xs[:, 2 * w:3 * w]
        o1 = 3 * w
        lora = lambda u, w_ref: _dot(u.astype(BF16), w_ref[...].astype(BF16))
        w_raw = w0_ref[...] + lora(jnp.tanh(xs[:, o1:o1 + LANES]), w2_ref)
        log_w = -jnp.exp(-_softplus(-w_raw) - 0.5)
        a = _sigmoid(a0_ref[...] + lora(xs[:, o1 + LANES:o1 + 2 * LANES], a2_ref))
        g = lora(_sigmoid(xs[:, o1 + 2 * LANES:]), g2_ref)

        def head_dot(u, sel_ref):
            hi, lo = _split_bf16(u)
            sel = sel_ref[...].astype(BF16)
            return _dot(hi.astype(BF16), sel) + _dot(lo.astype(BF16), sel)

        kk = k * kk_ref[...]
        kk = kk * head_dot(lax.rsqrt(head_dot(kk * kk, e_ref) + 1e-6), et_ref)
        r_o[...] = r.astype(r_o.dtype)
        k_o[...] = (k * (1.0 + (a - 1.0) * ka_ref[...])).astype(k_o.dtype)
        v_o[...] = v.astype(v_o.dtype)
        kk_o[...] = kk.astype(kk_o.dtype)
        b_o[...] = (kk * a).astype(b_o.dtype)
        lw_o[...] = log_w
        g_o[...] = g.astype(g_o.dtype)

    e, et = _head_sum_matrices(RWKV_HEADS, RWKV_HD)
    tok = pl.BlockSpec((tt, w), lambda b, ti: (b * nt + ti, 0))
    full = lambda arr: pl.BlockSpec(arr.shape, lambda b, ti: (0,) * arr.ndim)
    params = [prm['mu'], prm['w0'], prm['a0'], prm['k_k'], prm['k_a'], prm['w2'], prm['a2'], prm['g2'], e, et]
    outs = pl.pallas_call(
        body, grid=(nb, nt), name="rwkv_prep",
        in_specs=[pl.BlockSpec((tt, RWKV_PCOLS), lambda b, ti: (rb0 + b * nt + ti, 0)),
                  pl.BlockSpec((1, 1, RWKV_PCOLS), lambda b, ti: (b, 0, 0))] + [full(p) for p in params],
        out_specs=[tok] * 7 + [pl.BlockSpec((1, 1, RWKV_PCOLS), lambda b, ti: (b, 0, 0))],
        out_shape=[jax.ShapeDtypeStruct((nb * t, w), dt) for dt in (BF16, BF16, BF16, BF16, BF16, F32, BF16)]
        + [jax.ShapeDtypeStruct((nb, 1, RWKV_PCOLS), F32)],
        scratch_shapes=[pltpu.VMEM((1, RWKV_PCOLS), F32)],
        compiler_params=_cp("parallel", "arbitrary"))(cols, shift_prev, *params)
    return outs


def _cumsum_rows(x, n):
    row = lax.broadcasted_iota(jnp.int32, (n, 1), 0)
    sh = 1
    while sh < n:
        x = x + jnp.where(row >= sh, pltpu.roll(x, sh, 0), 0.0)
        sh *= 2
    return x


def _split_bf16(x):
    hi = x.astype(BF16).astype(F32)
    return hi, x - hi


def _bmm(a, b, ca, cb, passes):
    dn = (((ca,), (cb,)), ((0,), (0,)))
    if passes == 6:
        return lax.dot_general(a, b, dn, precision=HP, preferred_element_type=F32)
    if passes == 3:
        ah, al = _split_bf16(a)
        bh, bl = _split_bf16(b)
        a = jnp.concatenate([ah, ah, al], axis=ca)
        b = jnp.concatenate([bh, bl, bh], axis=cb)
    return lax.dot_general(a.astype(BF16), b.astype(BF16), dn, preferred_element_type=F32)


def _heads(x, nh, hd):
    return jnp.stack([x[:, h * hd:(h + 1) * hd] for h in range(nh)], axis=0)


def _unheads(x):
    return jnp.concatenate([x[h] for h in range(x.shape[0])], axis=1)


def _unit_lower_inverse(nmat, n, passes):
    eye = (lax.broadcasted_iota(jnp.int32, (n, n), 0) == lax.broadcasted_iota(jnp.int32, (n, n), 1)).astype(F32)
    t = eye + nmat
    p = nmat
    steps = int(math.ceil(math.log2(n))) - 1
    for _ in range(steps):
        p = _bmm(p, p, 2, 1, passes)
        t = t + _bmm(p, t, 2, 1, passes)
    return t


def _rwkv_scan(r, k, v, kk, bvec, logw, g, s0, prm, *, nb, t):
    c = min(SCAN_CHUNK, t)
    assert t % c == 0
    nc = t // c
    hd, nh = RWKV_HD, RWKV_HEADS

    def body(r_ref, k_ref, v_ref, kk_ref, b_ref, lw_ref, g_ref, s0_ref, rk_ref, lg_ref, lb_ref, o_ref, sf_ref, st):
        ci = pl.program_id(1)

        @pl.when(ci == 0)
        def _():
            st[...] = s0_ref[0]

        lw = lw_ref[...]
        lc = _cumsum_rows(lw, c)
        rr, kx, vv, bb = (ref[...].astype(F32) for ref in (r_ref, k_ref, v_ref, b_ref))
        g_inv = jnp.exp(-lc)
        g_end = jnp.exp(lc[c - 1:c, :] - lc)
        hs = lambda a: _heads(a, nh, hd)
        at, rt = hs(-kk_ref[...].astype(F32) * jnp.exp(lc - lw)), hs(rr * jnp.exp(lc))
        bt, kt = hs(bb * g_inv), hs(kx * g_inv)
        bg, kg = hs(bb * g_end), hs(kx * g_end)
        w_end = hs(jnp.exp(lc[c - 1:c, :]))
        v_h = hs(vv)
        ri = lax.broadcasted_iota(jnp.int32, (c, c), 0)
        cj = lax.broadcasted_iota(jnp.int32, (c, c), 1)
        strict = ri > cj
        incl2 = lax.broadcasted_iota(jnp.int32, (c, 2 * c), 0) >= (lax.broadcasted_iota(jnp.int32, (c, 2 * c), 1) & (c - 1))
        s_all = st[...]
        pp = RWKV_PASSES
        lhs = jnp.concatenate([at, rt], axis=1)
        a_all = _bmm(lhs, jnp.concatenate([bt, kt], axis=1), 2, 2, pp['a'])
        x_all = _bmm(lhs, s_all, 2, 2, pp['x'])
        tinv = _unit_lower_inverse(jnp.where(strict, a_all[:, :c, :c], 0.0), c, pp['inv'])
        a_ak = jnp.where(strict, a_all[:, :c, c:], 0.0)
        u = _bmm(tinv, x_all[:, :c] + _bmm(a_ak, v_h, 2, 1, pp['u']), 2, 1, pp['u'])
        uv = jnp.concatenate([u, v_h], axis=1)
        y = x_all[:, c:] + _bmm(jnp.where(incl2, a_all[:, c:, :], 0.0), uv, 2, 1, pp['y'])
        st[...] = s_all * w_end + _bmm(uv, jnp.concatenate([bg, kg], axis=1), 1, 1, pp['s'])
        mu = jnp.mean(y, axis=-1, keepdims=True)
        yc = y - mu
        var = jnp.mean(yc * yc, axis=-1, keepdims=True)
        yn = yc * lax.rsqrt(var + RWKV_GN_EPS) * hs(lg_ref[...]) + hs(lb_ref[...])
        bonus = jnp.sum(hs(rr * kx * rk_ref[...]), axis=-1, keepdims=True) * v_h
        o_ref[...] = (_unheads(yn + bonus) * g_ref[...].astype(F32)).astype(o_ref.dtype)

        @pl.when(ci == nc - 1)
        def _():
            sf_ref[0] = st[...]

    tok = pl.BlockSpec((c, RWKV_W), lambda b, ci: (b * nc + ci, 0))
    stt = pl.BlockSpec((1, nh, hd, hd), lambda b, ci: (b, 0, 0, 0))
    par = pl.BlockSpec((1, RWKV_W), lambda b, ci: (0, 0))
    return pl.pallas_call(
        body, grid=(nb, nc), name="rwkv_scan",
        in_specs=[tok] * 7 + [stt, par, par, par],
        out_specs=[tok, stt],
        out_shape=[jax.ShapeDtypeStruct((nb * t, RWKV_W), BF16), jax.ShapeDtypeStruct((nb, nh, hd, hd), F32)],
        scratch_shapes=[pltpu.VMEM((nh, hd, hd), F32)],
        compiler_params=_cp("parallel", "arbitrary"))(r, k, v, kk, bvec, logw, g, s0, prm['r_k'], prm['ln_g'], prm['ln_b'])


class _Geom:
    def __init__(self, bp, tp, bs, ts, past):
        self.bp, self.tp, self.bs, self.ts, self.past = bp, tp, bs, ts, past
        self.mp, self.ms = bp * tp, bs * ts
        self.m = self.mp + self.ms
        self.tm = math.gcd(self.m, 512)
        assert self.tm % 16 == 0 and tp % CHUNK == 0 and past % CHUNK == 0 and ts == CHUNK
        self.tq_p = min(1024, tp)
        self.kpad = -(-(past + ts) // (3 * LANES)) * (3 * LANES)
        self.tk_s = self.kpad


def _pad_cols(w, n):
    return jnp.pad(w, ((0, 0), (0, n - w.shape[1])))


def _rope_table(geo):
    half = MLA_ROPE // 2
    pos = jnp.concatenate([jnp.tile(jnp.arange(geo.tp, dtype=jnp.int32), geo.bp),
                           geo.past + jnp.tile(jnp.arange(geo.ts, dtype=jnp.int32), geo.bs)])
    inv_freq = ROPE_THETA ** (-jnp.arange(half, dtype=F32) / half)
    ang = pos.astype(F32)[:, None] * inv_freq[None, :]
    cos, sin = jnp.cos(ang), jnp.sin(ang)
    z = jnp.zeros_like(cos)
    zz = jnp.zeros((pos.shape[0], LANES - MLA_ROPE), F32)
    return jnp.concatenate([cos, cos, zz, -sin, z, zz, z, sin, zz], axis=1)


def _rwkv_cols_layout(a):
    o1 = 3 * RWKV_W
    o2 = o1 + RWKV_DECAY_LORA
    o3 = o2 + RWKV_A_LORA
    pad = [(0, 0)] * (a.ndim - 1)
    return jnp.concatenate([a[..., :o1],
                            jnp.pad(a[..., o1:o2], pad + [(0, LANES - RWKV_DECAY_LORA)]),
                            jnp.pad(a[..., o2:o3], pad + [(0, LANES - RWKV_A_LORA)]),
                            a[..., o3:]], axis=-1)


def _rwkv_cols_unlayout(a):
    o1 = 3 * RWKV_W
    return jnp.concatenate([a[..., :o1], a[..., o1:o1 + RWKV_DECAY_LORA],
                            a[..., o1 + LANES:o1 + LANES + RWKV_A_LORA], a[..., o1 + 2 * LANES:]], axis=-1)


def _even_mixer(geo, xb, rope_tab, w_in, kv_norm, w_uk, w_uv, rw, cache_ckv, cache_kr, shift_prev, wkv0):
    tm, m, mp = geo.tm, geo.m, geo.mp
    d = w_in.shape[0]
    nq = MLA_HEADS * MLA_QK
    wq = w_in[:, :nq].reshape(d, MLA_HEADS, MLA_QK)
    w_qn = wq[:, :, :MLA_NOPE].reshape(d, MLA_HEADS * MLA_NOPE).astype(BF16)
    w_qr = jnp.pad(wq[:, :, MLA_NOPE:], ((0, 0), (0, 0), (0, LANES - MLA_ROPE))).reshape(d, MLA_HEADS * LANES).astype(BF16)
    w_ckv = w_in[:, nq:nq + MLA_KV_LORA].astype(BF16)
    w_kr = _pad_cols(w_in[:, nq + MLA_KV_LORA:nq + MLA_KV_LORA + MLA_ROPE], LANES).astype(BF16)
    w_rw = _rwkv_cols_layout(w_in[:, nq + MLA_KV_LORA + MLA_ROPE:]).astype(BF16)
    scale = MLA_QK ** -0.5

    (qn,) = _mm([(xb, w_qn)], tm=tm, tn=None, epilogue=lambda acc: (acc * scale,), outs=[(None, BF16)])
    (qr,) = _mm([(xb, w_qr)], tm=tm, tn=MLA_HEADS * LANES, aux=[(rope_tab, 'row')],
                epilogue=lambda acc, tab: (_rope_lanes(acc, tab) * scale,), outs=[(MLA_HEADS * LANES, BF16)])
    groups = ((0, mp), (mp, geo.ms))
    (ckv_p, ckvb_p), (ckv_s, ckvb_s) = (
        _mm([(xb, w_ckv)], tm=tm, tn=MLA_KV_LORA, aux=[(kv_norm.reshape(1, -1), 'col')], rows=g,
            epilogue=_ep_rms, outs=[(MLA_KV_LORA, F32), (MLA_KV_LORA, BF16)]) for g in groups)
    (kr_p, krb_p), (kr_s, krb_s) = (
        _mm([(xb, w_kr)], tm=tm, tn=LANES, aux=[(rope_tab, 'row')], rows=g,
            epilogue=lambda acc, tab: (_rope_lanes(acc, tab),) * 2, outs=[(LANES, F32), (LANES, BF16)]) for g in groups)
    (rwc,) = _mm([(xb, w_rw)], tm=tm, tn=None, epilogue=lambda acc: (acc,), outs=[(None, F32)])

    w_k = w_uk.reshape(MLA_KV_LORA, -1).astype(BF16)
    w_vt = w_uv.reshape(MLA_KV_LORA, -1).T.astype(BF16)
    (kn_p,) = _mm([(ckvb_p, w_k)], tm=tm, tn=None, epilogue=lambda acc: (acc,), outs=[(None, BF16)])
    vt_p = _mm_t(w_vt, ckvb_p, nb=geo.bp, t=geo.tp, row0=0)
    o_p = _flash([qn, qr], [(kn_p, 0, True), (krb_p, 0, False)], vt_p,
                 nb=geo.bp, tq_total=geo.tp, tk_total=geo.tp, q_row0=0, q_off=0, tq=geo.tq_p, tk=geo.tq_p,
                 chunked=True, nh=MLA_HEADS)
    padk = geo.kpad - geo.past - geo.ts
    ckv_all = jnp.concatenate([cache_ckv.astype(BF16), ckvb_s.reshape(geo.bs, geo.ts, -1),
                               jnp.zeros((geo.bs, padk, MLA_KV_LORA), BF16)], axis=1).reshape(geo.bs * geo.kpad, -1)
    kr_all = jnp.concatenate([jnp.pad(cache_kr, ((0, 0), (0, 0), (0, LANES - MLA_ROPE))).astype(BF16),
                              krb_s.reshape(geo.bs, geo.ts, -1),
                              jnp.zeros((geo.bs, padk, LANES), BF16)], axis=1).reshape(geo.bs * geo.kpad, -1)
    (kn_all,) = _mm([(ckv_all, w_k)], tm=3 * LANES, tn=None, epilogue=lambda acc: (acc,), outs=[(None, BF16)])
    vt_all = _mm_t(w_vt, ckv_all, nb=geo.bs, t=geo.kpad, row0=0)
    o_s = _flash([qn, qr], [(kn_all, 0, True), (kr_all, 0, False)], vt_all,
                 nb=geo.bs, tq_total=geo.ts, tk_total=geo.kpad, q_row0=mp, q_off=geo.past, tq=geo.ts, tk=geo.tk_s,
                 chunked=True, nh=MLA_HEADS)
    o_mla = jnp.concatenate([o_p, o_s], axis=0)

    outs_p = _rwkv_prep(rwc, jnp.zeros((geo.bp, 1, RWKV_PCOLS), F32), rw, nb=geo.bp, t=geo.tp, row0=0)
    outs_s = _rwkv_prep(rwc, _rwkv_cols_layout(shift_prev)[:, None, :], rw, nb=geo.bs, t=geo.ts, row0=mp)
    y_p, wkv_p = _rwkv_scan(*outs_p[:7], jnp.zeros((geo.bp, RWKV_HEADS, RWKV_HD, RWKV_HD), F32), rw, nb=geo.bp, t=geo.tp)
    y_s, wkv_s = _rwkv_scan(*outs_s[:7], wkv0, rw, nb=geo.bs, t=geo.ts)
    o_rwkv = jnp.concatenate([y_p, y_s], axis=0)
    states = dict(
        ckv_p=ckv_p.reshape(geo.bp, geo.tp, -1), ckv_s=ckv_s.reshape(geo.bs, geo.ts, -1),
        kr_p=kr_p[:, :MLA_ROPE].reshape(geo.bp, geo.tp, -1), kr_s=kr_s[:, :MLA_ROPE].reshape(geo.bs, geo.ts, -1),
        sh_p=_rwkv_cols_unlayout(outs_p[7][:, 0]), sh_s=_rwkv_cols_unlayout(outs_s[7][:, 0]),
        rwkv_p=wkv_p, rwkv_s=wkv_s)
    return o_mla, o_rwkv, states


def _head_rms(x, g, nh, scale=1.0):
    outs = []
    for h in range(nh):
        seg = x[:, h * LANES:(h + 1) * LANES]
        outs.append(seg * lax.rsqrt(jnp.mean(seg * seg, axis=-1, keepdims=True) + RMS_EPS) * (g * scale))
    return jnp.concatenate(outs, axis=1)


def _head_l2(x, nh, scale=1.0):
    outs = []
    for h in range(nh):
        seg = x[:, h * LANES:(h + 1) * LANES]
        outs.append(seg * (lax.rsqrt(jnp.sum(seg * seg, axis=-1, keepdims=True) + 1e-6) * scale))
    return jnp.concatenate(outs, axis=1)


def _cumsum_time(x):
    nb, t, w = x.shape

    def body(x_ref, o_ref):
        o_ref[0] = _cumsum_rows(x_ref[0], t)

    spec = pl.BlockSpec((1, t, w), lambda b: (b, 0, 0))
    return pl.pallas_call(body, grid=(nb,), in_specs=[spec], out_specs=spec, name="cumsum_time",
                          out_shape=jax.ShapeDtypeStruct(x.shape, F32), compiler_params=_cp("parallel"))(x)


def _gdn_prep(cols, conv_prev, conv_w, *, nb, t, row0):
    tt = min(256, t)
    assert t % tt == 0 and row0 % tt == 0
    nt = t // tt
    rb0 = row0 // tt
    w = GDN_W

    def body(c_ref, cp_ref, cw_ref, q_o, k_o, v_o, carry):
        ti = pl.program_id(1)

        @pl.when(ti == 0)
        def _():
            carry[...] = cp_ref[0]

        c = c_ref[...]
        ext = jnp.concatenate([carry[...], c], axis=0)
        carry[...] = c[tt - 8:tt, :]
        cw = cw_ref[...]
        acc = c * cw[GDN_CONV - 1:GDN_CONV, :]
        for j in range(GDN_CONV - 1):
            back = GDN_CONV - 1 - j
            acc = acc + ext[8 - back:8 - back + tt, :] * cw[j:j + 1, :]
        qkv = _silu(acc)
        q_o[...] = _head_l2(qkv[:, :w], GDN_HEADS, GDN_HD ** -0.5).astype(q_o.dtype)
        k_o[...] = _head_l2(qkv[:, w:2 * w], GDN_HEADS).astype(k_o.dtype)
        v_o[...] = qkv[:, 2 * w:].astype(v_o.dtype)

    tok = pl.BlockSpec((tt, w), lambda b, ti: (b * nt + ti, 0))
    return pl.pallas_call(
        body, grid=(nb, nt), name="gdn_prep",
        in_specs=[pl.BlockSpec((tt, GDN_QKV), lambda b, ti: (rb0 + b * nt + ti, 0)),
                  pl.BlockSpec((1, 8, GDN_QKV), lambda b, ti: (b, 0, 0)),
                  pl.BlockSpec((8, GDN_QKV), lambda b, ti: (0, 0))],
        out_specs=[tok] * 3,
        out_shape=[jax.ShapeDtypeStruct((nb * t, w), BF16)] * 3,
        scratch_shapes=[pltpu.VMEM((8, GDN_QKV), F32)],
        compiler_params=_cp("parallel", "arbitrary"))(cols, conv_prev, conv_w)


def _gdn_scan(q, k, v, gb, z, s0, norm_g, *, nb, t, gb_row0, z_row0):
    c = min(SCAN_CHUNK, t)
    ns = math.gcd(nb, GDN_STREAMS)
    assert t % c == 0 and gb_row0 % c == 0 and z_row0 % c == 0
    nc = t // c
    nh, hd = GDN_HEADS, GDN_HD

    def body(*refs):
        q_ref, k_ref, v_ref = refs[:3]
        gb_refs, z_refs = refs[3:3 + ns], refs[3 + ns:3 + 2 * ns]
        s0_ref, ng_ref, o_ref, sf_ref, st = refs[3 + 2 * ns:]
        ci = pl.program_id(1)

        @pl.when(ci == 0)
        def _():
            for g in range(ns):
                st[g * nh:(g + 1) * nh] = s0_ref[g]

        ri = lax.broadcasted_iota(jnp.int32, (c, c), 0)
        cj = lax.broadcasted_iota(jnp.int32, (c, c), 1)
        tril, strict = ri >= cj, ri > cj
        gcols, grows, bcols = [], [], []
        for g in range(ns):
            gbv = gb_refs[g][...]
            gc = _cumsum_rows(gbv, c)
            gct = gc.T
            gcols.append(jnp.stack([gc[:, h:h + 1] for h in range(nh)], axis=0))
            grows.append(gct[:nh][:, None, :])
            bcols.append(jnp.stack([gbv[:, nh + h:nh + h + 1] for h in range(nh)], axis=0))
        gcol, grow, bcol = (jnp.concatenate(parts, axis=0) for parts in (gcols, grows, bcols))
        hs = lambda ref: jnp.concatenate([_heads(ref[g].astype(F32), nh, hd) for g in range(ns)], axis=0)
        k_h, q_h, v_h = hs(k_ref), hs(q_ref), hs(v_ref)
        decay = jnp.where(tril, jnp.exp(jnp.where(tril, gcol - grow, 0.0)), 0.0)
        pp = GDN_PASSES
        kb = k_h * bcol
        prods = _bmm(jnp.concatenate([kb, q_h], axis=1), k_h, 2, 2, pp['a'])
        low = jnp.where(strict, prods[:, :c] * decay, 0.0)
        a_qk = jnp.where(tril, prods[:, c:] * decay, 0.0)
        tinv = _unit_lower_inverse(-low, c, pp['inv'])
        e_g = jnp.exp(gcol)
        uw = _bmm(tinv, jnp.concatenate([v_h * bcol, kb * e_g], axis=2), 2, 1, pp['u'])
        s_all = st[...]
        ws_qs = _bmm(jnp.concatenate([uw[:, :, hd:], q_h * e_g], axis=1), s_all, 2, 1, pp['x'])
        v_new = uw[:, :, :hd] - ws_qs[:, :c]
        o = ws_qs[:, c:] + _bmm(a_qk, v_new, 2, 1, pp['y'])
        g_last = gcol[:, c - 1:c, :]
        st[...] = s_all * jnp.exp(g_last) + _bmm(k_h * jnp.exp(g_last - gcol), v_new, 1, 1, pp['s'])
        o = o * lax.rsqrt(jnp.mean(o * o, axis=-1, keepdims=True) + RMS_EPS) * ng_ref[...]
        for g in range(ns):
            o_ref[g] = (_unheads(o[g * nh:(g + 1) * nh]) * _silu(z_refs[g][...].astype(F32))).astype(o_ref.dtype)

        @pl.when(ci == nc - 1)
        def _():
            for g in range(ns):
                sf_ref[g] = st[g * nh:(g + 1) * nh]

    as3 = lambda a: a.reshape(nb, t, GDN_W)
    tok = pl.BlockSpec((ns, c, GDN_W), lambda b, ci: (b, ci, 0))
    stt = pl.BlockSpec((ns, nh, hd, hd), lambda b, ci: (b, 0, 0, 0))
    row_spec = lambda width, row0, g: pl.BlockSpec((c, width), lambda b, ci: (row0 // c + (b * ns + g) * nc + ci, 0))
    o, s_fin = pl.pallas_call(
        body, grid=(nb // ns, nc), name="gdn_scan",
        in_specs=[tok] * 3 + [row_spec(LANES, gb_row0, g) for g in range(ns)]
        + [row_spec(GDN_W, z_row0, g) for g in range(ns)] + [stt, pl.BlockSpec((1, hd), lambda b, ci: (0, 0))],
        out_specs=[tok, stt],
        out_shape=[jax.ShapeDtypeStruct((nb, t, GDN_W), BF16), jax.ShapeDtypeStruct((nb, nh, hd, hd), F32)],
        scratch_shapes=[pltpu.VMEM((ns * nh, hd, hd), F32)],
        compiler_params=_cp("parallel", "arbitrary"))(as3(q), as3(k), as3(v), *([gb] * ns), *([z] * ns), s0, norm_g)
    return o.reshape(nb * t, GDN_W), s_fin


def _odd_mixer(geo, xb, w_in, fx, gd, cache_k, cache_v, cache_logf, conv_prev, gdn0):
    tm, m, mp = geo.tm, geo.m, geo.mp
    fw = FOX_W
    o_f = 4 * fw
    o_g = o_f + FOX_HEADS
    w_q, w_k, w_v, w_gate = (w_in[:, i * fw:(i + 1) * fw].astype(BF16) for i in range(4))
    w_f = _pad_cols(w_in[:, o_f:o_g], LANES).astype(BF16)
    w_qkv = w_in[:, o_g:o_g + GDN_QKV].astype(BF16)
    w_ab = _pad_cols(w_in[:, o_g + GDN_QKV:o_g + GDN_QKV + 2 * GDN_HEADS], LANES).astype(BF16)
    w_z = w_in[:, o_g + GDN_QKV + 2 * GDN_HEADS:].astype(BF16)
    qn, kn = fx['q_norm'].reshape(1, -1), fx['k_norm'].reshape(1, -1)
    f_bias = jnp.pad(fx['f_bias'], (0, LANES - FOX_HEADS)).reshape(1, -1)
    scale = FOX_HD ** -0.5

    (q,) = _mm([(xb, w_q)], tm=tm, tn=fw, aux=[(qn, 'row0')],
               epilogue=lambda acc, g: (_head_rms(acc, g, FOX_HEADS, scale),), outs=[(fw, BF16)])
    groups = ((0, mp), (mp, geo.ms))
    (fk_p, kb_p), (fk_s, kb_s) = (
        _mm([(xb, w_k)], tm=tm, tn=fw, aux=[(kn, 'row0')], rows=g,
            epilogue=lambda acc, g_: (_head_rms(acc, g_, FOX_HEADS),) * 2, outs=[(fw, F32), (fw, BF16)]) for g in groups)
    (fv_p,), (fv_s,) = (_mm([(xb, w_v)], tm=tm, tn=None, rows=g, epilogue=lambda acc: (acc,), outs=[(None, F32)])
                      for g in groups)
    w_vt = w_v.T
    vt_p = _mm_t(w_vt, xb, nb=geo.bp, t=geo.tp, row0=0)
    vt_s = _mm_t(w_vt, xb, nb=geo.bs, t=geo.ts, row0=mp)
    (gate,) = _mm([(xb, w_gate)], tm=tm, tn=None, epilogue=lambda acc: (_sigmoid(acc),), outs=[(None, BF16)])
    (logf,) = _mm([(xb, w_f)], tm=tm, tn=LANES, aux=[(f_bias, 'col')],
                  epilogue=lambda acc, fb: (-_softplus(-(acc + fb)),), outs=[(LANES, F32)])

    cum_p = _cumsum_time(logf[:mp].reshape(geo.bp, geo.tp, LANES))[:, :, :FOX_HEADS]
    bias_p = (jnp.transpose(cum_p, (0, 2, 1))[:, :, None, :], jnp.transpose(cum_p, (0, 2, 1))[..., None])
    o_p = _flash([q], [(kb_p, 0, True)], vt_p, nb=geo.bp, tq_total=geo.tp, tk_total=geo.tp, q_row0=0, q_off=0,
                 tq=geo.tq_p, tk=geo.tq_p, chunked=False, nh=FOX_HEADS, bias=bias_p, gate=gate)
    padk = geo.kpad - geo.past - geo.ts
    lf_all = jnp.concatenate([jnp.pad(cache_logf, ((0, 0), (0, 0), (0, LANES - FOX_HEADS))),
                              logf[mp:].reshape(geo.bs, geo.ts, LANES), jnp.zeros((geo.bs, padk, LANES), F32)], axis=1)
    cum_s = jnp.transpose(_cumsum_time(lf_all)[:, :, :FOX_HEADS], (0, 2, 1))
    bias_s = (cum_s[:, :, None, geo.past:geo.past + geo.ts], cum_s[..., None])
    k_all = jnp.concatenate([cache_k.reshape(geo.bs, geo.past, fw).astype(BF16), kb_s.reshape(geo.bs, geo.ts, fw),
                             jnp.zeros((geo.bs, padk, fw), BF16)], axis=1).reshape(geo.bs * geo.kpad, fw)
    vt_all = jnp.concatenate([jnp.transpose(cache_v.reshape(geo.bs, geo.past, fw), (0, 2, 1)).astype(BF16),
                              vt_s.reshape(geo.bs, fw, geo.ts), jnp.zeros((geo.bs, fw, padk), BF16)],
                             axis=2).reshape(geo.bs * fw, geo.kpad)
    o_s = _flash([q], [(k_all, 0, True)], vt_all, nb=geo.bs, tq_total=geo.ts, tk_total=geo.kpad, q_row0=mp,
                 q_off=geo.past, tq=geo.ts, tk=geo.tk_s, chunked=False, nh=FOX_HEADS, bias=bias_s, gate=gate)
    o_fox = jnp.concatenate([o_p, o_s], axis=0)

    (qkv_raw,) = _mm([(xb, w_qkv)], tm=tm, tn=None, epilogue=lambda acc: (acc,), outs=[(None, F32)])
    lane = jnp.arange(LANES)
    neg_a = jnp.where(lane < GDN_HEADS, -jnp.exp(jnp.pad(gd['a_log'], (0, LANES - GDN_HEADS))), 0.0).reshape(1, -1)
    dtb = jnp.pad(gd['dt_bias'], (0, LANES - GDN_HEADS)).reshape(1, -1)
    is_g = (lane < GDN_HEADS).astype(F32).reshape(1, -1)
    (gb,) = _mm([(xb, w_ab)], tm=tm, tn=LANES, aux=[(neg_a, 'col'), (dtb, 'col'), (is_g, 'col')],
                epilogue=lambda acc, na, db, ig: (jnp.where(ig > 0.5, na * _softplus(acc + db), _sigmoid(acc)),),
                outs=[(LANES, F32)])
    (z,) = _mm([(xb, w_z)], tm=tm, tn=None, epilogue=lambda acc: (acc,), outs=[(None, BF16)])
    conv_w = jnp.pad(gd['conv_w'], ((0, 8 - GDN_CONV), (0, 0)))
    norm_g = gd['norm'].reshape(1, -1)
    prev8 = lambda a: jnp.pad(a, ((0, 0), (8 - (GDN_CONV - 1), 0), (0, 0)))
    q_p, k_p, v_p = _gdn_prep(qkv_raw, jnp.zeros((geo.bp, 8, GDN_QKV), F32), conv_w, nb=geo.bp, t=geo.tp, row0=0)
    q_s, k_s, v_s = _gdn_prep(qkv_raw, prev8(conv_prev), conv_w, nb=geo.bs, t=geo.ts, row0=mp)
    y_p, s_p = _gdn_scan(q_p, k_p, v_p, gb, z, jnp.zeros((geo.bp, GDN_HEADS, GDN_HD, GDN_HD), F32), norm_g,
                         nb=geo.bp, t=geo.tp, gb_row0=0, z_row0=0)
    y_s, s_s = _gdn_scan(q_s, k_s, v_s, gb, z, gdn0, norm_g, nb=geo.bs, t=geo.ts, gb_row0=mp, z_row0=mp)
    o_gdn = jnp.concatenate([y_p, y_s], axis=0)
    nconv = GDN_CONV - 1
    last_rows = lambda row0, nb, t: (row0 + jnp.arange(nb)[:, None] * t + (t - nconv + jnp.arange(nconv))[None, :]).reshape(-1)
    states = dict(
        fk_p=fk_p.reshape(geo.bp, geo.tp, FOX_HEADS, FOX_HD), fk_s=fk_s.reshape(geo.bs, geo.ts, FOX_HEADS, FOX_HD),
        fv_p=fv_p.reshape(geo.bp, geo.tp, FOX_HEADS, FOX_HD), fv_s=fv_s.reshape(geo.bs, geo.ts, FOX_HEADS, FOX_HD),
        fl_p=logf[:mp, :FOX_HEADS].reshape(geo.bp, geo.tp, FOX_HEADS),
        fl_s=logf[mp:, :FOX_HEADS].reshape(geo.bs, geo.ts, FOX_HEADS),
        cv_p=jnp.take(qkv_raw, last_rows(0, geo.bp, geo.tp), axis=0).reshape(geo.bp, nconv, GDN_QKV),
        cv_s=jnp.take(qkv_raw, last_rows(mp, geo.bs, geo.ts), axis=0).reshape(geo.bs, nconv, GDN_QKV),
        gdn_p=s_p, gdn_s=s_s)
    return o_fox, o_gdn, states


MOE_BLOCK = 512
MOE_COMBINE_ROWS = 256


def _moe_route(x, router, router_bias, *, tm):
    m = x.shape[0]
    assert m % tm == 0
    per_group = N_EXPERTS // N_GROUPS

    def lane_max(v):
        return jnp.max(v, axis=-1, keepdims=True)

    def first_lane(mask, lane):
        return jnp.min(jnp.where(mask, lane.astype(F32), float(LANES)), axis=-1, keepdims=True).astype(jnp.int32)

    def group_all(v, lane, op):
        sh = 1
        while sh < per_group:
            partner = jnp.where((lane & sh) == 0, pltpu.roll(v, LANES - sh, 1), pltpu.roll(v, sh, 1))
            v = op(v, partner)
            sh *= 2
        return v

    def body(x_ref, r_ref, b_ref, e_o, g_o, k_o, cnt_o, carry):
        i = pl.program_id(0)

        @pl.when(i == 0)
        def _():
            carry[...] = jnp.zeros(carry.shape, F32)

        lane = lax.broadcasted_iota(jnp.int32, (tm, LANES), 1)
        valid = lane < N_EXPERTS
        neg = -jnp.inf
        scores = _sigmoid(_dot(x_ref[...], r_ref[...]))
        biased = jnp.where(valid, scores + b_ref[...], neg)
        m1 = group_all(biased, lane, jnp.maximum)
        first = group_all(jnp.where(biased == m1, lane, LANES), lane, jnp.minimum)
        m2 = group_all(jnp.where(lane == first, neg, biased), lane, jnp.maximum)
        grp = jnp.where(valid & ((lane & (per_group - 1)) == 0), m1 + m2, neg)
        emask = jnp.zeros((tm, LANES), jnp.bool_)
        for _ in range(TOPK_GROUPS):
            idx = first_lane(grp == lane_max(grp), lane)
            emask = emask | ((lane - idx >= 0) & (lane - idx < per_group))
            grp = jnp.where(lane == idx, neg, grp)
        cur = jnp.where(emask & valid, biased, neg)
        picks = []
        sel = jnp.zeros((tm, LANES), jnp.bool_)
        for _ in range(TOP_K):
            idx = first_lane(cur == lane_max(cur), lane)
            pick = lane == idx
            picks.append((idx, pick))
            sel = sel | pick
            cur = jnp.where(pick, neg, cur)
        selw = jnp.where(sel, scores, 0.0)
        gates_dense = selw / jnp.sum(selw, axis=-1, keepdims=True) * ROUTED_SCALE
        ri = lax.broadcasted_iota(jnp.int32, (tm, tm), 0)
        cj = lax.broadcasted_iota(jnp.int32, (tm, tm), 1)
        p01 = jnp.where(sel, 1.0, 0.0)
        before = _dot(jnp.where(ri > cj, 1.0, 0.0).astype(BF16), p01.astype(BF16)) + carry[...]
        carry[...] = carry[...] + jnp.sum(p01, axis=0, keepdims=True)
        cnt_o[...] = carry[...]
        e_out = jnp.zeros((tm, LANES), jnp.int32)
        g_out = jnp.zeros((tm, LANES), F32)
        k_out = jnp.zeros((tm, LANES), jnp.int32)
        for j, (idx, pick) in enumerate(picks):
            e_out = jnp.where(lane == j, idx, e_out)
            g_out = jnp.where(lane == j, jnp.sum(jnp.where(pick, gates_dense, 0.0), axis=-1, keepdims=True), g_out)
            rank = jnp.sum(jnp.where(pick, before, 0.0), axis=-1, keepdims=True)
            k_out = jnp.where(lane == j, rank.astype(jnp.int32), k_out)
        e_o[...] = e_out
        g_o[...] = g_out
        k_o[...] = k_out

    tok = pl.BlockSpec((tm, LANES), lambda i: (i, 0))
    one = pl.BlockSpec((1, LANES), lambda i: (0, 0))
    return pl.pallas_call(
        body, grid=(m // tm,), name="moe_route",
        in_specs=[pl.BlockSpec((tm, x.shape[1]), lambda i: (i, 0)), pl.BlockSpec(router.shape, lambda i: (0, 0)), one],
        out_specs=[tok, tok, tok, one],
        out_shape=[jax.ShapeDtypeStruct((m, LANES), jnp.int32), jax.ShapeDtypeStruct((m, LANES), F32),
                   jax.ShapeDtypeStruct((m, LANES), jnp.int32), jax.ShapeDtypeStruct((1, LANES), F32)],
        scratch_shapes=[pltpu.VMEM((1, LANES), F32)],
        compiler_params=_cp("arbitrary"))(x, router, router_bias)


def _moe_dispatch(x_words, dest, cap, *, tb):
    m, wd = x_words.shape
    nt = m // tb

    def body(dest_ref, x_ref, out_ref, sem):
        def row_copy(i, j):
            return pltpu.make_async_copy(x_ref.at[pl.ds(i, 1)], out_ref.at[pl.ds(dest_ref[0, 0, i * TOP_K + j], 1)], sem)

        def issue(i, carry):
            for j in range(TOP_K):
                row_copy(i, j).start()
            return carry

        lax.fori_loop(0, tb, issue, 0, unroll=8)
        for _ in range(TOP_K):
            pltpu.make_async_copy(x_ref, out_ref.at[pl.ds(0, tb)], sem).wait()

    return pl.pallas_call(
        body, grid=(nt,), name="moe_dispatch",
        in_specs=[pl.BlockSpec((1, 1, tb * TOP_K), lambda i: (i, 0, 0), memory_space=pltpu.SMEM),
                  pl.BlockSpec((tb, wd), lambda i: (i, 0))],
        out_specs=pl.BlockSpec(memory_space=pl.ANY),
        out_shape=jax.ShapeDtypeStruct((cap, wd), x_words.dtype),
        scratch_shapes=[pltpu.SemaphoreType.DMA(())],
        compiler_params=_cp("arbitrary"))(dest, x_words)


def _pack_halves(y):
    n = y.shape[1] // 2
    bits = lambda a: lax.bitcast_convert_type(a.astype(BF16).astype(F32), jnp.uint32)
    return (bits(y[:, :n]) >> 16) | (bits(y[:, n:]) & jnp.uint32(0xFFFF0000))


def _unpack_halves(w):
    lo = lax.bitcast_convert_type(w << 16, F32).astype(BF16)
    hi = lax.bitcast_convert_type(w & jnp.uint32(0xFFFF0000), F32).astype(BF16)
    return lo, hi


def _ep_ln_packed(acc, x, g, b):
    y = _ep_ln(acc, x, g, b)
    return y, _pack_halves(y)


def _moe_experts(xs_words, blk_exp, blk_next, blk_rows, n_used, w_gate, w_up, w_down, layer):
    cap, half = xs_words.shape
    d = 2 * half
    nblk = cap // MOE_BLOCK
    ed = w_gate.shape[3]

    def body(be_ref, bn_ref, br_ref, nu_ref, x_ref, wg_hbm, wu_hbm, wd_hbm, o_ref, wg_f, wu_f, wd_f, wg_b, wu_b, wd_b,
             slot_ref, sem):
        i = pl.program_id(0)

        def fetch(e, slot):
            return [pltpu.make_async_copy(src.at[layer, e], dst.at[slot], sem.at[slot])
                    for src, dst in ((wg_hbm, wg_f), (wu_hbm, wu_f), (wd_hbm, wd_f))]

        @pl.when(i < nu_ref[0])
        def _():
            first = i == 0
            e = be_ref[i]

            @pl.when(first)
            def _():
                slot_ref[0] = 0
                for cp in fetch(e, 0):
                    cp.start()

            @pl.when(first | (e != be_ref[jnp.maximum(i - 1, 0)]))
            def _():
                slot = jnp.where(first, 0, 1 - slot_ref[0])
                slot_ref[0] = slot
                for cp in fetch(e, slot):
                    cp.wait()
                wg_b[...] = wg_f[slot].astype(BF16)
                wu_b[...] = wu_f[slot].astype(BF16)
                wd_b[...] = wd_f[slot].astype(BF16)

                @pl.when(bn_ref[i] != e)
                def _():
                    for cp in fetch(bn_ref[i], 1 - slot):
                        cp.start()

            row = lax.broadcasted_iota(jnp.int32, (MOE_BLOCK, 1), 0)
            lo, hi = _unpack_halves(jnp.where(row < br_ref[i], x_ref[...], jnp.uint32(0)))
            hg = _dot(lo, wg_b[:half]) + _dot(hi, wg_b[half:])
            hu = _dot(lo, wu_b[:half]) + _dot(hi, wu_b[half:])
            o_ref[...] = _pack_halves(_dot((_silu(hg) * hu).astype(BF16), wd_b[...]))

    blk = lambda i, be, bn, br, nu: (jnp.minimum(i, nu[0] - 1), 0)
    hbm = pl.BlockSpec(memory_space=pl.ANY)
    grid_spec = pltpu.PrefetchScalarGridSpec(
        num_scalar_prefetch=4, grid=(nblk,),
        in_specs=[pl.BlockSpec((MOE_BLOCK, half), blk), hbm, hbm, hbm],
        out_specs=pl.BlockSpec((MOE_BLOCK, half), blk),
        scratch_shapes=[pltpu.VMEM((2, d, ed), F32), pltpu.VMEM((2, d, ed), F32), pltpu.VMEM((2, ed, d), F32),
                        pltpu.VMEM((d, ed), BF16), pltpu.VMEM((d, ed), BF16), pltpu.VMEM((ed, d), BF16),
                        pltpu.SMEM((1,), jnp.int32), pltpu.SemaphoreType.DMA((2,))])
    return pl.pallas_call(body, grid_spec=grid_spec, out_shape=jax.ShapeDtypeStruct((cap, half), jnp.uint32), name="moe_experts",
                          compiler_params=_cp("arbitrary"))(blk_exp, blk_next, blk_rows, n_used, xs_words, w_gate, w_up, w_down)


def _moe_combine(ys_words, dest, gates, *, tc):
    cap, half = ys_words.shape
    d = 2 * half
    m = gates.shape[0]
    nt = m // tc

    def body(dest0_ref, destn_ref, g_ref, y_ref, o_ref, buf, sem):
        i = pl.program_id(0)

        def issue(dref, slot):
            def one(r, carry):
                for j in range(TOP_K):
                    pltpu.make_async_copy(y_ref.at[pl.ds(dref[0, 0, r * TOP_K + j], 1)],
                                          buf.at[slot, j, pl.ds(r, 1)], sem.at[slot]).start()
                return carry
            lax.fori_loop(0, tc, one, 0, unroll=8)

        @pl.when(i == 0)
        def _():
            issue(dest0_ref, 0)

        @pl.when(i + 1 < nt)
        def _():
            issue(destn_ref, (i + 1) % 2)

        slot = i % 2
        for j in range(TOP_K):
            pltpu.make_async_copy(y_ref.at[pl.ds(0, tc)], buf.at[slot, j], sem.at[slot]).wait()
        g = g_ref[...]
        acc_lo = acc_hi = None
        for j in range(TOP_K):
            lo, hi = _unpack_halves(buf[slot, j])
            gj = g[:, j:j + 1]
            acc_lo = lo.astype(F32) * gj if acc_lo is None else acc_lo + lo.astype(F32) * gj
            acc_hi = hi.astype(F32) * gj if acc_hi is None else acc_hi + hi.astype(F32) * gj
        o_ref[:, :half] = acc_lo
        o_ref[:, half:] = acc_hi

    return pl.pallas_call(
        body, grid=(nt,), name="moe_combine",
        in_specs=[pl.BlockSpec((1, 1, tc * TOP_K), lambda i: (0, 0, 0), memory_space=pltpu.SMEM),
                  pl.BlockSpec((1, 1, tc * TOP_K), lambda i: (jnp.minimum(i + 1, nt - 1), 0, 0), memory_space=pltpu.SMEM),
                  pl.BlockSpec((tc, LANES), lambda i: (i, 0)),
                  pl.BlockSpec(memory_space=pl.ANY)],
        out_specs=pl.BlockSpec((tc, d), lambda i: (i, 0)),
        out_shape=jax.ShapeDtypeStruct((m, d), F32),
        scratch_shapes=[pltpu.VMEM((2, TOP_K, tc, half), jnp.uint32), pltpu.SemaphoreType.DMA((2,))],
        compiler_params=_cp("arbitrary"))(dest, dest, gates, ys_words)


def _moe_layer(geo, x_res, xb, x_words, layer, router, router_bias, w_gate, w_up, w_down, ws_gate, ws_up, ws_down,
               ln_g, ln_b, out_dtype, out_rows):
    m, d = x_res.shape
    tm = geo.tm
    eidx, gates, rank, counts = _moe_route(xb, _pad_cols(router, LANES).astype(BF16),
                                           jnp.pad(router_bias, (0, LANES - N_EXPERTS)).reshape(1, -1), tm=tm)
    cnt = counts[0, :N_EXPERTS].astype(jnp.int32)
    padded = (cnt + MOE_BLOCK - 1) // MOE_BLOCK * MOE_BLOCK
    pad_end = jnp.cumsum(padded)
    pad_start = pad_end - padded
    n_blocks = -(-(m * TOP_K) // MOE_BLOCK) + N_EXPERTS
    cap = n_blocks * MOE_BLOCK
    blk_row0 = jnp.arange(n_blocks, dtype=jnp.int32) * MOE_BLOCK
    blk_exp = jnp.minimum(jnp.sum((pad_end[None, :] <= blk_row0[:, None]).astype(jnp.int32), axis=1), N_EXPERTS - 1)
    blk_rows = jnp.clip(cnt[blk_exp] - (blk_row0 - pad_start[blk_exp]), 0, MOE_BLOCK).astype(jnp.int32)
    n_used = (pad_end[-1:] // MOE_BLOCK).astype(jnp.int32)
    eid = jnp.arange(N_EXPERTS, dtype=jnp.int32)
    later = jnp.where((eid[None, :] > eid[:, None]) & (cnt[None, :] > 0), eid[None, :], N_EXPERTS)
    next_e = jnp.min(later, axis=1)
    next_e = jnp.where(next_e == N_EXPERTS, eid, next_e)
    blk_next = next_e[blk_exp].astype(jnp.int32)
    dest = jnp.sum(jnp.where(eidx[:, :TOP_K, None] == eid, pad_start, 0), axis=-1) + rank[:, :TOP_K]

    tb = tm
    xs_words = _moe_dispatch(x_words, dest.reshape(m // tb, 1, tb * TOP_K), cap, tb=tb)
    ys = _moe_experts(xs_words, blk_exp, blk_next, blk_rows, n_used, w_gate, w_up, w_down, layer)
    tc = min(MOE_COMBINE_ROWS, tm)
    routed = _moe_combine(ys, dest.reshape(m // tc, 1, tc * TOP_K), gates, tc=tc)

    sd = ws_gate.shape[1]
    w_sh = jnp.concatenate([ws_gate, ws_up], axis=1).astype(BF16)
    (hs,) = _mm([(xb, w_sh)], tm=tm, tn=2 * sd, epilogue=lambda acc: (_silu(acc[:, :sd]) * acc[:, sd:],), outs=[(sd, BF16)])
    ys = [_mm([(hs, ws_down.astype(BF16))], tm=tm, tn=d, rows=g,
              aux=[(x_res, 'rowcol'), (routed, 'rowcol'), (ln_g.reshape(1, -1), 'col'), (ln_b.reshape(1, -1), 'col')],
              epilogue=lambda acc, xr, rt, g_, b: (_ep_ln(acc + rt, xr, g_, b),), outs=[(d, out_dtype)], name="moe_out")[0]
          for g in out_rows]
    return ys


def _rwkv_params(li, mu, w0, w2, a0, a2, g2, k_k, k_a, r_k, ln_g, ln_b):
    row = lambda a: a[li].reshape(1, -1)
    padr = lambda a, n: jnp.pad(a, ((0, n - a.shape[0]), (0, 0)))
    return dict(mu=_rwkv_cols_layout(mu[li])[None, :], w0=row(w0), a0=row(a0), k_k=row(k_k), k_a=row(k_a),
                w2=padr(w2[li], LANES), a2=padr(a2[li], LANES), g2=g2[li], r_k=row(r_k), ln_g=row(ln_g), ln_b=row(ln_b))


def _mix_out(geo, x_res, o_a, o_b, w_out, ln_g, ln_b):
    d = x_res.shape[1]
    ka = o_a.shape[1]
    return _mm([(o_a, w_out[:ka].astype(BF16)), (o_b, w_out[ka:].astype(BF16))], tm=geo.tm, tn=d,
               aux=[(x_res, 'rowcol'), (ln_g.reshape(1, -1), 'col'), (ln_b.reshape(1, -1), 'col')],
               epilogue=_ep_ln_packed, outs=[(d, BF16), (d // 2, jnp.uint32)], name="mix_out")


def kernel(x_prompt, x_sample, cache_mla_ckv, cache_mla_krope, state_rwkv_shift, state_rwkv_wkv, cache_fox_k, cache_fox_v, cache_fox_logf, state_gdn_conv, state_gdn_wkv, ln1_g, ln1_b, ln2_g, ln2_b, ev_w_in, ev_w_out, mla_kv_norm, mla_w_uk, mla_w_uv, rwkv_mu, rwkv_w0, rwkv_w2, rwkv_a0, rwkv_a2, rwkv_g2, rwkv_k_k, rwkv_k_a, rwkv_r_k, rwkv_ln_g, rwkv_ln_b, od_w_in, od_w_out, fox_q_norm, fox_k_norm, fox_f_bias, gdn_conv_w, gdn_a_log, gdn_dt_bias, gdn_norm, moe_router, moe_router_bias, moe_w_gate, moe_w_up, moe_w_down, moe_ws_gate, moe_ws_up, moe_ws_down):
    bp, tp, d = x_prompt.shape
    bs, ts, _ = x_sample.shape
    geo = _Geom(bp, tp, bs, ts, cache_mla_ckv.shape[2])
    x = xb = jnp.concatenate([x_prompt.reshape(bp * tp, d).astype(BF16), x_sample.reshape(bs * ts, d).astype(BF16)], axis=0)
    rope_tab = _rope_table(geo)
    st = {}
    n_layers = ln1_g.shape[0]
    for layer in range(n_layers):
        li = layer // 2
        if layer % 2 == 0:
            rw = _rwkv_params(li, rwkv_mu, rwkv_w0, rwkv_w2, rwkv_a0, rwkv_a2, rwkv_g2, rwkv_k_k, rwkv_k_a, rwkv_r_k,
                              rwkv_ln_g, rwkv_ln_b)
            o_a, o_b, new = _even_mixer(geo, xb, rope_tab, ev_w_in[li], mla_kv_norm[li], mla_w_uk[li], mla_w_uv[li], rw,
                                        cache_mla_ckv[li], cache_mla_krope[li], state_rwkv_shift[li], state_rwkv_wkv[li])
            w_out = ev_w_out[li]
        else:
            fx = {'q_norm': fox_q_norm[li], 'k_norm': fox_k_norm[li], 'f_bias': fox_f_bias[li]}
            gd = {'conv_w': gdn_conv_w[li], 'a_log': gdn_a_log[li], 'dt_bias': gdn_dt_bias[li], 'norm': gdn_norm[li]}
            o_a, o_b, new = _odd_mixer(geo, xb, od_w_in[li], fx, gd, cache_fox_k[li], cache_fox_v[li], cache_fox_logf[li],
                                       state_gdn_conv[li], state_gdn_wkv[li])
            w_out = od_w_out[li]
        for name, val in new.items():
            st.setdefault(name, []).append(val)
        xb, x_words = _mix_out(geo, x, o_a, o_b, w_out, ln1_g[layer], ln1_b[layer])
        last = layer == n_layers - 1
        ys = _moe_layer(geo, xb, xb, x_words, layer, moe_router[layer], moe_router_bias[layer], moe_w_gate, moe_w_up,
                        moe_w_down, moe_ws_gate[layer], moe_ws_up[layer], moe_ws_down[layer], ln2_g[layer], ln2_b[layer],
                        F32 if last else BF16, ((0, geo.mp), (geo.mp, geo.ms)) if last else ((0, geo.m),))
        x = xb = ys[0]
    names = ('ckv', 'kr', 'sh', 'rwkv', 'fk', 'fv', 'fl', 'cv', 'gdn')
    return ((ys[0].reshape(bp, tp, d), ys[1].reshape(bs, ts, d))
            + tuple(jnp.stack(st[n + '_p']) for n in names) + tuple(jnp.stack(st[n + '_s']) for n in names))
```

```python
import math

import jax
import jax.numpy as jnp
from jax import lax
from jax.experimental import pallas as pl
from jax.experimental.pallas import tpu as pltpu

F32 = jnp.float32
BF16 = jnp.bfloat16
HP = lax.Precision.HIGHEST

D_MODEL = 2048
DEPTH = 2
DN_ALPHA = float((2.0 * DEPTH) ** 0.25)
LN_EPS = 1e-5
RMS_EPS = 1e-6
CHUNK = 64
LANES = 128

MLA_HEADS, MLA_NOPE, MLA_ROPE, MLA_VDIM, MLA_KV_LORA = 8, 128, 64, 128, 512
MLA_QK = MLA_NOPE + MLA_ROPE
ROPE_THETA = 10000.0
RWKV_HEADS, RWKV_HD = 16, 64
RWKV_W = RWKV_HEADS * RWKV_HD
RWKV_DECAY_LORA, RWKV_A_LORA, RWKV_G_LORA = 96, 96, 256
RWKV_GN_EPS = 64e-5
RWKV_PCOLS = 3 * RWKV_W + 2 * LANES + RWKV_G_LORA
FOX_HEADS, FOX_HD = 8, 128
FOX_W = FOX_HEADS * FOX_HD
GDN_HEADS, GDN_HD, GDN_CONV = 8, 128, 4
GDN_W = GDN_HEADS * GDN_HD
GDN_QKV = 3 * GDN_W
N_EXPERTS, TOP_K, N_GROUPS, TOPK_GROUPS = 64, 6, 8, 4
EXPERT_DIM, SHARED_DIM = 512, 512
ROUTED_SCALE = 2.5
SCAN_CHUNK = 64
GDN_STREAMS = 4
RWKV_STREAMS = 2
FLASH_KEY_SUBTILE = 1024
NEG_BIG = -1e30
RWKV_PASSES = dict(a=1, x=1, inv=1, u=1, y=1, s=1)
GDN_PASSES = dict(a=1, x=1, inv=1, u=1, y=1, s=1)

VMEM_LIMIT_BYTES = 56 * 1024 * 1024
MM_MAX_COLS = 1792


def _cp(*sem):
    return pltpu.CompilerParams(dimension_semantics=("arbitrary",) * len(sem), vmem_limit_bytes=VMEM_LIMIT_BYTES)


def _dot(a, b, prec=None):
    return jnp.dot(a, b, precision=prec, preferred_element_type=F32)


def _dot_nt(a, b, prec=None):
    return lax.dot_general(a, b, (((1,), (1,)), ((), ())), precision=prec, preferred_element_type=F32)


def _sigmoid(x):
    return 1.0 / (1.0 + jnp.exp(-x))


def _softplus(x):
    return jnp.maximum(x, 0.0) + jnp.log(1.0 + jnp.exp(-jnp.abs(x)))


def _silu(x):
    return x * _sigmoid(x)


def _mm(pairs, *, tm, tn, epilogue, outs, aux=(), rows=None, name="mm"):
    row0, m = rows if rows is not None else (0, pairs[0][0].shape[0])
    n = pairs[0][1].shape[1]
    if tn is None:
        tn = n if n <= MM_MAX_COLS else max(c for c in range(LANES, MM_MAX_COLS + 1, LANES) if n % c == 0)
    outs = [(tn if ow is None else ow, dt) for ow, dt in outs]
    assert m % tm == 0 and row0 % tm == 0 and n % tn == 0, (m, row0, n, tm, tn)
    nj, ni = n // tn, m // tm
    rb0 = row0 // tm
    in_specs, args = [], []
    for a, w in pairs:
        k = a.shape[1]
        in_specs += [pl.BlockSpec((tm, k), lambda j, i: (rb0 + i, 0)), pl.BlockSpec((k, tn), lambda j, i: (0, j))]
        args += [a, w]
    for arr, kind in aux:
        if kind == 'row':
            in_specs.append(pl.BlockSpec((tm, arr.shape[1]), lambda j, i: (rb0 + i, 0)))
        elif kind == 'rowcol':
            in_specs.append(pl.BlockSpec((tm, tn), lambda j, i: (rb0 + i, j)))
        elif kind == 'row0':
            in_specs.append(pl.BlockSpec(arr.shape, lambda j, i: (0, 0)))
        else:
            in_specs.append(pl.BlockSpec((1, tn), lambda j, i: (0, j)))
        args.append(arr)
    out_shape = [jax.ShapeDtypeStruct((m, ow * nj), dt) for ow, dt in outs]
    out_specs = [pl.BlockSpec((tm, ow), lambda j, i: (i, j)) for ow, dt in outs]
    n_pairs, n_aux = len(pairs), len(aux)

    def body(*refs):
        acc = None
        for p in range(n_pairs):
            d = _dot(refs[2 * p][...], refs[2 * p + 1][...])
            acc = d if acc is None else acc + d
        res = epilogue(acc, *[r[...] for r in refs[2 * n_pairs:2 * n_pairs + n_aux]])
        for o_ref, val in zip(refs[2 * n_pairs + n_aux:], res):
            o_ref[...] = val.astype(o_ref.dtype)

    res = pl.pallas_call(body, grid=(nj, ni), in_specs=in_specs, out_specs=out_specs, out_shape=out_shape, name=name,
                         compiler_params=_cp("parallel", "parallel"))(*args)
    return res


def _mm_t(w_t, a, *, nb, t, row0, name="mm_t"):
    n, k = w_t.shape
    tm = next((c for c in (512, 384, 256, 128) if t % c == 0), t)
    assert t % tm == 0 and row0 % tm == 0
    nt = t // tm

    def body(w_ref, a_ref, o_ref):
        o_ref[...] = _dot_nt(w_ref[...], a_ref[...]).astype(o_ref.dtype)

    return pl.pallas_call(
        body, grid=(nb, nt), name=name,
        in_specs=[pl.BlockSpec((n, k), lambda b, i: (0, 0)), pl.BlockSpec((tm, k), lambda b, i: (row0 // tm + b * nt + i, 0))],
        out_specs=pl.BlockSpec((n, tm), lambda b, i: (b, i)),
        out_shape=jax.ShapeDtypeStruct((nb * n, t), BF16),
        compiler_params=_cp("parallel", "parallel"))(w_t, a)


def _rope_lanes(x, tab):
    c, s1, s2 = tab[:, :LANES], tab[:, LANES:2 * LANES], tab[:, 2 * LANES:]
    n = x.shape[1]
    lo = pltpu.roll(x, n - MLA_ROPE // 2, 1)
    hi = pltpu.roll(x, MLA_ROPE // 2, 1)
    if n > LANES:
        reps = n // LANES
        c, s1, s2 = (jnp.concatenate([t] * reps, axis=1) for t in (c, s1, s2))
    return x * c + lo * s1 + hi * s2


def _ep_rms(acc, g):
    y = acc * lax.rsqrt(jnp.mean(acc * acc, axis=-1, keepdims=True) + RMS_EPS) * g
    return y, y


def _ep_ln(acc, x, g, b):
    z = DN_ALPHA * x.astype(F32) + acc
    mu = jnp.mean(z, axis=-1, keepdims=True)
    zc = z - mu
    var = jnp.mean(zc * zc, axis=-1, keepdims=True)
    return zc * lax.rsqrt(var + LN_EPS) * g + b


def _flash(qs, ks, vt, *, nb, tq_total, tk_total, q_row0, q_off, tq, tk, chunked, nh, bias=None, gate=None,
           out_dtype=BF16, name="flash"):
    assert tq_total % tq == 0 and tk_total % tk == 0 and q_row0 % tq == 0
    nq, nk = tq_total // tq, tk_total // tk
    qrb0 = q_row0 // tq
    ts = tk if tk % FLASH_KEY_SUBTILE else min(tk, FLASH_KEY_SUBTILE)
    for qi_s in range(nq):
        last_s = min(nk - 1, (q_off + qi_s * tq + tq - 1) // tk)
        assert last_s * tk <= q_off + qi_s * tq + 1, "key tiles before the last needed one must be fully visible"
    n_q, n_k = len(qs), len(ks)
    has_bias, has_gate = bias is not None, gate is not None

    def last_k(qi):
        return jnp.minimum(nk - 1, (q_off + qi * tq + tq - 1) // tk)

    kblk = lambda qi, ki: jnp.minimum(ki, last_k(qi))
    q_map = lambda b, h, qi, ki: (qrb0 + b * nq + qi, h)
    in_specs, args = [], []
    for q in qs:
        in_specs.append(pl.BlockSpec((tq, LANES), q_map))
        args.append(q)
    for arr, c0, per_head in ks:
        if per_head:
            in_specs.append(pl.BlockSpec((tk, LANES), lambda b, h, qi, ki, c0=c0: (b * nk + kblk(qi, ki), c0 + h)))
        else:
            in_specs.append(pl.BlockSpec((tk, LANES), lambda b, h, qi, ki, c0=c0: (b * nk + kblk(qi, ki), c0)))
        args.append(arr)
    in_specs.append(pl.BlockSpec((LANES, tk), lambda b, h, qi, ki: (b * nh + h, kblk(qi, ki))))
    args.append(vt)
    if has_bias:
        in_specs.append(pl.BlockSpec((1, 1, 1, tq), lambda b, h, qi, ki: (b, h, 0, qi)))
        in_specs.append(pl.BlockSpec((1, 1, tk, 1), lambda b, h, qi, ki: (b, h, kblk(qi, ki), 0)))
        args += list(bias)
    if has_gate:
        in_specs.append(pl.BlockSpec((tq, LANES), q_map))
        args.append(gate)

    def body(*refs):
        q_refs = refs[:n_q]
        k_refs = refs[n_q:n_q + n_k]
        vt_ref = refs[n_q + n_k]
        pos = n_q + n_k + 1
        if has_bias:
            qb_ref, kb_ref = refs[pos], refs[pos + 1]
            pos += 2
        if has_gate:
            gate_ref = refs[pos]
            pos += 1
        o_ref, m_ref, l_ref, acc_ref = refs[pos:pos + 4]
        qi, ki = pl.program_id(2), pl.program_id(3)

        @pl.when(ki == 0)
        def _():
            m_ref[...] = jnp.full(m_ref.shape, NEG_BIG, F32)
            l_ref[...] = jnp.zeros(l_ref.shape, F32)
            acc_ref[...] = jnp.zeros(acc_ref.shape, F32)

        def step(masked):
            q_all = jnp.concatenate([q_ref[...] for q_ref in q_refs], axis=1) if n_q > 1 else q_refs[0][...]
            qpos = q_off + qi * tq + lax.broadcasted_iota(jnp.int32, (1, tq), 1)
            qlim = (qpos | (CHUNK - 1)) if chunked else qpos
            subs = [slice(j * ts, (j + 1) * ts) for j in range(tk // ts)]
            ss = []
            for j, rows in enumerate(subs):
                k_all = jnp.concatenate([k_ref[rows, :] for k_ref in k_refs], axis=1) if n_k > 1 else k_refs[0][rows, :]
                s = _dot_nt(k_all, q_all)
                if has_bias:
                    s = s + (qb_ref[0, 0] - kb_ref[0, 0, rows, :])
                if masked:
                    kpos = ki * tk + j * ts + lax.broadcasted_iota(jnp.int32, (ts, 1), 0)
                    s = jnp.where(kpos <= qlim, s, NEG_BIG)
                ss.append(s)
            m_old = m_ref[...]
            m_new = m_old
            for s in ss:
                m_new = jnp.maximum(m_new, jnp.max(s, axis=0, keepdims=True))
            alpha = jnp.exp(m_old - m_new)
            l_new = alpha * l_ref[...]
            acc_new = alpha * acc_ref[...]
            for rows, s in zip(subs, ss):
                p = jnp.exp(s - m_new)
                l_new = l_new + jnp.sum(p, axis=0, keepdims=True)
                acc_new = acc_new + _dot(vt_ref[:, rows], p.astype(BF16))
            m_ref[...] = m_new
            l_ref[...] = l_new
            acc_ref[...] = acc_new

        @pl.when(ki < last_k(qi))
        def _():
            step(False)

        @pl.when(ki == last_k(qi))
        def _():
            step(True)

        @pl.when(ki == nk - 1)
        def _():
            o = (acc_ref[...] / l_ref[...]).T
            if has_gate:
                o = o * gate_ref[...].astype(F32)
            o_ref[...] = o.astype(o_ref.dtype)

    return pl.pallas_call(
        body, grid=(nb, nh, nq, nk), in_specs=in_specs, name=name,
        out_specs=pl.BlockSpec((tq, LANES), lambda b, h, qi, ki: (b * nq + qi, h)),
        out_shape=jax.ShapeDtypeStruct((nb * tq_total, nh * LANES), out_dtype),
        scratch_shapes=[pltpu.VMEM((1, tq), F32), pltpu.VMEM((1, tq), F32), pltpu.VMEM((LANES, tq), F32)],
        compiler_params=_cp("parallel", "parallel", "parallel", "arbitrary"))(*args)


def _head_sum_matrices(n_heads, hd):
    lane = jnp.arange(n_heads * hd)[:, None] // hd
    e = (lane == jnp.arange(LANES)[None, :]).astype(F32)
    return e, e.T


def _rwkv_prep(cols, shift_prev, prm, *, nb, t, row0):
    tt = min(256, t)
    assert t % tt == 0 and row0 % tt == 0
    nt = t // tt
    rb0 = row0 // tt
    w = RWKV_W

    def body(c_ref, sp_ref, mu_ref, w0_ref, a0_ref, kk_ref, ka_ref, w2_ref, a2_ref, g2_ref, e_ref, et_ref,
             r_o, k_o, v_o, kk_o, b_o, lw_o, g_o, sh_o, carry):
        ti = pl.program_id(1)

        @pl.when(ti == 0)
        def _():
            carry[...] = sp_ref[0]

        c = c_ref[...]
        row = lax.broadcasted_iota(jnp.int32, (tt, 1), 0)
        prev = jnp.where(row == 0, carry[...], pltpu.roll(c, 1, 0))
        carry[...] = c[tt - 1:tt, :]
        sh_o[0] = c[tt - 1:tt, :]
        xs = c + (prev - c) * mu_ref[...]
        r, k, v = xs[:, :w], xs[:, w:2 * w], xs[:, 2 * w:3 * w]
        o1 = 3 * w
        lora = lambda u, w_ref: _dot(u.astype(BF16), w_ref[...].astype(BF16))
        w_raw = w0_ref[...] + lora(jnp.tanh(xs[:, o1:o1 + LANES]), w2_ref)
        log_w = -jnp.exp(-_softplus(-w_raw) - 0.5)
        a = _sigmoid(a0_ref[...] + lora(xs[:, o1 + LANES:o1 + 2 * LANES], a2_ref))
        g = lora(_sigmoid(xs[:, o1 + 2 * LANES:]), g2_ref)

        def head_dot(u, sel_ref):
            hi, lo = _split_bf16(u)
            sel = sel_ref[...].astype(BF16)
            return _dot(hi.astype(BF16), sel) + _dot(lo.astype(BF16), sel)

        kk = k * kk_ref[...]
        kk = kk * head_dot(lax.rsqrt(head_dot(kk * kk, e_ref) + 1e-6), et_ref)
        r_o[...] = r.astype(r_o.dtype)
        k_o[...] = (k * (1.0 + (a - 1.0) * ka_ref[...])).astype(k_o.dtype)
        v_o[...] = v.astype(v_o.dtype)
        kk_o[...] = kk.astype(kk_o.dtype)
        b_o[...] = (kk * a).astype(b_o.dtype)
        lw_o[...] = log_w
        g_o[...] = g.astype(g_o.dtype)

    e, et = _head_sum_matrices(RWKV_HEADS, RWKV_HD)
    tok = pl.BlockSpec((tt, w), lambda b, ti: (b * nt + ti, 0))
    full = lambda arr: pl.BlockSpec(arr.shape, lambda b, ti: (0,) * arr.ndim)
    params = [prm['mu'], prm['w0'], prm['a0'], prm['k_k'], prm['k_a'], prm['w2'], prm['a2'], prm['g2'], e, et]
    outs = pl.pallas_call(
        body, grid=(nb, nt), name="rwkv_prep",
        in_specs=[pl.BlockSpec((tt, RWKV_PCOLS), lambda b, ti: (rb0 + b * nt + ti, 0)),
                  pl.BlockSpec((1, 1, RWKV_PCOLS), lambda b, ti: (b, 0, 0))] + [full(p) for p in params],
        out_specs=[tok] * 7 + [pl.BlockSpec((1, 1, RWKV_PCOLS), lambda b, ti: (b, 0, 0))],
        out_shape=[jax.ShapeDtypeStruct((nb * t, w), dt) for dt in (BF16, BF16, BF16, BF16, BF16, F32, BF16)]
        + [jax.ShapeDtypeStruct((nb, 1, RWKV_PCOLS), F32)],
        scratch_shapes=[pltpu.VMEM((1, RWKV_PCOLS), F32)],
        compiler_params=_cp("parallel", "arbitrary"))(cols, shift_prev, *params)
    return outs


def _cumsum_rows(x, n):
    row = lax.broadcasted_iota(jnp.int32, (n, 1), 0)
    sh = 1
    while sh < n:
        x = x + jnp.where(row >= sh, pltpu.roll(x, sh, 0), 0.0)
        sh *= 2
    return x


def _split_bf16(x):
    hi = x.astype(BF16).astype(F32)
    return hi, x - hi


def _bmm(a, b, ca, cb, passes):
    dn = (((ca,), (cb,)), ((0,), (0,)))
    if passes == 6:
        return lax.dot_general(a, b, dn, precision=HP, preferred_element_type=F32)
    if passes == 3:
        ah, al = _split_bf16(a)
        bh, bl = _split_bf16(b)
        a = jnp.concatenate([ah, ah, al], axis=ca)
        b = jnp.concatenate([bh, bl, bh], axis=cb)
    return lax.dot_general(a.astype(BF16), b.astype(BF16), dn, preferred_element_type=F32)


def _heads(x, nh, hd):
    return jnp.stack([x[:, h * hd:(h + 1) * hd] for h in range(nh)], axis=0)


def _unheads(x):
    return jnp.concatenate([x[h] for h in range(x.shape[0])], axis=1)


def _unit_lower_inverse(nmat, n, passes):
    eye = (lax.broadcasted_iota(jnp.int32, (n, n), 0) == lax.broadcasted_iota(jnp.int32, (n, n), 1)).astype(F32)
    t = eye + nmat
    p = nmat
    steps = int(math.ceil(math.log2(n))) - 1
    for _ in range(steps):
        p = _bmm(p, p, 2, 1, passes)
        t = t + _bmm(p, t, 2, 1, passes)
    return t


def _rwkv_scan(r, k, v, kk, bvec, logw, g, s0, prm, *, nb, t):
    c = min(SCAN_CHUNK, t)
    ns = math.gcd(nb, RWKV_STREAMS)
    assert t % c == 0
    nc = t // c
    hd, nh = RWKV_HD, RWKV_HEADS

    def body(r_ref, k_ref, v_ref, kk_ref, b_ref, lw_ref, g_ref, s0_ref, rk_ref, lg_ref, lb_ref, o_ref, sf_ref, st):
        ci = pl.program_id(1)

        @pl.when(ci == 0)
        def _():
            for s in range(ns):
                st[s * nh:(s + 1) * nh] = s0_ref[s]

        hs = lambda a: _heads(a, nh, hd)
        parts = []
        for s in range(ns):
            lw = lw_ref[s]
            lc = _cumsum_rows(lw, c)
            rr, kx, vv, bb = (ref[s].astype(F32) for ref in (r_ref, k_ref, v_ref, b_ref))
            g_inv = jnp.exp(-lc)
            g_end = jnp.exp(lc[c - 1:c, :] - lc)
            parts.append(dict(
                at=hs(-kk_ref[s].astype(F32) * jnp.exp(lc - lw)), rt=hs(rr * jnp.exp(lc)),
                bt=hs(bb * g_inv), kt=hs(kx * g_inv), bg=hs(bb * g_end), kg=hs(kx * g_end),
                w_end=hs(jnp.exp(lc[c - 1:c, :])), v_h=hs(vv), rk=hs(rr * kx * rk_ref[...])))
        cat = lambda name: jnp.concatenate([p[name] for p in parts], axis=0)
        at, rt, bt, kt, bg, kg, w_end, v_h = (cat(n) for n in ('at', 'rt', 'bt', 'kt', 'bg', 'kg', 'w_end', 'v_h'))
        ri = lax.broadcasted_iota(jnp.int32, (c, c), 0)
        cj = lax.broadcasted_iota(jnp.int32, (c, c), 1)
        strict = ri > cj
        incl2 = lax.broadcasted_iota(jnp.int32, (c, 2 * c), 0) >= (lax.broadcasted_iota(jnp.int32, (c, 2 * c), 1) & (c - 1))
        s_all = st[...]
        pp = RWKV_PASSES
        lhs = jnp.concatenate([at, rt], axis=1)
        a_all = _bmm(lhs, jnp.concatenate([bt, kt], axis=1), 2, 2, pp['a'])
        x_all = _bmm(lhs, s_all, 2, 2, pp['x'])
        tinv = _unit_lower_inverse(jnp.where(strict, a_all[:, :c, :c], 0.0), c, pp['inv'])
        a_ak = jnp.where(strict, a_all[:, :c, c:], 0.0)
        u = _bmm(tinv, x_all[:, :c] + _bmm(a_ak, v_h, 2, 1, pp['u']), 2, 1, pp['u'])
        uv = jnp.concatenate([u, v_h], axis=1)
        y = x_all[:, c:] + _bmm(jnp.where(incl2, a_all[:, c:, :], 0.0), uv, 2, 1, pp['y'])
        st[...] = s_all * w_end + _bmm(uv, jnp.concatenate([bg, kg], axis=1), 1, 1, pp['s'])
        mu = jnp.mean(y, axis=-1, keepdims=True)
        yc = y - mu
        var = jnp.mean(yc * yc, axis=-1, keepdims=True)
        lng, lnb = hs(lg_ref[...]), hs(lb_ref[...])
        for s in range(ns):
            sl = slice(s * nh, (s + 1) * nh)
            yn = yc[sl] * lax.rsqrt(var[sl] + RWKV_GN_EPS) * lng + lnb
            bonus = jnp.sum(parts[s]['rk'], axis=-1, keepdims=True) * v_h[sl]
            o_ref[s] = (_unheads(yn + bonus) * g_ref[s].astype(F32)).astype(o_ref.dtype)

        @pl.when(ci == nc - 1)
        def _():
            for s in range(ns):
                sf_ref[s] = st[s * nh:(s + 1) * nh]

    as3 = lambda a: a.reshape(nb, t, RWKV_W)
    tok = pl.BlockSpec((ns, c, RWKV_W), lambda b, ci: (b, ci, 0))
    stt = pl.BlockSpec((ns, nh, hd, hd), lambda b, ci: (b, 0, 0, 0))
    par = pl.BlockSpec((1, RWKV_W), lambda b, ci: (0, 0))
    o, s_fin = pl.pallas_call(
        body, grid=(nb // ns, nc), name="rwkv_scan",
        in_specs=[tok] * 7 + [stt, par, par, par],
        out_specs=[tok, stt],
        out_shape=[jax.ShapeDtypeStruct((nb, t, RWKV_W), BF16), jax.ShapeDtypeStruct((nb, nh, hd, hd), F32)],
        scratch_shapes=[pltpu.VMEM((ns * nh, hd, hd), F32)],
        compiler_params=_cp("parallel", "arbitrary"))(*(as3(a) for a in (r, k, v, kk, bvec, logw, g)), s0,
                                                      prm['r_k'], prm['ln_g'], prm['ln_b'])
    return o.reshape(nb * t, RWKV_W), s_fin


class _Geom:
    def __init__(self, bp, tp, bs, ts, past):
        self.bp, self.tp, self.bs, self.ts, self.past = bp, tp, bs, ts, past
        self.mp, self.ms = bp * tp, bs * ts
        self.m = self.mp + self.ms
        self.tm = math.gcd(self.m, 512)
        assert self.tm % 16 == 0 and tp % CHUNK == 0 and past % CHUNK == 0 and ts == CHUNK
        self.tq_p = min(1024, tp)
        self.kpad = -(-(past + ts) // (3 * LANES)) * (3 * LANES)
        self.tk_s = self.kpad


def _pad_cols(w, n):
    return jnp.pad(w, ((0, 0), (0, n - w.shape[1])))


def _rope_table(geo):
    half = MLA_ROPE // 2
    pos = jnp.concatenate([jnp.tile(jnp.arange(geo.tp, dtype=jnp.int32), geo.bp),
                           geo.past + jnp.tile(jnp.arange(geo.ts, dtype=jnp.int32), geo.bs)])
    inv_freq = ROPE_THETA ** (-jnp.arange(half, dtype=F32) / half)
    ang = pos.astype(F32)[:, None] * inv_freq[None, :]
    cos, sin = jnp.cos(ang), jnp.sin(ang)
    z = jnp.zeros_like(cos)
    zz = jnp.zeros((pos.shape[0], LANES - MLA_ROPE), F32)
    return jnp.concatenate([cos, cos, zz, -sin, z, zz, z, sin, zz], axis=1)


def _rwkv_cols_layout(a):
    o1 = 3 * RWKV_W
    o2 = o1 + RWKV_DECAY_LORA
    o3 = o2 + RWKV_A_LORA
    pad = [(0, 0)] * (a.ndim - 1)
    return jnp.concatenate([a[..., :o1],
                            jnp.pad(a[..., o1:o2], pad + [(0, LANES - RWKV_DECAY_LORA)]),
                            jnp.pad(a[..., o2:o3], pad + [(0, LANES - RWKV_A_LORA)]),
                            a[..., o3:]], axis=-1)


def _rwkv_cols_unlayout(a):
    o1 = 3 * RWKV_W
    return jnp.concatenate([a[..., :o1], a[..., o1:o1 + RWKV_DECAY_LORA],
                            a[..., o1 + LANES:o1 + LANES + RWKV_A_LORA], a[..., o1 + 2 * LANES:]], axis=-1)


def _even_mixer(geo, xb, rope_tab, w_in, kv_norm, w_uk, w_uv, rw, cache_ckv, cache_kr, shift_prev, wkv0):
    tm, m, mp = geo.tm, geo.m, geo.mp
    d = w_in.shape[0]
    nq = MLA_HEADS * MLA_QK
    wq = w_in[:, :nq].reshape(d, MLA_HEADS, MLA_QK)
    w_qn = wq[:, :, :MLA_NOPE].reshape(d, MLA_HEADS * MLA_NOPE).astype(BF16)
    w_qr = jnp.pad(wq[:, :, MLA_NOPE:], ((0, 0), (0, 0), (0, LANES - MLA_ROPE))).reshape(d, MLA_HEADS * LANES).astype(BF16)
    w_ckv = w_in[:, nq:nq + MLA_KV_LORA].astype(BF16)
    w_kr = _pad_cols(w_in[:, nq + MLA_KV_LORA:nq + MLA_KV_LORA + MLA_ROPE], LANES).astype(BF16)
    w_rw = _rwkv_cols_layout(w_in[:, nq + MLA_KV_LORA + MLA_ROPE:]).astype(BF16)
    scale = MLA_QK ** -0.5

    (qn,) = _mm([(xb, w_qn)], tm=tm, tn=None, epilogue=lambda acc: (acc * scale,), outs=[(None, BF16)])
    (qr,) = _mm([(xb, w_qr)], tm=tm, tn=MLA_HEADS * LANES, aux=[(rope_tab, 'row')],
                epilogue=lambda acc, tab: (_rope_lanes(acc, tab) * scale,), outs=[(MLA_HEADS * LANES, BF16)])
    groups = ((0, mp), (mp, geo.ms))
    (ckv_p, ckvb_p), (ckv_s, ckvb_s) = (
        _mm([(xb, w_ckv)], tm=tm, tn=MLA_KV_LORA, aux=[(kv_norm.reshape(1, -1), 'col')], rows=g,
            epilogue=_ep_rms, outs=[(MLA_KV_LORA, F32), (MLA_KV_LORA, BF16)]) for g in groups)
    (kr_p, krb_p), (kr_s, krb_s) = (
        _mm([(xb, w_kr)], tm=tm, tn=LANES, aux=[(rope_tab, 'row')], rows=g,
            epilogue=lambda acc, tab: (_rope_lanes(acc, tab),) * 2, outs=[(LANES, F32), (LANES, BF16)]) for g in groups)
    (rwc,) = _mm([(xb, w_rw)], tm=tm, tn=None, epilogue=lambda acc: (acc,), outs=[(None, F32)])

    w_k = w_uk.reshape(MLA_KV_LORA, -1).astype(BF16)
    w_vt = w_uv.reshape(MLA_KV_LORA, -1).T.astype(BF16)
    (kn_p,) = _mm([(ckvb_p, w_k)], tm=tm, tn=None, epilogue=lambda acc: (acc,), outs=[(None, BF16)])
    vt_p = _mm_t(w_vt, ckvb_p, nb=geo.bp, t=geo.tp, row0=0)
    o_p = _flash([qn, qr], [(kn_p, 0, True), (krb_p, 0, False)], vt_p,
                 nb=geo.bp, tq_total=geo.tp, tk_total=geo.tp, q_row0=0, q_off=0, tq=geo.tq_p, tk=geo.tq_p,
                 chunked=True, nh=MLA_HEADS)
    padk = geo.kpad - geo.past - geo.ts
    ckv_all = jnp.concatenate([cache_ckv.astype(BF16), ckvb_s.reshape(geo.bs, geo.ts, -1),
                               jnp.zeros((geo.bs, padk, MLA_KV_LORA), BF16)], axis=1).reshape(geo.bs * geo.kpad, -1)
    kr_all = jnp.concatenate([jnp.pad(cache_kr, ((0, 0), (0, 0), (0, LANES - MLA_ROPE))).astype(BF16),
                              krb_s.reshape(geo.bs, geo.ts, -1),
                              jnp.zeros((geo.bs, padk, LANES), BF16)], axis=1).reshape(geo.bs * geo.kpad, -1)
    (kn_all,) = _mm([(ckv_all, w_k)], tm=3 * LANES, tn=None, epilogue=lambda acc: (acc,), outs=[(None, BF16)])
    vt_all = _mm_t(w_vt, ckv_all, nb=geo.bs, t=geo.kpad, row0=0)
    o_s = _flash([qn, qr], [(kn_all, 0, True), (kr_all, 0, False)], vt_all,
                 nb=geo.bs, tq_total=geo.ts, tk_total=geo.kpad, q_row0=mp, q_off=geo.past, tq=geo.ts, tk=geo.tk_s,
                 chunked=True, nh=MLA_HEADS)
    o_mla = jnp.concatenate([o_p, o_s], axis=0)

    outs_p = _rwkv_prep(rwc, jnp.zeros((geo.bp, 1, RWKV_PCOLS), F32), rw, nb=geo.bp, t=geo.tp, row0=0)
    outs_s = _rwkv_prep(rwc, _rwkv_cols_layout(shift_prev)[:, None, :], rw, nb=geo.bs, t=geo.ts, row0=mp)
    y_p, wkv_p = _rwkv_scan(*outs_p[:7], jnp.zeros((geo.bp, RWKV_HEADS, RWKV_HD, RWKV_HD), F32), rw, nb=geo.bp, t=geo.tp)
    y_s, wkv_s = _rwkv_scan(*outs_s[:7], wkv0, rw, nb=geo.bs, t=geo.ts)
    o_rwkv = jnp.concatenate([y_p, y_s], axis=0)
    states = dict(
        ckv_p=ckv_p.reshape(geo.bp, geo.tp, -1), ckv_s=ckv_s.reshape(geo.bs, geo.ts, -1),
        kr_p=kr_p[:, :MLA_ROPE].reshape(geo.bp, geo.tp, -1), kr_s=kr_s[:, :MLA_ROPE].reshape(geo.bs, geo.ts, -1),
        sh_p=_rwkv_cols_unlayout(outs_p[7][:, 0]), sh_s=_rwkv_cols_unlayout(outs_s[7][:, 0]),
        rwkv_p=wkv_p, rwkv_s=wkv_s)
    return o_mla, o_rwkv, states


def _head_rms(x, g, nh, scale=1.0):
    outs = []
    for h in range(nh):
        seg = x[:, h * LANES:(h + 1) * LANES]
        outs.append(seg * lax.rsqrt(jnp.mean(seg * seg, axis=-1, keepdims=True) + RMS_EPS) * (g * scale))
    return jnp.concatenate(outs, axis=1)


def _head_l2(x, nh, scale=1.0):
    outs = []
    for h in range(nh):
        seg = x[:, h * LANES:(h + 1) * LANES]
        outs.append(seg * (lax.rsqrt(jnp.sum(seg * seg, axis=-1, keepdims=True) + 1e-6) * scale))
    return jnp.concatenate(outs, axis=1)


def _cumsum_time(x):
    nb, t, w = x.shape

    def body(x_ref, o_ref):
        o_ref[0] = _cumsum_rows(x_ref[0], t)

    spec = pl.BlockSpec((1, t, w), lambda b: (b, 0, 0))
    return pl.pallas_call(body, grid=(nb,), in_specs=[spec], out_specs=spec, name="cumsum_time",
                          out_shape=jax.ShapeDtypeStruct(x.shape, F32), compiler_params=_cp("parallel"))(x)


def _gdn_prep(cols, conv_prev, conv_w, *, nb, t, row0):
    tt = min(256, t)
    assert t % tt == 0 and row0 % tt == 0
    nt = t // tt
    rb0 = row0 // tt
    w = GDN_W

    def body(c_ref, cp_ref, cw_ref, q_o, k_o, v_o, carry):
        ti = pl.program_id(1)

        @pl.when(ti == 0)
        def _():
            carry[...] = cp_ref[0]

        c = c_ref[...]
        ext = jnp.concatenate([carry[...], c], axis=0)
        carry[...] = c[tt - 8:tt, :]
        cw = cw_ref[...]
        acc = c * cw[GDN_CONV - 1:GDN_CONV, :]
        for j in range(GDN_CONV - 1):
            back = GDN_CONV - 1 - j
            acc = acc + ext[8 - back:8 - back + tt, :] * cw[j:j + 1, :]
        qkv = _silu(acc)
        q_o[...] = _head_l2(qkv[:, :w], GDN_HEADS, GDN_HD ** -0.5).astype(q_o.dtype)
        k_o[...] = _head_l2(qkv[:, w:2 * w], GDN_HEADS).astype(k_o.dtype)
        v_o[...] = qkv[:, 2 * w:].astype(v_o.dtype)

    tok = pl.BlockSpec((tt, w), lambda b, ti: (b * nt + ti, 0))
    return pl.pallas_call(
        body, grid=(nb, nt), name="gdn_prep",
        in_specs=[pl.BlockSpec((tt, GDN_QKV), lambda b, ti: (rb0 + b * nt + ti, 0)),
                  pl.BlockSpec((1, 8, GDN_QKV), lambda b, ti: (b, 0, 0)),
                  pl.BlockSpec((8, GDN_QKV), lambda b, ti: (0, 0))],
        out_specs=[tok] * 3,
        out_shape=[jax.ShapeDtypeStruct((nb * t, w), BF16)] * 3,
        scratch_shapes=[pltpu.VMEM((8, GDN_QKV), F32)],
        compiler_params=_cp("parallel", "arbitrary"))(cols, conv_prev, conv_w)


def _gdn_scan(q, k, v, gb, z, s0, norm_g, *, nb, t, gb_row0, z_row0):
    c = min(SCAN_CHUNK, t)
    ns = math.gcd(nb, GDN_STREAMS)
    assert t % c == 0 and gb_row0 % c == 0 and z_row0 % c == 0
    nc = t // c
    nh, hd = GDN_HEADS, GDN_HD

    def body(*refs):
        q_ref, k_ref, v_ref = refs[:3]
        gb_refs, z_refs = refs[3:3 + ns], refs[3 + ns:3 + 2 * ns]
        s0_ref, ng_ref, o_ref, sf_ref, st = refs[3 + 2 * ns:]
        ci = pl.program_id(1)

        @pl.when(ci == 0)
        def _():
            for g in range(ns):
                st[g * nh:(g + 1) * nh] = s0_ref[g]

        ri = lax.broadcasted_iota(jnp.int32, (c, c), 0)
        cj = lax.broadcasted_iota(jnp.int32, (c, c), 1)
        tril, strict = ri >= cj, ri > cj
        gcols, grows, bcols = [], [], []
        for g in range(ns):
            gbv = gb_refs[g][...]
            gc = _cumsum_rows(gbv, c)
            gct = gc.T
            gcols.append(jnp.stack([gc[:, h:h + 1] for h in range(nh)], axis=0))
            grows.append(gct[:nh][:, None, :])
            bcols.append(jnp.stack([gbv[:, nh + h:nh + h + 1] for h in range(nh)], axis=0))
        gcol, grow, bcol = (jnp.concatenate(parts, axis=0) for parts in (gcols, grows, bcols))
        hs = lambda ref: jnp.concatenate([_heads(ref[g].astype(F32), nh, hd) for g in range(ns)], axis=0)
        k_h, q_h, v_h = hs(k_ref), hs(q_ref), hs(v_ref)
        decay = jnp.where(tril, jnp.exp(jnp.where(tril, gcol - grow, 0.0)), 0.0)
        pp = GDN_PASSES
        kb = k_h * bcol
        prods = _bmm(jnp.concatenate([kb, q_h], axis=1), k_h, 2, 2, pp['a'])
        low = jnp.where(strict, prods[:, :c] * decay, 0.0)
        a_qk = jnp.where(tril, prods[:, c:] * decay, 0.0)
        tinv = _unit_lower_inverse(-low, c, pp['inv'])
        e_g = jnp.exp(gcol)
        uw = _bmm(tinv, jnp.concatenate([v_h * bcol, kb * e_g], axis=2), 2, 1, pp['u'])
        s_all = st[...]
        ws_qs = _bmm(jnp.concatenate([uw[:, :, hd:], q_h * e_g], axis=1), s_all, 2, 1, pp['x'])
        v_new = uw[:, :, :hd] - ws_qs[:, :c]
        o = ws_qs[:, c:] + _bmm(a_qk, v_new, 2, 1, pp['y'])
        g_last = gcol[:, c - 1:c, :]
        st[...] = s_all * jnp.exp(g_last) + _bmm(k_h * jnp.exp(g_last - gcol), v_new, 1, 1, pp['s'])
        o = o * lax.rsqrt(jnp.mean(o * o, axis=-1, keepdims=True) + RMS_EPS) * ng_ref[...]
        for g in range(ns):
            o_ref[g] = (_unheads(o[g * nh:(g + 1) * nh]) * _silu(z_refs[g][...].astype(F32))).astype(o_ref.dtype)

        @pl.when(ci == nc - 1)
        def _():
            for g in range(ns):
                sf_ref[g] = st[g * nh:(g + 1) * nh]

    as3 = lambda a: a.reshape(nb, t, GDN_W)
    tok = pl.BlockSpec((ns, c, GDN_W), lambda b, ci: (b, ci, 0))
    stt = pl.BlockSpec((ns, nh, hd, hd), lambda b, ci: (b, 0, 0, 0))
    row_spec = lambda width, row0, g: pl.BlockSpec((c, width), lambda b, ci: (row0 // c + (b * ns + g) * nc + ci, 0))
    o, s_fin = pl.pallas_call(
        body, grid=(nb // ns, nc), name="gdn_scan",
        in_specs=[tok] * 3 + [row_spec(LANES, gb_row0, g) for g in range(ns)]
        + [row_spec(GDN_W, z_row0, g) for g in range(ns)] + [stt, pl.BlockSpec((1, hd), lambda b, ci: (0, 0))],
        out_specs=[tok, stt],
        out_shape=[jax.ShapeDtypeStruct((nb, t, GDN_W), BF16), jax.ShapeDtypeStruct((nb, nh, hd, hd), F32)],
        scratch_shapes=[pltpu.VMEM((ns * nh, hd, hd), F32)],
        compiler_params=_cp("parallel", "arbitrary"))(as3(q), as3(k), as3(v), *([gb] * ns), *([z] * ns), s0, norm_g)
    return o.reshape(nb * t, GDN_W), s_fin


def _odd_mixer(geo, xb, w_in, fx, gd, cache_k, cache_v, cache_logf, conv_prev, gdn0):
    tm, m, mp = geo.tm, geo.m, geo.mp
    fw = FOX_W
    o_f = 4 * fw
    o_g = o_f + FOX_HEADS
    w_q, w_k, w_v, w_gate = (w_in[:, i * fw:(i + 1) * fw].astype(BF16) for i in range(4))
    w_f = _pad_cols(w_in[:, o_f:o_g], LANES).astype(BF16)
    w_qkv = w_in[:, o_g:o_g + GDN_QKV].astype(BF16)
    w_ab = _pad_cols(w_in[:, o_g + GDN_QKV:o_g + GDN_QKV + 2 * GDN_HEADS], LANES).astype(BF16)
    w_z = w_in[:, o_g + GDN_QKV + 2 * GDN_HEADS:].astype(BF16)
    qn, kn = fx['q_norm'].reshape(1, -1), fx['k_norm'].reshape(1, -1)
    f_bias = jnp.pad(fx['f_bias'], (0, LANES - FOX_HEADS)).reshape(1, -1)
    scale = FOX_HD ** -0.5

    (q,) = _mm([(xb, w_q)], tm=tm, tn=fw, aux=[(qn, 'row0')],
               epilogue=lambda acc, g: (_head_rms(acc, g, FOX_HEADS, scale),), outs=[(fw, BF16)])
    groups = ((0, mp), (mp, geo.ms))
    (fk_p, kb_p), (fk_s, kb_s) = (
        _mm([(xb, w_k)], tm=tm, tn=fw, aux=[(kn, 'row0')], rows=g,
            epilogue=lambda acc, g_: (_head_rms(acc, g_, FOX_HEADS),) * 2, outs=[(fw, F32), (fw, BF16)]) for g in groups)
    (fv_p,), (fv_s,) = (_mm([(xb, w_v)], tm=tm, tn=None, rows=g, epilogue=lambda acc: (acc,), outs=[(None, F32)])
                      for g in groups)
    w_vt = w_v.T
    vt_p = _mm_t(w_vt, xb, nb=geo.bp, t=geo.tp, row0=0)
    vt_s = _mm_t(w_vt, xb, nb=geo.bs, t=geo.ts, row0=mp)
    (gate,) = _mm([(xb, w_gate)], tm=tm, tn=None, epilogue=lambda acc: (_sigmoid(acc),), outs=[(None, BF16)])
    (logf,) = _mm([(xb, w_f)], tm=tm, tn=LANES, aux=[(f_bias, 'col')],
                  epilogue=lambda acc, fb: (-_softplus(-(acc + fb)),), outs=[(LANES, F32)])

    cum_p = _cumsum_time(logf[:mp].reshape(geo.bp, geo.tp, LANES))[:, :, :FOX_HEADS]
    bias_p = (jnp.transpose(cum_p, (0, 2, 1))[:, :, None, :], jnp.transpose(cum_p, (0, 2, 1))[..., None])
    o_p = _flash([q], [(kb_p, 0, True)], vt_p, nb=geo.bp, tq_total=geo.tp, tk_total=geo.tp, q_row0=0, q_off=0,
                 tq=geo.tq_p, tk=geo.tq_p, chunked=False, nh=FOX_HEADS, bias=bias_p, gate=gate)
    padk = geo.kpad - geo.past - geo.ts
    lf_all = jnp.concatenate([jnp.pad(cache_logf, ((0, 0), (0, 0), (0, LANES - FOX_HEADS))),
                              logf[mp:].reshape(geo.bs, geo.ts, LANES), jnp.zeros((geo.bs, padk, LANES), F32)], axis=1)
    cum_s = jnp.transpose(_cumsum_time(lf_all)[:, :, :FOX_HEADS], (0, 2, 1))
    bias_s = (cum_s[:, :, None, geo.past:geo.past + geo.ts], cum_s[..., None])
    k_all = jnp.concatenate([cache_k.reshape(geo.bs, geo.past, fw).astype(BF16), kb_s.reshape(geo.bs, geo.ts, fw),
                             jnp.zeros((geo.bs, padk, fw), BF16)], axis=1).reshape(geo.bs * geo.kpad, fw)
    vt_all = jnp.concatenate([jnp.transpose(cache_v.reshape(geo.bs, geo.past, fw), (0, 2, 1)).astype(BF16),
                              vt_s.reshape(geo.bs, fw, geo.ts), jnp.zeros((geo.bs, fw, padk), BF16)],
                             axis=2).reshape(geo.bs * fw, geo.kpad)
    o_s = _flash([q], [(k_all, 0, True)], vt_all, nb=geo.bs, tq_total=geo.ts, tk_total=geo.kpad, q_row0=mp,
                 q_off=geo.past, tq=geo.ts, tk=geo.tk_s, chunked=False, nh=FOX_HEADS, bias=bias_s, gate=gate)
    o_fox = jnp.concatenate([o_p, o_s], axis=0)

    (qkv_raw,) = _mm([(xb, w_qkv)], tm=tm, tn=None, epilogue=lambda acc: (acc,), outs=[(None, F32)])
    lane = jnp.arange(LANES)
    neg_a = jnp.where(lane < GDN_HEADS, -jnp.exp(jnp.pad(gd['a_log'], (0, LANES - GDN_HEADS))), 0.0).reshape(1, -1)
    dtb = jnp.pad(gd['dt_bias'], (0, LANES - GDN_HEADS)).reshape(1, -1)
    is_g = (lane < GDN_HEADS).astype(F32).reshape(1, -1)
    (gb,) = _mm([(xb, w_ab)], tm=tm, tn=LANES, aux=[(neg_a, 'col'), (dtb, 'col'), (is_g, 'col')],
                epilogue=lambda acc, na, db, ig: (jnp.where(ig > 0.5, na * _softplus(acc + db), _sigmoid(acc)),),
                outs=[(LANES, F32)])
    (z,) = _mm([(xb, w_z)], tm=tm, tn=None, epilogue=lambda acc: (acc,), outs=[(None, BF16)])
    conv_w = jnp.pad(gd['conv_w'], ((0, 8 - GDN_CONV), (0, 0)))
    norm_g = gd['norm'].reshape(1, -1)
    prev8 = lambda a: jnp.pad(a, ((0, 0), (8 - (GDN_CONV - 1), 0), (0, 0)))
    q_p, k_p, v_p = _gdn_prep(qkv_raw, jnp.zeros((geo.bp, 8, GDN_QKV), F32), conv_w, nb=geo.bp, t=geo.tp, row0=0)
    q_s, k_s, v_s = _gdn_prep(qkv_raw, prev8(conv_prev), conv_w, nb=geo.bs, t=geo.ts, row0=mp)
    y_p, s_p = _gdn_scan(q_p, k_p, v_p, gb, z, jnp.zeros((geo.bp, GDN_HEADS, GDN_HD, GDN_HD), F32), norm_g,
                         nb=geo.bp, t=geo.tp, gb_row0=0, z_row0=0)
    y_s, s_s = _gdn_scan(q_s, k_s, v_s, gb, z, gdn0, norm_g, nb=geo.bs, t=geo.ts, gb_row0=mp, z_row0=mp)
    o_gdn = jnp.concatenate([y_p, y_s], axis=0)
    nconv = GDN_CONV - 1
    last_rows = lambda row0, nb, t: (row0 + jnp.arange(nb)[:, None] * t + (t - nconv + jnp.arange(nconv))[None, :]).reshape(-1)
    states = dict(
        fk_p=fk_p.reshape(geo.bp, geo.tp, FOX_HEADS, FOX_HD), fk_s=fk_s.reshape(geo.bs, geo.ts, FOX_HEADS, FOX_HD),
        fv_p=fv_p.reshape(geo.bp, geo.tp, FOX_HEADS, FOX_HD), fv_s=fv_s.reshape(geo.bs, geo.ts, FOX_HEADS, FOX_HD),
        fl_p=logf[:mp, :FOX_HEADS].reshape(geo.bp, geo.tp, FOX_HEADS),
        fl_s=logf[mp:, :FOX_HEADS].reshape(geo.bs, geo.ts, FOX_HEADS),
        cv_p=jnp.take(qkv_raw, last_rows(0, geo.bp, geo.tp), axis=0).reshape(geo.bp, nconv, GDN_QKV),
        cv_s=jnp.take(qkv_raw, last_rows(mp, geo.bs, geo.ts), axis=0).reshape(geo.bs, nconv, GDN_QKV),
        gdn_p=s_p, gdn_s=s_s)
    return o_fox, o_gdn, states


MOE_BLOCK = 512
MOE_COMBINE_ROWS = 256


def _moe_route(x, router, router_bias, *, tm):
    m = x.shape[0]
    assert m % tm == 0
    per_group = N_EXPERTS // N_GROUPS

    def lane_max(v):
        return jnp.max(v, axis=-1, keepdims=True)

    def first_lane(mask, lane):
        return jnp.min(jnp.where(mask, lane.astype(F32), float(LANES)), axis=-1, keepdims=True).astype(jnp.int32)

    def group_all(v, lane, op):
        sh = 1
        while sh < per_group:
            partner = jnp.where((lane & sh) == 0, pltpu.roll(v, LANES - sh, 1), pltpu.roll(v, sh, 1))
            v = op(v, partner)
            sh *= 2
        return v

    def body(x_ref, r_ref, b_ref, e_o, g_o, k_o, cnt_o, carry):
        i = pl.program_id(0)

        @pl.when(i == 0)
        def _():
            carry[...] = jnp.zeros(carry.shape, F32)

        lane = lax.broadcasted_iota(jnp.int32, (tm, LANES), 1)
        valid = lane < N_EXPERTS
        neg = -jnp.inf
        scores = _sigmoid(_dot(x_ref[...], r_ref[...]))
        biased = jnp.where(valid, scores + b_ref[...], neg)
        m1 = group_all(biased, lane, jnp.maximum)
        first = group_all(jnp.where(biased == m1, lane, LANES), lane, jnp.minimum)
        m2 = group_all(jnp.where(lane == first, neg, biased), lane, jnp.maximum)
        grp = jnp.where(valid & ((lane & (per_group - 1)) == 0), m1 + m2, neg)
        emask = jnp.zeros((tm, LANES), jnp.bool_)
        for _ in range(TOPK_GROUPS):
            idx = first_lane(grp == lane_max(grp), lane)
            emask = emask | ((lane - idx >= 0) & (lane - idx < per_group))
            grp = jnp.where(lane == idx, neg, grp)
        cur = jnp.where(emask & valid, biased, neg)
        picks = []
        sel = jnp.zeros((tm, LANES), jnp.bool_)
        for _ in range(TOP_K):
            idx = first_lane(cur == lane_max(cur), lane)
            pick = lane == idx
            picks.append((idx, pick))
            sel = sel | pick
            cur = jnp.where(pick, neg, cur)
        selw = jnp.where(sel, scores, 0.0)
        gates_dense = selw / jnp.sum(selw, axis=-1, keepdims=True) * ROUTED_SCALE
        ri = lax.broadcasted_iota(jnp.int32, (tm, tm), 0)
        cj = lax.broadcasted_iota(jnp.int32, (tm, tm), 1)
        p01 = jnp.where(sel, 1.0, 0.0)
        before = _dot(jnp.where(ri > cj, 1.0, 0.0).astype(BF16), p01.astype(BF16)) + carry[...]
        carry[...] = carry[...] + jnp.sum(p01, axis=0, keepdims=True)
        cnt_o[...] = carry[...]
        e_out = jnp.zeros((tm, LANES), jnp.int32)
        g_out = jnp.zeros((tm, LANES), F32)
        k_out = jnp.zeros((tm, LANES), jnp.int32)
        for j, (idx, pick) in enumerate(picks):
            e_out = jnp.where(lane == j, idx, e_out)
            g_out = jnp.where(lane == j, jnp.sum(jnp.where(pick, gates_dense, 0.0), axis=-1, keepdims=True), g_out)
            rank = jnp.sum(jnp.where(pick, before, 0.0), axis=-1, keepdims=True)
            k_out = jnp.where(lane == j, rank.astype(jnp.int32), k_out)
        e_o[...] = e_out
        g_o[...] = g_out
        k_o[...] = k_out

    tok = pl.BlockSpec((tm, LANES), lambda i: (i, 0))
    one = pl.BlockSpec((1, LANES), lambda i: (0, 0))
    return pl.pallas_call(
        body, grid=(m // tm,), name="moe_route",
        in_specs=[pl.BlockSpec((tm, x.shape[1]), lambda i: (i, 0)), pl.BlockSpec(router.shape, lambda i: (0, 0)), one],
        out_specs=[tok, tok, tok, one],
        out_shape=[jax.ShapeDtypeStruct((m, LANES), jnp.int32), jax.ShapeDtypeStruct((m, LANES), F32),
                   jax.ShapeDtypeStruct((m, LANES), jnp.int32), jax.ShapeDtypeStruct((1, LANES), F32)],
        scratch_shapes=[pltpu.VMEM((1, LANES), F32)],
        compiler_params=_cp("arbitrary"))(x, router, router_bias)


def _moe_dispatch(x_words, dest, cap, *, tb):
    m, wd = x_words.shape
    nt = m // tb

    def body(dest_ref, x_ref, out_ref, sem):
        def row_copy(i, j):
            return pltpu.make_async_copy(x_ref.at[pl.ds(i, 1)], out_ref.at[pl.ds(dest_ref[0, 0, i * TOP_K + j], 1)], sem)

        def issue(i, carry):
            for j in range(TOP_K):
                row_copy(i, j).start()
            return carry

        lax.fori_loop(0, tb, issue, 0, unroll=8)
        for _ in range(TOP_K):
            pltpu.make_async_copy(x_ref, out_ref.at[pl.ds(0, tb)], sem).wait()

    return pl.pallas_call(
        body, grid=(nt,), name="moe_dispatch",
        in_specs=[pl.BlockSpec((1, 1, tb * TOP_K), lambda i: (i, 0, 0), memory_space=pltpu.SMEM),
                  pl.BlockSpec((tb, wd), lambda i: (i, 0))],
        out_specs=pl.BlockSpec(memory_space=pl.ANY),
        out_shape=jax.ShapeDtypeStruct((cap, wd), x_words.dtype),
        scratch_shapes=[pltpu.SemaphoreType.DMA(())],
        compiler_params=_cp("arbitrary"))(dest, x_words)


def _pack_halves(y):
    n = y.shape[1] // 2
    bits = lambda a: lax.bitcast_convert_type(a.astype(BF16).astype(F32), jnp.uint32)
    return (bits(y[:, :n]) >> 16) | (bits(y[:, n:]) & jnp.uint32(0xFFFF0000))


def _unpack_halves(w):
    lo = lax.bitcast_convert_type(w << 16, F32).astype(BF16)
    hi = lax.bitcast_convert_type(w & jnp.uint32(0xFFFF0000), F32).astype(BF16)
    return lo, hi


def _ep_ln_packed(acc, x, g, b):
    y = _ep_ln(acc, x, g, b)
    return y, _pack_halves(y)


def _moe_experts(xs_words, blk_exp, blk_next, blk_rows, n_used, w_gate, w_up, w_down, layer):
    cap, half = xs_words.shape
    d = 2 * half
    nblk = cap // MOE_BLOCK
    ed = w_gate.shape[3]

    def body(be_ref, bn_ref, br_ref, nu_ref, x_ref, wg_hbm, wu_hbm, wd_hbm, o_ref, wg_f, wu_f, wd_f, wg_b, wu_b, wd_b,
             slot_ref, sem):
        i = pl.program_id(0)

        def fetch(e, slot):
            return [pltpu.make_async_copy(src.at[layer, e], dst.at[slot], sem.at[slot])
                    for src, dst in ((wg_hbm, wg_f), (wu_hbm, wu_f), (wd_hbm, wd_f))]

        @pl.when(i < nu_ref[0])
        def _():
            first = i == 0
            e = be_ref[i]

            @pl.when(first)
            def _():
                slot_ref[0] = 0
                for cp in fetch(e, 0):
                    cp.start()

            @pl.when(first | (e != be_ref[jnp.maximum(i - 1, 0)]))
            def _():
                slot = jnp.where(first, 0, 1 - slot_ref[0])
                slot_ref[0] = slot
                for cp in fetch(e, slot):
                    cp.wait()
                wg_b[...] = wg_f[slot].astype(BF16)
                wu_b[...] = wu_f[slot].astype(BF16)
                wd_b[...] = wd_f[slot].astype(BF16)

                @pl.when(bn_ref[i] != e)
                def _():
                    for cp in fetch(bn_ref[i], 1 - slot):
                        cp.start()

            row = lax.broadcasted_iota(jnp.int32, (MOE_BLOCK, 1), 0)
            lo, hi = _unpack_halves(jnp.where(row < br_ref[i], x_ref[...], jnp.uint32(0)))
            hg = _dot(lo, wg_b[:half]) + _dot(hi, wg_b[half:])
            hu = _dot(lo, wu_b[:half]) + _dot(hi, wu_b[half:])
            o_ref[...] = _pack_halves(_dot((_silu(hg) * hu).astype(BF16), wd_b[...]))

    blk = lambda i, be, bn, br, nu: (jnp.minimum(i, nu[0] - 1), 0)
    hbm = pl.BlockSpec(memory_space=pl.ANY)
    grid_spec = pltpu.PrefetchScalarGridSpec(
        num_scalar_prefetch=4, grid=(nblk,),
        in_specs=[pl.BlockSpec((MOE_BLOCK, half), blk), hbm, hbm, hbm],
        out_specs=pl.BlockSpec((MOE_BLOCK, half), blk),
        scratch_shapes=[pltpu.VMEM((2, d, ed), F32), pltpu.VMEM((2, d, ed), F32), pltpu.VMEM((2, ed, d), F32),
                        pltpu.VMEM((d, ed), BF16), pltpu.VMEM((d, ed), BF16), pltpu.VMEM((ed, d), BF16),
                        pltpu.SMEM((1,), jnp.int32), pltpu.SemaphoreType.DMA((2,))])
    return pl.pallas_call(body, grid_spec=grid_spec, out_shape=jax.ShapeDtypeStruct((cap, half), jnp.uint32), name="moe_experts",
                          compiler_params=_cp("arbitrary"))(blk_exp, blk_next, blk_rows, n_used, xs_words, w_gate, w_up, w_down)


def _moe_combine(ys_words, dest, gates, *, tc):
    cap, half = ys_words.shape
    d = 2 * half
    m = gates.shape[0]
    nt = m // tc

    def body(dest0_ref, destn_ref, g_ref, y_ref, o_ref, buf, sem):
        i = pl.program_id(0)

        def issue(dref, slot):
            def one(r, carry):
                for j in range(TOP_K):
                    pltpu.make_async_copy(y_ref.at[pl.ds(dref[0, 0, r * TOP_K + j], 1)],
                                          buf.at[slot, j, pl.ds(r, 1)], sem.at[slot]).start()
                return carry
            lax.fori_loop(0, tc, one, 0, unroll=8)

        @pl.when(i == 0)
        def _():
            issue(dest0_ref, 0)

        @pl.when(i + 1 < nt)
        def _():
            issue(destn_ref, (i + 1) % 2)

        slot = i % 2
        for j in range(TOP_K):
            pltpu.make_async_copy(y_ref.at[pl.ds(0, tc)], buf.at[slot, j], sem.at[slot]).wait()
        g = g_ref[...]
        acc_lo = acc_hi = None
        for j in range(TOP_K):
            lo, hi = _unpack_halves(buf[slot, j])
            gj = g[:, j:j + 1]
            acc_lo = lo.astype(F32) * gj if acc_lo is None else acc_lo + lo.astype(F32) * gj
            acc_hi = hi.astype(F32) * gj if acc_hi is None else acc_hi + hi.astype(F32) * gj
        o_ref[:, :half] = acc_lo
        o_ref[:, half:] = acc_hi

    return pl.pallas_call(
        body, grid=(nt,), name="moe_combine",
        in_specs=[pl.BlockSpec((1, 1, tc * TOP_K), lambda i: (0, 0, 0), memory_space=pltpu.SMEM),
                  pl.BlockSpec((1, 1, tc * TOP_K), lambda i: (jnp.minimum(i + 1, nt - 1), 0, 0), memory_space=pltpu.SMEM),
                  pl.BlockSpec((tc, LANES), lambda i: (i, 0)),
                  pl.BlockSpec(memory_space=pl.ANY)],
        out_specs=pl.BlockSpec((tc, d), lambda i: (i, 0)),
        out_shape=jax.ShapeDtypeStruct((m, d), F32),
        scratch_shapes=[pltpu.VMEM((2, TOP_K, tc, half), jnp.uint32), pltpu.SemaphoreType.DMA((2,))],
        compiler_params=_cp("arbitrary"))(dest, dest, gates, ys_words)


def _moe_layer(geo, x_res, xb, x_words, layer, router, router_bias, w_gate, w_up, w_down, ws_gate, ws_up, ws_down,
               ln_g, ln_b, out_dtype, out_rows):
    m, d = x_res.shape
    tm = geo.tm
    eidx, gates, rank, counts = _moe_route(xb, _pad_cols(router, LANES).astype(BF16),
                                           jnp.pad(router_bias, (0, LANES - N_EXPERTS)).reshape(1, -1), tm=tm)
    cnt = counts[0, :N_EXPERTS].astype(jnp.int32)
    padded = (cnt + MOE_BLOCK - 1) // MOE_BLOCK * MOE_BLOCK
    pad_end = jnp.cumsum(padded)
    pad_start = pad_end - padded
    n_blocks = -(-(m * TOP_K) // MOE_BLOCK) + N_EXPERTS
    cap = n_blocks * MOE_BLOCK
    blk_row0 = jnp.arange(n_blocks, dtype=jnp.int32) * MOE_BLOCK
    blk_exp = jnp.minimum(jnp.sum((pad_end[None, :] <= blk_row0[:, None]).astype(jnp.int32), axis=1), N_EXPERTS - 1)
    blk_rows = jnp.clip(cnt[blk_exp] - (blk_row0 - pad_start[blk_exp]), 0, MOE_BLOCK).astype(jnp.int32)
    n_used = (pad_end[-1:] // MOE_BLOCK).astype(jnp.int32)
    eid = jnp.arange(N_EXPERTS, dtype=jnp.int32)
    later = jnp.where((eid[None, :] > eid[:, None]) & (cnt[None, :] > 0), eid[None, :], N_EXPERTS)
    next_e = jnp.min(later, axis=1)
    next_e = jnp.where(next_e == N_EXPERTS, eid, next_e)
    blk_next = next_e[blk_exp].astype(jnp.int32)
    dest = jnp.sum(jnp.where(eidx[:, :TOP_K, None] == eid, pad_start, 0), axis=-1) + rank[:, :TOP_K]

    tb = tm
    xs_words = _moe_dispatch(x_words, dest.reshape(m // tb, 1, tb * TOP_K), cap, tb=tb)
    ys = _moe_experts(xs_words, blk_exp, blk_next, blk_rows, n_used, w_gate, w_up, w_down, layer)
    tc = min(MOE_COMBINE_ROWS, tm)
    routed = _moe_combine(ys, dest.reshape(m // tc, 1, tc * TOP_K), gates, tc=tc)

    sd = ws_gate.shape[1]
    w_sh = jnp.concatenate([ws_gate, ws_up], axis=1).astype(BF16)
    (hs,) = _mm([(xb, w_sh)], tm=tm, tn=2 * sd, epilogue=lambda acc: (_silu(acc[:, :sd]) * acc[:, sd:],), outs=[(sd, BF16)])
    ys = [_mm([(hs, ws_down.astype(BF16))], tm=tm, tn=d, rows=g,
              aux=[(x_res, 'rowcol'), (routed, 'rowcol'), (ln_g.reshape(1, -1), 'col'), (ln_b.reshape(1, -1), 'col')],
              epilogue=lambda acc, xr, rt, g_, b: (_ep_ln(acc + rt, xr, g_, b),), outs=[(d, out_dtype)], name="moe_out")[0]
          for g in out_rows]
    return ys


def _rwkv_params(li, mu, w0, w2, a0, a2, g2, k_k, k_a, r_k, ln_g, ln_b):
    row = lambda a: a[li].reshape(1, -1)
    padr = lambda a, n: jnp.pad(a, ((0, n - a.shape[0]), (0, 0)))
    return dict(mu=_rwkv_cols_layout(mu[li])[None, :], w0=row(w0), a0=row(a0), k_k=row(k_k), k_a=row(k_a),
                w2=padr(w2[li], LANES), a2=padr(a2[li], LANES), g2=g2[li], r_k=row(r_k), ln_g=row(ln_g), ln_b=row(ln_b))


def _mix_out(geo, x_res, o_a, o_b, w_out, ln_g, ln_b):
    d = x_res.shape[1]
    ka = o_a.shape[1]
    return _mm([(o_a, w_out[:ka].astype(BF16)), (o_b, w_out[ka:].astype(BF16))], tm=geo.tm, tn=d,
               aux=[(x_res, 'rowcol'), (ln_g.reshape(1, -1), 'col'), (ln_b.reshape(1, -1), 'col')],
               epilogue=_ep_ln_packed, outs=[(d, BF16), (d // 2, jnp.uint32)], name="mix_out")


def kernel(x_prompt, x_sample, cache_mla_ckv, cache_mla_krope, state_rwkv_shift, state_rwkv_wkv, cache_fox_k, cache_fox_v, cache_fox_logf, state_gdn_conv, state_gdn_wkv, ln1_g, ln1_b, ln2_g, ln2_b, ev_w_in, ev_w_out, mla_kv_norm, mla_w_uk, mla_w_uv, rwkv_mu, rwkv_w0, rwkv_w2, rwkv_a0, rwkv_a2, rwkv_g2, rwkv_k_k, rwkv_k_a, rwkv_r_k, rwkv_ln_g, rwkv_ln_b, od_w_in, od_w_out, fox_q_norm, fox_k_norm, fox_f_bias, gdn_conv_w, gdn_a_log, gdn_dt_bias, gdn_norm, moe_router, moe_router_bias, moe_w_gate, moe_w_up, moe_w_down, moe_ws_gate, moe_ws_up, moe_ws_down):
    bp, tp, d = x_prompt.shape
    bs, ts, _ = x_sample.shape
    geo = _Geom(bp, tp, bs, ts, cache_mla_ckv.shape[2])
    x = xb = jnp.concatenate([x_prompt.reshape(bp * tp, d).astype(BF16), x_sample.reshape(bs * ts, d).astype(BF16)], axis=0)
    rope_tab = _rope_table(geo)
    st = {}
    n_layers = ln1_g.shape[0]
    for layer in range(n_layers):
        li = layer // 2
        if layer % 2 == 0:
            rw = _rwkv_params(li, rwkv_mu, rwkv_w0, rwkv_w2, rwkv_a0, rwkv_a2, rwkv_g2, rwkv_k_k, rwkv_k_a, rwkv_r_k,
                              rwkv_ln_g, rwkv_ln_b)
            o_a, o_b, new = _even_mixer(geo, xb, rope_tab, ev_w_in[li], mla_kv_norm[li], mla_w_uk[li], mla_w_uv[li], rw,
                                        cache_mla_ckv[li], cache_mla_krope[li], state_rwkv_shift[li], state_rwkv_wkv[li])
            w_out = ev_w_out[li]
        else:
            fx = {'q_norm': fox_q_norm[li], 'k_norm': fox_k_norm[li], 'f_bias': fox_f_bias[li]}
            gd = {'conv_w': gdn_conv_w[li], 'a_log': gdn_a_log[li], 'dt_bias': gdn_dt_bias[li], 'norm': gdn_norm[li]}
            o_a, o_b, new = _odd_mixer(geo, xb, od_w_in[li], fx, gd, cache_fox_k[li], cache_fox_v[li], cache_fox_logf[li],
                                       state_gdn_conv[li], state_gdn_wkv[li])
            w_out = od_w_out[li]
        for name, val in new.items():
            st.setdefault(name, []).append(val)
        xb, x_words = _mix_out(geo, x, o_a, o_b, w_out, ln1_g[layer], ln1_b[layer])
        last = layer == n_layers - 1
        ys = _moe_layer(geo, xb, xb, x_words, layer, moe_router[layer], moe_router_bias[layer], moe_w_gate, moe_w_up,
                        moe_w_down, moe_ws_gate[layer], moe_ws_up[layer], moe_ws_down[layer], ln2_g[layer], ln2_b[layer],
                        F32 if last else BF16, ((0, geo.mp), (geo.mp, geo.ms)) if last else ((0, geo.m),))
        x = xb = ys[0]
    names = ('ckv', 'kr', 'sh', 'rwkv', 'fk', 'fv', 'fl', 'cv', 'gdn')
    return ((ys[0].reshape(bp, tp, d), ys[1].reshape(bs, ts, d))
            + tuple(jnp.stack(st[n + '_p']) for n in names) + tuple(jnp.stack(st[n + '_s']) for n in names))
```

```python
import math

import jax
import jax.numpy as jnp
from jax import lax
from jax.experimental import pallas as pl
from jax.experimental.pallas import tpu as pltpu

F32 = jnp.float32
BF16 = jnp.bfloat16
HP = lax.Precision.HIGHEST

D_MODEL = 2048
DEPTH = 2
DN_ALPHA = float((2.0 * DEPTH) ** 0.25)
LN_EPS = 1e-5
RMS_EPS = 1e-6
CHUNK = 64
LANES = 128

MLA_HEADS, MLA_NOPE, MLA_ROPE, MLA_VDIM, MLA_KV_LORA = 8, 128, 64, 128, 512
MLA_QK = MLA_NOPE + MLA_ROPE
ROPE_THETA = 10000.0
RWKV_HEADS, RWKV_HD = 16, 64
RWKV_W = RWKV_HEADS * RWKV_HD
RWKV_DECAY_LORA, RWKV_A_LORA, RWKV_G_LORA = 96, 96, 256
RWKV_GN_EPS = 64e-5
RWKV_PCOLS = 3 * RWKV_W + 2 * LANES + RWKV_G_LORA
FOX_HEADS, FOX_HD = 8, 128
FOX_W = FOX_HEADS * FOX_HD
GDN_HEADS, GDN_HD, GDN_CONV = 8, 128, 4
GDN_W = GDN_HEADS * GDN_HD
GDN_QKV = 3 * GDN_W
N_EXPERTS, TOP_K, N_GROUPS, TOPK_GROUPS = 64, 6, 8, 4
EXPERT_DIM, SHARED_DIM = 512, 512
ROUTED_SCALE = 2.5
SCAN_CHUNK = 64
GDN_STREAMS = 4
RWKV_STREAMS = 2
FLASH_KEY_SUBTILE = 1024
NEG_BIG = -1e30
RWKV_PASSES = dict(a=1, x=1, inv=1, u=1, y=1, s=1)
GDN_PASSES = dict(a=1, x=1, inv=1, u=1, y=1, s=1)

VMEM_LIMIT_BYTES = 56 * 1024 * 1024
MM_MAX_COLS = 1792


def _cp(*sem):
    return pltpu.CompilerParams(dimension_semantics=("arbitrary",) * len(sem), vmem_limit_bytes=VMEM_LIMIT_BYTES)


def _dot(a, b, prec=None):
    return jnp.dot(a, b, precision=prec, preferred_element_type=F32)


def _dot_nt(a, b, prec=None):
    return lax.dot_general(a, b, (((1,), (1,)), ((), ())), precision=prec, preferred_element_type=F32)


def _sigmoid(x):
    return 1.0 / (1.0 + jnp.exp(-x))


def _softplus(x):
    return jnp.maximum(x, 0.0) + jnp.log(1.0 + jnp.exp(-jnp.abs(x)))


def _silu(x):
    return x * _sigmoid(x)


def _mm(pairs, *, tm, tn, epilogue, outs, aux=(), rows=None, name="mm"):
    row0, m = rows if rows is not None else (0, pairs[0][0].shape[0])
    n = pairs[0][1].shape[1]
    if tn is None:
        tn = n if n <= MM_MAX_COLS else max(c for c in range(LANES, MM_MAX_COLS + 1, LANES) if n % c == 0)
    outs = [(tn if ow is None else ow, dt) for ow, dt in outs]
    assert m % tm == 0 and row0 % tm == 0 and n % tn == 0, (m, row0, n, tm, tn)
    nj, ni = n // tn, m // tm
    rb0 = row0 // tm
    in_specs, args = [], []
    for a, w in pairs:
        k = a.shape[1]
        in_specs += [pl.BlockSpec((tm, k), lambda j, i: (rb0 + i, 0)), pl.BlockSpec((k, tn), lambda j, i: (0, j))]
        args += [a, w]
    for arr, kind in aux:
        if kind == 'row':
            in_specs.append(pl.BlockSpec((tm, arr.shape[1]), lambda j, i: (rb0 + i, 0)))
        elif kind == 'rowcol':
            in_specs.append(pl.BlockSpec((tm, tn), lambda j, i: (rb0 + i, j)))
        elif kind == 'row0':
            in_specs.append(pl.BlockSpec(arr.shape, lambda j, i: (0, 0)))
        else:
            in_specs.append(pl.BlockSpec((1, tn), lambda j, i: (0, j)))
        args.append(arr)
    out_shape = [jax.ShapeDtypeStruct((m, ow * nj), dt) for ow, dt in outs]
    out_specs = [pl.BlockSpec((tm, ow), lambda j, i: (i, j)) for ow, dt in outs]
    n_pairs, n_aux = len(pairs), len(aux)

    def body(*refs):
        acc = None
        for p in range(n_pairs):
            d = _dot(refs[2 * p][...], refs[2 * p + 1][...])
            acc = d if acc is None else acc + d
        res = epilogue(acc, *[r[...] for r in refs[2 * n_pairs:2 * n_pairs + n_aux]])
        for o_ref, val in zip(refs[2 * n_pairs + n_aux:], res):
            o_ref[...] = val.astype(o_ref.dtype)

    res = pl.pallas_call(body, grid=(nj, ni), in_specs=in_specs, out_specs=out_specs, out_shape=out_shape, name=name,
                         compiler_params=_cp("parallel", "parallel"))(*args)
    return res


def _mm_t(w_t, a, *, nb, t, row0, name="mm_t"):
    n, k = w_t.shape
    tm = next((c for c in (512, 384, 256, 128) if t % c == 0), t)
    assert t % tm == 0 and row0 % tm == 0
    nt = t // tm

    def body(w_ref, a_ref, o_ref):
        o_ref[...] = _dot_nt(w_ref[...], a_ref[...]).astype(o_ref.dtype)

    return pl.pallas_call(
        body, grid=(nb, nt), name=name,
        in_specs=[pl.BlockSpec((n, k), lambda b, i: (0, 0)), pl.BlockSpec((tm, k), lambda b, i: (row0 // tm + b * nt + i, 0))],
        out_specs=pl.BlockSpec((n, tm), lambda b, i: (b, i)),
        out_shape=jax.ShapeDtypeStruct((nb * n, t), BF16),
        compiler_params=_cp("parallel", "parallel"))(w_t, a)


def _rope_lanes(x, tab):
    c, s1, s2 = tab[:, :LANES], tab[:, LANES:2 * LANES], tab[:, 2 * LANES:]
    n = x.shape[1]
    lo = pltpu.roll(x, n - MLA_ROPE // 2, 1)
    hi = pltpu.roll(x, MLA_ROPE // 2, 1)
    if n > LANES:
        reps = n // LANES
        c, s1, s2 = (jnp.concatenate([t] * reps, axis=1) for t in (c, s1, s2))
    return x * c + lo * s1 + hi * s2


def _ep_rms(acc, g):
    y = acc * lax.rsqrt(jnp.mean(acc * acc, axis=-1, keepdims=True) + RMS_EPS) * g
    return y, y


def _ep_ln(acc, x, g, b):
    z = DN_ALPHA * x.astype(F32) + acc
    mu = jnp.mean(z, axis=-1, keepdims=True)
    zc = z - mu
    var = jnp.mean(zc * zc, axis=-1, keepdims=True)
    return zc * lax.rsqrt(var + LN_EPS) * g + b


def _flash(qs, ks, vt, *, nb, tq_total, tk_total, q_row0, q_off, tq, tk, chunked, nh, bias=None, gate=None,
           out_dtype=BF16, name="flash"):
    assert tq_total % tq == 0 and tk_total % tk == 0 and q_row0 % tq == 0
    nq, nk = tq_total // tq, tk_total // tk
    qrb0 = q_row0 // tq
    ts = tk if tk % FLASH_KEY_SUBTILE else min(tk, FLASH_KEY_SUBTILE)
    for qi_s in range(nq):
        last_s = min(nk - 1, (q_off + qi_s * tq + tq - 1) // tk)
        assert last_s * tk <= q_off + qi_s * tq + 1, "key tiles before the last needed one must be fully visible"
    n_q, n_k = len(qs), len(ks)
    has_bias, has_gate = bias is not None, gate is not None

    def last_k(qi):
        return jnp.minimum(nk - 1, (q_off + qi * tq + tq - 1) // tk)

    kblk = lambda qi, ki: jnp.minimum(ki, last_k(qi))
    q_map = lambda b, h, qi, ki: (qrb0 + b * nq + qi, h)
    in_specs, args = [], []
    for q in qs:
        in_specs.append(pl.BlockSpec((tq, LANES), q_map))
        args.append(q)
    for arr, c0, per_head in ks:
        if per_head:
            in_specs.append(pl.BlockSpec((tk, LANES), lambda b, h, qi, ki, c0=c0: (b * nk + kblk(qi, ki), c0 + h)))
        else:
            in_specs.append(pl.BlockSpec((tk, LANES), lambda b, h, qi, ki, c0=c0: (b * nk + kblk(qi, ki), c0)))
        args.append(arr)
    in_specs.append(pl.BlockSpec((LANES, tk), lambda b, h, qi, ki: (b * nh + h, kblk(qi, ki))))
    args.append(vt)
    if has_bias:
        in_specs.append(pl.BlockSpec((1, 1, 1, tq), lambda b, h, qi, ki: (b, h, 0, qi)))
        in_specs.append(pl.BlockSpec((1, 1, tk, 1), lambda b, h, qi, ki: (b, h, kblk(qi, ki), 0)))
        args += list(bias)
    if has_gate:
        in_specs.append(pl.BlockSpec((tq, LANES), q_map))
        args.append(gate)

    def body(*refs):
        q_refs = refs[:n_q]
        k_refs = refs[n_q:n_q + n_k]
        vt_ref = refs[n_q + n_k]
        pos = n_q + n_k + 1
        if has_bias:
            qb_ref, kb_ref = refs[pos], refs[pos + 1]
            pos += 2
        if has_gate:
            gate_ref = refs[pos]
            pos += 1
        o_ref, m_ref, l_ref, acc_ref = refs[pos:pos + 4]
        qi, ki = pl.program_id(2), pl.program_id(3)

        @pl.when(ki == 0)
        def _():
            m_ref[...] = jnp.full(m_ref.shape, NEG_BIG, F32)
            l_ref[...] = jnp.zeros(l_ref.shape, F32)
            acc_ref[...] = jnp.zeros(acc_ref.shape, F32)

        def step(masked):
            q_all = jnp.concatenate([q_ref[...] for q_ref in q_refs], axis=1) if n_q > 1 else q_refs[0][...]
            qpos = q_off + qi * tq + lax.broadcasted_iota(jnp.int32, (1, tq), 1)
            qlim = (qpos | (CHUNK - 1)) if chunked else qpos
            subs = [slice(j * ts, (j + 1) * ts) for j in range(tk // ts)]
            ss = []
            for j, rows in enumerate(subs):
                k_all = jnp.concatenate([k_ref[rows, :] for k_ref in k_refs], axis=1) if n_k > 1 else k_refs[0][rows, :]
                s = _dot_nt(k_all, q_all)
                if has_bias:
                    s = s + (qb_ref[0, 0] - kb_ref[0, 0, rows, :])
                if masked:
                    kpos = ki * tk + j * ts + lax.broadcasted_iota(jnp.int32, (ts, 1), 0)
                    s = jnp.where(kpos <= qlim, s, NEG_BIG)
                ss.append(s)
            m_old = m_ref[...]
            m_new = m_old
            for s in ss:
                m_new = jnp.maximum(m_new, jnp.max(s, axis=0, keepdims=True))
            alpha = jnp.exp(m_old - m_new)
            l_new = alpha * l_ref[...]
            acc_new = alpha * acc_ref[...]
            for rows, s in zip(subs, ss):
                p = jnp.exp(s - m_new)
                l_new = l_new + jnp.sum(p, axis=0, keepdims=True)
                acc_new = acc_new + _dot(vt_ref[:, rows], p.astype(BF16))
            m_ref[...] = m_new
            l_ref[...] = l_new
            acc_ref[...] = acc_new

        @pl.when(ki < last_k(qi))
        def _():
            step(False)

        @pl.when(ki == last_k(qi))
        def _():
            step(True)

        @pl.when(ki == nk - 1)
        def _():
            o = (acc_ref[...] / l_ref[...]).T
            if has_gate:
                o = o * gate_ref[...].astype(F32)
            o_ref[...] = o.astype(o_ref.dtype)

    return pl.pallas_call(
        body, grid=(nb, nh, nq, nk), in_specs=in_specs, name=name,
        out_specs=pl.BlockSpec((tq, LANES), lambda b, h, qi, ki: (b * nq + qi, h)),
        out_shape=jax.ShapeDtypeStruct((nb * tq_total, nh * LANES), out_dtype),
        scratch_shapes=[pltpu.VMEM((1, tq), F32), pltpu.VMEM((1, tq), F32), pltpu.VMEM((LANES, tq), F32)],
        compiler_params=_cp("parallel", "parallel", "parallel", "arbitrary"))(*args)


def _head_sum_matrices(n_heads, hd):
    lane = jnp.arange(n_heads * hd)[:, None] // hd
    e = (lane == jnp.arange(LANES)[None, :]).astype(F32)
    return e, e.T


def _rwkv_prep(cols, shift_prev, prm, *, nb, t, row0):
    tt = min(256, t)
    assert t % tt == 0 and row0 % tt == 0
    nt = t // tt
    rb0 = row0 // tt
    w = RWKV_W

    def body(c_ref, sp_ref, mu_ref, w0_ref, a0_ref, kk_ref, ka_ref, w2_ref, a2_ref, g2_ref, e_ref, et_ref,
             r_o, k_o, v_o, kk_o, b_o, lw_o, g_o, sh_o, carry):
        ti = pl.program_id(1)

        @pl.when(ti == 0)
        def _():
            carry[...] = sp_ref[0]

        c = c_ref[...]
        row = lax.broadcasted_iota(jnp.int32, (tt, 1), 0)
        prev = jnp.where(row == 0, carry[...], pltpu.roll(c, 1, 0))
        carry[...] = c[tt - 1:tt, :]
        sh_o[0] = c[tt - 1:tt, :]
        xs = c + (prev - c) * mu_ref[...]
        r, k, v = xs[:, :w], xs[:, w:2 * w], xs[:, 2 * w:3 * w]
        o1 = 3 * w
        lora = lambda u, w_ref: _dot(u.astype(BF16), w_ref[...].astype(BF16))
        w_raw = w0_ref[...] + lora(jnp.tanh(xs[:, o1:o1 + LANES]), w2_ref)
        log_w = -jnp.exp(-_softplus(-w_raw) - 0.5)
        a = _sigmoid(a0_ref[...] + lora(xs[:, o1 + LANES:o1 + 2 * LANES], a2_ref))
        g = lora(_sigmoid(xs[:, o1 + 2 * LANES:]), g2_ref)

        def head_dot(u, sel_ref):
            hi, lo = _split_bf16(u)
            sel = sel_ref[...].astype(BF16)
            return _dot(hi.astype(BF16), sel) + _dot(lo.astype(BF16), sel)

        kk = k * kk_ref[...]
        kk = kk * head_dot(lax.rsqrt(head_dot(kk * kk, e_ref) + 1e-6), et_ref)
        r_o[...] = r.astype(r_o.dtype)
        k_o[...] = (k * (1.0 + (a - 1.0) * ka_ref[...])).astype(k_o.dtype)
        v_o[...] = v.astype(v_o.dtype)
        kk_o[...] = kk.astype(kk_o.dtype)
        b_o[...] = (kk * a).astype(b_o.dtype)
        lw_o[...] = log_w
        g_o[...] = g.astype(g_o.dtype)

    e, et = _head_sum_matrices(RWKV_HEADS, RWKV_HD)
    tok = pl.BlockSpec((tt, w), lambda b, ti: (b * nt + ti, 0))
    full = lambda arr: pl.BlockSpec(arr.shape, lambda b, ti: (0,) * arr.ndim)
    params = [prm['mu'], prm['w0'], prm['a0'], prm['k_k'], prm['k_a'], prm['w2'], prm['a2'], prm['g2'], e, et]
    outs = pl.pallas_call(
        body, grid=(nb, nt), name="rwkv_prep",
        in_specs=[pl.BlockSpec((tt, RWKV_PCOLS), lambda b, ti: (rb0 + b * nt + ti, 0)),
                  pl.BlockSpec((1, 1, RWKV_PCOLS), lambda b, ti: (b, 0, 0))] + [full(p) for p in params],
        out_specs=[tok] * 7 + [pl.BlockSpec((1, 1, RWKV_PCOLS), lambda b, ti: (b, 0, 0))],
        out_shape=[jax.ShapeDtypeStruct((nb * t, w), dt) for dt in (BF16, BF16, BF16, BF16, BF16, F32, BF16)]
        + [jax.ShapeDtypeStruct((nb, 1, RWKV_PCOLS), F32)],
        scratch_shapes=[pltpu.VMEM((1, RWKV_PCOLS), F32)],
        compiler_params=_cp("parallel", "arbitrary"))(cols, shift_prev, *params)
    return outs


def _cumsum_rows(x, n):
    row = lax.broadcasted_iota(jnp.int32, (n, 1), 0)
    sh = 1
    while sh < n:
        x = x + jnp.where(row >= sh, pltpu.roll(x, sh, 0), 0.0)
        sh *= 2
    return x


def _split_bf16(x):
    hi = x.astype(BF16).astype(F32)
    return hi, x - hi


def _bmm(a, b, ca, cb, passes):
    dn = (((ca,), (cb,)), ((0,), (0,)))
    if passes == 6:
        return lax.dot_general(a, b, dn, precision=HP, preferred_element_type=F32)
    if passes == 3:
        ah, al = _split_bf16(a)
        bh, bl = _split_bf16(b)
        a = jnp.concatenate([ah, ah, al], axis=ca)
        b = jnp.concatenate([bh, bl, bh], axis=cb)
    return lax.dot_general(a.astype(BF16), b.astype(BF16), dn, preferred_element_type=F32)


def _heads(x, nh, hd):
    return jnp.stack([x[:, h * hd:(h + 1) * hd] for h in range(nh)], axis=0)


def _unheads(x):
    return jnp.concatenate([x[h] for h in range(x.shape[0])], axis=1)


def _unit_lower_inverse(nmat, n, passes):
    eye = (lax.broadcasted_iota(jnp.int32, (n, n), 0) == lax.broadcasted_iota(jnp.int32, (n, n), 1)).astype(F32)
    t = eye + nmat
    p = nmat
    steps = int(math.ceil(math.log2(n))) - 1
    for _ in range(steps):
        p = _bmm(p, p, 2, 1, passes)
        t = t + _bmm(p, t, 2, 1, passes)
    return t


def _rwkv_scan(r, k, v, kk, bvec, logw, g, s0, prm, *, nb, t):
    c = min(SCAN_CHUNK, t)
    ns = math.gcd(nb, RWKV_STREAMS)
    assert t % c == 0
    nc = t // c
    hd, nh = RWKV_HD, RWKV_HEADS

    def body(r_ref, k_ref, v_ref, kk_ref, b_ref, lw_ref, g_ref, s0_ref, rk_ref, lg_ref, lb_ref, o_ref, sf_ref, st):
        ci = pl.program_id(1)

        @pl.when(ci == 0)
        def _():
            for s in range(ns):
                st[s * nh:(s + 1) * nh] = s0_ref[s]

        hs = lambda a: _heads(a, nh, hd)
        parts = []
        for s in range(ns):
            lw = lw_ref[s]
            lc = _cumsum_rows(lw, c)
            rr, kx, vv, bb = (ref[s].astype(F32) for ref in (r_ref, k_ref, v_ref, b_ref))
            g_inv = jnp.exp(-lc)
            g_end = jnp.exp(lc[c - 1:c, :] - lc)
            parts.append(dict(
                at=hs(-kk_ref[s].astype(F32) * jnp.exp(lc - lw)), rt=hs(rr * jnp.exp(lc)),
                bt=hs(bb * g_inv), kt=hs(kx * g_inv), bg=hs(bb * g_end), kg=hs(kx * g_end),
                w_end=hs(jnp.exp(lc[c - 1:c, :])), v_h=hs(vv), rk=hs(rr * kx * rk_ref[...])))
        cat = lambda name: jnp.concatenate([p[name] for p in parts], axis=0)
        at, rt, bt, kt, bg, kg, w_end, v_h = (cat(n) for n in ('at', 'rt', 'bt', 'kt', 'bg', 'kg', 'w_end', 'v_h'))
        ri = lax.broadcasted_iota(jnp.int32, (c, c), 0)
        cj = lax.broadcasted_iota(jnp.int32, (c, c), 1)
        strict = ri > cj
        incl2 = lax.broadcasted_iota(jnp.int32, (c, 2 * c), 0) >= (lax.broadcasted_iota(jnp.int32, (c, 2 * c), 1) & (c - 1))
        s_all = st[...]
        pp = RWKV_PASSES
        lhs = jnp.concatenate([at, rt], axis=1)
        a_all = _bmm(lhs, jnp.concatenate([bt, kt], axis=1), 2, 2, pp['a'])
        x_all = _bmm(lhs, s_all, 2, 2, pp['x'])
        tinv = _unit_lower_inverse(jnp.where(strict, a_all[:, :c, :c], 0.0), c, pp['inv'])
        a_ak = jnp.where(strict, a_all[:, :c, c:], 0.0)
        u = _bmm(tinv, x_all[:, :c] + _bmm(a_ak, v_h, 2, 1, pp['u']), 2, 1, pp['u'])
        uv = jnp.concatenate([u, v_h], axis=1)
        y = x_all[:, c:] + _bmm(jnp.where(incl2, a_all[:, c:, :], 0.0), uv, 2, 1, pp['y'])
        st[...] = s_all * w_end + _bmm(uv, jnp.concatenate([bg, kg], axis=1), 1, 1, pp['s'])
        mu = jnp.mean(y, axis=-1, keepdims=True)
        yc = y - mu
        var = jnp.mean(yc * yc, axis=-1, keepdims=True)
        lng, lnb = hs(lg_ref[...]), hs(lb_ref[...])
        for s in range(ns):
            sl = slice(s * nh, (s + 1) * nh)
            yn = yc[sl] * lax.rsqrt(var[sl] + RWKV_GN_EPS) * lng + lnb
            bonus = jnp.sum(parts[s]['rk'], axis=-1, keepdims=True) * v_h[sl]
            o_ref[s] = (_unheads(yn + bonus) * g_ref[s].astype(F32)).astype(o_ref.dtype)

        @pl.when(ci == nc - 1)
        def _():
            for s in range(ns):
                sf_ref[s] = st[s * nh:(s + 1) * nh]

    as3 = lambda a: a.reshape(nb, t, RWKV_W)
    tok = pl.BlockSpec((ns, c, RWKV_W), lambda b, ci: (b, ci, 0))
    stt = pl.BlockSpec((ns, nh, hd, hd), lambda b, ci: (b, 0, 0, 0))
    par = pl.BlockSpec((1, RWKV_W), lambda b, ci: (0, 0))
    o, s_fin = pl.pallas_call(
        body, grid=(nb // ns, nc), name="rwkv_scan",
        in_specs=[tok] * 7 + [stt, par, par, par],
        out_specs=[tok, stt],
        out_shape=[jax.ShapeDtypeStruct((nb, t, RWKV_W), BF16), jax.ShapeDtypeStruct((nb, nh, hd, hd), F32)],
        scratch_shapes=[pltpu.VMEM((ns * nh, hd, hd), F32)],
        compiler_params=_cp("parallel", "arbitrary"))(*(as3(a) for a in (r, k, v, kk, bvec, logw, g)), s0,
                                                      prm['r_k'], prm['ln_g'], prm['ln_b'])
    return o.reshape(nb * t, RWKV_W), s_fin


class _Geom:
    def __init__(self, bp, tp, bs, ts, past):
        self.bp, self.tp, self.bs, self.ts, self.past = bp, tp, bs, ts, past
        self.mp, self.ms = bp * tp, bs * ts
        self.m = self.mp + self.ms
        self.tm = math.gcd(self.m, 512)
        assert self.tm % 16 == 0 and tp % CHUNK == 0 and past % CHUNK == 0 and ts == CHUNK
        self.tq_p = min(1024, tp)
        self.kpad = -(-(past + ts) // (3 * LANES)) * (3 * LANES)
        self.tk_s = self.kpad


def _pad_cols(w, n):
    return jnp.pad(w, ((0, 0), (0, n - w.shape[1])))


def _rope_table(geo):
    half = MLA_ROPE // 2
    pos = jnp.concatenate([jnp.tile(jnp.arange(geo.tp, dtype=jnp.int32), geo.bp),
                           geo.past + jnp.tile(jnp.arange(geo.ts, dtype=jnp.int32), geo.bs)])
    inv_freq = ROPE_THETA ** (-jnp.arange(half, dtype=F32) / half)
    ang = pos.astype(F32)[:, None] * inv_freq[None, :]
    cos, sin = jnp.cos(ang), jnp.sin(ang)
    z = jnp.zeros_like(cos)
    zz = jnp.zeros((pos.shape[0], LANES - MLA_ROPE), F32)
    return jnp.concatenate([cos, cos, zz, -sin, z, zz, z, sin, zz], axis=1)


def _rwkv_cols_layout(a):
    o1 = 3 * RWKV_W
    o2 = o1 + RWKV_DECAY_LORA
    o3 = o2 + RWKV_A_LORA
    pad = [(0, 0)] * (a.ndim - 1)
    return jnp.concatenate([a[..., :o1],
                            jnp.pad(a[..., o1:o2], pad + [(0, LANES - RWKV_DECAY_LORA)]),
                            jnp.pad(a[..., o2:o3], pad + [(0, LANES - RWKV_A_LORA)]),
                            a[..., o3:]], axis=-1)


def _rwkv_cols_unlayout(a):
    o1 = 3 * RWKV_W
    return jnp.concatenate([a[..., :o1], a[..., o1:o1 + RWKV_DECAY_LORA],
                            a[..., o1 + LANES:o1 + LANES + RWKV_A_LORA], a[..., o1 + 2 * LANES:]], axis=-1)


def _even_mixer(geo, xb, rope_tab, w_in, kv_norm, w_uk, w_uv, rw, cache_ckv, cache_kr, shift_prev, wkv0):
    tm, m, mp = geo.tm, geo.m, geo.mp
    d = w_in.shape[0]
    nq = MLA_HEADS * MLA_QK
    wq = w_in[:, :nq].reshape(d, MLA_HEADS, MLA_QK)
    w_qn = wq[:, :, :MLA_NOPE].reshape(d, MLA_HEADS * MLA_NOPE).astype(BF16)
    w_qr = jnp.pad(wq[:, :, MLA_NOPE:], ((0, 0), (0, 0), (0, LANES - MLA_ROPE))).reshape(d, MLA_HEADS * LANES).astype(BF16)
    w_ckv = w_in[:, nq:nq + MLA_KV_LORA].astype(BF16)
    w_kr = _pad_cols(w_in[:, nq + MLA_KV_LORA:nq + MLA_KV_LORA + MLA_ROPE], LANES).astype(BF16)
    w_rw = _rwkv_cols_layout(w_in[:, nq + MLA_KV_LORA + MLA_ROPE:]).astype(BF16)
    scale = MLA_QK ** -0.5

    (qn,) = _mm([(xb, w_qn)], tm=tm, tn=None, epilogue=lambda acc: (acc * scale,), outs=[(None, BF16)])
    (qr,) = _mm([(xb, w_qr)], tm=tm, tn=MLA_HEADS * LANES, aux=[(rope_tab, 'row')],
                epilogue=lambda acc, tab: (_rope_lanes(acc, tab) * scale,), outs=[(MLA_HEADS * LANES, BF16)])
    groups = ((0, mp), (mp, geo.ms))
    (ckv_p, ckvb_p), (ckv_s, ckvb_s) = (
        _mm([(xb, w_ckv)], tm=tm, tn=MLA_KV_LORA, aux=[(kv_norm.reshape(1, -1), 'col')], rows=g,
            epilogue=_ep_rms, outs=[(MLA_KV_LORA, F32), (MLA_KV_LORA, BF16)]) for g in groups)
    (kr_p, krb_p), (kr_s, krb_s) = (
        _mm([(xb, w_kr)], tm=tm, tn=LANES, aux=[(rope_tab, 'row')], rows=g,
            epilogue=lambda acc, tab: (_rope_lanes(acc, tab),) * 2, outs=[(LANES, F32), (LANES, BF16)]) for g in groups)
    (rwc,) = _mm([(xb, w_rw)], tm=tm, tn=None, epilogue=lambda acc: (acc,), outs=[(None, F32)])

    w_k = w_uk.reshape(MLA_KV_LORA, -1).astype(BF16)
    w_vt = w_uv.reshape(MLA_KV_LORA, -1).T.astype(BF16)
    (kn_p,) = _mm([(ckvb_p, w_k)], tm=tm, tn=None, epilogue=lambda acc: (acc,), outs=[(None, BF16)])
    vt_p = _mm_t(w_vt, ckvb_p, nb=geo.bp, t=geo.tp, row0=0)
    o_p = _flash([qn, qr], [(kn_p, 0, True), (krb_p, 0, False)], vt_p,
                 nb=geo.bp, tq_total=geo.tp, tk_total=geo.tp, q_row0=0, q_off=0, tq=geo.tq_p, tk=geo.tq_p,
                 chunked=True, nh=MLA_HEADS)
    padk = geo.kpad - geo.past - geo.ts
    ckv_all = jnp.concatenate([cache_ckv.astype(BF16), ckvb_s.reshape(geo.bs, geo.ts, -1),
                               jnp.zeros((geo.bs, padk, MLA_KV_LORA), BF16)], axis=1).reshape(geo.bs * geo.kpad, -1)
    kr_all = jnp.concatenate([jnp.pad(cache_kr, ((0, 0), (0, 0), (0, LANES - MLA_ROPE))).astype(BF16),
                              krb_s.reshape(geo.bs, geo.ts, -1),
                              jnp.zeros((geo.bs, padk, LANES), BF16)], axis=1).reshape(geo.bs * geo.kpad, -1)
    (kn_all,) = _mm([(ckv_all, w_k)], tm=3 * LANES, tn=None, epilogue=lambda acc: (acc,), outs=[(None, BF16)])
    vt_all = _mm_t(w_vt, ckv_all, nb=geo.bs, t=geo.kpad, row0=0)
    o_s = _flash([qn, qr], [(kn_all, 0, True), (kr_all, 0, False)], vt_all,
                 nb=geo.bs, tq_total=geo.ts, tk_total=geo.kpad, q_row0=mp, q_off=geo.past, tq=geo.ts, tk=geo.tk_s,
                 chunked=True, nh=MLA_HEADS)
    o_mla = jnp.concatenate([o_p, o_s], axis=0)

    outs_p = _rwkv_prep(rwc, jnp.zeros((geo.bp, 1, RWKV_PCOLS), F32), rw, nb=geo.bp, t=geo.tp, row0=0)
    outs_s = _rwkv_prep(rwc, _rwkv_cols_layout(shift_prev)[:, None, :], rw, nb=geo.bs, t=geo.ts, row0=mp)
    y_p, wkv_p = _rwkv_scan(*outs_p[:7], jnp.zeros((geo.bp, RWKV_HEADS, RWKV_HD, RWKV_HD), F32), rw, nb=geo.bp, t=geo.tp)
    y_s, wkv_s = _rwkv_scan(*outs_s[:7], wkv0, rw, nb=geo.bs, t=geo.ts)
    o_rwkv = jnp.concatenate([y_p, y_s], axis=0)
    states = dict(
        ckv_p=ckv_p.reshape(geo.bp, geo.tp, -1), ckv_s=ckv_s.reshape(geo.bs, geo.ts, -1),
        kr_p=kr_p[:, :MLA_ROPE].reshape(geo.bp, geo.tp, -1), kr_s=kr_s[:, :MLA_ROPE].reshape(geo.bs, geo.ts, -1),
        sh_p=_rwkv_cols_unlayout(outs_p[7][:, 0]), sh_s=_rwkv_cols_unlayout(outs_s[7][:, 0]),
        rwkv_p=wkv_p, rwkv_s=wkv_s)
    return o_mla, o_rwkv, states


def _head_rms(x, g, nh, scale=1.0):
    outs = []
    for h in range(nh):
        seg = x[:, h * LANES:(h + 1) * LANES]
        outs.append(seg * lax.rsqrt(jnp.mean(seg * seg, axis=-1, keepdims=True) + RMS_EPS) * (g * scale))
    return jnp.concatenate(outs, axis=1)


def _head_l2(x, nh, scale=1.0):
    outs = []
    for h in range(nh):
        seg = x[:, h * LANES:(h + 1) * LANES]
        outs.append(seg * (lax.rsqrt(jnp.sum(seg * seg, axis=-1, keepdims=True) + 1e-6) * scale))
    return jnp.concatenate(outs, axis=1)


def _cumsum_time(x):
    nb, t, w = x.shape

    def body(x_ref, o_ref):
        o_ref[0] = _cumsum_rows(x_ref[0], t)

    spec = pl.BlockSpec((1, t, w), lambda b: (b, 0, 0))
    return pl.pallas_call(body, grid=(nb,), in_specs=[spec], out_specs=spec, name="cumsum_time",
                          out_shape=jax.ShapeDtypeStruct(x.shape, F32), compiler_params=_cp("parallel"))(x)


def _gdn_prep(cols, conv_prev, conv_w, *, nb, t, row0):
    tt = min(256, t)
    assert t % tt == 0 and row0 % tt == 0
    nt = t // tt
    rb0 = row0 // tt
    w = GDN_W

    def body(c_ref, cp_ref, cw_ref, q_o, k_o, v_o, carry):
        ti = pl.program_id(1)

        @pl.when(ti == 0)
        def _():
            carry[...] = cp_ref[0]

        c = c_ref[...]
        ext = jnp.concatenate([carry[...], c], axis=0)
        carry[...] = c[tt - 8:tt, :]
        cw = cw_ref[...]
        acc = c * cw[GDN_CONV - 1:GDN_CONV, :]
        for j in range(GDN_CONV - 1):
            back = GDN_CONV - 1 - j
            acc = acc + ext[8 - back:8 - back + tt, :] * cw[j:j + 1, :]
        qkv = _silu(acc)
        q_o[...] = _head_l2(qkv[:, :w], GDN_HEADS, GDN_HD ** -0.5).astype(q_o.dtype)
        k_o[...] = _head_l2(qkv[:, w:2 * w], GDN_HEADS).astype(k_o.dtype)
        v_o[...] = qkv[:, 2 * w:].astype(v_o.dtype)

    tok = pl.BlockSpec((tt, w), lambda b, ti: (b * nt + ti, 0))
    return pl.pallas_call(
        body, grid=(nb, nt), name="gdn_prep",
        in_specs=[pl.BlockSpec((tt, GDN_QKV), lambda b, ti: (rb0 + b * nt + ti, 0)),
                  pl.BlockSpec((1, 8, GDN_QKV), lambda b, ti: (b, 0, 0)),
                  pl.BlockSpec((8, GDN_QKV), lambda b, ti: (0, 0))],
        out_specs=[tok] * 3,
        out_shape=[jax.ShapeDtypeStruct((nb * t, w), BF16)] * 3,
        scratch_shapes=[pltpu.VMEM((8, GDN_QKV), F32)],
        compiler_params=_cp("parallel", "arbitrary"))(cols, conv_prev, conv_w)


def _gdn_scan(q, k, v, gb, z, s0, norm_g, *, nb, t, gb_row0, z_row0):
    c = min(SCAN_CHUNK, t)
    ns = math.gcd(nb, GDN_STREAMS)
    assert t % c == 0 and gb_row0 % c == 0 and z_row0 % c == 0
    nc = t // c
    nh, hd = GDN_HEADS, GDN_HD

    def body(*refs):
        q_ref, k_ref, v_ref = refs[:3]
        gb_refs, z_refs = refs[3:3 + ns], refs[3 + ns:3 + 2 * ns]
        s0_ref, ng_ref, o_ref, sf_ref, st = refs[3 + 2 * ns:]
        ci = pl.program_id(1)

        @pl.when(ci == 0)
        def _():
            for g in range(ns):
                st[g * nh:(g + 1) * nh] = s0_ref[g]

        ri = lax.broadcasted_iota(jnp.int32, (c, c), 0)
        cj = lax.broadcasted_iota(jnp.int32, (c, c), 1)
        tril, strict = ri >= cj, ri > cj
        gcols, grows, bcols = [], [], []
        for g in range(ns):
            gbv = gb_refs[g][...]
            gc = _cumsum_rows(gbv, c)
            gct = gc.T
            gcols.append(jnp.stack([gc[:, h:h + 1] for h in range(nh)], axis=0))
            grows.append(gct[:nh][:, None, :])
            bcols.append(jnp.stack([gbv[:, nh + h:nh + h + 1] for h in range(nh)], axis=0))
        gcol, grow, bcol = (jnp.concatenate(parts, axis=0) for parts in (gcols, grows, bcols))
        hs = lambda ref: jnp.concatenate([_heads(ref[g].astype(F32), nh, hd) for g in range(ns)], axis=0)
        k_h, q_h, v_h = hs(k_ref), hs(q_ref), hs(v_ref)
        decay = jnp.where(tril, jnp.exp(jnp.where(tril, gcol - grow, 0.0)), 0.0)
        pp = GDN_PASSES
        kb = k_h * bcol
        prods = _bmm(jnp.concatenate([kb, q_h], axis=1), k_h, 2, 2, pp['a'])
        low = jnp.where(strict, prods[:, :c] * decay, 0.0)
        a_qk = jnp.where(tril, prods[:, c:] * decay, 0.0)
        tinv = _unit_lower_inverse(-low, c, pp['inv'])
        e_g = jnp.exp(gcol)
        uw = _bmm(tinv, jnp.concatenate([v_h * bcol, kb * e_g], axis=2), 2, 1, pp['u'])
        s_all = st[...]
        ws_qs = _bmm(jnp.concatenate([uw[:, :, hd:], q_h * e_g], axis=1), s_all, 2, 1, pp['x'])
        v_new = uw[:, :, :hd] - ws_qs[:, :c]
        o = ws_qs[:, c:] + _bmm(a_qk, v_new, 2, 1, pp['y'])
        g_last = gcol[:, c - 1:c, :]
        st[...] = s_all * jnp.exp(g_last) + _bmm(k_h * jnp.exp(g_last - gcol), v_new, 1, 1, pp['s'])
        o = o * lax.rsqrt(jnp.mean(o * o, axis=-1, keepdims=True) + RMS_EPS) * ng_ref[...]
        for g in range(ns):
            o_ref[g] = (_unheads(o[g * nh:(g + 1) * nh]) * _silu(z_refs[g][...].astype(F32))).astype(o_ref.dtype)

        @pl.when(ci == nc - 1)
        def _():
            for g in range(ns):
                sf_ref[g] = st[g * nh:(g + 1) * nh]

    as3 = lambda a: a.reshape(nb, t, GDN_W)
    tok = pl.BlockSpec((ns, c, GDN_W), lambda b, ci: (b, ci, 0))
    stt = pl.BlockSpec((ns, nh, hd, hd), lambda b, ci: (b, 0, 0, 0))
    row_spec = lambda width, row0, g: pl.BlockSpec((c, width), lambda b, ci: (row0 // c + (b * ns + g) * nc + ci, 0))
    o, s_fin = pl.pallas_call(
        body, grid=(nb // ns, nc), name="gdn_scan",
        in_specs=[tok] * 3 + [row_spec(LANES, gb_row0, g) for g in range(ns)]
        + [row_spec(GDN_W, z_row0, g) for g in range(ns)] + [stt, pl.BlockSpec((1, hd), lambda b, ci: (0, 0))],
        out_specs=[tok, stt],
        out_shape=[jax.ShapeDtypeStruct((nb, t, GDN_W), BF16), jax.ShapeDtypeStruct((nb, nh, hd, hd), F32)],
        scratch_shapes=[pltpu.VMEM((ns * nh, hd, hd), F32)],
        compiler_params=_cp("parallel", "arbitrary"))(as3(q), as3(k), as3(v), *([gb] * ns), *([z] * ns), s0, norm_g)
    return o.reshape(nb * t, GDN_W), s_fin


def _odd_mixer(geo, xb, w_in, fx, gd, cache_k, cache_v, cache_logf, conv_prev, gdn0):
    tm, m, mp = geo.tm, geo.m, geo.mp
    fw = FOX_W
    o_f = 4 * fw
    o_g = o_f + FOX_HEADS
    w_q, w_k, w_v, w_gate = (w_in[:, i * fw:(i + 1) * fw].astype(BF16) for i in range(4))
    w_f = _pad_cols(w_in[:, o_f:o_g], LANES).astype(BF16)
    w_qkv = w_in[:, o_g:o_g + GDN_QKV].astype(BF16)
    w_ab = _pad_cols(w_in[:, o_g + GDN_QKV:o_g + GDN_QKV + 2 * GDN_HEADS], LANES).astype(BF16)
    w_z = w_in[:, o_g + GDN_QKV + 2 * GDN_HEADS:].astype(BF16)
    qn, kn = fx['q_norm'].reshape(1, -1), fx['k_norm'].reshape(1, -1)
    f_bias = jnp.pad(fx['f_bias'], (0, LANES - FOX_HEADS)).reshape(1, -1)
    scale = FOX_HD ** -0.5

    (q,) = _mm([(xb, w_q)], tm=tm, tn=fw, aux=[(qn, 'row0')],
               epilogue=lambda acc, g: (_head_rms(acc, g, FOX_HEADS, scale),), outs=[(fw, BF16)])
    groups = ((0, mp), (mp, geo.ms))
    (fk_p, kb_p), (fk_s, kb_s) = (
        _mm([(xb, w_k)], tm=tm, tn=fw, aux=[(kn, 'row0')], rows=g,
            epilogue=lambda acc, g_: (_head_rms(acc, g_, FOX_HEADS),) * 2, outs=[(fw, F32), (fw, BF16)]) for g in groups)
    (fv_p,), (fv_s,) = (_mm([(xb, w_v)], tm=tm, tn=None, rows=g, epilogue=lambda acc: (acc,), outs=[(None, F32)])
                      for g in groups)
    w_vt = w_v.T
    vt_p = _mm_t(w_vt, xb, nb=geo.bp, t=geo.tp, row0=0)
    vt_s = _mm_t(w_vt, xb, nb=geo.bs, t=geo.ts, row0=mp)
    (gate,) = _mm([(xb, w_gate)], tm=tm, tn=None, epilogue=lambda acc: (_sigmoid(acc),), outs=[(None, BF16)])
    (logf,) = _mm([(xb, w_f)], tm=tm, tn=LANES, aux=[(f_bias, 'col')],
                  epilogue=lambda acc, fb: (-_softplus(-(acc + fb)),), outs=[(LANES, F32)])

    cum_p = _cumsum_time(logf[:mp].reshape(geo.bp, geo.tp, LANES))[:, :, :FOX_HEADS]
    bias_p = (jnp.transpose(cum_p, (0, 2, 1))[:, :, None, :], jnp.transpose(cum_p, (0, 2, 1))[..., None])
    o_p = _flash([q], [(kb_p, 0, True)], vt_p, nb=geo.bp, tq_total=geo.tp, tk_total=geo.tp, q_row0=0, q_off=0,
                 tq=geo.tq_p, tk=geo.tq_p, chunked=False, nh=FOX_HEADS, bias=bias_p, gate=gate)
    padk = geo.kpad - geo.past - geo.ts
    lf_all = jnp.concatenate([jnp.pad(cache_logf, ((0, 0), (0, 0), (0, LANES - FOX_HEADS))),
                              logf[mp:].reshape(geo.bs, geo.ts, LANES), jnp.zeros((geo.bs, padk, LANES), F32)], axis=1)
    cum_s = jnp.transpose(_cumsum_time(lf_all)[:, :, :FOX_HEADS], (0, 2, 1))
    bias_s = (cum_s[:, :, None, geo.past:geo.past + geo.ts], cum_s[..., None])
    k_all = jnp.concatenate([cache_k.reshape(geo.bs, geo.past, fw).astype(BF16), kb_s.reshape(geo.bs, geo.ts, fw),
                             jnp.zeros((geo.bs, padk, fw), BF16)], axis=1).reshape(geo.bs * geo.kpad, fw)
    vt_all = jnp.concatenate([jnp.transpose(cache_v.reshape(geo.bs, geo.past, fw), (0, 2, 1)).astype(BF16),
                              vt_s.reshape(geo.bs, fw, geo.ts), jnp.zeros((geo.bs, fw, padk), BF16)],
                             axis=2).reshape(geo.bs * fw, geo.kpad)
    o_s = _flash([q], [(k_all, 0, True)], vt_all, nb=geo.bs, tq_total=geo.ts, tk_total=geo.kpad, q_row0=mp,
                 q_off=geo.past, tq=geo.ts, tk=geo.tk_s, chunked=False, nh=FOX_HEADS, bias=bias_s, gate=gate)
    o_fox = jnp.concatenate([o_p, o_s], axis=0)

    (qkv_raw,) = _mm([(xb, w_qkv)], tm=tm, tn=None, epilogue=lambda acc: (acc,), outs=[(None, F32)])
    lane = jnp.arange(LANES)
    neg_a = jnp.where(lane < GDN_HEADS, -jnp.exp(jnp.pad(gd['a_log'], (0, LANES - GDN_HEADS))), 0.0).reshape(1, -1)
    dtb = jnp.pad(gd['dt_bias'], (0, LANES - GDN_HEADS)).reshape(1, -1)
    is_g = (lane < GDN_HEADS).astype(F32).reshape(1, -1)
    (gb,) = _mm([(xb, w_ab)], tm=tm, tn=LANES, aux=[(neg_a, 'col'), (dtb, 'col'), (is_g, 'col')],
                epilogue=lambda acc, na, db, ig: (jnp.where(ig > 0.5, na * _softplus(acc + db), _sigmoid(acc)),),
                outs=[(LANES, F32)])
    (z,) = _mm([(xb, w_z)], tm=tm, tn=None, epilogue=lambda acc: (acc,), outs=[(None, BF16)])
    conv_w = jnp.pad(gd['conv_w'], ((0, 8 - GDN_CONV), (0, 0)))
    norm_g = gd['norm'].reshape(1, -1)
    prev8 = lambda a: jnp.pad(a, ((0, 0), (8 - (GDN_CONV - 1), 0), (0, 0)))
    q_p, k_p, v_p = _gdn_prep(qkv_raw, jnp.zeros((geo.bp, 8, GDN_QKV), F32), conv_w, nb=geo.bp, t=geo.tp, row0=0)
    q_s, k_s, v_s = _gdn_prep(qkv_raw, prev8(conv_prev), conv_w, nb=geo.bs, t=geo.ts, row0=mp)
    y_p, s_p = _gdn_scan(q_p, k_p, v_p, gb, z, jnp.zeros((geo.bp, GDN_HEADS, GDN_HD, GDN_HD), F32), norm_g,
                         nb=geo.bp, t=geo.tp, gb_row0=0, z_row0=0)
    y_s, s_s = _gdn_scan(q_s, k_s, v_s, gb, z, gdn0, norm_g, nb=geo.bs, t=geo.ts, gb_row0=mp, z_row0=mp)
    o_gdn = jnp.concatenate([y_p, y_s], axis=0)
    nconv = GDN_CONV - 1
    last_rows = lambda row0, nb, t: (row0 + jnp.arange(nb)[:, None] * t + (t - nconv + jnp.arange(nconv))[None, :]).reshape(-1)
    states = dict(
        fk_p=fk_p.reshape(geo.bp, geo.tp, FOX_HEADS, FOX_HD), fk_s=fk_s.reshape(geo.bs, geo.ts, FOX_HEADS, FOX_HD),
        fv_p=fv_p.reshape(geo.bp, geo.tp, FOX_HEADS, FOX_HD), fv_s=fv_s.reshape(geo.bs, geo.ts, FOX_HEADS, FOX_HD),
        fl_p=logf[:mp, :FOX_HEADS].reshape(geo.bp, geo.tp, FOX_HEADS),
        fl_s=logf[mp:, :FOX_HEADS].reshape(geo.bs, geo.ts, FOX_HEADS),
        cv_p=jnp.take(qkv_raw, last_rows(0, geo.bp, geo.tp), axis=0).reshape(geo.bp, nconv, GDN_QKV),
        cv_s=jnp.take(qkv_raw, last_rows(mp, geo.bs, geo.ts), axis=0).reshape(geo.bs, nconv, GDN_QKV),
        gdn_p=s_p, gdn_s=s_s)
    return o_fox, o_gdn, states


MOE_BLOCK = 512
MOE_COMBINE_ROWS = 256


def _moe_route(x, router, router_bias, *, tm):
    m = x.shape[0]
    assert m % tm == 0
    per_group = N_EXPERTS // N_GROUPS

    def lane_max(v):
        return jnp.max(v, axis=-1, keepdims=True)

    def first_lane(mask, lane):
        return jnp.min(jnp.where(mask, lane.astype(F32), float(LANES)), axis=-1, keepdims=True).astype(jnp.int32)

    def group_all(v, lane, op):
        sh = 1
        while sh < per_group:
            partner = jnp.where((lane & sh) == 0, pltpu.roll(v, LANES - sh, 1), pltpu.roll(v, sh, 1))
            v = op(v, partner)
            sh *= 2
        return v

    def body(x_ref, r_ref, b_ref, e_o, g_o, k_o, cnt_o, carry):
        i = pl.program_id(0)

        @pl.when(i == 0)
        def _():
            carry[...] = jnp.zeros(carry.shape, F32)

        lane = lax.broadcasted_iota(jnp.int32, (tm, LANES), 1)
        valid = lane < N_EXPERTS
        neg = -jnp.inf
        scores = _sigmoid(_dot(x_ref[...], r_ref[...]))
        biased = jnp.where(valid, scores + b_ref[...], neg)
        m1 = group_all(biased, lane, jnp.maximum)
        first = group_all(jnp.where(biased == m1, lane, LANES), lane, jnp.minimum)
        m2 = group_all(jnp.where(lane == first, neg, biased), lane, jnp.maximum)
        grp = jnp.where(valid & ((lane & (per_group - 1)) == 0), m1 + m2, neg)
        emask = jnp.zeros((tm, LANES), jnp.bool_)
        for _ in range(TOPK_GROUPS):
            idx = first_lane(grp == lane_max(grp), lane)
            emask = emask | ((lane - idx >= 0) & (lane - idx < per_group))
            grp = jnp.where(lane == idx, neg, grp)
        cur = jnp.where(emask & valid, biased, neg)
        picks = []
        sel = jnp.zeros((tm, LANES), jnp.bool_)
        for _ in range(TOP_K):
            idx = first_lane(cur == lane_max(cur), lane)
            pick = lane == idx
            picks.append((idx, pick))
            sel = sel | pick
            cur = jnp.where(pick, neg, cur)
        selw = jnp.where(sel, scores, 0.0)
        gates_dense = selw / jnp.sum(selw, axis=-1, keepdims=True) * ROUTED_SCALE
        ri = lax.broadcasted_iota(jnp.int32, (tm, tm), 0)
        cj = lax.broadcasted_iota(jnp.int32, (tm, tm), 1)
        p01 = jnp.where(sel, 1.0, 0.0)
        before = _dot(jnp.where(ri > cj, 1.0, 0.0).astype(BF16), p01.astype(BF16)) + carry[...]
        carry[...] = carry[...] + jnp.sum(p01, axis=0, keepdims=True)
        cnt_o[...] = carry[...]
        e_out = jnp.zeros((tm, LANES), jnp.int32)
        g_out = jnp.zeros((tm, LANES), F32)
        k_out = jnp.zeros((tm, LANES), jnp.int32)
        for j, (idx, pick) in enumerate(picks):
            e_out = jnp.where(lane == j, idx, e_out)
            g_out = jnp.where(lane == j, jnp.sum(jnp.where(pick, gates_dense, 0.0), axis=-1, keepdims=True), g_out)
            rank = jnp.sum(jnp.where(pick, before, 0.0), axis=-1, keepdims=True)
            k_out = jnp.where(lane == j, rank.astype(jnp.int32), k_out)
        e_o[...] = e_out
        g_o[...] = g_out
        k_o[...] = k_out

    tok = pl.BlockSpec((tm, LANES), lambda i: (i, 0))
    one = pl.BlockSpec((1, LANES), lambda i: (0, 0))
    return pl.pallas_call(
        body, grid=(m // tm,), name="moe_route",
        in_specs=[pl.BlockSpec((tm, x.shape[1]), lambda i: (i, 0)), pl.BlockSpec(router.shape, lambda i: (0, 0)), one],
        out_specs=[tok, tok, tok, one],
        out_shape=[jax.ShapeDtypeStruct((m, LANES), jnp.int32), jax.ShapeDtypeStruct((m, LANES), F32),
                   jax.ShapeDtypeStruct((m, LANES), jnp.int32), jax.ShapeDtypeStruct((1, LANES), F32)],
        scratch_shapes=[pltpu.VMEM((1, LANES), F32)],
        compiler_params=_cp("arbitrary"))(x, router, router_bias)


def _moe_dispatch(x_words, dest, cap, *, tb):
    m, wd = x_words.shape
    nt = m // tb

    def body(dest_ref, x_ref, out_ref, sem):
        def row_copy(i, j):
            return pltpu.make_async_copy(x_ref.at[pl.ds(i, 1)], out_ref.at[pl.ds(dest_ref[0, 0, i * TOP_K + j], 1)], sem)

        def issue(i, carry):
            for j in range(TOP_K):
                row_copy(i, j).start(priority=j % 2)
            return carry

        lax.fori_loop(0, tb, issue, 0, unroll=8)
        for _ in range(TOP_K):
            pltpu.make_async_copy(x_ref, out_ref.at[pl.ds(0, tb)], sem).wait()

    return pl.pallas_call(
        body, grid=(nt,), name="moe_dispatch",
        in_specs=[pl.BlockSpec((1, 1, tb * TOP_K), lambda i: (i, 0, 0), memory_space=pltpu.SMEM),
                  pl.BlockSpec((tb, wd), lambda i: (i, 0))],
        out_specs=pl.BlockSpec(memory_space=pl.ANY),
        out_shape=jax.ShapeDtypeStruct((cap, wd), x_words.dtype),
        scratch_shapes=[pltpu.SemaphoreType.DMA(())],
        compiler_params=_cp("arbitrary"))(dest, x_words)


def _pack_halves(y):
    n = y.shape[1] // 2
    bits = lambda a: lax.bitcast_convert_type(a.astype(BF16).astype(F32), jnp.uint32)
    return (bits(y[:, :n]) >> 16) | (bits(y[:, n:]) & jnp.uint32(0xFFFF0000))


def _unpack_halves(w):
    lo = lax.bitcast_convert_type(w << 16, F32).astype(BF16)
    hi = lax.bitcast_convert_type(w & jnp.uint32(0xFFFF0000), F32).astype(BF16)
    return lo, hi


def _ep_ln_packed(acc, x, g, b):
    y = _ep_ln(acc, x, g, b)
    return y, _pack_halves(y)


def _moe_experts(xs_words, blk_exp, blk_next, blk_rows, n_used, w_gate, w_up, w_down, layer):
    cap, half = xs_words.shape
    d = 2 * half
    nblk = cap // MOE_BLOCK
    ed = w_gate.shape[3]

    def body(be_ref, bn_ref, br_ref, nu_ref, x_ref, wg_hbm, wu_hbm, wd_hbm, o_ref, wg_f, wu_f, wd_f, wg_b, wu_b, wd_b,
             slot_ref, sem):
        i = pl.program_id(0)

        def fetch(e, slot):
            return [pltpu.make_async_copy(src.at[layer, e], dst.at[slot], sem.at[slot])
                    for src, dst in ((wg_hbm, wg_f), (wu_hbm, wu_f), (wd_hbm, wd_f))]

        @pl.when(i < nu_ref[0])
        def _():
            first = i == 0
            e = be_ref[i]

            @pl.when(first)
            def _():
                slot_ref[0] = 0
                for cp in fetch(e, 0):
                    cp.start()

            @pl.when(first | (e != be_ref[jnp.maximum(i - 1, 0)]))
            def _():
                slot = jnp.where(first, 0, 1 - slot_ref[0])
                slot_ref[0] = slot
                for cp in fetch(e, slot):
                    cp.wait()
                wg_b[...] = wg_f[slot].astype(BF16)
                wu_b[...] = wu_f[slot].astype(BF16)
                wd_b[...] = wd_f[slot].astype(BF16)

                @pl.when(bn_ref[i] != e)
                def _():
                    for cp in fetch(bn_ref[i], 1 - slot):
                        cp.start()

            row = lax.broadcasted_iota(jnp.int32, (MOE_BLOCK, 1), 0)
            lo, hi = _unpack_halves(jnp.where(row < br_ref[i], x_ref[...], jnp.uint32(0)))
            hg = _dot(lo, wg_b[:half]) + _dot(hi, wg_b[half:])
            hu = _dot(lo, wu_b[:half]) + _dot(hi, wu_b[half:])
            o_ref[...] = _pack_halves(_dot((_silu(hg) * hu).astype(BF16), wd_b[...]))

    blk = lambda i, be, bn, br, nu: (jnp.minimum(i, nu[0] - 1), 0)
    hbm = pl.BlockSpec(memory_space=pl.ANY)
    grid_spec = pltpu.PrefetchScalarGridSpec(
        num_scalar_prefetch=4, grid=(nblk,),
        in_specs=[pl.BlockSpec((MOE_BLOCK, half), blk), hbm, hbm, hbm],
        out_specs=pl.BlockSpec((MOE_BLOCK, half), blk),
        scratch_shapes=[pltpu.VMEM((2, d, ed), F32), pltpu.VMEM((2, d, ed), F32), pltpu.VMEM((2, ed, d), F32),
                        pltpu.VMEM((d, ed), BF16), pltpu.VMEM((d, ed), BF16), pltpu.VMEM((ed, d), BF16),
                        pltpu.SMEM((1,), jnp.int32), pltpu.SemaphoreType.DMA((2,))])
    return pl.pallas_call(body, grid_spec=grid_spec, out_shape=jax.ShapeDtypeStruct((cap, half), jnp.uint32), name="moe_experts",
                          compiler_params=_cp("arbitrary"))(blk_exp, blk_next, blk_rows, n_used, xs_words, w_gate, w_up, w_down)


def _moe_combine(ys_words, dest, gates, *, tc):
    cap, half = ys_words.shape
    d = 2 * half
    m = gates.shape[0]
    nt = m // tc

    def body(dest0_ref, destn_ref, g_ref, y_ref, o_ref, buf, sem):
        i = pl.program_id(0)

        def issue(dref, slot):
            def one(r, carry):
                for j in range(TOP_K):
                    pltpu.make_async_copy(y_ref.at[pl.ds(dref[0, 0, r * TOP_K + j], 1)],
                                          buf.at[slot, j, pl.ds(r, 1)], sem.at[slot]).start(priority=j % 2)
                return carry
            lax.fori_loop(0, tc, one, 0, unroll=8)

        @pl.when(i == 0)
        def _():
            issue(dest0_ref, 0)

        @pl.when(i + 1 < nt)
        def _():
            issue(destn_ref, (i + 1) % 2)

        slot = i % 2
        for j in range(TOP_K):
            pltpu.make_async_copy(y_ref.at[pl.ds(0, tc)], buf.at[slot, j], sem.at[slot]).wait()
        g = g_ref[...]
        acc_lo = acc_hi = None
        for j in range(TOP_K):
            lo, hi = _unpack_halves(buf[slot, j])
            gj = g[:, j:j + 1]
            acc_lo = lo.astype(F32) * gj if acc_lo is None else acc_lo + lo.astype(F32) * gj
            acc_hi = hi.astype(F32) * gj if acc_hi is None else acc_hi + hi.astype(F32) * gj
        o_ref[:, :half] = acc_lo
        o_ref[:, half:] = acc_hi

    return pl.pallas_call(
        body, grid=(nt,), name="moe_combine",
        in_specs=[pl.BlockSpec((1, 1, tc * TOP_K), lambda i: (0, 0, 0), memory_space=pltpu.SMEM),
                  pl.BlockSpec((1, 1, tc * TOP_K), lambda i: (jnp.minimum(i + 1, nt - 1), 0, 0), memory_space=pltpu.SMEM),
                  pl.BlockSpec((tc, LANES), lambda i: (i, 0)),
                  pl.BlockSpec(memory_space=pl.ANY)],
        out_specs=pl.BlockSpec((tc, d), lambda i: (i, 0)),
        out_shape=jax.ShapeDtypeStruct((m, d), F32),
        scratch_shapes=[pltpu.VMEM((2, TOP_K, tc, half), jnp.uint32), pltpu.SemaphoreType.DMA((2,))],
        compiler_params=_cp("arbitrary"))(dest, dest, gates, ys_words)


def _moe_layer(geo, x_res, xb, x_words, layer, router, router_bias, w_gate, w_up, w_down, ws_gate, ws_up, ws_down,
               ln_g, ln_b, out_dtype, out_rows):
    m, d = x_res.shape
    tm = geo.tm
    eidx, gates, rank, counts = _moe_route(xb, _pad_cols(router, LANES).astype(BF16),
                                           jnp.pad(router_bias, (0, LANES - N_EXPERTS)).reshape(1, -1), tm=tm)
    cnt = counts[0, :N_EXPERTS].astype(jnp.int32)
    padded = (cnt + MOE_BLOCK - 1) // MOE_BLOCK * MOE_BLOCK
    pad_end = jnp.cumsum(padded)
    pad_start = pad_end - padded
    n_blocks = -(-(m * TOP_K) // MOE_BLOCK) + N_EXPERTS
    cap = n_blocks * MOE_BLOCK
    blk_row0 = jnp.arange(n_blocks, dtype=jnp.int32) * MOE_BLOCK
    blk_exp = jnp.minimum(jnp.sum((pad_end[None, :] <= blk_row0[:, None]).astype(jnp.int32), axis=1), N_EXPERTS - 1)
    blk_rows = jnp.clip(cnt[blk_exp] - (blk_row0 - pad_start[blk_exp]), 0, MOE_BLOCK).astype(jnp.int32)
    n_used = (pad_end[-1:] // MOE_BLOCK).astype(jnp.int32)
    eid = jnp.arange(N_EXPERTS, dtype=jnp.int32)
    later = jnp.where((eid[None, :] > eid[:, None]) & (cnt[None, :] > 0), eid[None, :], N_EXPERTS)
    next_e = jnp.min(later, axis=1)
    next_e = jnp.where(next_e == N_EXPERTS, eid, next_e)
    blk_next = next_e[blk_exp].astype(jnp.int32)
    dest = jnp.sum(jnp.where(eidx[:, :TOP_K, None] == eid, pad_start, 0), axis=-1) + rank[:, :TOP_K]

    tb = tm
    xs_words = _moe_dispatch(x_words, dest.reshape(m // tb, 1, tb * TOP_K), cap, tb=tb)
    ys = _moe_experts(xs_words, blk_exp, blk_next, blk_rows, n_used, w_gate, w_up, w_down, layer)
    tc = min(MOE_COMBINE_ROWS, tm)
    routed = _moe_combine(ys, dest.reshape(m // tc, 1, tc * TOP_K), gates, tc=tc)

    sd = ws_gate.shape[1]
    w_sh = jnp.concatenate([ws_gate, ws_up], axis=1).astype(BF16)
    (hs,) = _mm([(xb, w_sh)], tm=tm, tn=2 * sd, epilogue=lambda acc: (_silu(acc[:, :sd]) * acc[:, sd:],), outs=[(sd, BF16)])
    ys = [_mm([(hs, ws_down.astype(BF16))], tm=tm, tn=d, rows=g,
              aux=[(x_res, 'rowcol'), (routed, 'rowcol'), (ln_g.reshape(1, -1), 'col'), (ln_b.reshape(1, -1), 'col')],
              epilogue=lambda acc, xr, rt, g_, b: (_ep_ln(acc + rt, xr, g_, b),), outs=[(d, out_dtype)], name="moe_out")[0]
          for g in out_rows]
    return ys


def _rwkv_params(li, mu, w0, w2, a0, a2, g2, k_k, k_a, r_k, ln_g, ln_b):
    row = lambda a: a[li].reshape(1, -1)
    padr = lambda a, n: jnp.pad(a, ((0, n - a.shape[0]), (0, 0)))
    return dict(mu=_rwkv_cols_layout(mu[li])[None, :], w0=row(w0), a0=row(a0), k_k=row(k_k), k_a=row(k_a),
                w2=padr(w2[li], LANES), a2=padr(a2[li], LANES), g2=g2[li], r_k=row(r_k), ln_g=row(ln_g), ln_b=row(ln_b))


def _mix_out(geo, x_res, o_a, o_b, w_out, ln_g, ln_b):
    d = x_res.shape[1]
    ka = o_a.shape[1]
    return _mm([(o_a, w_out[:ka].astype(BF16)), (o_b, w_out[ka:].astype(BF16))], tm=geo.tm, tn=d,
               aux=[(x_res, 'rowcol'), (ln_g.reshape(1, -1), 'col'), (ln_b.reshape(1, -1), 'col')],
               epilogue=_ep_ln_packed, outs=[(d, BF16), (d // 2, jnp.uint32)], name="mix_out")


def kernel(x_prompt, x_sample, cache_mla_ckv, cache_mla_krope, state_rwkv_shift, state_rwkv_wkv, cache_fox_k, cache_fox_v, cache_fox_logf, state_gdn_conv, state_gdn_wkv, ln1_g, ln1_b, ln2_g, ln2_b, ev_w_in, ev_w_out, mla_kv_norm, mla_w_uk, mla_w_uv, rwkv_mu, rwkv_w0, rwkv_w2, rwkv_a0, rwkv_a2, rwkv_g2, rwkv_k_k, rwkv_k_a, rwkv_r_k, rwkv_ln_g, rwkv_ln_b, od_w_in, od_w_out, fox_q_norm, fox_k_norm, fox_f_bias, gdn_conv_w, gdn_a_log, gdn_dt_bias, gdn_norm, moe_router, moe_router_bias, moe_w_gate, moe_w_up, moe_w_down, moe_ws_gate, moe_ws_up, moe_ws_down):
    bp, tp, d = x_prompt.shape
    bs, ts, _ = x_sample.shape
    geo = _Geom(bp, tp, bs, ts, cache_mla_ckv.shape[2])
    x = xb = jnp.concatenate([x_prompt.reshape(bp * tp, d).astype(BF16), x_sample.reshape(bs * ts, d).astype(BF16)], axis=0)
    rope_tab = _rope_table(geo)
    st = {}
    n_layers = ln1_g.shape[0]
    for layer in range(n_layers):
        li = layer // 2
        if layer % 2 == 0:
            rw = _rwkv_params(li, rwkv_mu, rwkv_w0, rwkv_w2, rwkv_a0, rwkv_a2, rwkv_g2, rwkv_k_k, rwkv_k_a, rwkv_r_k,
                              rwkv_ln_g, rwkv_ln_b)
            o_a, o_b, new = _even_mixer(geo, xb, rope_tab, ev_w_in[li], mla_kv_norm[li], mla_w_uk[li], mla_w_uv[li], rw,
                                        cache_mla_ckv[li], cache_mla_krope[li], state_rwkv_shift[li], state_rwkv_wkv[li])
            w_out = ev_w_out[li]
        else:
            fx = {'q_norm': fox_q_norm[li], 'k_norm': fox_k_norm[li], 'f_bias': fox_f_bias[li]}
            gd = {'conv_w': gdn_conv_w[li], 'a_log': gdn_a_log[li], 'dt_bias': gdn_dt_bias[li], 'norm': gdn_norm[li]}
            o_a, o_b, new = _odd_mixer(geo, xb, od_w_in[li], fx, gd, cache_fox_k[li], cache_fox_v[li], cache_fox_logf[li],
                                       state_gdn_conv[li], state_gdn_wkv[li])
            w_out = od_w_out[li]
        for name, val in new.items():
            st.setdefault(name, []).append(val)
        xb, x_words = _mix_out(geo, x, o_a, o_b, w_out, ln1_g[layer], ln1_b[layer])
        last = layer == n_layers - 1
        ys = _moe_layer(geo, xb, xb, x_words, layer, moe_router[layer], moe_router_bias[layer], moe_w_gate, moe_w_up,
                        moe_w_down, moe_ws_gate[layer], moe_ws_up[layer], moe_ws_down[layer], ln2_g[layer], ln2_b[layer],
                        F32 if last else BF16, ((0, geo.mp), (geo.mp, geo.ms)) if last else ((0, geo.m),))
        x = xb = ys[0]
    names = ('ckv', 'kr', 'sh', 'rwkv', 'fk', 'fv', 'fl', 'cv', 'gdn')
    return ((ys[0].reshape(bp, tp, d), ys[1].reshape(bs, ts, d))
            + tuple(jnp.stack(st[n + '_p']) for n in names) + tuple(jnp.stack(st[n + '_s']) for n in names))
```
